```python
import jax, jax.numpy as jnp
from jax import lax
import numpy as np


D_MODEL = 4096
BATCH = 8
SEQ = 4096
DEPTH = 1

PLE_DIM = 256
ATT_HEAD_DIM = 128
ATT_HEADS_PER_GROUP = 8
ATT_GROUPS = ((128, 1), (512, 4), (2048, 16))
N_ATT_GROUPS = 3
ATT_BLOCK = 128
ATT_QKV_WIDTH = N_ATT_GROUPS * ATT_HEADS_PER_GROUP * ATT_HEAD_DIM
ATT_OUT_WIDTH = ATT_HEADS_PER_GROUP * ATT_HEAD_DIM
RET_HEADS = 8
RET_QK_DIM = 256
RET_V_DIM = 512
RET_QK_WIDTH = RET_HEADS * RET_QK_DIM
RET_V_WIDTH = RET_HEADS * RET_V_DIM
RET_CHUNK = 128
RET_ROPE_BASE = 10000.0
IN_WIDTH = 3 * ATT_QKV_WIDTH + 2 * RET_QK_WIDTH + 2 * RET_V_WIDTH + 2 * D_MODEL
FFN_HIDDEN = -(-8 * D_MODEL // (3 * 256)) * 256
LN_EPS = 1e-5
GN_EPS = 1e-6
DEEPNORM_ALPHA = (2 * DEPTH) ** 0.25
DEEPNORM_BETA = (8 * DEPTH) ** -0.25
NEG_INF = -1e30

kernel_name = 'hybrid_dilated_attn_retention_deepnorm_block'


def _layer_norm(t, g, b):
    tf = t.astype(jnp.float32)
    mu = jnp.mean(tf, axis=-1, keepdims=True)
    var = jnp.mean(jnp.square(tf - mu), axis=-1, keepdims=True)
    return ((tf - mu) * lax.rsqrt(var + LN_EPS) * g + b).astype(t.dtype)


def _dilated_window_attention(q, k, v, window, dilation):
    b, s, h, e = q.shape
    w_sub = window // dilation
    span = dilation * ATT_BLOCK
    s_pad = -(-s // span) * span
    n_blk = s_pad // span
    pad = ((0, 0), (0, s_pad - s), (0, 0), (0, 0))

    def blocks(t):
        return jnp.pad(t, pad).reshape(b, n_blk, ATT_BLOCK, dilation, h, e)

    def with_prev(t):
        prev = jnp.pad(t[:, :-1], ((0, 0), (1, 0), (0, 0), (0, 0), (0, 0), (0, 0)))
        return jnp.concatenate([prev, t], axis=2)

    qb = blocks(q)
    kk = with_prev(blocks(k))
    vv = with_prev(blocks(v))
    qi = jnp.arange(ATT_BLOCK)[:, None]
    ki = jnp.arange(2 * ATT_BLOCK)[None, :]
    dist = qi + ATT_BLOCK - ki
    nb = jnp.arange(n_blk)[:, None, None]
    valid = (dist >= 0) & (dist <= w_sub) & (nb * ATT_BLOCK + ki - ATT_BLOCK >= 0)
    scores = jnp.einsum('bnqrhe,bnkrhe->bnrhqk', qb, kk).astype(jnp.float32) * (e ** -0.5)
    scores = jnp.where(valid[None, :, None, None], scores, NEG_INF)
    m = jnp.max(scores, axis=-1, keepdims=True)
    pr = jnp.exp(scores - m)
    denom = jnp.sum(pr, axis=-1)
    o = jnp.einsum('bnrhqk,bnkrhe->bnqrhe', pr.astype(v.dtype), vv).astype(jnp.float32)
    o = o / jnp.transpose(denom, (0, 1, 4, 2, 3))[..., None]
    lse = jnp.transpose(m[..., 0] + jnp.log(denom), (0, 1, 4, 2, 3))
    o = o.reshape(b, s_pad, h, e)[:, :s]
    lse = lse.reshape(b, s_pad, h)[:, :s]
    return o, lse


def _rotary(t, pos):
    half = t.shape[-1] // 2
    inv_freq = RET_ROPE_BASE ** (-jnp.arange(half, dtype=jnp.float32) / half)
    ang = pos[:, None] * inv_freq[None, :]
    cos = jnp.cos(ang)[None, :, None, :]
    sin = jnp.sin(ang)[None, :, None, :]
    t1 = t[..., :half].astype(jnp.float32)
    t2 = t[..., half:].astype(jnp.float32)
    return jnp.concatenate([t1 * cos - t2 * sin, t1 * sin + t2 * cos], axis=-1).astype(t.dtype)


def _retention(q, k, v):
    b, s, h, dk = q.shape
    dv = v.shape[-1]
    c = RET_CHUNK
    nc = s // c
    log_gamma = jnp.log(1.0 - 2.0 ** (-5.0 - jnp.arange(h, dtype=jnp.float32)))
    idx = jnp.arange(c, dtype=jnp.float32)
    diff = idx[:, None] - idx[None, :]
    intra = jnp.where(diff >= 0, jnp.exp(jnp.maximum(diff, 0.0)[None] * log_gamma[:, None, None]), 0.0)
    cross_decay = jnp.exp((idx + 1.0)[None, :] * log_gamma[:, None])[..., None]
    state_decay = jnp.exp((c - 1.0 - idx)[None, :] * log_gamma[:, None])[..., None]
    chunk_decay = jnp.exp(c * log_gamma)[:, None, None]

    def chunks(t):
        return t.reshape(b, nc, c, h, t.shape[-1]).transpose(1, 0, 3, 2, 4)

    def step(state, qkv):
        qc, kc, vc = qkv
        att = jnp.einsum('bhnd,bhmd->bhnm', qc, kc) * intra
        inner = jnp.einsum('bhnm,bhme->bhne', att, vc)
        cross = jnp.einsum('bhnd,bhde->bhne', qc, state) * cross_decay
        new_state = chunk_decay * state + jnp.einsum('bhmd,bhme->bhde', kc * state_decay, vc)
        return new_state, inner + cross

    init = jnp.zeros((b, h, dk, dv), jnp.float32)
    _, out = lax.scan(step, init, (chunks(q), chunks(k), chunks(v)))
    return out.transpose(1, 0, 3, 2, 4).reshape(b, s, h, dv)


def _fwd_setup_inputs(seed: int = 0) -> dict:
    key = jax.random.key(seed)
    ks = jax.random.split(key, 17)

    def nrm(k, shape, scale):
        return jax.random.normal(k, shape, jnp.float32) * scale

    return {
        'x': nrm(ks[0], (BATCH, SEQ, D_MODEL), 1.0),
        'p': nrm(ks[1], (DEPTH, BATCH, SEQ, PLE_DIM), 1.0),
        'w_in': nrm(ks[2], (DEPTH, D_MODEL, IN_WIDTH), D_MODEL ** -0.5),
        'w_attn_out': nrm(ks[3], (DEPTH, ATT_OUT_WIDTH, D_MODEL), ATT_OUT_WIDTH ** -0.5),
        'w_ret_out': nrm(ks[4], (DEPTH, RET_V_WIDTH, D_MODEL), RET_V_WIDTH ** -0.5),
        'ret_gn_g': 1.0 + nrm(ks[5], (DEPTH, RET_V_WIDTH), 0.02),
        'w_o': nrm(ks[6], (DEPTH, D_MODEL, D_MODEL), DEEPNORM_BETA * D_MODEL ** -0.5),
        'ln1_g': 1.0 + nrm(ks[7], (DEPTH, D_MODEL), 0.02),
        'ln1_b': nrm(ks[8], (DEPTH, D_MODEL), 0.02),
        'w_ffn_gate': nrm(ks[9], (DEPTH, D_MODEL, FFN_HIDDEN), D_MODEL ** -0.5),
        'w_ffn_up': nrm(ks[10], (DEPTH, D_MODEL, FFN_HIDDEN), D_MODEL ** -0.5),
        'w_ffn_down': nrm(ks[11], (DEPTH, FFN_HIDDEN, D_MODEL), DEEPNORM_BETA * FFN_HIDDEN ** -0.5),
        'w_ple_gate': nrm(ks[12], (DEPTH, D_MODEL, D_MODEL), D_MODEL ** -0.5),
        'w_ple_up': nrm(ks[13], (DEPTH, PLE_DIM, D_MODEL), DEEPNORM_BETA * PLE_DIM ** -0.5),
        'ln2_g': 1.0 + nrm(ks[14], (DEPTH, D_MODEL), 0.02),
        'ln2_b': nrm(ks[15], (DEPTH, D_MODEL), 0.02),
    }


def _fwd_reference(x, p, w_in, w_attn_out, w_ret_out, ret_gn_g, w_o, ln1_g, ln1_b, w_ffn_gate, w_ffn_up,
              w_ffn_down, w_ple_gate, w_ple_up, ln2_g, ln2_b):
    b, s, _ = x.shape
    pos = jnp.arange(s, dtype=jnp.float32)
    widths = [ATT_QKV_WIDTH] * 3 + [RET_QK_WIDTH] * 2 + [RET_V_WIDTH] * 2 + [D_MODEL] * 2
    splits = np.cumsum(widths)[:-1].tolist()
    h = x
    for i in range(DEPTH):
        proj = h @ w_in[i]
        qa, ka, va, qr, kr, vr, gr, gate_att, gate_ret = jnp.split(proj, splits, axis=-1)

        qa = qa.reshape(b, s, N_ATT_GROUPS, ATT_HEADS_PER_GROUP, ATT_HEAD_DIM)
        ka = ka.reshape(b, s, N_ATT_GROUPS, ATT_HEADS_PER_GROUP, ATT_HEAD_DIM)
        va = va.reshape(b, s, N_ATT_GROUPS, ATT_HEADS_PER_GROUP, ATT_HEAD_DIM)
        outs, lses = [], []
        for g, (win, dil) in enumerate(ATT_GROUPS):
            o_g, lse_g = _dilated_window_attention(qa[:, :, g], ka[:, :, g], va[:, :, g], win, dil)
            outs.append(o_g)
            lses.append(lse_g)
        wts = jax.nn.softmax(jnp.stack(lses, axis=0), axis=0)
        o_att = jnp.sum(wts[..., None] * jnp.stack(outs, axis=0), axis=0)
        o_att = o_att.astype(x.dtype).reshape(b, s, ATT_OUT_WIDTH)
        y_att = o_att @ w_attn_out[i]

        qr = _rotary(qr.reshape(b, s, RET_HEADS, RET_QK_DIM), pos)
        kr = _rotary(kr.reshape(b, s, RET_HEADS, RET_QK_DIM), pos) * (RET_QK_DIM ** -0.5)
        vr = vr.reshape(b, s, RET_HEADS, RET_V_DIM)
        r = _retention(qr, kr, vr)
        mu = jnp.mean(r, axis=-1, keepdims=True)
        var = jnp.mean(jnp.square(r - mu), axis=-1, keepdims=True)
        r = ((r - mu) * lax.rsqrt(var + GN_EPS)).reshape(b, s, RET_V_WIDTH) * ret_gn_g[i]
        y_ret = (jax.nn.silu(gr) * r.astype(x.dtype)) @ w_ret_out[i]

        mixed = jax.nn.sigmoid(gate_att) * y_att + jax.nn.sigmoid(gate_ret) * y_ret
        h = _layer_norm(DEEPNORM_ALPHA * h + mixed @ w_o[i], ln1_g[i], ln1_b[i])

        ffn = (jax.nn.silu(h @ w_ffn_gate[i]) * (h @ w_ffn_up[i])) @ w_ffn_down[i]
        ple = jax.nn.sigmoid(h @ w_ple_gate[i]) * (p[i] @ w_ple_up[i])
        h = _layer_norm(DEEPNORM_ALPHA * h + ffn + ple, ln2_g[i], ln2_b[i])
    return h


import jax as _jax
import jax.numpy as _jnp

TWIN_FORMAT = 'train_step'
FWD_PARAMS = ['x', 'p', 'w_in', 'w_attn_out', 'w_ret_out', 'ret_gn_g', 'w_o', 'ln1_g', 'ln1_b', 'w_ffn_gate', 'w_ffn_up', 'w_ffn_down', 'w_ple_gate', 'w_ple_up', 'ln2_g', 'ln2_b']
TWIN_WEIGHTS = ['w_in', 'w_attn_out', 'w_ret_out', 'ret_gn_g', 'w_o', 'ln1_g', 'ln1_b', 'w_ffn_gate', 'w_ffn_up', 'w_ffn_down', 'w_ple_gate', 'w_ple_up', 'ln2_g', 'ln2_b']
TWIN_DIFF_INPUT = 'x'
TWIN_INPUTS = ['x', 'p', 'w_in', 'w_attn_out', 'w_ret_out', 'ret_gn_g', 'w_o', 'ln1_g', 'ln1_b', 'w_ffn_gate', 'w_ffn_up', 'w_ffn_down', 'w_ple_gate', 'w_ple_up', 'ln2_g', 'ln2_b', 'loss_target', 'm_w_in', 'm_w_attn_out', 'm_w_ret_out', 'm_ret_gn_g', 'm_w_o', 'm_ln1_g', 'm_ln1_b', 'm_w_ffn_gate', 'm_w_ffn_up', 'm_w_ffn_down', 'm_w_ple_gate', 'm_w_ple_up', 'm_ln2_g', 'm_ln2_b', 'v_w_in', 'v_w_attn_out', 'v_w_ret_out', 'v_ret_gn_g', 'v_w_o', 'v_ln1_g', 'v_ln1_b', 'v_w_ffn_gate', 'v_w_ffn_up', 'v_w_ffn_down', 'v_w_ple_gate', 'v_w_ple_up', 'v_ln2_g', 'v_ln2_b']
TWIN_OUTPUTS = ['loss', 'grad_x', 'grad_w_in', 'grad_w_attn_out', 'grad_w_ret_out', 'grad_ret_gn_g', 'grad_w_o', 'grad_ln1_g', 'grad_ln1_b', 'grad_w_ffn_gate', 'grad_w_ffn_up', 'grad_w_ffn_down', 'grad_w_ple_gate', 'grad_w_ple_up', 'grad_ln2_g', 'grad_ln2_b', 'delta_w_in', 'delta_w_attn_out', 'delta_w_ret_out', 'delta_ret_gn_g', 'delta_w_o', 'delta_ln1_g', 'delta_ln1_b', 'delta_w_ffn_gate', 'delta_w_ffn_up', 'delta_w_ffn_down', 'delta_w_ple_gate', 'delta_w_ple_up', 'delta_ln2_g', 'delta_ln2_b', 'new_m_w_in', 'new_m_w_attn_out', 'new_m_w_ret_out', 'new_m_ret_gn_g', 'new_m_w_o', 'new_m_ln1_g', 'new_m_ln1_b', 'new_m_w_ffn_gate', 'new_m_w_ffn_up', 'new_m_w_ffn_down', 'new_m_w_ple_gate', 'new_m_w_ple_up', 'new_m_ln2_g', 'new_m_ln2_b', 'new_v_w_in', 'new_v_w_attn_out', 'new_v_w_ret_out', 'new_v_ret_gn_g', 'new_v_w_o', 'new_v_ln1_g', 'new_v_ln1_b', 'new_v_w_ffn_gate', 'new_v_w_ffn_up', 'new_v_w_ffn_down', 'new_v_w_ple_gate', 'new_v_w_ple_up', 'new_v_ln2_g', 'new_v_ln2_b']
TWIN_LEAF_KINDS = {'loss': 'loss', 'grad_x': 'grad_x', 'grad_w_in': 'grad_w', 'grad_w_attn_out': 'grad_w', 'grad_w_ret_out': 'grad_w', 'grad_ret_gn_g': 'grad_w', 'grad_w_o': 'grad_w', 'grad_ln1_g': 'grad_w', 'grad_ln1_b': 'grad_w', 'grad_w_ffn_gate': 'grad_w', 'grad_w_ffn_up': 'grad_w', 'grad_w_ffn_down': 'grad_w', 'grad_w_ple_gate': 'grad_w', 'grad_w_ple_up': 'grad_w', 'grad_ln2_g': 'grad_w', 'grad_ln2_b': 'grad_w', 'delta_w_in': 'delta_w', 'delta_w_attn_out': 'delta_w', 'delta_w_ret_out': 'delta_w', 'delta_ret_gn_g': 'delta_w', 'delta_w_o': 'delta_w', 'delta_ln1_g': 'delta_w', 'delta_ln1_b': 'delta_w', 'delta_w_ffn_gate': 'delta_w', 'delta_w_ffn_up': 'delta_w', 'delta_w_ffn_down': 'delta_w', 'delta_w_ple_gate': 'delta_w', 'delta_w_ple_up': 'delta_w', 'delta_ln2_g': 'delta_w', 'delta_ln2_b': 'delta_w', 'new_m_w_in': 'new_m', 'new_m_w_attn_out': 'new_m', 'new_m_w_ret_out': 'new_m', 'new_m_ret_gn_g': 'new_m', 'new_m_w_o': 'new_m', 'new_m_ln1_g': 'new_m', 'new_m_ln1_b': 'new_m', 'new_m_w_ffn_gate': 'new_m', 'new_m_w_ffn_up': 'new_m', 'new_m_w_ffn_down': 'new_m', 'new_m_w_ple_gate': 'new_m', 'new_m_w_ple_up': 'new_m', 'new_m_ln2_g': 'new_m', 'new_m_ln2_b': 'new_m', 'new_v_w_in': 'new_v', 'new_v_w_attn_out': 'new_v', 'new_v_w_ret_out': 'new_v', 'new_v_ret_gn_g': 'new_v', 'new_v_w_o': 'new_v', 'new_v_ln1_g': 'new_v', 'new_v_ln1_b': 'new_v', 'new_v_w_ffn_gate': 'new_v', 'new_v_w_ffn_up': 'new_v', 'new_v_w_ffn_down': 'new_v', 'new_v_w_ple_gate': 'new_v', 'new_v_w_ple_up': 'new_v', 'new_v_ln2_g': 'new_v', 'new_v_ln2_b': 'new_v'}


def _forward(args):
    return _fwd_reference(*[args[k] for k in FWD_PARAMS])


def _output_shape():
    out = _jax.eval_shape(lambda: _forward(_fwd_setup_inputs(0)))
    return out.shape, out.dtype

N_MICROBATCH = 1
ADAM_LR = 0.001
ADAM_B1 = 0.9
ADAM_B2 = 0.999
ADAM_EPS = 1e-08
ADAM_WD = 0.01
ADAM_STEP = 10
PER_EXAMPLE_BATCH_AXIS = {'x': 0, 'p': 1, 'loss_target': 0}
SHARED_INPUTS = []
_WEIGHT_DTYPES = {'w_in': _jnp.float32, 'w_attn_out': _jnp.float32, 'w_ret_out': _jnp.float32, 'ret_gn_g': _jnp.float32, 'w_o': _jnp.float32, 'ln1_g': _jnp.float32, 'ln1_b': _jnp.float32, 'w_ffn_gate': _jnp.float32, 'w_ffn_up': _jnp.float32, 'w_ffn_down': _jnp.float32, 'w_ple_gate': _jnp.float32, 'w_ple_up': _jnp.float32, 'ln2_g': _jnp.float32, 'ln2_b': _jnp.float32}
MOMENT_SCALE = {'w_in': 5.660812e-03, 'w_attn_out': 2.238217e-03, 'w_ret_out': 7.167087e-03, 'ret_gn_g': 7.257144e-03, 'w_o': 1.258152e-02, 'ln1_g': 2.166475e-01, 'ln1_b': 1.069995e-01, 'w_ffn_gate': 7.734228e-03, 'w_ffn_up': 7.488089e-03, 'w_ffn_down': 2.062182e-02, 'w_ple_gate': 4.347274e-03, 'w_ple_up': 1.875109e-02, 'ln2_g': 7.990386e+00, 'ln2_b': 1.581852e-01}


def _to_microbatches(a, axis):
    t = _jnp.moveaxis(a, axis, 0)
    t = t.reshape((N_MICROBATCH, t.shape[0] // N_MICROBATCH) + t.shape[1:])
    return _jnp.moveaxis(t, 1, axis + 1)


def setup_inputs(seed: int = 0) -> dict:
    inp = _fwd_setup_inputs(seed)
    key = _jax.random.fold_in(_jax.random.key(seed), 7919)
    shape, _ = _output_shape()
    out = dict(inp)
    out["loss_target"] = _jax.random.normal(_jax.random.fold_in(key, 0), shape, _jnp.float32)
    for i, name in enumerate(TWIN_WEIGHTS):
        w = inp[name].astype(_jnp.float32)
        if MOMENT_SCALE is None:
            s = _jnp.sqrt(_jnp.mean(_jnp.square(w)) + 1e-30)
        else:
            s = MOMENT_SCALE[name]
        km, kv = _jax.random.split(_jax.random.fold_in(key, i + 1))
        out[name] = w
        out["m_" + name] = s * _jax.random.normal(km, w.shape, _jnp.float32)
        out["v_" + name] = (s * s) * _jax.random.uniform(kv, w.shape, _jnp.float32, 0.5, 1.5)
    if N_MICROBATCH > 1:
        for name, axis in PER_EXAMPLE_BATCH_AXIS.items():
            out[name] = _to_microbatches(out[name], axis)
    return {'x': out['x'], 'p': out['p'], 'w_in': out['w_in'], 'w_attn_out': out['w_attn_out'], 'w_ret_out': out['w_ret_out'], 'ret_gn_g': out['ret_gn_g'], 'w_o': out['w_o'], 'ln1_g': out['ln1_g'], 'ln1_b': out['ln1_b'], 'w_ffn_gate': out['w_ffn_gate'], 'w_ffn_up': out['w_ffn_up'], 'w_ffn_down': out['w_ffn_down'], 'w_ple_gate': out['w_ple_gate'], 'w_ple_up': out['w_ple_up'], 'ln2_g': out['ln2_g'], 'ln2_b': out['ln2_b'], 'loss_target': out['loss_target'], 'm_w_in': out['m_w_in'], 'm_w_attn_out': out['m_w_attn_out'], 'm_w_ret_out': out['m_w_ret_out'], 'm_ret_gn_g': out['m_ret_gn_g'], 'm_w_o': out['m_w_o'], 'm_ln1_g': out['m_ln1_g'], 'm_ln1_b': out['m_ln1_b'], 'm_w_ffn_gate': out['m_w_ffn_gate'], 'm_w_ffn_up': out['m_w_ffn_up'], 'm_w_ffn_down': out['m_w_ffn_down'], 'm_w_ple_gate': out['m_w_ple_gate'], 'm_w_ple_up': out['m_w_ple_up'], 'm_ln2_g': out['m_ln2_g'], 'm_ln2_b': out['m_ln2_b'], 'v_w_in': out['v_w_in'], 'v_w_attn_out': out['v_w_attn_out'], 'v_w_ret_out': out['v_w_ret_out'], 'v_ret_gn_g': out['v_ret_gn_g'], 'v_w_o': out['v_w_o'], 'v_ln1_g': out['v_ln1_g'], 'v_ln1_b': out['v_ln1_b'], 'v_w_ffn_gate': out['v_w_ffn_gate'], 'v_w_ffn_up': out['v_w_ffn_up'], 'v_w_ffn_down': out['v_w_ffn_down'], 'v_w_ple_gate': out['v_w_ple_gate'], 'v_w_ple_up': out['v_w_ple_up'], 'v_ln2_g': out['v_ln2_g'], 'v_ln2_b': out['v_ln2_b']}


def _loss(weights, diff, rest, loss_target):
    with _jax.named_scope("forward"):
        args = {**rest, TWIN_DIFF_INPUT: diff, **{k: w.astype(_WEIGHT_DTYPES[k]) for k, w in weights.items()}}
        y = _forward(args)
    with _jax.named_scope("loss_head"):
        err = _jnp.square(y.astype(_jnp.float32) - loss_target)
        return 0.5 * _jnp.sum(_jnp.mean(err, axis=-1)) if err.ndim else 0.5 * err


def _adamw(w, g, m, v):
    m = ADAM_B1 * m + (1.0 - ADAM_B1) * g
    v = ADAM_B2 * v + (1.0 - ADAM_B2) * _jnp.square(g)
    m_hat = m / (1.0 - ADAM_B1 ** ADAM_STEP)
    v_hat = v / (1.0 - ADAM_B2 ** ADAM_STEP)
    delta = -ADAM_LR * (m_hat / (_jnp.sqrt(v_hat) + ADAM_EPS) + ADAM_WD * w)
    return delta, m, v


def reference(x, p, w_in, w_attn_out, w_ret_out, ret_gn_g, w_o, ln1_g, ln1_b, w_ffn_gate, w_ffn_up, w_ffn_down, w_ple_gate, w_ple_up, ln2_g, ln2_b, loss_target, m_w_in, m_w_attn_out, m_w_ret_out, m_ret_gn_g, m_w_o, m_ln1_g, m_ln1_b, m_w_ffn_gate, m_w_ffn_up, m_w_ffn_down, m_w_ple_gate, m_w_ple_up, m_ln2_g, m_ln2_b, v_w_in, v_w_attn_out, v_w_ret_out, v_ret_gn_g, v_w_o, v_ln1_g, v_ln1_b, v_w_ffn_gate, v_w_ffn_up, v_w_ffn_down, v_w_ple_gate, v_w_ple_up, v_ln2_g, v_ln2_b):
    given = dict(x=x, p=p, w_in=w_in, w_attn_out=w_attn_out, w_ret_out=w_ret_out, ret_gn_g=ret_gn_g, w_o=w_o, ln1_g=ln1_g, ln1_b=ln1_b, w_ffn_gate=w_ffn_gate, w_ffn_up=w_ffn_up, w_ffn_down=w_ffn_down, w_ple_gate=w_ple_gate, w_ple_up=w_ple_up, ln2_g=ln2_g, ln2_b=ln2_b, loss_target=loss_target, m_w_in=m_w_in, m_w_attn_out=m_w_attn_out, m_w_ret_out=m_w_ret_out, m_ret_gn_g=m_ret_gn_g, m_w_o=m_w_o, m_ln1_g=m_ln1_g, m_ln1_b=m_ln1_b, m_w_ffn_gate=m_w_ffn_gate, m_w_ffn_up=m_w_ffn_up, m_w_ffn_down=m_w_ffn_down, m_w_ple_gate=m_w_ple_gate, m_w_ple_up=m_w_ple_up, m_ln2_g=m_ln2_g, m_ln2_b=m_ln2_b, v_w_in=v_w_in, v_w_attn_out=v_w_attn_out, v_w_ret_out=v_w_ret_out, v_ret_gn_g=v_ret_gn_g, v_w_o=v_w_o, v_ln1_g=v_ln1_g, v_ln1_b=v_ln1_b, v_w_ffn_gate=v_w_ffn_gate, v_w_ffn_up=v_w_ffn_up, v_w_ffn_down=v_w_ffn_down, v_w_ple_gate=v_w_ple_gate, v_w_ple_up=v_w_ple_up, v_ln2_g=v_ln2_g, v_ln2_b=v_ln2_b)
    weights = {n: given[n] for n in TWIN_WEIGHTS}
    shared = {n: given[n] for n in SHARED_INPUTS}
    per_example = {n: given[n] for n in ['x', 'p']}
    grad_fn = _jax.value_and_grad(_loss, argnums=(0, 1))

    def one_microbatch(ex, loss_target):
        ex = dict(ex)
        diff = ex.pop(TWIN_DIFF_INPUT)
        return grad_fn(weights, diff, {**shared, **ex}, loss_target)

    if N_MICROBATCH == 1:
        loss, (grad_w, grad_x) = one_microbatch(per_example, given["loss_target"])
    else:
        def body(carry, xs):
            loss_sum, grad_sum = carry
            l_k, (gw_k, gx_k) = one_microbatch(xs[0], xs[1])
            with _jax.named_scope("update"):
                return (loss_sum + l_k, _jax.tree.map(_jnp.add, grad_sum, gw_k)), gx_k

        init = (_jnp.zeros((), _jnp.float32), _jax.tree.map(_jnp.zeros_like, weights))
        (loss, grad_w), grad_x = _jax.lax.scan(body, init, (per_example, given["loss_target"]))
    with _jax.named_scope("update"):
        delta_w, new_m, new_v = {}, {}, {}
        for n in TWIN_WEIGHTS:
            delta_w[n], new_m[n], new_v[n] = _adamw(weights[n], grad_w[n], given["m_" + n], given["v_" + n])
    return (loss, grad_x, *[grad_w[n] for n in TWIN_WEIGHTS], *[delta_w[n] for n in TWIN_WEIGHTS],
            *[new_m[n] for n in TWIN_WEIGHTS], *[new_v[n] for n in TWIN_WEIGHTS])
```

```python
import functools
import math

import jax
import jax.numpy as jnp
import numpy as np
from jax import lax
from jax.experimental import pallas as pl
from jax.experimental.pallas import tpu as pltpu

F32 = jnp.float32
BF16 = jnp.bfloat16
MESH = pl.DeviceIdType.MESH
ANY = pl.BlockSpec(memory_space=pl.ANY)

ATT_BLOCK = 128
ATT_GROUPS = ((128, 1), (512, 4), (2048, 16))
LN_EPS = 1e-5
GN_EPS = 1e-6
NEG_INF = -1e30
ROPE_BASE = 10000.0
ADAM_LR, ADAM_B1, ADAM_B2, ADAM_EPS, ADAM_WD, ADAM_STEP = 0.001, 0.9, 0.999, 1e-08, 0.01, 10
VMEM_LIMIT = 56 * 1024 * 1024


class _Cfg:
    def __init__(self, seq, d, ple, ahd, ahg, rh, rqk, rv, ffn, cw):
        self.seq, self.d, self.ple, self.ahd, self.ahg = seq, d, ple, ahd, ahg
        self.rh, self.rqk, self.rv, self.ffn, self.cw = rh, rqk, rv, ffn, cw
        self.aw = ahg * ahd
        self.att_w = 3 * self.aw
        self.rqk_w = rh * rqk
        self.rv_w = rh * rv
        offs = np.cumsum([0] + [self.att_w] * 3 + [self.rqk_w] * 2 + [self.rv_w] * 2 + [d] * 2)
        (self.o_qa, self.o_ka, self.o_va, self.o_qr, self.o_kr, self.o_vr, self.o_gr, self.o_ga, self.o_g2,
         self.in_w) = [int(v) for v in offs]
        self.alpha = 2.0 ** 0.25
        self.ffn_shard = -(-(ffn // 4) // 128) * 128
        self.ffn_p = 4 * self.ffn_shard
        assert self.rv_w == d and d % cw == 0
        for o in (self.o_gr, self.o_ga, self.o_g2):
            assert o % cw == 0
        assert self.o_qr % rqk == 0 and self.o_kr % rqk == 0 and self.o_vr % rv == 0 and self.o_gr % rv == 0
        assert rqk // 2 % 128 == 0 and seq % (ATT_BLOCK * 16) == 0


_FULL = _Cfg(seq=4096, d=4096, ple=256, ahd=128, ahg=8, rh=8, rqk=256, rv=512, ffn=11008, cw=1024)


def _tile(n, target, q=128):
    t = min(n, target) // q * q
    while t >= q:
        if n % t == 0:
            return t
        t -= q
    return n


def _params(sem=None):
    return pltpu.CompilerParams(dimension_semantics=sem, vmem_limit_bytes=VMEM_LIMIT)


def _sigmoid(x):
    return 1.0 / (1.0 + jnp.exp(-x))


def _mm(a, b, mode, out_dtype, name, add=None, add_scale=1.0):
    if mode == "nn":
        (m, k), (k2, n) = a.shape, b.shape
    elif mode == "nt":
        (m, k), (n, k2) = a.shape, b.shape
    else:
        (k, m), (k2, n) = a.shape, b.shape
    assert k == k2, (name, a.shape, b.shape)
    tm, tn, tk = _tile(m, 1024), _tile(n, 1024), _tile(k, 512)
    nk = k // tk
    if mode == "tn":
        a_spec = pl.BlockSpec((tk, tm), lambda i, j, kk: (kk, i))
        dims = (((0,), (0,)), ((), ()))
    else:
        a_spec = pl.BlockSpec((tm, tk), lambda i, j, kk: (i, kk))
        dims = (((1,), (1,)), ((), ())) if mode == "nt" else (((1,), (0,)), ((), ()))
    if mode == "nt":
        b_spec = pl.BlockSpec((tn, tk), lambda i, j, kk: (j, kk))
    else:
        b_spec = pl.BlockSpec((tk, tn), lambda i, j, kk: (kk, j))
    o_spec = pl.BlockSpec((tm, tn), lambda i, j, kk: (i, j))
    has_add = add is not None

    def body(*refs):
        if has_add:
            a_ref, b_ref, add_ref, o_ref, acc_ref = refs
        else:
            a_ref, b_ref, o_ref, acc_ref = refs
        kk = pl.program_id(2)

        @pl.when(kk == 0)
        def _():
            acc_ref[...] = jnp.zeros_like(acc_ref)

        acc_ref[...] += lax.dot_general(a_ref[...], b_ref[...], dims, preferred_element_type=F32)

        @pl.when(kk == nk - 1)
        def _():
            r = acc_ref[...]
            if has_add:
                r = r + add_scale * add_ref[...]
            o_ref[...] = r.astype(o_ref.dtype)

    return pl.pallas_call(
        body, name=name, grid=(m // tm, n // tn, nk),
        in_specs=[a_spec, b_spec] + ([o_spec] if has_add else []), out_specs=o_spec,
        out_shape=jax.ShapeDtypeStruct((m, n), out_dtype), scratch_shapes=[pltpu.VMEM((tm, tn), F32)],
        compiler_params=_params(("parallel", "parallel", "arbitrary")),
    )(*((a, b, add) if has_add else (a, b)))


def _seg_blocks(cfg):
    return [cfg.seq // ATT_BLOCK // dil for _, dil in ATT_GROUPS]


def _group_select(g, vals):
    out = jnp.int32(vals[-1])
    for i in range(len(vals) - 2, -1, -1):
        out = jnp.where(g == i, jnp.int32(vals[i]), out)
    return out


def _col(tile, h):
    lane = lax.broadcasted_iota(jnp.int32, tile.shape, 1)
    return jnp.sum(jnp.where(lane == h, tile, 0.0), axis=1, keepdims=True)


def _set_col(tile, h, col):
    lane = lax.broadcasted_iota(jnp.int32, tile.shape, 1)
    return jnp.where(lane == h, col, tile)


_NT = (((1,), (1,)), ((), ()))
_TN = (((0,), (0,)), ((), ()))


def _dot(a, b):
    return jnp.dot(a, b, preferred_element_type=F32)


def _dot_nt(a, b):
    return lax.dot_general(a, b, _NT, preferred_element_type=F32)


def _dot_tn(a, b):
    return lax.dot_general(a, b, _TN, preferred_element_type=F32)


def _attn_fwd(cfg, q3, k3, v3):
    nb = cfg.seq // ATT_BLOCK
    segs = _seg_blocks(cfg)
    scale = cfg.ahd ** -0.5
    blk = ATT_BLOCK

    def body(q_ref, kp_ref, kc_ref, vp_ref, vc_ref, o_ref, lse_ref):
        g, b = pl.program_id(0), pl.program_id(1)
        has_prev = (b & (_group_select(g, segs) - 1)) != 0
        qi = lax.broadcasted_iota(jnp.int32, (blk, blk), 0)
        kj = lax.broadcasted_iota(jnp.int32, (blk, blk), 1)
        valid_c = kj <= qi
        valid_p = jnp.logical_and(kj >= qi, has_prev)
        lse = jnp.zeros((blk, cfg.ahg), F32)
        for h in range(cfg.ahg):
            hs = slice(h * cfg.ahd, (h + 1) * cfg.ahd)
            q = q_ref[:, hs]
            s_c = jnp.where(valid_c, _dot_nt(q, kc_ref[:, hs]) * scale, NEG_INF)
            s_p = jnp.where(valid_p, _dot_nt(q, kp_ref[:, hs]) * scale, NEG_INF)
            m = jnp.maximum(jnp.max(s_c, axis=1, keepdims=True), jnp.max(s_p, axis=1, keepdims=True))
            p_c, p_p = jnp.exp(s_c - m), jnp.exp(s_p - m)
            den = jnp.sum(p_c, axis=1, keepdims=True) + jnp.sum(p_p, axis=1, keepdims=True)
            o = _dot(p_c.astype(BF16), vc_ref[:, hs]) + _dot(p_p.astype(BF16), vp_ref[:, hs])
            o_ref[:, hs] = (o / den).astype(o_ref.dtype)
            lse = _set_col(lse, h, m + jnp.log(den))
        lse_ref[...] = lse

    cur = pl.BlockSpec((None, blk, cfg.aw), lambda g, b: (g, b, 0))
    prev = pl.BlockSpec((None, blk, cfg.aw), lambda g, b: (g, jnp.maximum(b - 1, 0), 0))
    return pl.pallas_call(
        body, name="attn_fwd", grid=(3, nb), in_specs=[cur, prev, cur, prev, cur],
        out_specs=[cur, pl.BlockSpec((None, blk, cfg.ahg), lambda g, b: (g, b, 0))],
        out_shape=[jax.ShapeDtypeStruct((3, cfg.seq, cfg.aw), BF16), jax.ShapeDtypeStruct((3, cfg.seq, cfg.ahg), F32)],
        compiler_params=_params(("parallel", "parallel")),
    )(q3, k3, k3, v3, v3)


def _attn_combine(cfg, o3, lse3):
    tm = 256

    def body(o_ref, lse_ref, oa_ref, l_ref):
        l0, l1, l2 = lse_ref[0], lse_ref[1], lse_ref[2]
        m = jnp.maximum(jnp.maximum(l0, l1), l2)
        big = m + jnp.log(jnp.exp(l0 - m) + jnp.exp(l1 - m) + jnp.exp(l2 - m))
        ws = [jnp.exp(l0 - big), jnp.exp(l1 - big), jnp.exp(l2 - big)]
        for h in range(cfg.ahg):
            hs = slice(h * cfg.ahd, (h + 1) * cfg.ahd)
            acc = _col(ws[0], h) * o_ref[0, :, hs].astype(F32)
            acc += _col(ws[1], h) * o_ref[1, :, hs].astype(F32)
            acc += _col(ws[2], h) * o_ref[2, :, hs].astype(F32)
            oa_ref[:, hs] = acc.astype(oa_ref.dtype)
        l_ref[...] = big

    return pl.pallas_call(
        body, name="attn_combine", grid=(cfg.seq // tm,),
        in_specs=[pl.BlockSpec((3, tm, cfg.aw), lambda i: (0, i, 0)), pl.BlockSpec((3, tm, cfg.ahg), lambda i: (0, i, 0))],
        out_specs=[pl.BlockSpec((tm, cfg.aw), lambda i: (i, 0)), pl.BlockSpec((tm, cfg.ahg), lambda i: (i, 0))],
        out_shape=[jax.ShapeDtypeStruct((cfg.seq, cfg.aw), BF16), jax.ShapeDtypeStruct((cfg.seq, cfg.ahg), F32)],
        compiler_params=_params(("parallel",)),
    )(o3, lse3)


def _attn_delta(cfg, do, o):
    tm = 256

    def body(do_ref, o_ref, d_ref):
        out = jnp.zeros((tm, cfg.ahg), F32)
        for h in range(cfg.ahg):
            hs = slice(h * cfg.ahd, (h + 1) * cfg.ahd)
            prod = do_ref[:, hs].astype(F32) * o_ref[:, hs].astype(F32)
            out = _set_col(out, h, jnp.sum(prod, axis=1, keepdims=True))
        d_ref[...] = out

    row = pl.BlockSpec((tm, cfg.aw), lambda i: (i, 0))
    return pl.pallas_call(
        body, name="attn_delta", grid=(cfg.seq // tm,), in_specs=[row, row],
        out_specs=pl.BlockSpec((tm, cfg.ahg), lambda i: (i, 0)),
        out_shape=jax.ShapeDtypeStruct((cfg.seq, cfg.ahg), F32), compiler_params=_params(("parallel",)),
    )(do, o)


def _attn_bwd(cfg, q3, k3, v3, do3, l3, d3):
    nb = cfg.seq // ATT_BLOCK
    segs = _seg_blocks(cfg)
    scale = cfg.ahd ** -0.5
    blk = ATT_BLOCK

    def body(q_ref, qn_ref, do_ref, don_ref, l_ref, ln_ref, d_ref, dn_ref, kp_ref, kc_ref, vp_ref, vc_ref,
             dq_ref, dk_ref, dv_ref):
        g, b = pl.program_id(0), pl.program_id(1)
        seg_mask = _group_select(g, segs) - 1
        has_prev = (b & seg_mask) != 0
        has_next = jnp.logical_and(b + 1 < nb, ((b + 1) & seg_mask) != 0)
        qi = lax.broadcasted_iota(jnp.int32, (blk, blk), 0)
        kj = lax.broadcasted_iota(jnp.int32, (blk, blk), 1)
        valid_c = kj <= qi
        valid_p = jnp.logical_and(kj >= qi, has_prev)
        valid_n = jnp.logical_and(kj >= qi, has_next)
        lse, lse_n, dlt, dlt_n = l_ref[...], ln_ref[...], d_ref[...], dn_ref[...]
        for h in range(cfg.ahg):
            hs = slice(h * cfg.ahd, (h + 1) * cfg.ahd)
            q, qn, do, don = q_ref[:, hs], qn_ref[:, hs], do_ref[:, hs], don_ref[:, hs]
            kc, kp, vc, vp = kc_ref[:, hs], kp_ref[:, hs], vc_ref[:, hs], vp_ref[:, hs]
            lh, lnh, dh, dnh = _col(lse, h), _col(lse_n, h), _col(dlt, h), _col(dlt_n, h)
            p_c = jnp.where(valid_c, jnp.exp(_dot_nt(q, kc) * scale - lh), 0.0)
            p_p = jnp.where(valid_p, jnp.exp(_dot_nt(q, kp) * scale - lh), 0.0)
            ds_c = (p_c * (_dot_nt(do, vc) - dh)).astype(BF16)
            ds_p = (p_p * (_dot_nt(do, vp) - dh)).astype(BF16)
            dq_ref[:, hs] = ((_dot(ds_c, kc) + _dot(ds_p, kp)) * scale).astype(dq_ref.dtype)
            p_n = jnp.where(valid_n, jnp.exp(_dot_nt(qn, kc) * scale - lnh), 0.0)
            ds_n = (p_n * (_dot_nt(don, vc) - dnh)).astype(BF16)
            dk = _dot_tn(ds_c, q) + _dot_tn(ds_n, qn)
            dv = _dot_tn(p_c.astype(BF16), do) + _dot_tn(p_n.astype(BF16), don)
            dk_ref[:, hs] = (dk * scale).astype(dk_ref.dtype)
            dv_ref[:, hs] = dv.astype(dv_ref.dtype)

    def spec(width, shift):
        if shift == 0:
            return pl.BlockSpec((None, blk, width), lambda g, b: (g, b, 0))
        if shift > 0:
            return pl.BlockSpec((None, blk, width), lambda g, b: (g, jnp.minimum(b + 1, nb - 1), 0))
        return pl.BlockSpec((None, blk, width), lambda g, b: (g, jnp.maximum(b - 1, 0), 0))

    w, hw = cfg.aw, cfg.ahg
    out = jax.ShapeDtypeStruct((3, cfg.seq, w), BF16)
    return pl.pallas_call(
        body, name="attn_bwd", grid=(3, nb),
        in_specs=[spec(w, 0), spec(w, 1), spec(w, 0), spec(w, 1), spec(hw, 0), spec(hw, 1), spec(hw, 0), spec(hw, 1),
                  spec(w, -1), spec(w, 0), spec(w, -1), spec(w, 0)],
        out_specs=[spec(w, 0)] * 3, out_shape=[out, out, out], compiler_params=_params(("parallel", "parallel")),
    )(q3, q3, do3, do3, l3, l3, d3, d3, k3, k3, v3, v3)


def _deinterleave(t, dil):
    s, w = t.shape
    return t if dil == 1 else t.reshape(s // dil, dil, w).transpose(1, 0, 2).reshape(s, w)


def _interleave(t, dil):
    s, w = t.shape
    return t if dil == 1 else t.reshape(dil, s // dil, w).transpose(1, 0, 2).reshape(s, w)


def _to_groups(cfg, t):
    return jnp.stack([_deinterleave(t[:, g * cfg.aw:(g + 1) * cfg.aw], dil) for g, (_, dil) in enumerate(ATT_GROUPS)])


def _from_groups(t3):
    return [_interleave(t3[g], dil) for g, (_, dil) in enumerate(ATT_GROUPS)]


def _rope_tables(cfg):
    half = cfg.rqk // 2
    pos = jnp.arange(cfg.seq, dtype=F32)
    inv_freq = ROPE_BASE ** (-jnp.arange(half, dtype=F32) / half)
    ang = pos[:, None] * inv_freq[None, :]
    return jnp.cos(ang), jnp.sin(ang)


def _log_gamma(cfg):
    lg = jnp.log(1.0 - 2.0 ** (-5.0 - jnp.arange(cfg.rh, dtype=F32)))
    return jnp.broadcast_to(lg[:, None, None], (cfg.rh, 1, max(cfg.rqk, cfg.rv)))


def _rot(t, c, s, half):
    t1, t2 = t[:, :half], t[:, half:]
    return jnp.concatenate([t1 * c - t2 * s, t1 * s + t2 * c], axis=1)


def _unrot(d, c, s, half):
    d1, d2 = d[:, :half], d[:, half:]
    return jnp.concatenate([d1 * c + d2 * s, d2 * c - d1 * s], axis=1)


def _decays(lg_ref, cfg):
    blk = ATT_BLOCK
    lg_v, lg_k = lg_ref[:, :cfg.rv], lg_ref[:, :cfg.rqk]
    qi = lax.broadcasted_iota(jnp.int32, (blk, blk), 0)
    kj = lax.broadcasted_iota(jnp.int32, (blk, blk), 1)
    diff = (qi - kj).astype(F32)
    intra = jnp.where(diff >= 0, jnp.exp(jnp.maximum(diff, 0.0) * lg_ref[:, :blk]), 0.0)
    idx_v = lax.broadcasted_iota(jnp.int32, (blk, cfg.rv), 0).astype(F32)
    idx_k = lax.broadcasted_iota(jnp.int32, (blk, cfg.rqk), 0).astype(F32)
    cross = jnp.exp((idx_v + 1.0) * lg_v)
    state = jnp.exp((blk - 1.0 - idx_k) * lg_k)
    chunk = jnp.exp(float(blk) * lg_v)
    return intra, cross, state, chunk


def _ret_fwd(cfg, proj, cos, sin, lg):
    blk, nc, half = ATT_BLOCK, cfg.seq // ATT_BLOCK, cfg.rqk // 2
    kscale = cfg.rqk ** -0.5

    def body(q_ref, k_ref, v_ref, cos_ref, sin_ref, lg_ref, rn_ref, rstd_ref, st_ref, state):
        n = pl.program_id(1)

        @pl.when(n == 0)
        def _():
            state[...] = jnp.zeros_like(state)

        intra, cross_d, state_d, chunk_d = _decays(lg_ref, cfg)
        c, s = cos_ref[...], sin_ref[...]
        qb = _rot(q_ref[...].astype(F32), c, s, half).astype(BF16)
        kf = _rot(k_ref[...].astype(F32), c, s, half) * kscale
        vb = v_ref[...]
        prev = state[...]
        st_ref[...] = prev.astype(BF16)
        att = _dot_nt(qb, kf.astype(BF16)) * intra
        out = _dot(att.astype(BF16), vb) + _dot(qb, prev.astype(BF16)) * cross_d
        state[...] = chunk_d * prev + _dot_tn((kf * state_d).astype(BF16), vb)
        mu = jnp.mean(out, axis=1, keepdims=True)
        cen = out - mu
        rstd = lax.rsqrt(jnp.mean(cen * cen, axis=1, keepdims=True) + GN_EPS)
        rn_ref[...] = (cen * rstd).astype(rn_ref.dtype)
        rstd_ref[...] = rstd

    oq, ok, ov = cfg.o_qr // cfg.rqk, cfg.o_kr // cfg.rqk, cfg.o_vr // cfg.rv
    tab = pl.BlockSpec((blk, half), lambda h, n: (n, 0))
    return pl.pallas_call(
        body, name="ret_fwd", grid=(cfg.rh, nc),
        in_specs=[pl.BlockSpec((blk, cfg.rqk), lambda h, n: (n, oq + h)), pl.BlockSpec((blk, cfg.rqk), lambda h, n: (n, ok + h)),
                  pl.BlockSpec((blk, cfg.rv), lambda h, n: (n, ov + h)), tab, tab,
                  pl.BlockSpec((None, 1, lg.shape[2]), lambda h, n: (h, 0, 0))],
        out_specs=[pl.BlockSpec((blk, cfg.rv), lambda h, n: (n, h)), pl.BlockSpec((None, blk, 1), lambda h, n: (h, n, 0)),
                   pl.BlockSpec((None, None, cfg.rqk, cfg.rv), lambda h, n: (h, n, 0, 0))],
        out_shape=[jax.ShapeDtypeStruct((cfg.seq, cfg.rv_w), BF16), jax.ShapeDtypeStruct((cfg.rh, cfg.seq, 1), F32),
                   jax.ShapeDtypeStruct((cfg.rh, nc, cfg.rqk, cfg.rv), BF16)],
        scratch_shapes=[pltpu.VMEM((cfg.rqk, cfg.rv), F32)], compiler_params=_params(("parallel", "arbitrary")),
    )(proj, proj, proj, cos, sin, lg)


def _ret_bwd(cfg, proj, d_r, states, cos, sin, lg):
    blk, nc, half = ATT_BLOCK, cfg.seq // ATT_BLOCK, cfg.rqk // 2
    kscale = cfg.rqk ** -0.5

    def body(q_ref, k_ref, v_ref, do_ref, st_ref, cos_ref, sin_ref, lg_ref, dq_ref, dk_ref, dv_ref, dstate):
        n = pl.program_id(1)

        @pl.when(n == 0)
        def _():
            dstate[...] = jnp.zeros_like(dstate)

        intra, cross_d, state_d, chunk_d = _decays(lg_ref, cfg)
        c, s = cos_ref[...], sin_ref[...]
        qb = _rot(q_ref[...].astype(F32), c, s, half).astype(BF16)
        kf = _rot(k_ref[...].astype(F32), c, s, half) * kscale
        kb, ksb = kf.astype(BF16), (kf * state_d).astype(BF16)
        vb, prev = v_ref[...], st_ref[...]
        do = do_ref[...].astype(F32)
        dob, docb = do.astype(BF16), (do * cross_d).astype(BF16)
        dsb = dstate[...].astype(BF16)
        att = (_dot_nt(qb, kb) * intra).astype(BF16)
        datt = (_dot_nt(dob, vb) * intra).astype(BF16)
        d_q = _dot(datt, kb) + _dot_nt(docb, prev)
        d_k = _dot_tn(datt, qb) + _dot_nt(vb, dsb) * state_d
        d_v = _dot_tn(att, dob) + _dot(ksb, dsb)
        dstate[...] = chunk_d * dstate[...] + _dot_tn(qb, docb)
        dq_ref[...] = _unrot(d_q, c, s, half).astype(dq_ref.dtype)
        dk_ref[...] = _unrot(d_k * kscale, c, s, half).astype(dk_ref.dtype)
        dv_ref[...] = d_v.astype(dv_ref.dtype)

    oq, ok, ov = cfg.o_qr // cfg.rqk, cfg.o_kr // cfg.rqk, cfg.o_vr // cfg.rv
    last = nc - 1
    tab = pl.BlockSpec((blk, half), lambda h, n: (last - n, 0))
    qk_out = pl.BlockSpec((blk, cfg.rqk), lambda h, n: (last - n, h))
    v_out = pl.BlockSpec((blk, cfg.rv), lambda h, n: (last - n, h))
    return pl.pallas_call(
        body, name="ret_bwd", grid=(cfg.rh, nc),
        in_specs=[pl.BlockSpec((blk, cfg.rqk), lambda h, n: (last - n, oq + h)),
                  pl.BlockSpec((blk, cfg.rqk), lambda h, n: (last - n, ok + h)),
                  pl.BlockSpec((blk, cfg.rv), lambda h, n: (last - n, ov + h)), v_out,
                  pl.BlockSpec((None, None, cfg.rqk, cfg.rv), lambda h, n: (h, last - n, 0, 0)), tab, tab,
                  pl.BlockSpec((None, 1, lg.shape[2]), lambda h, n: (h, 0, 0))],
        out_specs=[qk_out, qk_out, v_out],
        out_shape=[jax.ShapeDtypeStruct((cfg.seq, cfg.rqk_w), BF16), jax.ShapeDtypeStruct((cfg.seq, cfg.rqk_w), BF16),
                   jax.ShapeDtypeStruct((cfg.seq, cfg.rv_w), BF16)],
        scratch_shapes=[pltpu.VMEM((cfg.rqk, cfg.rv), F32)], compiler_params=_params(("parallel", "arbitrary")),
    )(proj, proj, proj, d_r, states, cos, sin, lg)


def _ret_gate(cfg, proj, rn, gn_g):
    tm, cw = 256, cfg.cw
    og = cfg.o_gr // cw

    def body(g_ref, rn_ref, w_ref, o_ref):
        g = g_ref[...].astype(F32)
        o_ref[...] = (g * _sigmoid(g) * (rn_ref[...].astype(F32) * w_ref[...])).astype(o_ref.dtype)

    blk = pl.BlockSpec((tm, cw), lambda i, j: (i, j))
    return pl.pallas_call(
        body, name="ret_gate", grid=(cfg.seq // tm, cfg.rv_w // cw),
        in_specs=[pl.BlockSpec((tm, cw), lambda i, j: (i, og + j)), blk, pl.BlockSpec((1, cw), lambda i, j: (0, j))],
        out_specs=blk, out_shape=jax.ShapeDtypeStruct((cfg.seq, cfg.rv_w), BF16),
        compiler_params=_params(("parallel", "parallel")),
    )(proj, rn, gn_g)


def _mix(cfg, proj, y_att, y_ret):
    tm, cw = 256, cfg.cw
    oa, o2 = cfg.o_ga // cw, cfg.o_g2 // cw

    def body(ga_ref, g2_ref, ya_ref, yr_ref, o_ref):
        r = _sigmoid(ga_ref[...].astype(F32)) * ya_ref[...].astype(F32)
        r += _sigmoid(g2_ref[...].astype(F32)) * yr_ref[...].astype(F32)
        o_ref[...] = r.astype(o_ref.dtype)

    blk = pl.BlockSpec((tm, cw), lambda i, j: (i, j))
    return pl.pallas_call(
        body, name="mix", grid=(cfg.seq // tm, cfg.d // cw),
        in_specs=[pl.BlockSpec((tm, cw), lambda i, j: (i, oa + j)), pl.BlockSpec((tm, cw), lambda i, j: (i, o2 + j)), blk, blk],
        out_specs=blk, out_shape=jax.ShapeDtypeStruct((cfg.seq, cfg.d), BF16),
        compiler_params=_params(("parallel", "parallel")),
    )(proj, proj, y_att, y_ret)


def _mix_bwd(cfg, proj, y_att, y_ret, d_mixed):
    tm, cw = 256, cfg.cw
    oa, o2 = cfg.o_ga // cw, cfg.o_g2 // cw

    def body(ga_ref, g2_ref, ya_ref, yr_ref, dm_ref, dya_ref, dyr_ref, dga_ref, dg2_ref):
        dm = dm_ref[...].astype(F32)
        sa, s2 = _sigmoid(ga_ref[...].astype(F32)), _sigmoid(g2_ref[...].astype(F32))
        dya_ref[...] = (dm * sa).astype(dya_ref.dtype)
        dyr_ref[...] = (dm * s2).astype(dyr_ref.dtype)
        dga_ref[...] = (dm * ya_ref[...].astype(F32) * sa * (1.0 - sa)).astype(dga_ref.dtype)
        dg2_ref[...] = (dm * yr_ref[...].astype(F32) * s2 * (1.0 - s2)).astype(dg2_ref.dtype)

    blk = pl.BlockSpec((tm, cw), lambda i, j: (i, j))
    out = jax.ShapeDtypeStruct((cfg.seq, cfg.d), BF16)
    return pl.pallas_call(
        body, name="mix_bwd", grid=(cfg.seq // tm, cfg.d // cw),
        in_specs=[pl.BlockSpec((tm, cw), lambda i, j: (i, oa + j)), pl.BlockSpec((tm, cw), lambda i, j: (i, o2 + j)), blk, blk, blk],
        out_specs=[blk] * 4, out_shape=[out] * 4, compiler_params=_params(("parallel", "parallel")),
    )(proj, proj, y_att, y_ret, d_mixed)


def _ret_gate_bwd(cfg, proj, rn, gn_g, rstd, d_rg):
    tm, rv = 256, cfg.rv
    og = cfg.o_gr // rv

    def body(g_ref, rn_ref, w_ref, rstd_ref, drg_ref, dg_ref, dr_ref, gw_ref):
        i = pl.program_id(1)
        g, rn, w = g_ref[...].astype(F32), rn_ref[...].astype(F32), w_ref[...]
        drg = drg_ref[...].astype(F32)
        sg = _sigmoid(g)
        silu = g * sg
        dg_ref[...] = (drg * (rn * w) * (sg * (1.0 + g * (1.0 - sg)))).astype(dg_ref.dtype)
        drn = drg * silu * w
        part = jnp.sum(drg * silu * rn, axis=0, keepdims=True)

        @pl.when(i == 0)
        def _():
            gw_ref[...] = part

        @pl.when(i > 0)
        def _():
            gw_ref[...] += part

        m1 = jnp.mean(drn, axis=1, keepdims=True)
        m2 = jnp.mean(drn * rn, axis=1, keepdims=True)
        dr_ref[...] = (rstd_ref[...] * (drn - m1 - rn * m2)).astype(dr_ref.dtype)

    blk = pl.BlockSpec((tm, rv), lambda h, i: (i, h))
    vec = pl.BlockSpec((1, rv), lambda h, i: (0, h))
    out = jax.ShapeDtypeStruct((cfg.seq, cfg.rv_w), BF16)
    return pl.pallas_call(
        body, name="ret_gate_bwd", grid=(cfg.rh, cfg.seq // tm),
        in_specs=[pl.BlockSpec((tm, rv), lambda h, i: (i, og + h)), blk, vec, pl.BlockSpec((None, tm, 1), lambda h, i: (h, i, 0)), blk],
        out_specs=[blk, blk, vec], out_shape=[out, out, jax.ShapeDtypeStruct((1, cfg.rv_w), F32)],
        compiler_params=_params(("parallel", "arbitrary")),
    )(proj, rn, gn_g, rstd, d_rg)


def _ln1(cfg, x, mo, g, b):
    tm, d = 128, cfg.d

    def body(x_ref, mo_ref, g_ref, b_ref, xh_ref, h_ref, rstd_ref):
        z = cfg.alpha * x_ref[...] + mo_ref[...]
        cen = z - jnp.mean(z, axis=1, keepdims=True)
        rstd = lax.rsqrt(jnp.mean(cen * cen, axis=1, keepdims=True) + LN_EPS)
        xh = cen * rstd
        xh_ref[...] = xh
        h_ref[...] = (xh * g_ref[...] + b_ref[...]).astype(h_ref.dtype)
        rstd_ref[...] = rstd

    row = pl.BlockSpec((tm, d), lambda i: (i, 0))
    vec = pl.BlockSpec((1, d), lambda i: (0, 0))
    col = pl.BlockSpec((tm, 1), lambda i: (i, 0))
    return pl.pallas_call(
        body, name="ln1", grid=(cfg.seq // tm,), in_specs=[row, row, vec, vec], out_specs=[row, row, col],
        out_shape=[jax.ShapeDtypeStruct((cfg.seq, d), F32), jax.ShapeDtypeStruct((cfg.seq, d), BF16),
                   jax.ShapeDtypeStruct((cfg.seq, 1), F32)],
        compiler_params=_params(("parallel",)),
    )(x, mo, g, b)


def _ffn_act(cfg, u, t):
    tm, cf = 256, _tile(cfg.ffn_p, 2048)

    def body(u_ref, t_ref, a_ref):
        uu = u_ref[...].astype(F32)
        a_ref[...] = (uu * _sigmoid(uu) * t_ref[...].astype(F32)).astype(a_ref.dtype)

    blk = pl.BlockSpec((tm, cf), lambda i, j: (i, j))
    return pl.pallas_call(
        body, name="ffn_act", grid=(cfg.seq // tm, cfg.ffn_p // cf), in_specs=[blk, blk], out_specs=blk,
        out_shape=jax.ShapeDtypeStruct((cfg.seq, cfg.ffn_p), BF16), compiler_params=_params(("parallel", "parallel")),
    )(u, t)


def _ffn_act_bwd(cfg, u, t, d_a):
    tm, cf = 256, _tile(cfg.ffn_p, 2048)

    def body(u_ref, t_ref, da_ref, du_ref, dt_ref):
        uu, da = u_ref[...].astype(F32), da_ref[...].astype(F32)
        sg = _sigmoid(uu)
        du_ref[...] = (da * t_ref[...].astype(F32) * (sg * (1.0 + uu * (1.0 - sg)))).astype(du_ref.dtype)
        dt_ref[...] = (da * uu * sg).astype(dt_ref.dtype)

    blk = pl.BlockSpec((tm, cf), lambda i, j: (i, j))
    out = jax.ShapeDtypeStruct((cfg.seq, cfg.ffn_p), BF16)
    return pl.pallas_call(
        body, name="ffn_act_bwd", grid=(cfg.seq // tm, cfg.ffn_p // cf), in_specs=[blk, blk, blk], out_specs=[blk, blk],
        out_shape=[out, out], compiler_params=_params(("parallel", "parallel")),
    )(u, t, d_a)


def _head(cfg, xhat1, ffn, hp, pu, tgt, g1, b1, g2, b2):
    tm, d = 64, cfg.d

    def body(xh_ref, ffn_ref, hp_ref, pu_ref, tgt_ref, g1_ref, b1_ref, g2_ref, b2_ref,
             loss_ref, dzf_ref, dzb_ref, dpg_ref, dpu_ref, gg_ref, gb_ref):
        i = pl.program_id(0)
        h1 = xh_ref[...] * g1_ref[...] + b1_ref[...]
        sg, pu = _sigmoid(hp_ref[...].astype(F32)), pu_ref[...].astype(F32)
        z = cfg.alpha * h1 + ffn_ref[...] + sg * pu
        cen = z - jnp.mean(z, axis=1, keepdims=True)
        rstd = lax.rsqrt(jnp.mean(cen * cen, axis=1, keepdims=True) + LN_EPS)
        xh2 = cen * rstd
        err = xh2 * g2_ref[...] + b2_ref[...] - tgt_ref[...]
        dy = err * (1.0 / d)
        part_l = jnp.sum(jnp.sum(err * err, axis=1, keepdims=True), axis=0, keepdims=True) * (0.5 / d)
        part_g = jnp.sum(dy * xh2, axis=0, keepdims=True)
        part_b = jnp.sum(dy, axis=0, keepdims=True)

        @pl.when(i == 0)
        def _():
            loss_ref[...] = jnp.zeros_like(loss_ref)
            gg_ref[...] = jnp.zeros_like(gg_ref)
            gb_ref[...] = jnp.zeros_like(gb_ref)

        loss_ref[...] += jnp.broadcast_to(part_l, loss_ref.shape)
        gg_ref[...] += part_g
        gb_ref[...] += part_b
        dxh = dy * g2_ref[...]
        m1 = jnp.mean(dxh, axis=1, keepdims=True)
        m2 = jnp.mean(dxh * xh2, axis=1, keepdims=True)
        dz = rstd * (dxh - m1 - xh2 * m2)
        dzf_ref[...] = dz
        dzb_ref[...] = dz.astype(dzb_ref.dtype)
        dpg_ref[...] = (dz * pu * sg * (1.0 - sg)).astype(dpg_ref.dtype)
        dpu_ref[...] = (dz * sg).astype(dpu_ref.dtype)

    row = pl.BlockSpec((tm, d), lambda i: (i, 0))
    vec = pl.BlockSpec((1, d), lambda i: (0, 0))
    bf = jax.ShapeDtypeStruct((cfg.seq, d), BF16)
    vec_out = jax.ShapeDtypeStruct((1, d), F32)
    return pl.pallas_call(
        body, name="head", grid=(cfg.seq // tm,), in_specs=[row] * 5 + [vec] * 4,
        out_specs=[pl.BlockSpec((1, 128), lambda i: (0, 0)), row, row, row, row, vec, vec],
        out_shape=[jax.ShapeDtypeStruct((1, 128), F32), jax.ShapeDtypeStruct((cfg.seq, d), F32), bf, bf, bf, vec_out, vec_out],
        compiler_params=_params(("arbitrary",)),
    )(xhat1, ffn, hp, pu, tgt, g1, b1, g2, b2)


def _ln1_bwd(cfg, dh_mm, dz2, xhat1, rstd1, g1):
    tm, d = 128, cfg.d

    def body(dh_ref, dz_ref, xh_ref, rstd_ref, g_ref, dzf_ref, dzb_ref, gg_ref, gb_ref):
        i = pl.program_id(0)
        dh = cfg.alpha * dz_ref[...] + dh_ref[...]
        xh = xh_ref[...]

        @pl.when(i == 0)
        def _():
            gg_ref[...] = jnp.zeros_like(gg_ref)
            gb_ref[...] = jnp.zeros_like(gb_ref)

        gg_ref[...] += jnp.sum(dh * xh, axis=0, keepdims=True)
        gb_ref[...] += jnp.sum(dh, axis=0, keepdims=True)
        dxh = dh * g_ref[...]
        m1 = jnp.mean(dxh, axis=1, keepdims=True)
        m2 = jnp.mean(dxh * xh, axis=1, keepdims=True)
        dz = rstd_ref[...] * (dxh - m1 - xh * m2)
        dzf_ref[...] = dz
        dzb_ref[...] = dz.astype(dzb_ref.dtype)

    row = pl.BlockSpec((tm, d), lambda i: (i, 0))
    vec = pl.BlockSpec((1, d), lambda i: (0, 0))
    vec_out = jax.ShapeDtypeStruct((1, d), F32)
    return pl.pallas_call(
        body, name="ln1_bwd", grid=(cfg.seq // tm,),
        in_specs=[row, row, row, pl.BlockSpec((tm, 1), lambda i: (i, 0)), vec], out_specs=[row, row, vec, vec],
        out_shape=[jax.ShapeDtypeStruct((cfg.seq, d), F32), jax.ShapeDtypeStruct((cfg.seq, d), BF16), vec_out, vec_out],
        compiler_params=_params(("arbitrary",)),
    )(dh_mm, dz2, xhat1, rstd1, g1)


def _add_n(parts, out_dtype, name):
    r, c = parts[0].shape
    tr, tc = _tile(r, 256, 8), _tile(c, 2048)
    n = len(parts)

    def body(*refs):
        acc = refs[0][...].astype(F32)
        for ref in refs[1:n]:
            acc = acc + ref[...].astype(F32)
        refs[n][...] = acc.astype(refs[n].dtype)

    blk = pl.BlockSpec((tr, tc), lambda i, j: (i, j))
    return pl.pallas_call(
        body, name=name, grid=(r // tr, c // tc), in_specs=[blk] * n, out_specs=blk,
        out_shape=jax.ShapeDtypeStruct((r, c), out_dtype), compiler_params=_params(("parallel", "parallel")),
    )(*parts)


def _adamw(w, g, m, v, name):
    r, c = w.shape
    tr, tc = _tile(r, 256, 8), _tile(c, 2048)
    bc1, bc2 = 1.0 - ADAM_B1 ** ADAM_STEP, 1.0 - ADAM_B2 ** ADAM_STEP

    def body(w_ref, g_ref, m_ref, v_ref, d_ref, mo_ref, vo_ref):
        gg = g_ref[...]
        mn = ADAM_B1 * m_ref[...] + (1.0 - ADAM_B1) * gg
        vn = ADAM_B2 * v_ref[...] + (1.0 - ADAM_B2) * (gg * gg)
        d_ref[...] = -ADAM_LR * ((mn / bc1) / (jnp.sqrt(vn / bc2) + ADAM_EPS) + ADAM_WD * w_ref[...])
        mo_ref[...] = mn
        vo_ref[...] = vn

    blk = pl.BlockSpec((tr, tc), lambda i, j: (i, j))
    out = jax.ShapeDtypeStruct((r, c), F32)
    return pl.pallas_call(
        body, name=name, grid=(r // tr, c // tc), in_specs=[blk] * 4, out_specs=[blk] * 3, out_shape=[out] * 3,
        compiler_params=_params(("parallel", "parallel")),
    )(w, g, m, v)


def _place():
    x, y, c = lax.axis_index("x"), lax.axis_index("y"), lax.axis_index("c")
    others = [(1 - x, y), (x, 1 - y), (1 - x, 1 - y)]
    return x, y, c, others


def _slab(ref, ax, chip, half, n_shard, n_half):
    if ax == 0:
        return ref.at[pl.ds(chip * n_shard + half * n_half, n_half), :]
    return ref.at[pl.ds(half * n_half, n_half), pl.ds(chip * n_shard, n_shard)]


def _all_gather(ws, ax, name):
    r, cdim = ws.shape
    full_shape = (4 * r, cdim) if ax == 0 else (r, 4 * cdim)
    n_shard = r if ax == 0 else cdim
    n_half = r // 2

    def body(ws_ref, full_ref, send_sems, recv_sems, local_sem):
        x, y, c, others = _place()
        me = 2 * x + y
        sibling = (x, y, 1 - c)

        def region(chip, half):
            return _slab(full_ref, ax, chip, half, n_shard, n_half)

        def copy(k, chip, half, to, src=None):
            dst = region(chip, half)
            return pltpu.make_async_remote_copy(src_ref=dst if src is None else src, dst_ref=dst, send_sem=send_sems.at[k],
                                                recv_sem=recv_sems.at[k], device_id=to, device_id_type=MESH)

        if ax == 0:
            own_dst = full_ref.at[pl.ds(me * r, r), :]
        else:
            own_dst = full_ref.at[:, pl.ds(me * cdim, cdim)]
        mine = pltpu.make_async_copy(ws_ref, own_dst, local_sem)
        mine.start()
        my_half = ws_ref.at[pl.ds(c * n_half, n_half), :]
        first = [copy(j, me, c, (*chip, c), src=my_half) for j, chip in enumerate(others)]
        for cp in first:
            cp.start()
        passed = [copy(3 + j, 2 * chip[0] + chip[1], c, sibling) for j, chip in enumerate(others)]
        for j, chip in enumerate(others):
            copy(j, 2 * chip[0] + chip[1], c, (x, y, c)).wait_recv()
            passed[j].start()
        for j, chip in enumerate(others):
            copy(3 + j, 2 * chip[0] + chip[1], 1 - c, (x, y, c)).wait_recv()
        for cp in first + passed:
            cp.wait_send()
        mine.wait()

    return pl.pallas_call(
        body, name=name, in_specs=[ANY], out_specs=ANY, out_shape=jax.ShapeDtypeStruct(full_shape, ws.dtype),
        scratch_shapes=[pltpu.SemaphoreType.DMA((6,)), pltpu.SemaphoreType.DMA((6,)), pltpu.SemaphoreType.DMA],
    )(ws)


def _pair_exchange(g, ax, name):
    rf, cf = g.shape
    n_shard = (rf if ax == 0 else cf) // 4
    n_half = (n_shard if ax == 0 else rf) // 2
    ncol = cf if ax == 0 else n_shard

    def body(g_ref, own_ref, recv_ref, send_sems, recv_sems, local_sems):
        x, y, c, _ = _place()
        sibling = (x, y, 1 - c)
        sends, locals_ = [], []
        for j in range(4):
            locals_.append(pltpu.make_async_copy(_slab(g_ref, ax, j, c, n_shard, n_half), own_ref.at[j], local_sems.at[j]))
            sends.append(pltpu.make_async_remote_copy(
                src_ref=_slab(g_ref, ax, j, 1 - c, n_shard, n_half), dst_ref=recv_ref.at[j], send_sem=send_sems.at[j],
                recv_sem=recv_sems.at[j], device_id=sibling, device_id_type=MESH))
        for cp in sends + locals_:
            cp.start()
        for cp in sends:
            cp.wait()
        for cp in locals_:
            cp.wait()

    shape = jax.ShapeDtypeStruct((4, n_half, ncol), g.dtype)
    return pl.pallas_call(
        body, name=name, in_specs=[ANY], out_specs=[ANY, ANY], out_shape=[shape, shape],
        scratch_shapes=[pltpu.SemaphoreType.DMA((4,)), pltpu.SemaphoreType.DMA((4,)), pltpu.SemaphoreType.DMA((4,))],
    )(g)


def _chip_scatter(pair, name):
    _, n_half, ncol = pair.shape

    def body(p_ref, own_ref, recv_ref, send_sems, recv_sems, local_sem):
        x, y, c, others = _place()
        me = 2 * x + y
        mine = pltpu.make_async_copy(p_ref.at[me], own_ref, local_sem)
        mine.start()
        sends = [pltpu.make_async_remote_copy(
            src_ref=p_ref.at[2 * chip[0] + chip[1]], dst_ref=recv_ref.at[k], send_sem=send_sems.at[k],
            recv_sem=recv_sems.at[k], device_id=(*chip, c), device_id_type=MESH) for k, chip in enumerate(others)]
        for cp in sends:
            cp.start()
        for cp in sends:
            cp.wait()
        mine.wait()

    return pl.pallas_call(
        body, name=name, in_specs=[ANY], out_specs=[ANY, ANY],
        out_shape=[jax.ShapeDtypeStruct((n_half, ncol), pair.dtype), jax.ShapeDtypeStruct((3, n_half, ncol), pair.dtype)],
        scratch_shapes=[pltpu.SemaphoreType.DMA((3,)), pltpu.SemaphoreType.DMA((3,)), pltpu.SemaphoreType.DMA],
    )(pair)


def _sibling_share(red, name):
    n_half, ncol = red.shape

    def body(r_ref, out_ref, send_sem, recv_sem, local_sem):
        x, y, c, _ = _place()
        mine = pltpu.make_async_copy(r_ref, out_ref.at[pl.ds(c * n_half, n_half), :], local_sem)
        mine.start()
        send = pltpu.make_async_remote_copy(
            src_ref=r_ref, dst_ref=out_ref.at[pl.ds(c * n_half, n_half), :], send_sem=send_sem, recv_sem=recv_sem,
            device_id=(x, y, 1 - c), device_id_type=MESH)
        send.start()
        send.wait()
        mine.wait()

    return pl.pallas_call(
        body, name=name, in_specs=[ANY], out_specs=ANY, out_shape=jax.ShapeDtypeStruct((2 * n_half, ncol), red.dtype),
        scratch_shapes=[pltpu.SemaphoreType.DMA, pltpu.SemaphoreType.DMA, pltpu.SemaphoreType.DMA],
    )(red)


def _reduce_scatter(g, ax, name):
    own, recv = _pair_exchange(g, ax, name + "_pair")
    four, n_half, ncol = own.shape
    pair = _add_n([own.reshape(four * n_half, ncol), recv.reshape(four * n_half, ncol)], BF16, name + "_pairsum")
    mine, got = _chip_scatter(pair.reshape(four, n_half, ncol), name + "_scatter")
    red = _add_n([mine, got[0], got[1], got[2]], F32, name + "_sum")
    shard = _sibling_share(red, name + "_share")
    if ax == 0:
        return shard
    return shard


def _vec_all_reduce(v, name):
    rows, d = v.shape

    def body(v_ref, out_ref, gath_ref, send_sems, recv_sems):
        x, y, c, _ = _place()
        me = 4 * x + 2 * y + c
        gath_ref[me] = v_ref[...]
        flips = [(fx, fy, fc) for fx in (0, 1) for fy in (0, 1) for fc in (0, 1)][1:]
        sends = []
        for k, (fx, fy, fc) in enumerate(flips):
            peer = (1 - x if fx else x, 1 - y if fy else y, 1 - c if fc else c)
            sends.append(pltpu.make_async_remote_copy(
                src_ref=v_ref, dst_ref=gath_ref.at[me], send_sem=send_sems.at[k], recv_sem=recv_sems.at[k],
                device_id=peer, device_id_type=MESH))
        for cp in sends:
            cp.start()
        for cp in sends:
            cp.wait()
        acc = gath_ref[0]
        for dev in range(1, 8):
            acc = acc + gath_ref[dev]
        out_ref[...] = acc

    vm = pl.BlockSpec(memory_space=pltpu.VMEM)
    return pl.pallas_call(
        body, name=name, in_specs=[vm], out_specs=vm, out_shape=jax.ShapeDtypeStruct((rows, d), F32),
        scratch_shapes=[pltpu.VMEM((8, rows, d), F32), pltpu.SemaphoreType.DMA((7,)), pltpu.SemaphoreType.DMA((7,))],
    )(v)


_SHARD_AXIS = {"w_in": 1, "w_attn_out": 1, "w_ret_out": 0, "w_o": 0, "w_ffn_gate": 1, "w_ffn_up": 1, "w_ffn_down": 0,
               "w_ple_gate": 0, "w_ple_up": 1}
_FFN_HIDDEN_AXIS = {"w_ffn_gate": 1, "w_ffn_up": 1, "w_ffn_down": 0}
_VECTORS = ("ret_gn_g", "ln1_g", "ln1_b", "ln2_g", "ln2_b")
_WEIGHTS = ("w_in", "w_attn_out", "w_ret_out", "ret_gn_g", "w_o", "ln1_g", "ln1_b", "w_ffn_gate", "w_ffn_up", "w_ffn_down",
            "w_ple_gate", "w_ple_up", "ln2_g", "ln2_b")


def _local_grads(cfg, x, p, tgt, wf, vec):
    xb, pb = x.astype(BF16), p.astype(BF16)
    cos, sin = _rope_tables(cfg)
    lg = _log_gamma(cfg)
    aw = cfg.aw

    proj = _mm(xb, wf["w_in"], "nn", BF16, "proj")
    q3 = _to_groups(cfg, proj[:, cfg.o_qa:cfg.o_qa + cfg.att_w])
    k3 = _to_groups(cfg, proj[:, cfg.o_ka:cfg.o_ka + cfg.att_w])
    v3 = _to_groups(cfg, proj[:, cfg.o_va:cfg.o_va + cfg.att_w])
    o3, lse3 = _attn_fwd(cfg, q3, k3, v3)
    o_att, lse = _attn_combine(cfg, jnp.stack(_from_groups(o3)), jnp.stack(_from_groups(lse3)))
    y_att = _mm(o_att, wf["w_attn_out"], "nn", BF16, "y_att")
    rn, rstd_r, states = _ret_fwd(cfg, proj, cos, sin, lg)
    rg = _ret_gate(cfg, proj, rn, vec["ret_gn_g"])
    y_ret = _mm(rg, wf["w_ret_out"], "nn", BF16, "y_ret")
    mixed = _mix(cfg, proj, y_att, y_ret)
    mo = _mm(mixed, wf["w_o"], "nn", F32, "mixed_out")
    xhat1, h1, rstd1 = _ln1(cfg, x, mo, vec["ln1_g"], vec["ln1_b"])
    u = _mm(h1, wf["w_ffn_gate"], "nn", BF16, "ffn_u")
    t = _mm(h1, wf["w_ffn_up"], "nn", BF16, "ffn_t")
    a = _ffn_act(cfg, u, t)
    ffn = _mm(a, wf["w_ffn_down"], "nn", F32, "ffn_down")
    hp = _mm(h1, wf["w_ple_gate"], "nn", BF16, "ple_gate")
    pu = _mm(pb, wf["w_ple_up"], "nn", BF16, "ple_up")
    loss, dz2, dz2b, d_pg, d_pu, g_ln2g, g_ln2b = _head(cfg, xhat1, ffn, hp, pu, tgt, vec["ln1_g"], vec["ln1_b"],
                                                        vec["ln2_g"], vec["ln2_b"])

    grads = {}
    d_a = _mm(dz2b, wf["w_ffn_down"], "nt", BF16, "d_a")
    d_u, d_t = _ffn_act_bwd(cfg, u, t, d_a)
    grads["w_ffn_down"] = _mm(a, dz2b, "tn", BF16, "g_ffn_down")
    grads["w_ffn_gate"] = _mm(h1, d_u, "tn", BF16, "g_ffn_gate")
    grads["w_ffn_up"] = _mm(h1, d_t, "tn", BF16, "g_ffn_up")
    grads["w_ple_gate"] = _mm(h1, d_pg, "tn", BF16, "g_ple_gate")
    grads["w_ple_up"] = _mm(pb, d_pu, "tn", BF16, "g_ple_up")
    dh = _mm(d_u, wf["w_ffn_gate"], "nt", F32, "dh_u")
    dh = _mm(d_t, wf["w_ffn_up"], "nt", F32, "dh_t", add=dh)
    dh = _mm(d_pg, wf["w_ple_gate"], "nt", F32, "dh_pg", add=dh)
    dz1, dz1b, g_ln1g, g_ln1b = _ln1_bwd(cfg, dh, dz2, xhat1, rstd1, vec["ln1_g"])
    d_mixed = _mm(dz1b, wf["w_o"], "nt", BF16, "d_mixed")
    grads["w_o"] = _mm(mixed, dz1b, "tn", BF16, "g_o")
    d_ya, d_yr, d_ga, d_g2 = _mix_bwd(cfg, proj, y_att, y_ret, d_mixed)
    d_oatt = _mm(d_ya, wf["w_attn_out"], "nt", BF16, "d_oatt")
    grads["w_attn_out"] = _mm(o_att, d_ya, "tn", BF16, "g_attn_out")
    d_rg = _mm(d_yr, wf["w_ret_out"], "nt", BF16, "d_rg")
    grads["w_ret_out"] = _mm(rg, d_yr, "tn", BF16, "g_ret_out")
    d_gr, d_r, g_gn = _ret_gate_bwd(cfg, proj, rn, vec["ret_gn_g"], rstd_r, d_rg)
    d_qr, d_kr, d_vr = _ret_bwd(cfg, proj, d_r, states, cos, sin, lg)
    delta = _attn_delta(cfg, d_oatt, o_att)
    do3 = jnp.stack([_deinterleave(d_oatt, dil) for _, dil in ATT_GROUPS])
    l3 = jnp.stack([_deinterleave(lse, dil) for _, dil in ATT_GROUPS])
    dl3 = jnp.stack([_deinterleave(delta, dil) for _, dil in ATT_GROUPS])
    dq3, dk3, dv3 = _attn_bwd(cfg, q3, k3, v3, do3, l3, dl3)
    dproj = jnp.concatenate(_from_groups(dq3) + _from_groups(dk3) + _from_groups(dv3) + [d_qr, d_kr, d_vr, d_gr, d_ga, d_g2], axis=1)
    grads["w_in"] = _mm(xb, dproj, "tn", BF16, "g_in")
    grad_x = _mm(dproj, wf["w_in"], "nt", F32, "grad_x", add=dz1, add_scale=cfg.alpha)
    vgrads = {"ret_gn_g": g_gn, "ln1_g": g_ln1g, "ln1_b": g_ln1b, "ln2_g": g_ln2g, "ln2_b": g_ln2b}
    return loss, grad_x, grads, vgrads


def _step(cfg, x, p, tgt, w, m, v):
    mats = [n for n in _WEIGHTS if n in _SHARD_AXIS]
    ffn_pad = cfg.ffn_shard - cfg.ffn // 4

    def shard_bf16(n):
        ws = w[n][0].astype(BF16)
        if n in _FFN_HIDDEN_AXIS and ffn_pad:
            ws = jnp.pad(ws, [(0, ffn_pad) if a == _FFN_HIDDEN_AXIS[n] else (0, 0) for a in (0, 1)])
        return ws

    def unpad(n, t):
        if n in _FFN_HIDDEN_AXIS and ffn_pad:
            return t[:cfg.ffn // 4] if _FFN_HIDDEN_AXIS[n] == 0 else t[:, :cfg.ffn // 4]
        return t

    wf = {n: _all_gather(shard_bf16(n), _SHARD_AXIS[n], "ag_" + n) for n in mats}
    vec = {n: w[n] for n in _VECTORS}
    loss, grad_x, grads, vgrads = _local_grads(cfg, x[0], p[0, 0], tgt[0], wf, vec)
    loss = lax.psum(loss[0, 0], ("x", "y", "c"))
    red = {n: unpad(n, _reduce_scatter(grads[n], _SHARD_AXIS[n], "rs_" + n)) for n in mats}
    stacked = jnp.concatenate([vgrads[n] for n in _VECTORS] + [jnp.zeros((3, cfg.d), F32)], axis=0)
    vsum = _vec_all_reduce(stacked, "vec_all_reduce")
    for i, n in enumerate(_VECTORS):
        red[n] = vsum[i:i + 1]
    g_out, d_out, m_out, v_out = [], [], [], []
    for n in _WEIGHTS:
        dlt, mn, vn = _adamw(w[n][0] if n in _SHARD_AXIS else w[n], red[n], m[n][0] if n in _SHARD_AXIS else m[n],
                             v[n][0] if n in _SHARD_AXIS else v[n], "adamw_" + n)
        lead = (lambda t: t[None]) if n in _SHARD_AXIS else (lambda t: t)
        g_out.append(lead(red[n]))
        d_out.append(lead(dlt))
        m_out.append(lead(mn))
        v_out.append(lead(vn))
    return (loss, grad_x[None], *g_out, *d_out, *m_out, *v_out)


def kernel(x, p, w_in, w_attn_out, w_ret_out, ret_gn_g, w_o, ln1_g, ln1_b, w_ffn_gate, w_ffn_up, w_ffn_down, w_ple_gate, w_ple_up, ln2_g, ln2_b, loss_target, m_w_in, m_w_attn_out, m_w_ret_out, m_ret_gn_g, m_w_o, m_ln1_g, m_ln1_b, m_w_ffn_gate, m_w_ffn_up, m_w_ffn_down, m_w_ple_gate, m_w_ple_up, m_ln2_g, m_ln2_b, v_w_in, v_w_attn_out, v_w_ret_out, v_ret_gn_g, v_w_o, v_ln1_g, v_ln1_b, v_w_ffn_gate, v_w_ffn_up, v_w_ffn_down, v_w_ple_gate, v_w_ple_up, v_ln2_g, v_ln2_b):
    w = dict(zip(_WEIGHTS, (w_in, w_attn_out, w_ret_out, ret_gn_g, w_o, ln1_g, ln1_b, w_ffn_gate, w_ffn_up, w_ffn_down,
                            w_ple_gate, w_ple_up, ln2_g, ln2_b)))
    m = dict(zip(_WEIGHTS, (m_w_in, m_w_attn_out, m_w_ret_out, m_ret_gn_g, m_w_o, m_ln1_g, m_ln1_b, m_w_ffn_gate, m_w_ffn_up,
                            m_w_ffn_down, m_w_ple_gate, m_w_ple_up, m_ln2_g, m_ln2_b)))
    v = dict(zip(_WEIGHTS, (v_w_in, v_w_attn_out, v_w_ret_out, v_ret_gn_g, v_w_o, v_ln1_g, v_ln1_b, v_w_ffn_gate, v_w_ffn_up,
                            v_w_ffn_down, v_w_ple_gate, v_w_ple_up, v_ln2_g, v_ln2_b)))
    return _step(_FULL, x, p, loss_target, w, m, v)
```

```python
import functools
import math

import jax
import jax.numpy as jnp
import numpy as np
from jax import lax
from jax.experimental import pallas as pl
from jax.experimental.pallas import tpu as pltpu

F32 = jnp.float32
BF16 = jnp.bfloat16
MESH = pl.DeviceIdType.MESH
ANY = pl.BlockSpec(memory_space=pl.ANY)

ATT_BLOCK = 128
ATT_GROUPS = ((128, 1), (512, 4), (2048, 16))
LN_EPS = 1e-5
GN_EPS = 1e-6
NEG_INF = -1e30
ROPE_BASE = 10000.0
ADAM_LR, ADAM_B1, ADAM_B2, ADAM_EPS, ADAM_WD, ADAM_STEP = 0.001, 0.9, 0.999, 1e-08, 0.01, 10
VMEM_LIMIT = 56 * 1024 * 1024
STAGE_BYTES = 4 * 1024 * 1024


class _Cfg:
    def __init__(self, seq, d, ple, ahd, ahg, rh, rqk, rv, ffn, cw):
        self.seq, self.d, self.ple, self.ahd, self.ahg = seq, d, ple, ahd, ahg
        self.rh, self.rqk, self.rv, self.ffn, self.cw = rh, rqk, rv, ffn, cw
        self.aw = ahg * ahd
        self.att_w = 3 * self.aw
        self.rqk_w = rh * rqk
        self.rv_w = rh * rv
        offs = np.cumsum([0] + [self.att_w] * 3 + [self.rqk_w] * 2 + [self.rv_w] * 2 + [d] * 2)
        (self.o_qa, self.o_ka, self.o_va, self.o_qr, self.o_kr, self.o_vr, self.o_gr, self.o_ga, self.o_g2,
         self.in_w) = [int(v) for v in offs]
        self.alpha = 2.0 ** 0.25
        self.ffn_shard = -(-(ffn // 4) // 128) * 128
        self.ffn_p = 4 * self.ffn_shard
        assert self.rv_w == d and d % cw == 0
        for o in (self.o_gr, self.o_ga, self.o_g2):
            assert o % cw == 0
        assert self.o_qr % rqk == 0 and self.o_kr % rqk == 0 and self.o_vr % rv == 0 and self.o_gr % rv == 0
        assert rqk // 2 % 128 == 0 and seq % (ATT_BLOCK * 16) == 0


_FULL = _Cfg(seq=4096, d=4096, ple=256, ahd=128, ahg=8, rh=8, rqk=256, rv=512, ffn=11008, cw=1024)


def _tile(n, target, q=128):
    t = min(n, target) // q * q
    while t >= q:
        if n % t == 0:
            return t
        t -= q
    return n


def _row_tile(n, row_bytes, target_bytes, q=8):
    best = None
    for t in range(q, n + 1, q):
        if n % t == 0 and (best is None or t * row_bytes <= target_bytes):
            best = t
            if t * row_bytes > target_bytes:
                break
    return n if best is None else best


def _params(sem=None):
    return pltpu.CompilerParams(dimension_semantics=sem, vmem_limit_bytes=VMEM_LIMIT)


def _sigmoid(x):
    return 1.0 / (1.0 + jnp.exp(-x))


def _mm(a, b, mode, out_dtype, name, add=None, add_scale=1.0):
    if mode == "nn":
        (m, k), (k2, n) = a.shape, b.shape
    elif mode == "nt":
        (m, k), (n, k2) = a.shape, b.shape
    else:
        (k, m), (k2, n) = a.shape, b.shape
    assert k == k2, (name, a.shape, b.shape)
    tm, tn, tk = _tile(m, 1024), _tile(n, 1024), _tile(k, 512)
    nk = k // tk
    if mode == "tn":
        a_spec = pl.BlockSpec((tk, tm), lambda i, j, kk: (kk, i))
        dims = (((0,), (0,)), ((), ()))
    else:
        a_spec = pl.BlockSpec((tm, tk), lambda i, j, kk: (i, kk))
        dims = (((1,), (1,)), ((), ())) if mode == "nt" else (((1,), (0,)), ((), ()))
    if mode == "nt":
        b_spec = pl.BlockSpec((tn, tk), lambda i, j, kk: (j, kk))
    else:
        b_spec = pl.BlockSpec((tk, tn), lambda i, j, kk: (kk, j))
    o_spec = pl.BlockSpec((tm, tn), lambda i, j, kk: (i, j))
    has_add = add is not None

    def body(*refs):
        if has_add:
            a_ref, b_ref, add_ref, o_ref, acc_ref = refs
        else:
            a_ref, b_ref, o_ref, acc_ref = refs
        kk = pl.program_id(2)

        @pl.when(kk == 0)
        def _():
            acc_ref[...] = jnp.zeros_like(acc_ref)

        acc_ref[...] += lax.dot_general(a_ref[...], b_ref[...], dims, preferred_element_type=F32)

        @pl.when(kk == nk - 1)
        def _():
            r = acc_ref[...]
            if has_add:
                r = r + add_scale * add_ref[...]
            o_ref[...] = r.astype(o_ref.dtype)

    return pl.pallas_call(
        body, name=name, grid=(m // tm, n // tn, nk),
        in_specs=[a_spec, b_spec] + ([o_spec] if has_add else []), out_specs=o_spec,
        out_shape=jax.ShapeDtypeStruct((m, n), out_dtype), scratch_shapes=[pltpu.VMEM((tm, tn), F32)],
        compiler_params=_params(("parallel", "parallel", "arbitrary")),
    )(*((a, b, add) if has_add else (a, b)))


def _seg_blocks(cfg):
    return [cfg.seq // ATT_BLOCK // dil for _, dil in ATT_GROUPS]


def _group_select(g, vals):
    out = jnp.int32(vals[-1])
    for i in range(len(vals) - 2, -1, -1):
        out = jnp.where(g == i, jnp.int32(vals[i]), out)
    return out


def _col(tile, h):
    lane = lax.broadcasted_iota(jnp.int32, tile.shape, 1)
    return jnp.sum(jnp.where(lane == h, tile, 0.0), axis=1, keepdims=True)


def _set_col(tile, h, col):
    lane = lax.broadcasted_iota(jnp.int32, tile.shape, 1)
    return jnp.where(lane == h, col, tile)


_NT = (((1,), (1,)), ((), ()))
_TN = (((0,), (0,)), ((), ()))


def _dot(a, b):
    return jnp.dot(a, b, preferred_element_type=F32)


def _dot_nt(a, b):
    return lax.dot_general(a, b, _NT, preferred_element_type=F32)


def _dot_tn(a, b):
    return lax.dot_general(a, b, _TN, preferred_element_type=F32)


def _attn_fwd(cfg, q3, k3, v3):
    nb = cfg.seq // ATT_BLOCK
    segs = _seg_blocks(cfg)
    scale = cfg.ahd ** -0.5
    blk = ATT_BLOCK

    def body(q_ref, kp_ref, kc_ref, vp_ref, vc_ref, o_ref, lse_ref):
        g, b = pl.program_id(0), pl.program_id(1)
        has_prev = (b & (_group_select(g, segs) - 1)) != 0
        qi = lax.broadcasted_iota(jnp.int32, (blk, blk), 0)
        kj = lax.broadcasted_iota(jnp.int32, (blk, blk), 1)
        valid_c = kj <= qi
        valid_p = jnp.logical_and(kj >= qi, has_prev)
        lse = jnp.zeros((blk, cfg.ahg), F32)
        for h in range(cfg.ahg):
            hs = slice(h * cfg.ahd, (h + 1) * cfg.ahd)
            q = q_ref[:, hs]
            s_c = jnp.where(valid_c, _dot_nt(q, kc_ref[:, hs]) * scale, NEG_INF)
            s_p = jnp.where(valid_p, _dot_nt(q, kp_ref[:, hs]) * scale, NEG_INF)
            m = jnp.maximum(jnp.max(s_c, axis=1, keepdims=True), jnp.max(s_p, axis=1, keepdims=True))
            p_c, p_p = jnp.exp(s_c - m), jnp.exp(s_p - m)
            den = jnp.sum(p_c, axis=1, keepdims=True) + jnp.sum(p_p, axis=1, keepdims=True)
            o = _dot(p_c.astype(BF16), vc_ref[:, hs]) + _dot(p_p.astype(BF16), vp_ref[:, hs])
            o_ref[:, hs] = (o / den).astype(o_ref.dtype)
            lse = _set_col(lse, h, m + jnp.log(den))
        lse_ref[...] = lse

    cur = pl.BlockSpec((None, blk, cfg.aw), lambda g, b: (g, b, 0))
    prev = pl.BlockSpec((None, blk, cfg.aw), lambda g, b: (g, jnp.maximum(b - 1, 0), 0))
    return pl.pallas_call(
        body, name="attn_fwd", grid=(3, nb), in_specs=[cur, prev, cur, prev, cur],
        out_specs=[cur, pl.BlockSpec((None, blk, cfg.ahg), lambda g, b: (g, b, 0))],
        out_shape=[jax.ShapeDtypeStruct((3, cfg.seq, cfg.aw), BF16), jax.ShapeDtypeStruct((3, cfg.seq, cfg.ahg), F32)],
        compiler_params=_params(("parallel", "parallel")),
    )(q3, k3, k3, v3, v3)


def _attn_combine(cfg, o3, lse3):
    tm = 256

    def body(o_ref, lse_ref, oa_ref, l_ref):
        l0, l1, l2 = lse_ref[0], lse_ref[1], lse_ref[2]
        m = jnp.maximum(jnp.maximum(l0, l1), l2)
        big = m + jnp.log(jnp.exp(l0 - m) + jnp.exp(l1 - m) + jnp.exp(l2 - m))
        ws = [jnp.exp(l0 - big), jnp.exp(l1 - big), jnp.exp(l2 - big)]
        for h in range(cfg.ahg):
            hs = slice(h * cfg.ahd, (h + 1) * cfg.ahd)
            acc = _col(ws[0], h) * o_ref[0, :, hs].astype(F32)
            acc += _col(ws[1], h) * o_ref[1, :, hs].astype(F32)
            acc += _col(ws[2], h) * o_ref[2, :, hs].astype(F32)
            oa_ref[:, hs] = acc.astype(oa_ref.dtype)
        l_ref[...] = big

    return pl.pallas_call(
        body, name="attn_combine", grid=(cfg.seq // tm,),
        in_specs=[pl.BlockSpec((3, tm, cfg.aw), lambda i: (0, i, 0)), pl.BlockSpec((3, tm, cfg.ahg), lambda i: (0, i, 0))],
        out_specs=[pl.BlockSpec((tm, cfg.aw), lambda i: (i, 0)), pl.BlockSpec((tm, cfg.ahg), lambda i: (i, 0))],
        out_shape=[jax.ShapeDtypeStruct((cfg.seq, cfg.aw), BF16), jax.ShapeDtypeStruct((cfg.seq, cfg.ahg), F32)],
        compiler_params=_params(("parallel",)),
    )(o3, lse3)


def _attn_delta(cfg, do, o):
    tm = 256

    def body(do_ref, o_ref, d_ref):
        out = jnp.zeros((tm, cfg.ahg), F32)
        for h in range(cfg.ahg):
            hs = slice(h * cfg.ahd, (h + 1) * cfg.ahd)
            prod = do_ref[:, hs].astype(F32) * o_ref[:, hs].astype(F32)
            out = _set_col(out, h, jnp.sum(prod, axis=1, keepdims=True))
        d_ref[...] = out

    row = pl.BlockSpec((tm, cfg.aw), lambda i: (i, 0))
    return pl.pallas_call(
        body, name="attn_delta", grid=(cfg.seq // tm,), in_specs=[row, row],
        out_specs=pl.BlockSpec((tm, cfg.ahg), lambda i: (i, 0)),
        out_shape=jax.ShapeDtypeStruct((cfg.seq, cfg.ahg), F32), compiler_params=_params(("parallel",)),
    )(do, o)


def _attn_bwd(cfg, q3, k3, v3, do3, l3, d3):
    nb = cfg.seq // ATT_BLOCK
    segs = _seg_blocks(cfg)
    scale = cfg.ahd ** -0.5
    blk = ATT_BLOCK

    def body(q_ref, qn_ref, do_ref, don_ref, l_ref, ln_ref, d_ref, dn_ref, kp_ref, kc_ref, vp_ref, vc_ref,
             dq_ref, dk_ref, dv_ref):
        g, b = pl.program_id(0), pl.program_id(1)
        seg_mask = _group_select(g, segs) - 1
        has_prev = (b & seg_mask) != 0
        has_next = jnp.logical_and(b + 1 < nb, ((b + 1) & seg_mask) != 0)
        qi = lax.broadcasted_iota(jnp.int32, (blk, blk), 0)
        kj = lax.broadcasted_iota(jnp.int32, (blk, blk), 1)
        valid_c = kj <= qi
        valid_p = jnp.logical_and(kj >= qi, has_prev)
        valid_n = jnp.logical_and(kj >= qi, has_next)
        lse, lse_n, dlt, dlt_n = l_ref[...], ln_ref[...], d_ref[...], dn_ref[...]
        for h in range(cfg.ahg):
            hs = slice(h * cfg.ahd, (h + 1) * cfg.ahd)
            q, qn, do, don = q_ref[:, hs], qn_ref[:, hs], do_ref[:, hs], don_ref[:, hs]
            kc, kp, vc, vp = kc_ref[:, hs], kp_ref[:, hs], vc_ref[:, hs], vp_ref[:, hs]
            lh, lnh, dh, dnh = _col(lse, h), _col(lse_n, h), _col(dlt, h), _col(dlt_n, h)
            p_c = jnp.where(valid_c, jnp.exp(_dot_nt(q, kc) * scale - lh), 0.0)
            p_p = jnp.where(valid_p, jnp.exp(_dot_nt(q, kp) * scale - lh), 0.0)
            ds_c = (p_c * (_dot_nt(do, vc) - dh)).astype(BF16)
            ds_p = (p_p * (_dot_nt(do, vp) - dh)).astype(BF16)
            dq_ref[:, hs] = ((_dot(ds_c, kc) + _dot(ds_p, kp)) * scale).astype(dq_ref.dtype)
            p_n = jnp.where(valid_n, jnp.exp(_dot_nt(qn, kc) * scale - lnh), 0.0)
            ds_n = (p_n * (_dot_nt(don, vc) - dnh)).astype(BF16)
            dk = _dot_tn(ds_c, q) + _dot_tn(ds_n, qn)
            dv = _dot_tn(p_c.astype(BF16), do) + _dot_tn(p_n.astype(BF16), don)
            dk_ref[:, hs] = (dk * scale).astype(dk_ref.dtype)
            dv_ref[:, hs] = dv.astype(dv_ref.dtype)

    def spec(width, shift):
        if shift == 0:
            return pl.BlockSpec((None, blk, width), lambda g, b: (g, b, 0))
        if shift > 0:
            return pl.BlockSpec((None, blk, width), lambda g, b: (g, jnp.minimum(b + 1, nb - 1), 0))
        return pl.BlockSpec((None, blk, width), lambda g, b: (g, jnp.maximum(b - 1, 0), 0))

    w, hw = cfg.aw, cfg.ahg
    out = jax.ShapeDtypeStruct((3, cfg.seq, w), BF16)
    return pl.pallas_call(
        body, name="attn_bwd", grid=(3, nb),
        in_specs=[spec(w, 0), spec(w, 1), spec(w, 0), spec(w, 1), spec(hw, 0), spec(hw, 1), spec(hw, 0), spec(hw, 1),
                  spec(w, -1), spec(w, 0), spec(w, -1), spec(w, 0)],
        out_specs=[spec(w, 0)] * 3, out_shape=[out, out, out], compiler_params=_params(("parallel", "parallel")),
    )(q3, q3, do3, do3, l3, l3, d3, d3, k3, k3, v3, v3)


def _deinterleave(t, dil):
    s, w = t.shape
    return t if dil == 1 else t.reshape(s // dil, dil, w).transpose(1, 0, 2).reshape(s, w)


def _interleave(t, dil):
    s, w = t.shape
    return t if dil == 1 else t.reshape(dil, s // dil, w).transpose(1, 0, 2).reshape(s, w)


def _to_groups(cfg, t):
    return jnp.stack([_deinterleave(t[:, g * cfg.aw:(g + 1) * cfg.aw], dil) for g, (_, dil) in enumerate(ATT_GROUPS)])


def _from_groups(t3):
    return [_interleave(t3[g], dil) for g, (_, dil) in enumerate(ATT_GROUPS)]


def _rope_tables(cfg):
    half = cfg.rqk // 2
    pos = jnp.arange(cfg.seq, dtype=F32)
    inv_freq = ROPE_BASE ** (-jnp.arange(half, dtype=F32) / half)
    ang = pos[:, None] * inv_freq[None, :]
    return jnp.cos(ang), jnp.sin(ang)


def _log_gamma(cfg):
    lg = jnp.log(1.0 - 2.0 ** (-5.0 - jnp.arange(cfg.rh, dtype=F32)))
    return jnp.broadcast_to(lg[:, None, None], (cfg.rh, 1, max(cfg.rqk, cfg.rv)))


def _rot(t, c, s, half):
    t1, t2 = t[:, :half], t[:, half:]
    return jnp.concatenate([t1 * c - t2 * s, t1 * s + t2 * c], axis=1)


def _unrot(d, c, s, half):
    d1, d2 = d[:, :half], d[:, half:]
    return jnp.concatenate([d1 * c + d2 * s, d2 * c - d1 * s], axis=1)


def _decays(lg_ref, cfg):
    blk = ATT_BLOCK
    lg_v, lg_k = lg_ref[:, :cfg.rv], lg_ref[:, :cfg.rqk]
    qi = lax.broadcasted_iota(jnp.int32, (blk, blk), 0)
    kj = lax.broadcasted_iota(jnp.int32, (blk, blk), 1)
    diff = (qi - kj).astype(F32)
    intra = jnp.where(diff >= 0, jnp.exp(jnp.maximum(diff, 0.0) * lg_ref[:, :blk]), 0.0)
    idx_v = lax.broadcasted_iota(jnp.int32, (blk, cfg.rv), 0).astype(F32)
    idx_k = lax.broadcasted_iota(jnp.int32, (blk, cfg.rqk), 0).astype(F32)
    cross = jnp.exp((idx_v + 1.0) * lg_v)
    state = jnp.exp((blk - 1.0 - idx_k) * lg_k)
    chunk = jnp.exp(float(blk) * lg_v)
    return intra, cross, state, chunk


def _ret_fwd(cfg, proj, cos, sin, lg):
    blk, nc, half = ATT_BLOCK, cfg.seq // ATT_BLOCK, cfg.rqk // 2
    kscale = cfg.rqk ** -0.5

    def body(q_ref, k_ref, v_ref, cos_ref, sin_ref, lg_ref, rn_ref, rstd_ref, st_ref, state):
        n = pl.program_id(1)

        @pl.when(n == 0)
        def _():
            state[...] = jnp.zeros_like(state)

        intra, cross_d, state_d, chunk_d = _decays(lg_ref, cfg)
        c, s = cos_ref[...], sin_ref[...]
        qb = _rot(q_ref[...].astype(F32), c, s, half).astype(BF16)
        kf = _rot(k_ref[...].astype(F32), c, s, half) * kscale
        vb = v_ref[...]
        prev = state[...]
        st_ref[...] = prev.astype(BF16)
        att = _dot_nt(qb, kf.astype(BF16)) * intra
        out = _dot(att.astype(BF16), vb) + _dot(qb, prev.astype(BF16)) * cross_d
        state[...] = chunk_d * prev + _dot_tn((kf * state_d).astype(BF16), vb)
        mu = jnp.mean(out, axis=1, keepdims=True)
        cen = out - mu
        rstd = lax.rsqrt(jnp.mean(cen * cen, axis=1, keepdims=True) + GN_EPS)
        rn_ref[...] = (cen * rstd).astype(rn_ref.dtype)
        rstd_ref[...] = rstd

    oq, ok, ov = cfg.o_qr // cfg.rqk, cfg.o_kr // cfg.rqk, cfg.o_vr // cfg.rv
    tab = pl.BlockSpec((blk, half), lambda h, n: (n, 0))
    return pl.pallas_call(
        body, name="ret_fwd", grid=(cfg.rh, nc),
        in_specs=[pl.BlockSpec((blk, cfg.rqk), lambda h, n: (n, oq + h)), pl.BlockSpec((blk, cfg.rqk), lambda h, n: (n, ok + h)),
                  pl.BlockSpec((blk, cfg.rv), lambda h, n: (n, ov + h)), tab, tab,
                  pl.BlockSpec((None, 1, lg.shape[2]), lambda h, n: (h, 0, 0))],
        out_specs=[pl.BlockSpec((blk, cfg.rv), lambda h, n: (n, h)), pl.BlockSpec((None, blk, 1), lambda h, n: (h, n, 0)),
                   pl.BlockSpec((None, None, cfg.rqk, cfg.rv), lambda h, n: (h, n, 0, 0))],
        out_shape=[jax.ShapeDtypeStruct((cfg.seq, cfg.rv_w), BF16), jax.ShapeDtypeStruct((cfg.rh, cfg.seq, 1), F32),
                   jax.ShapeDtypeStruct((cfg.rh, nc, cfg.rqk, cfg.rv), BF16)],
        scratch_shapes=[pltpu.VMEM((cfg.rqk, cfg.rv), F32)], compiler_params=_params(("parallel", "arbitrary")),
    )(proj, proj, proj, cos, sin, lg)


def _ret_bwd(cfg, proj, d_r, states, cos, sin, lg):
    blk, nc, half = ATT_BLOCK, cfg.seq // ATT_BLOCK, cfg.rqk // 2
    kscale = cfg.rqk ** -0.5

    def body(q_ref, k_ref, v_ref, do_ref, st_ref, cos_ref, sin_ref, lg_ref, dq_ref, dk_ref, dv_ref, dstate):
        n = pl.program_id(1)

        @pl.when(n == 0)
        def _():
            dstate[...] = jnp.zeros_like(dstate)

        intra, cross_d, state_d, chunk_d = _decays(lg_ref, cfg)
        c, s = cos_ref[...], sin_ref[...]
        qb = _rot(q_ref[...].astype(F32), c, s, half).astype(BF16)
        kf = _rot(k_ref[...].astype(F32), c, s, half) * kscale
        kb, ksb = kf.astype(BF16), (kf * state_d).astype(BF16)
        vb, prev = v_ref[...], st_ref[...]
        do = do_ref[...].astype(F32)
        dob, docb = do.astype(BF16), (do * cross_d).astype(BF16)
        dsb = dstate[...].astype(BF16)
        att = (_dot_nt(qb, kb) * intra).astype(BF16)
        datt = (_dot_nt(dob, vb) * intra).astype(BF16)
        d_q = _dot(datt, kb) + _dot_nt(docb, prev)
        d_k = _dot_tn(datt, qb) + _dot_nt(vb, dsb) * state_d
        d_v = _dot_tn(att, dob) + _dot(ksb, dsb)
        dstate[...] = chunk_d * dstate[...] + _dot_tn(qb, docb)
        dq_ref[...] = _unrot(d_q, c, s, half).astype(dq_ref.dtype)
        dk_ref[...] = _unrot(d_k * kscale, c, s, half).astype(dk_ref.dtype)
        dv_ref[...] = d_v.astype(dv_ref.dtype)

    oq, ok, ov = cfg.o_qr // cfg.rqk, cfg.o_kr // cfg.rqk, cfg.o_vr // cfg.rv
    last = nc - 1
    tab = pl.BlockSpec((blk, half), lambda h, n: (last - n, 0))
    qk_out = pl.BlockSpec((blk, cfg.rqk), lambda h, n: (last - n, h))
    v_out = pl.BlockSpec((blk, cfg.rv), lambda h, n: (last - n, h))
    return pl.pallas_call(
        body, name="ret_bwd", grid=(cfg.rh, nc),
        in_specs=[pl.BlockSpec((blk, cfg.rqk), lambda h, n: (last - n, oq + h)),
                  pl.BlockSpec((blk, cfg.rqk), lambda h, n: (last - n, ok + h)),
                  pl.BlockSpec((blk, cfg.rv), lambda h, n: (last - n, ov + h)), v_out,
                  pl.BlockSpec((None, None, cfg.rqk, cfg.rv), lambda h, n: (h, last - n, 0, 0)), tab, tab,
                  pl.BlockSpec((None, 1, lg.shape[2]), lambda h, n: (h, 0, 0))],
        out_specs=[qk_out, qk_out, v_out],
        out_shape=[jax.ShapeDtypeStruct((cfg.seq, cfg.rqk_w), BF16), jax.ShapeDtypeStruct((cfg.seq, cfg.rqk_w), BF16),
                   jax.ShapeDtypeStruct((cfg.seq, cfg.rv_w), BF16)],
        scratch_shapes=[pltpu.VMEM((cfg.rqk, cfg.rv), F32)], compiler_params=_params(("parallel", "arbitrary")),
    )(proj, proj, proj, d_r, states, cos, sin, lg)


def _ret_gate(cfg, proj, rn, gn_g):
    tm, cw = 256, cfg.cw
    og = cfg.o_gr // cw

    def body(g_ref, rn_ref, w_ref, o_ref):
        g = g_ref[...].astype(F32)
        o_ref[...] = (g * _sigmoid(g) * (rn_ref[...].astype(F32) * w_ref[...])).astype(o_ref.dtype)

    blk = pl.BlockSpec((tm, cw), lambda i, j: (i, j))
    return pl.pallas_call(
        body, name="ret_gate", grid=(cfg.seq // tm, cfg.rv_w // cw),
        in_specs=[pl.BlockSpec((tm, cw), lambda i, j: (i, og + j)), blk, pl.BlockSpec((1, cw), lambda i, j: (0, j))],
        out_specs=blk, out_shape=jax.ShapeDtypeStruct((cfg.seq, cfg.rv_w), BF16),
        compiler_params=_params(("parallel", "parallel")),
    )(proj, rn, gn_g)


def _mix(cfg, proj, y_att, y_ret):
    tm, cw = 256, cfg.cw
    oa, o2 = cfg.o_ga // cw, cfg.o_g2 // cw

    def body(ga_ref, g2_ref, ya_ref, yr_ref, o_ref):
        r = _sigmoid(ga_ref[...].astype(F32)) * ya_ref[...].astype(F32)
        r += _sigmoid(g2_ref[...].astype(F32)) * yr_ref[...].astype(F32)
        o_ref[...] = r.astype(o_ref.dtype)

    blk = pl.BlockSpec((tm, cw), lambda i, j: (i, j))
    return pl.pallas_call(
        body, name="mix", grid=(cfg.seq // tm, cfg.d // cw),
        in_specs=[pl.BlockSpec((tm, cw), lambda i, j: (i, oa + j)), pl.BlockSpec((tm, cw), lambda i, j: (i, o2 + j)), blk, blk],
        out_specs=blk, out_shape=jax.ShapeDtypeStruct((cfg.seq, cfg.d), BF16),
        compiler_params=_params(("parallel", "parallel")),
    )(proj, proj, y_att, y_ret)


def _mix_bwd(cfg, proj, y_att, y_ret, d_mixed):
    tm, cw = 256, cfg.cw
    oa, o2 = cfg.o_ga // cw, cfg.o_g2 // cw

    def body(ga_ref, g2_ref, ya_ref, yr_ref, dm_ref, dya_ref, dyr_ref, dga_ref, dg2_ref):
        dm = dm_ref[...].astype(F32)
        sa, s2 = _sigmoid(ga_ref[...].astype(F32)), _sigmoid(g2_ref[...].astype(F32))
        dya_ref[...] = (dm * sa).astype(dya_ref.dtype)
        dyr_ref[...] = (dm * s2).astype(dyr_ref.dtype)
        dga_ref[...] = (dm * ya_ref[...].astype(F32) * sa * (1.0 - sa)).astype(dga_ref.dtype)
        dg2_ref[...] = (dm * yr_ref[...].astype(F32) * s2 * (1.0 - s2)).astype(dg2_ref.dtype)

    blk = pl.BlockSpec((tm, cw), lambda i, j: (i, j))
    out = jax.ShapeDtypeStruct((cfg.seq, cfg.d), BF16)
    return pl.pallas_call(
        body, name="mix_bwd", grid=(cfg.seq // tm, cfg.d // cw),
        in_specs=[pl.BlockSpec((tm, cw), lambda i, j: (i, oa + j)), pl.BlockSpec((tm, cw), lambda i, j: (i, o2 + j)), blk, blk, blk],
        out_specs=[blk] * 4, out_shape=[out] * 4, compiler_params=_params(("parallel", "parallel")),
    )(proj, proj, y_att, y_ret, d_mixed)


def _ret_gate_bwd(cfg, proj, rn, gn_g, rstd, d_rg):
    tm, rv = 256, cfg.rv
    og = cfg.o_gr // rv

    def body(g_ref, rn_ref, w_ref, rstd_ref, drg_ref, dg_ref, dr_ref, gw_ref):
        i = pl.program_id(1)
        g, rn, w = g_ref[...].astype(F32), rn_ref[...].astype(F32), w_ref[...]
        drg = drg_ref[...].astype(F32)
        sg = _sigmoid(g)
        silu = g * sg
        dg_ref[...] = (drg * (rn * w) * (sg * (1.0 + g * (1.0 - sg)))).astype(dg_ref.dtype)
        drn = drg * silu * w
        part = jnp.sum(drg * silu * rn, axis=0, keepdims=True)

        @pl.when(i == 0)
        def _():
            gw_ref[...] = part

        @pl.when(i > 0)
        def _():
            gw_ref[...] += part

        m1 = jnp.mean(drn, axis=1, keepdims=True)
        m2 = jnp.mean(drn * rn, axis=1, keepdims=True)
        dr_ref[...] = (rstd_ref[...] * (drn - m1 - rn * m2)).astype(dr_ref.dtype)

    blk = pl.BlockSpec((tm, rv), lambda h, i: (i, h))
    vec = pl.BlockSpec((1, rv), lambda h, i: (0, h))
    out = jax.ShapeDtypeStruct((cfg.seq, cfg.rv_w), BF16)
    return pl.pallas_call(
        body, name="ret_gate_bwd", grid=(cfg.rh, cfg.seq // tm),
        in_specs=[pl.BlockSpec((tm, rv), lambda h, i: (i, og + h)), blk, vec, pl.BlockSpec((None, tm, 1), lambda h, i: (h, i, 0)), blk],
        out_specs=[blk, blk, vec], out_shape=[out, out, jax.ShapeDtypeStruct((1, cfg.rv_w), F32)],
        compiler_params=_params(("parallel", "arbitrary")),
    )(proj, rn, gn_g, rstd, d_rg)


def _ln1(cfg, x, mo, g, b):
    tm, d = 128, cfg.d

    def body(x_ref, mo_ref, g_ref, b_ref, xh_ref, h_ref, rstd_ref):
        z = cfg.alpha * x_ref[...] + mo_ref[...]
        cen = z - jnp.mean(z, axis=1, keepdims=True)
        rstd = lax.rsqrt(jnp.mean(cen * cen, axis=1, keepdims=True) + LN_EPS)
        xh = cen * rstd
        xh_ref[...] = xh
        h_ref[...] = (xh * g_ref[...] + b_ref[...]).astype(h_ref.dtype)
        rstd_ref[...] = rstd

    row = pl.BlockSpec((tm, d), lambda i: (i, 0))
    vec = pl.BlockSpec((1, d), lambda i: (0, 0))
    col = pl.BlockSpec((tm, 1), lambda i: (i, 0))
    return pl.pallas_call(
        body, name="ln1", grid=(cfg.seq // tm,), in_specs=[row, row, vec, vec], out_specs=[row, row, col],
        out_shape=[jax.ShapeDtypeStruct((cfg.seq, d), F32), jax.ShapeDtypeStruct((cfg.seq, d), BF16),
                   jax.ShapeDtypeStruct((cfg.seq, 1), F32)],
        compiler_params=_params(("parallel",)),
    )(x, mo, g, b)


def _ffn_act(cfg, u, t):
    tm, cf = 256, _tile(cfg.ffn_p, 2048)

    def body(u_ref, t_ref, a_ref):
        uu = u_ref[...].astype(F32)
        a_ref[...] = (uu * _sigmoid(uu) * t_ref[...].astype(F32)).astype(a_ref.dtype)

    blk = pl.BlockSpec((tm, cf), lambda i, j: (i, j))
    return pl.pallas_call(
        body, name="ffn_act", grid=(cfg.seq // tm, cfg.ffn_p // cf), in_specs=[blk, blk], out_specs=blk,
        out_shape=jax.ShapeDtypeStruct((cfg.seq, cfg.ffn_p), BF16), compiler_params=_params(("parallel", "parallel")),
    )(u, t)


def _ffn_act_bwd(cfg, u, t, d_a):
    tm, cf = 256, _tile(cfg.ffn_p, 2048)

    def body(u_ref, t_ref, da_ref, du_ref, dt_ref):
        uu, da = u_ref[...].astype(F32), da_ref[...].astype(F32)
        sg = _sigmoid(uu)
        du_ref[...] = (da * t_ref[...].astype(F32) * (sg * (1.0 + uu * (1.0 - sg)))).astype(du_ref.dtype)
        dt_ref[...] = (da * uu * sg).astype(dt_ref.dtype)

    blk = pl.BlockSpec((tm, cf), lambda i, j: (i, j))
    out = jax.ShapeDtypeStruct((cfg.seq, cfg.ffn_p), BF16)
    return pl.pallas_call(
        body, name="ffn_act_bwd", grid=(cfg.seq // tm, cfg.ffn_p // cf), in_specs=[blk, blk, blk], out_specs=[blk, blk],
        out_shape=[out, out], compiler_params=_params(("parallel", "parallel")),
    )(u, t, d_a)


def _head(cfg, xhat1, ffn, hp, pu, tgt, g1, b1, g2, b2):
    tm, d = 64, cfg.d

    def body(xh_ref, ffn_ref, hp_ref, pu_ref, tgt_ref, g1_ref, b1_ref, g2_ref, b2_ref,
             loss_ref, dzf_ref, dzb_ref, dpg_ref, dpu_ref, gg_ref, gb_ref):
        i = pl.program_id(0)
        h1 = xh_ref[...] * g1_ref[...] + b1_ref[...]
        sg, pu = _sigmoid(hp_ref[...].astype(F32)), pu_ref[...].astype(F32)
        z = cfg.alpha * h1 + ffn_ref[...] + sg * pu
        cen = z - jnp.mean(z, axis=1, keepdims=True)
        rstd = lax.rsqrt(jnp.mean(cen * cen, axis=1, keepdims=True) + LN_EPS)
        xh2 = cen * rstd
        err = xh2 * g2_ref[...] + b2_ref[...] - tgt_ref[...]
        dy = err * (1.0 / d)
        part_l = jnp.sum(jnp.sum(err * err, axis=1, keepdims=True), axis=0, keepdims=True) * (0.5 / d)
        part_g = jnp.sum(dy * xh2, axis=0, keepdims=True)
        part_b = jnp.sum(dy, axis=0, keepdims=True)

        @pl.when(i == 0)
        def _():
            loss_ref[...] = jnp.zeros_like(loss_ref)
            gg_ref[...] = jnp.zeros_like(gg_ref)
            gb_ref[...] = jnp.zeros_like(gb_ref)

        loss_ref[...] += jnp.broadcast_to(part_l, loss_ref.shape)
        gg_ref[...] += part_g
        gb_ref[...] += part_b
        dxh = dy * g2_ref[...]
        m1 = jnp.mean(dxh, axis=1, keepdims=True)
        m2 = jnp.mean(dxh * xh2, axis=1, keepdims=True)
        dz = rstd * (dxh - m1 - xh2 * m2)
        dzf_ref[...] = dz
        dzb_ref[...] = dz.astype(dzb_ref.dtype)
        dpg_ref[...] = (dz * pu * sg * (1.0 - sg)).astype(dpg_ref.dtype)
        dpu_ref[...] = (dz * sg).astype(dpu_ref.dtype)

    row = pl.BlockSpec((tm, d), lambda i: (i, 0))
    vec = pl.BlockSpec((1, d), lambda i: (0, 0))
    bf = jax.ShapeDtypeStruct((cfg.seq, d), BF16)
    vec_out = jax.ShapeDtypeStruct((1, d), F32)
    return pl.pallas_call(
        body, name="head", grid=(cfg.seq // tm,), in_specs=[row] * 5 + [vec] * 4,
        out_specs=[pl.BlockSpec((1, 128), lambda i: (0, 0)), row, row, row, row, vec, vec],
        out_shape=[jax.ShapeDtypeStruct((1, 128), F32), jax.ShapeDtypeStruct((cfg.seq, d), F32), bf, bf, bf, vec_out, vec_out],
        compiler_params=_params(("arbitrary",)),
    )(xhat1, ffn, hp, pu, tgt, g1, b1, g2, b2)


def _ln1_bwd(cfg, dh_mm, dz2, xhat1, rstd1, g1):
    tm, d = 128, cfg.d

    def body(dh_ref, dz_ref, xh_ref, rstd_ref, g_ref, dzf_ref, dzb_ref, gg_ref, gb_ref):
        i = pl.program_id(0)
        dh = cfg.alpha * dz_ref[...] + dh_ref[...]
        xh = xh_ref[...]

        @pl.when(i == 0)
        def _():
            gg_ref[...] = jnp.zeros_like(gg_ref)
            gb_ref[...] = jnp.zeros_like(gb_ref)

        gg_ref[...] += jnp.sum(dh * xh, axis=0, keepdims=True)
        gb_ref[...] += jnp.sum(dh, axis=0, keepdims=True)
        dxh = dh * g_ref[...]
        m1 = jnp.mean(dxh, axis=1, keepdims=True)
        m2 = jnp.mean(dxh * xh, axis=1, keepdims=True)
        dz = rstd_ref[...] * (dxh - m1 - xh * m2)
        dzf_ref[...] = dz
        dzb_ref[...] = dz.astype(dzb_ref.dtype)

    row = pl.BlockSpec((tm, d), lambda i: (i, 0))
    vec = pl.BlockSpec((1, d), lambda i: (0, 0))
    vec_out = jax.ShapeDtypeStruct((1, d), F32)
    return pl.pallas_call(
        body, name="ln1_bwd", grid=(cfg.seq // tm,),
        in_specs=[row, row, row, pl.BlockSpec((tm, 1), lambda i: (i, 0)), vec], out_specs=[row, row, vec, vec],
        out_shape=[jax.ShapeDtypeStruct((cfg.seq, d), F32), jax.ShapeDtypeStruct((cfg.seq, d), BF16), vec_out, vec_out],
        compiler_params=_params(("arbitrary",)),
    )(dh_mm, dz2, xhat1, rstd1, g1)


def _adamw(w, g, m, v, name):
    r, c = w.shape
    tr, tc = _row_tile(r, c * 4, 2 * 1024 * 1024), c
    bc1, bc2 = 1.0 - ADAM_B1 ** ADAM_STEP, 1.0 - ADAM_B2 ** ADAM_STEP

    def body(w_ref, g_ref, m_ref, v_ref, d_ref, mo_ref, vo_ref):
        gg = g_ref[...]
        mn = ADAM_B1 * m_ref[...] + (1.0 - ADAM_B1) * gg
        vn = ADAM_B2 * v_ref[...] + (1.0 - ADAM_B2) * (gg * gg)
        d_ref[...] = -ADAM_LR * ((mn / bc1) / (jnp.sqrt(vn / bc2) + ADAM_EPS) + ADAM_WD * w_ref[...])
        mo_ref[...] = mn
        vo_ref[...] = vn

    blk = pl.BlockSpec((tr, tc), lambda i, j: (i, j))
    out = jax.ShapeDtypeStruct((r, c), F32)
    return pl.pallas_call(
        body, name=name, grid=(r // tr, c // tc), in_specs=[blk] * 4, out_specs=[blk] * 3, out_shape=[out] * 3,
        compiler_params=_params(("parallel", "parallel")),
    )(w, g, m, v)


def _place():
    x, y, c = lax.axis_index("x"), lax.axis_index("y"), lax.axis_index("c")
    others = [(1 - x, y), (x, 1 - y), (1 - x, 1 - y)]
    return x, y, c, others


def _all_gather(ws, ax, name):
    r, cdim = ws.shape
    full_shape = (4 * r, cdim) if ax == 0 else (r, 4 * cdim)
    n_shard = r if ax == 0 else cdim
    n_half = r // 2
    tr = _row_tile(n_half, cdim * 2, STAGE_BYTES, 16)
    nq = n_half // tr
    slots = 3

    def body(ws_ref, full_ref, buf, load_sems, write_sems, send_sems, pass_sems, recv_sems, sib_sem):
        x, y, c, others = _place()
        me = 2 * x + y
        sibling = (x, y, 1 - c)

        def region(chip, half, row0, rows):
            if ax == 0:
                return full_ref.at[pl.ds(chip * n_shard + half * n_half + row0, rows), :]
            return full_ref.at[pl.ds(half * n_half + row0, rows), pl.ds(chip * n_shard, n_shard)]

        pending = {}

        def free(slot):
            for cp, remote in pending.pop(slot, []):
                if remote:
                    cp.wait_send()
                else:
                    cp.wait()

        def load(src, slot):
            free(slot)
            cp = pltpu.make_async_copy(src, buf.at[slot], load_sems.at[slot])
            cp.start()
            cp.wait()

        step = 0
        for mine in (True, False):
            half = c if mine else 1 - c
            for q in range(nq):
                slot = step % slots
                load(ws_ref.at[pl.ds(half * n_half + q * tr, tr), :], slot)
                dst = region(me, half, q * tr, tr)
                cp = pltpu.make_async_copy(buf.at[slot], dst, write_sems.at[slot])
                cp.start()
                pending[slot] = [(cp, False)]
                if mine:
                    for k, chip in enumerate(others):
                        cp = pltpu.make_async_remote_copy(
                            src_ref=buf.at[slot], dst_ref=dst, send_sem=send_sems.at[3 * slot + k], recv_sem=recv_sems.at[k],
                            device_id=(*chip, c), device_id_type=MESH)
                        cp.start()
                        pending[slot].append((cp, True))
                step += 1
        for k, chip in enumerate(others):
            j = 2 * chip[0] + chip[1]
            landed = region(j, c, 0, n_half)
            pltpu.make_async_remote_copy(src_ref=landed, dst_ref=landed, send_sem=send_sems.at[0], recv_sem=recv_sems.at[k],
                                         device_id=(x, y, c), device_id_type=MESH).wait_recv()
            for q in range(nq):
                slot = step % slots
                part = region(j, c, q * tr, tr)
                load(part, slot)
                cp = pltpu.make_async_remote_copy(src_ref=buf.at[slot], dst_ref=part, send_sem=pass_sems.at[slot],
                                                  recv_sem=sib_sem, device_id=sibling, device_id_type=MESH)
                cp.start()
                pending[slot] = [(cp, True)]
                step += 1
        for slot in list(pending):
            free(slot)
        if ax == 0:
            three = full_ref.at[pl.ds(0, 3 * n_half), :]
        else:
            three = full_ref.at[pl.ds(0, n_half), pl.ds(0, 3 * n_shard)]
        pltpu.make_async_remote_copy(src_ref=three, dst_ref=three, send_sem=send_sems.at[0], recv_sem=sib_sem,
                                     device_id=(x, y, c), device_id_type=MESH).wait_recv()

    return pl.pallas_call(
        body, name=name, in_specs=[ANY], out_specs=ANY, out_shape=jax.ShapeDtypeStruct(full_shape, ws.dtype),
        scratch_shapes=[pltpu.VMEM((slots, tr, cdim), ws.dtype), pltpu.SemaphoreType.DMA((slots,)),
                        pltpu.SemaphoreType.DMA((slots,)), pltpu.SemaphoreType.DMA((3 * slots,)),
                        pltpu.SemaphoreType.DMA((slots,)), pltpu.SemaphoreType.DMA((3,)), pltpu.SemaphoreType.DMA],
        compiler_params=pltpu.CompilerParams(vmem_limit_bytes=VMEM_LIMIT),
    )(ws)


def _shard_dims(shape, ax):
    rf, cf = shape
    n_shard = (rf if ax == 0 else cf) // 4
    return n_shard, (n_shard if ax == 0 else rf) // 2, (cf if ax == 0 else n_shard)


def _mm_tn_pair(a, b, ax, name):
    (k, m), (k2, n) = a.shape, b.shape
    assert k == k2
    n_shard, n_half, _ = _shard_dims((m, n), ax)
    tm = _tile(n_half, 1024)
    if tm < 512:
        tm = n_half
    tn, tk = _tile(n, 1024), _tile(k, 512)
    ni, nj, nk = m // tm, n // tn, k // tk
    ntiles = ni * nj
    assert n_half % tm == 0 and ntiles >= 2

    def body(a_ref, b_ref, own_ref, recv_ref, acc_ref, stage, local_sems, send_sems, recv_sem):
        i, j, kk = pl.program_id(0), pl.program_id(1), pl.program_id(2)
        x, y, c, _ = _place()
        sibling = (x, y, 1 - c)
        t = i * nj + j

        def is_mine(ii):
            row0 = ii * tm
            half = row0 // n_half if ax == 1 else (row0 % n_shard) // n_half
            return half == c

        def copies(tt):
            ii, jj, slot = tt // nj, tt % nj, tt % 2
            where = (pl.ds(ii * tm, tm), pl.ds(jj * tn, tn))
            local = pltpu.make_async_copy(stage.at[slot], own_ref.at[where], local_sems.at[slot])
            remote = pltpu.make_async_remote_copy(src_ref=stage.at[slot], dst_ref=recv_ref.at[where], send_sem=send_sems.at[slot],
                                                  recv_sem=recv_sem, device_id=sibling, device_id_type=MESH)
            return is_mine(ii), local, remote

        def wait_tile(tt):
            mine, local, remote = copies(tt)

            @pl.when(mine)
            def _():
                local.wait()

            @pl.when(jnp.logical_not(mine))
            def _():
                remote.wait_send()

        @pl.when(kk == 0)
        def _():
            acc_ref[...] = jnp.zeros_like(acc_ref)

        acc_ref[...] += lax.dot_general(a_ref[...], b_ref[...], _TN, preferred_element_type=F32)

        @pl.when(kk == nk - 1)
        def _():
            @pl.when(t >= 2)
            def _():
                wait_tile(t - 2)

            stage[t % 2] = acc_ref[...].astype(stage.dtype)
            mine, local, remote = copies(t)

            @pl.when(mine)
            def _():
                local.start()

            @pl.when(jnp.logical_not(mine))
            def _():
                remote.start()

            @pl.when(t == ntiles - 1)
            def _():
                wait_tile(t - 1)
                wait_tile(t)
                half_rows = recv_ref.at[pl.ds(0, m // 2), :]
                pltpu.make_async_remote_copy(src_ref=half_rows, dst_ref=half_rows, send_sem=send_sems.at[0], recv_sem=recv_sem,
                                             device_id=sibling, device_id_type=MESH).wait_recv()

    out = jax.ShapeDtypeStruct((m, n), BF16)
    return pl.pallas_call(
        body, name=name, grid=(ni, nj, nk),
        in_specs=[pl.BlockSpec((tk, tm), lambda i, j, kk: (kk, i)), pl.BlockSpec((tk, tn), lambda i, j, kk: (kk, j))],
        out_specs=[ANY, ANY], out_shape=[out, out],
        scratch_shapes=[pltpu.VMEM((tm, tn), F32), pltpu.VMEM((2, tm, tn), BF16), pltpu.SemaphoreType.DMA((2,)),
                        pltpu.SemaphoreType.DMA((2,)), pltpu.SemaphoreType.DMA],
        compiler_params=_params(("arbitrary", "arbitrary", "arbitrary")),
    )(a, b)


def _other_chip(k, x, y):
    return jnp.where(k == 1, x, 1 - x), jnp.where(k == 0, y, 1 - y)


def _pairsum_scatter(own, recv, ax, coords, name):
    n_shard, n_half, ncol = _shard_dims(own.shape, ax)
    tr = _row_tile(n_half, ncol * 2, STAGE_BYTES // 2, 16)
    nr = n_half // tr

    def src_map(k):
        def index(r, co):
            jx, jy = _other_chip(k, co[0], co[1])
            j = 2 * jx + jy
            if ax == 0:
                return ((j * n_shard + co[2] * n_half) // tr + r, 0)
            return (co[2] * nr + r, j)
        return index

    def body(co_ref, o0, r0, o1, r1, o2, r2, out_ref, sbuf, send_sems, recv_sems):
        r = pl.program_id(0)
        x, y, c, others = _place()
        par = r % 2

        def send(k, slot):
            return pltpu.make_async_remote_copy(
                src_ref=sbuf.at[slot], dst_ref=out_ref.at[k, pl.ds(r * tr, tr), :], send_sem=send_sems.at[slot],
                recv_sem=recv_sems.at[k], device_id=(*others[k], c), device_id_type=MESH)

        for k, (own_ref, recv_ref) in enumerate(((o0, r0), (o1, r1), (o2, r2))):
            slot = 3 * par + k

            @pl.when(r >= 2)
            def _():
                send(k, slot).wait_send()

            sbuf[slot] = (own_ref[...].astype(F32) + recv_ref[...].astype(F32)).astype(sbuf.dtype)
            send(k, slot).start()

        @pl.when(r == nr - 1)
        def _():
            for k in range(3):
                send(k, 3 * par + k).wait_send()
                if nr >= 2:
                    send(k, 3 * (1 - par) + k).wait_send()
            for k in range(3):
                whole = out_ref.at[k]
                pltpu.make_async_remote_copy(src_ref=whole, dst_ref=whole, send_sem=send_sems.at[0], recv_sem=recv_sems.at[k],
                                             device_id=(x, y, c), device_id_type=MESH).wait_recv()

    blks = [pl.BlockSpec((tr, ncol), src_map(k)) for k in range(3) for _ in range(2)]
    return pl.pallas_call(
        body, name=name,
        grid_spec=pltpu.PrefetchScalarGridSpec(
            num_scalar_prefetch=1, grid=(nr,), in_specs=blks, out_specs=ANY,
            scratch_shapes=[pltpu.VMEM((6, tr, ncol), BF16), pltpu.SemaphoreType.DMA((6,)), pltpu.SemaphoreType.DMA((3,))]),
        out_shape=jax.ShapeDtypeStruct((3, n_half, ncol), BF16),
        compiler_params=_params(("arbitrary",)),
    )(coords, own, recv, own, recv, own, recv)


def _sum_share(own, recv, got, ax, coords, name):
    n_shard, n_half, ncol = _shard_dims(own.shape, ax)
    tr = _row_tile(n_half, ncol * 4, STAGE_BYTES, 16)
    nr = n_half // tr

    def src_map(r, co):
        me = 2 * co[0] + co[1]
        if ax == 0:
            return ((me * n_shard + co[2] * n_half) // tr + r, 0)
        return (co[2] * nr + r, me)

    def body(co_ref, own_ref, recv_ref, got_ref, out_ref, sbuf, local_sems, send_sems, recv_sem):
        r = pl.program_id(0)
        x, y, c, _ = _place()

        def copies(slot):
            rows = out_ref.at[pl.ds(c * n_half + r * tr, tr), :]
            local = pltpu.make_async_copy(sbuf.at[slot], rows, local_sems.at[slot])
            remote = pltpu.make_async_remote_copy(src_ref=sbuf.at[slot], dst_ref=rows, send_sem=send_sems.at[slot],
                                                  recv_sem=recv_sem, device_id=(x, y, 1 - c), device_id_type=MESH)
            return local, remote

        def wait_slot(slot):
            local, remote = copies(slot)
            local.wait()
            remote.wait_send()

        @pl.when(r >= 2)
        def _():
            wait_slot(r % 2)

        total = own_ref[...].astype(F32) + recv_ref[...].astype(F32)
        for kk in range(3):
            total = total + got_ref[kk].astype(F32)
        sbuf[r % 2] = total
        local, remote = copies(r % 2)
        local.start()
        remote.start()

        @pl.when(r == nr - 1)
        def _():
            wait_slot(r % 2)
            if nr >= 2:
                wait_slot(1 - r % 2)
            half = out_ref.at[pl.ds(0, n_half), :]
            pltpu.make_async_remote_copy(src_ref=half, dst_ref=half, send_sem=send_sems.at[0], recv_sem=recv_sem,
                                         device_id=(x, y, c), device_id_type=MESH).wait_recv()

    blk = pl.BlockSpec((tr, ncol), src_map)
    return pl.pallas_call(
        body, name=name,
        grid_spec=pltpu.PrefetchScalarGridSpec(
            num_scalar_prefetch=1, grid=(nr,), in_specs=[blk, blk, pl.BlockSpec((3, tr, ncol), lambda r, co: (0, r, 0))],
            out_specs=ANY,
            scratch_shapes=[pltpu.VMEM((2, tr, ncol), F32), pltpu.SemaphoreType.DMA((2,)), pltpu.SemaphoreType.DMA((2,)),
                            pltpu.SemaphoreType.DMA]),
        out_shape=jax.ShapeDtypeStruct((2 * n_half, ncol), F32),
        compiler_params=_params(("arbitrary",)),
    )(coords, own, recv, got)


def _reduce_scatter(own, recv, ax, coords, name):
    got = _pairsum_scatter(own, recv, ax, coords, name + "_scatter")
    return _sum_share(own, recv, got, ax, coords, name + "_share")


def _vec_all_reduce(v, name):
    rows, d = v.shape

    def body(v_ref, out_ref, gath_ref, send_sems, recv_sems):
        x, y, c, _ = _place()
        me = 4 * x + 2 * y + c
        gath_ref[me] = v_ref[...]
        flips = [(fx, fy, fc) for fx in (0, 1) for fy in (0, 1) for fc in (0, 1)][1:]
        sends = []
        for k, (fx, fy, fc) in enumerate(flips):
            peer = (1 - x if fx else x, 1 - y if fy else y, 1 - c if fc else c)
            sends.append(pltpu.make_async_remote_copy(
                src_ref=v_ref, dst_ref=gath_ref.at[me], send_sem=send_sems.at[k], recv_sem=recv_sems.at[k],
                device_id=peer, device_id_type=MESH))
        for cp in sends:
            cp.start()
        for cp in sends:
            cp.wait()
        acc = gath_ref[0]
        for dev in range(1, 8):
            acc = acc + gath_ref[dev]
        out_ref[...] = acc

    vm = pl.BlockSpec(memory_space=pltpu.VMEM)
    return pl.pallas_call(
        body, name=name, in_specs=[vm], out_specs=vm, out_shape=jax.ShapeDtypeStruct((rows, d), F32),
        scratch_shapes=[pltpu.VMEM((8, rows, d), F32), pltpu.SemaphoreType.DMA((7,)), pltpu.SemaphoreType.DMA((7,))],
    )(v)


_SHARD_AXIS = {"w_in": 1, "w_attn_out": 1, "w_ret_out": 0, "w_o": 0, "w_ffn_gate": 1, "w_ffn_up": 1, "w_ffn_down": 0,
               "w_ple_gate": 0, "w_ple_up": 1}
_FFN_HIDDEN_AXIS = {"w_ffn_gate": 1, "w_ffn_up": 1, "w_ffn_down": 0}
_VECTORS = ("ret_gn_g", "ln1_g", "ln1_b", "ln2_g", "ln2_b")
_WEIGHTS = ("w_in", "w_attn_out", "w_ret_out", "ret_gn_g", "w_o", "ln1_g", "ln1_b", "w_ffn_gate", "w_ffn_up", "w_ffn_down",
            "w_ple_gate", "w_ple_up", "ln2_g", "ln2_b")


def _plain_grad(a, b, weight, name):
    return _mm(a, b, "tn", BF16, name)


def _local_grads(cfg, x, p, tgt, wf, vec, grad_mm=_plain_grad):
    xb, pb = x.astype(BF16), p.astype(BF16)
    cos, sin = _rope_tables(cfg)
    lg = _log_gamma(cfg)

    proj = _mm(xb, wf["w_in"], "nn", BF16, "proj")
    q3 = _to_groups(cfg, proj[:, cfg.o_qa:cfg.o_qa + cfg.att_w])
    k3 = _to_groups(cfg, proj[:, cfg.o_ka:cfg.o_ka + cfg.att_w])
    v3 = _to_groups(cfg, proj[:, cfg.o_va:cfg.o_va + cfg.att_w])
    o3, lse3 = _attn_fwd(cfg, q3, k3, v3)
    o_att, lse = _attn_combine(cfg, jnp.stack(_from_groups(o3)), jnp.stack(_from_groups(lse3)))
    y_att = _mm(o_att, wf["w_attn_out"], "nn", BF16, "y_att")
    rn, rstd_r, states = _ret_fwd(cfg, proj, cos, sin, lg)
    rg = _ret_gate(cfg, proj, rn, vec["ret_gn_g"])
    y_ret = _mm(rg, wf["w_ret_out"], "nn", BF16, "y_ret")
    mixed = _mix(cfg, proj, y_att, y_ret)
    mo = _mm(mixed, wf["w_o"], "nn", F32, "mixed_out")
    xhat1, h1, rstd1 = _ln1(cfg, x, mo, vec["ln1_g"], vec["ln1_b"])
    u = _mm(h1, wf["w_ffn_gate"], "nn", BF16, "ffn_u")
    t = _mm(h1, wf["w_ffn_up"], "nn", BF16, "ffn_t")
    a = _ffn_act(cfg, u, t)
    ffn = _mm(a, wf["w_ffn_down"], "nn", F32, "ffn_down")
    hp = _mm(h1, wf["w_ple_gate"], "nn", BF16, "ple_gate")
    pu = _mm(pb, wf["w_ple_up"], "nn", BF16, "ple_up")
    loss, dz2, dz2b, d_pg, d_pu, g_ln2g, g_ln2b = _head(cfg, xhat1, ffn, hp, pu, tgt, vec["ln1_g"], vec["ln1_b"],
                                                        vec["ln2_g"], vec["ln2_b"])

    grads = {}
    d_a = _mm(dz2b, wf["w_ffn_down"], "nt", BF16, "d_a")
    d_u, d_t = _ffn_act_bwd(cfg, u, t, d_a)
    grads["w_ffn_down"] = grad_mm(a, dz2b, "w_ffn_down", "g_ffn_down")
    grads["w_ffn_gate"] = grad_mm(h1, d_u, "w_ffn_gate", "g_ffn_gate")
    grads["w_ffn_up"] = grad_mm(h1, d_t, "w_ffn_up", "g_ffn_up")
    grads["w_ple_gate"] = grad_mm(h1, d_pg, "w_ple_gate", "g_ple_gate")
    grads["w_ple_up"] = grad_mm(pb, d_pu, "w_ple_up", "g_ple_up")
    dh = _mm(d_u, wf["w_ffn_gate"], "nt", F32, "dh_u")
    dh = _mm(d_t, wf["w_ffn_up"], "nt", F32, "dh_t", add=dh)
    dh = _mm(d_pg, wf["w_ple_gate"], "nt", F32, "dh_pg", add=dh)
    dz1, dz1b, g_ln1g, g_ln1b = _ln1_bwd(cfg, dh, dz2, xhat1, rstd1, vec["ln1_g"])
    d_mixed = _mm(dz1b, wf["w_o"], "nt", BF16, "d_mixed")
    grads["w_o"] = grad_mm(mixed, dz1b, "w_o", "g_o")
    d_ya, d_yr, d_ga, d_g2 = _mix_bwd(cfg, proj, y_att, y_ret, d_mixed)
    d_oatt = _mm(d_ya, wf["w_attn_out"], "nt", BF16, "d_oatt")
    grads["w_attn_out"] = grad_mm(o_att, d_ya, "w_attn_out", "g_attn_out")
    d_rg = _mm(d_yr, wf["w_ret_out"], "nt", BF16, "d_rg")
    grads["w_ret_out"] = grad_mm(rg, d_yr, "w_ret_out", "g_ret_out")
    d_gr, d_r, g_gn = _ret_gate_bwd(cfg, proj, rn, vec["ret_gn_g"], rstd_r, d_rg)
    d_qr, d_kr, d_vr = _ret_bwd(cfg, proj, d_r, states, cos, sin, lg)
    delta = _attn_delta(cfg, d_oatt, o_att)
    do3 = jnp.stack([_deinterleave(d_oatt, dil) for _, dil in ATT_GROUPS])
    l3 = jnp.stack([_deinterleave(lse, dil) for _, dil in ATT_GROUPS])
    dl3 = jnp.stack([_deinterleave(delta, dil) for _, dil in ATT_GROUPS])
    dq3, dk3, dv3 = _attn_bwd(cfg, q3, k3, v3, do3, l3, dl3)
    dproj = jnp.concatenate(_from_groups(dq3) + _from_groups(dk3) + _from_groups(dv3) + [d_qr, d_kr, d_vr, d_gr, d_ga, d_g2], axis=1)
    grads["w_in"] = grad_mm(xb, dproj, "w_in", "g_in")
    grad_x = _mm(dproj, wf["w_in"], "nt", F32, "grad_x", add=dz1, add_scale=cfg.alpha)
    vgrads = {"ret_gn_g": g_gn, "ln1_g": g_ln1g, "ln1_b": g_ln1b, "ln2_g": g_ln2g, "ln2_b": g_ln2b}
    return loss, grad_x, grads, vgrads


def _step(cfg, x, p, tgt, w, m, v):
    mats = [n for n in _WEIGHTS if n in _SHARD_AXIS]
    ffn_pad = cfg.ffn_shard - cfg.ffn // 4

    def shard_bf16(n):
        ws = w[n][0].astype(BF16)
        if n in _FFN_HIDDEN_AXIS and ffn_pad:
            ws = jnp.pad(ws, [(0, ffn_pad) if a == _FFN_HIDDEN_AXIS[n] else (0, 0) for a in (0, 1)])
        return ws

    def unpad(n, t):
        if n in _FFN_HIDDEN_AXIS and ffn_pad:
            return t[:cfg.ffn // 4] if _FFN_HIDDEN_AXIS[n] == 0 else t[:, :cfg.ffn // 4]
        return t

    wf = {n: _all_gather(shard_bf16(n), _SHARD_AXIS[n], "ag_" + n) for n in mats}
    vec = {n: w[n] for n in _VECTORS}
    def pair_grad(a, b, weight, name):
        return _mm_tn_pair(a, b, _SHARD_AXIS[weight], name)

    loss, grad_x, grads, vgrads = _local_grads(cfg, x[0], p[0, 0], tgt[0], wf, vec, pair_grad)
    loss = lax.psum(loss[0, 0], ("x", "y", "c"))
    coords = jnp.stack([lax.axis_index("x"), lax.axis_index("y"), lax.axis_index("c")]).astype(jnp.int32)
    red = {n: unpad(n, _reduce_scatter(*grads[n], _SHARD_AXIS[n], coords, "rs_" + n)) for n in mats}
    stacked = jnp.concatenate([vgrads[n] for n in _VECTORS] + [jnp.zeros((3, cfg.d), F32)], axis=0)
    vsum = _vec_all_reduce(stacked, "vec_all_reduce")
    for i, n in enumerate(_VECTORS):
        red[n] = vsum[i:i + 1]
    g_out, d_out, m_out, v_out = [], [], [], []
    for n in _WEIGHTS:
        dlt, mn, vn = _adamw(w[n][0] if n in _SHARD_AXIS else w[n], red[n], m[n][0] if n in _SHARD_AXIS else m[n],
                             v[n][0] if n in _SHARD_AXIS else v[n], "adamw_" + n)
        lead = (lambda t: t[None]) if n in _SHARD_AXIS else (lambda t: t)
        g_out.append(lead(red[n]))
        d_out.append(lead(dlt))
        m_out.append(lead(mn))
        v_out.append(lead(vn))
    return (loss, grad_x[None], *g_out, *d_out, *m_out, *v_out)


def kernel(x, p, w_in, w_attn_out, w_ret_out, ret_gn_g, w_o, ln1_g, ln1_b, w_ffn_gate, w_ffn_up, w_ffn_down, w_ple_gate, w_ple_up, ln2_g, ln2_b, loss_target, m_w_in, m_w_attn_out, m_w_ret_out, m_ret_gn_g, m_w_o, m_ln1_g, m_ln1_b, m_w_ffn_gate, m_w_ffn_up, m_w_ffn_down, m_w_ple_gate, m_w_ple_up, m_ln2_g, m_ln2_b, v_w_in, v_w_attn_out, v_w_ret_out, v_ret_gn_g, v_w_o, v_ln1_g, v_ln1_b, v_w_ffn_gate, v_w_ffn_up, v_w_ffn_down, v_w_ple_gate, v_w_ple_up, v_ln2_g, v_ln2_b):
    w = dict(zip(_WEIGHTS, (w_in, w_attn_out, w_ret_out, ret_gn_g, w_o, ln1_g, ln1_b, w_ffn_gate, w_ffn_up, w_ffn_down,
                            w_ple_gate, w_ple_up, ln2_g, ln2_b)))
    m = dict(zip(_WEIGHTS, (m_w_in, m_w_attn_out, m_w_ret_out, m_ret_gn_g, m_w_o, m_ln1_g, m_ln1_b, m_w_ffn_gate, m_w_ffn_up,
                            m_w_ffn_down, m_w_ple_gate, m_w_ple_up, m_ln2_g, m_ln2_b)))
    v = dict(zip(_WEIGHTS, (v_w_in, v_w_attn_out, v_w_ret_out, v_ret_gn_g, v_w_o, v_ln1_g, v_ln1_b, v_w_ffn_gate, v_w_ffn_up,
                            v_w_ffn_down, v_w_ple_gate, v_w_ple_up, v_ln2_g, v_ln2_b)))
    return _step(_FULL, x, p, loss_target, w, m, v)
```

```python
import functools
import math

import jax
import jax.numpy as jnp
import numpy as np
from jax import lax
from jax.experimental import pallas as pl
from jax.experimental.pallas import tpu as pltpu

F32 = jnp.float32
BF16 = jnp.bfloat16
MESH = pl.DeviceIdType.MESH
ANY = pl.BlockSpec(memory_space=pl.ANY)

ATT_BLOCK = 128
ATT_GROUPS = ((128, 1), (512, 4), (2048, 16))
LN_EPS = 1e-5
GN_EPS = 1e-6
NEG_INF = -1e30
ROPE_BASE = 10000.0
ADAM_LR, ADAM_B1, ADAM_B2, ADAM_EPS, ADAM_WD, ADAM_STEP = 0.001, 0.9, 0.999, 1e-08, 0.01, 10
VMEM_LIMIT = 56 * 1024 * 1024
STAGE_BYTES = 4 * 1024 * 1024


class _Cfg:
    def __init__(self, seq, d, ple, ahd, ahg, rh, rqk, rv, ffn, cw):
        self.seq, self.d, self.ple, self.ahd, self.ahg = seq, d, ple, ahd, ahg
        self.rh, self.rqk, self.rv, self.ffn, self.cw = rh, rqk, rv, ffn, cw
        self.aw = ahg * ahd
        self.att_w = 3 * self.aw
        self.rqk_w = rh * rqk
        self.rv_w = rh * rv
        offs = np.cumsum([0] + [self.att_w] * 3 + [self.rqk_w] * 2 + [self.rv_w] * 2 + [d] * 2)
        (self.o_qa, self.o_ka, self.o_va, self.o_qr, self.o_kr, self.o_vr, self.o_gr, self.o_ga, self.o_g2,
         self.in_w) = [int(v) for v in offs]
        self.alpha = 2.0 ** 0.25
        self.ffn_shard = -(-(ffn // 4) // 128) * 128
        self.ffn_p = 4 * self.ffn_shard
        assert self.rv_w == d and d % cw == 0
        for o in (self.o_gr, self.o_ga, self.o_g2):
            assert o % cw == 0
        assert self.o_qr % rqk == 0 and self.o_kr % rqk == 0 and self.o_vr % rv == 0 and self.o_gr % rv == 0
        assert rqk // 2 % 128 == 0 and seq % (ATT_BLOCK * 16) == 0


_FULL = _Cfg(seq=4096, d=4096, ple=256, ahd=128, ahg=8, rh=8, rqk=256, rv=512, ffn=11008, cw=1024)


def _tile(n, target, q=128):
    t = min(n, target) // q * q
    while t >= q:
        if n % t == 0:
            return t
        t -= q
    return n


def _row_tile(n, row_bytes, target_bytes, q=8):
    best = None
    for t in range(q, n + 1, q):
        if n % t == 0 and (best is None or t * row_bytes <= target_bytes):
            best = t
            if t * row_bytes > target_bytes:
                break
    return n if best is None else best


def _params(sem=None):
    return pltpu.CompilerParams(dimension_semantics=sem, vmem_limit_bytes=VMEM_LIMIT)


def _sigmoid(x):
    return 1.0 / (1.0 + jnp.exp(-x))


class _Job:
    def __init__(self, inputs, out_shapes, n_sems, copies):
        self.inputs, self.out_shapes, self.n_sems, self.copies = list(inputs), list(out_shapes), n_sems, copies


def _mm(a, b, mode, out_dtype, name, add=None, add_scale=1.0, jobs=()):
    if mode == "nn":
        (m, k), (k2, n) = a.shape, b.shape
    elif mode == "nt":
        (m, k), (n, k2) = a.shape, b.shape
    else:
        (k, m), (k2, n) = a.shape, b.shape
    assert k == k2, (name, a.shape, b.shape)
    tm, tn, tk = _tile(m, 1024), _tile(n, 1024), _tile(k, 512)
    nk = k // tk
    if mode == "tn":
        a_spec = pl.BlockSpec((tk, tm), lambda i, j, kk: (kk, i))
        dims = (((0,), (0,)), ((), ()))
    else:
        a_spec = pl.BlockSpec((tm, tk), lambda i, j, kk: (i, kk))
        dims = (((1,), (1,)), ((), ())) if mode == "nt" else (((1,), (0,)), ((), ()))
    if mode == "nt":
        b_spec = pl.BlockSpec((tn, tk), lambda i, j, kk: (j, kk))
    else:
        b_spec = pl.BlockSpec((tk, tn), lambda i, j, kk: (kk, j))
    o_spec = pl.BlockSpec((tm, tn), lambda i, j, kk: (i, j))
    has_add = add is not None
    ni, nj = m // tm, n // tn
    jobs = list(jobs)
    n_main_in = 3 if has_add else 2
    n_job_in = sum(len(jb.inputs) for jb in jobs)
    n_job_out = sum(len(jb.out_shapes) for jb in jobs)

    def body(*refs):
        a_ref, b_ref = refs[0], refs[1]
        add_ref = refs[2] if has_add else None
        job_in = refs[n_main_in:n_main_in + n_job_in]
        o_ref = refs[n_main_in + n_job_in]
        job_out = refs[n_main_in + n_job_in + 1:n_main_in + n_job_in + 1 + n_job_out]
        acc_ref = refs[n_main_in + n_job_in + 1 + n_job_out]
        job_sems = refs[n_main_in + n_job_in + 2 + n_job_out:]
        i, j, kk = pl.program_id(0), pl.program_id(1), pl.program_id(2)

        def job_copies():
            found, pi, po = [], 0, 0
            for jb, sems in zip(jobs, job_sems):
                found += jb.copies(job_in[pi:pi + len(jb.inputs)], job_out[po:po + len(jb.out_shapes)], sems)
                pi, po = pi + len(jb.inputs), po + len(jb.out_shapes)
            return found

        if jobs:
            @pl.when(jnp.logical_and(jnp.logical_and(i == 0, j == 0), kk == 0))
            def _():
                for cp, _ in job_copies():
                    cp.start()

        @pl.when(kk == 0)
        def _():
            acc_ref[...] = jnp.zeros_like(acc_ref)

        acc_ref[...] += lax.dot_general(a_ref[...], b_ref[...], dims, preferred_element_type=F32)

        @pl.when(kk == nk - 1)
        def _():
            r = acc_ref[...]
            if has_add:
                r = r + add_scale * add_ref[...]
            o_ref[...] = r.astype(o_ref.dtype)

        if jobs:
            @pl.when(jnp.logical_and(jnp.logical_and(i == ni - 1, j == nj - 1), kk == nk - 1))
            def _():
                for cp, _ in job_copies():
                    cp.wait()

    job_inputs = [t for jb in jobs for t in jb.inputs]
    outs = pl.pallas_call(
        body, name=name, grid=(ni, nj, nk),
        in_specs=[a_spec, b_spec] + ([o_spec] if has_add else []) + [ANY] * n_job_in, out_specs=[o_spec] + [ANY] * n_job_out,
        out_shape=[jax.ShapeDtypeStruct((m, n), out_dtype)] + [s for jb in jobs for s in jb.out_shapes],
        scratch_shapes=[pltpu.VMEM((tm, tn), F32)] + [pltpu.SemaphoreType.DMA((jb.n_sems,)) for jb in jobs],
        compiler_params=_params(("arbitrary",) * 3 if jobs else ("parallel", "parallel", "arbitrary")),
    )(*((a, b, add) if has_add else (a, b)), *job_inputs)
    return outs if jobs else outs[0]


def _seg_blocks(cfg):
    return [cfg.seq // ATT_BLOCK // dil for _, dil in ATT_GROUPS]


def _group_select(g, vals):
    out = jnp.int32(vals[-1])
    for i in range(len(vals) - 2, -1, -1):
        out = jnp.where(g == i, jnp.int32(vals[i]), out)
    return out


def _col(tile, h):
    lane = lax.broadcasted_iota(jnp.int32, tile.shape, 1)
    return jnp.sum(jnp.where(lane == h, tile, 0.0), axis=1, keepdims=True)


def _set_col(tile, h, col):
    lane = lax.broadcasted_iota(jnp.int32, tile.shape, 1)
    return jnp.where(lane == h, col, tile)


_NT = (((1,), (1,)), ((), ()))
_TN = (((0,), (0,)), ((), ()))


def _dot(a, b):
    return jnp.dot(a, b, preferred_element_type=F32)


def _dot_nt(a, b):
    return lax.dot_general(a, b, _NT, preferred_element_type=F32)


def _dot_tn(a, b):
    return lax.dot_general(a, b, _TN, preferred_element_type=F32)


def _attn_fwd(cfg, q3, k3, v3):
    nb = cfg.seq // ATT_BLOCK
    segs = _seg_blocks(cfg)
    scale = cfg.ahd ** -0.5
    blk = ATT_BLOCK

    def body(q_ref, kp_ref, kc_ref, vp_ref, vc_ref, o_ref, lse_ref):
        g, b = pl.program_id(0), pl.program_id(1)
        has_prev = (b & (_group_select(g, segs) - 1)) != 0
        qi = lax.broadcasted_iota(jnp.int32, (blk, blk), 0)
        kj = lax.broadcasted_iota(jnp.int32, (blk, blk), 1)
        valid_c = kj <= qi
        valid_p = jnp.logical_and(kj >= qi, has_prev)
        lse = jnp.zeros((blk, cfg.ahg), F32)
        for h in range(cfg.ahg):
            hs = slice(h * cfg.ahd, (h + 1) * cfg.ahd)
            q = q_ref[:, hs]
            s_c = jnp.where(valid_c, _dot_nt(q, kc_ref[:, hs]) * scale, NEG_INF)
            s_p = jnp.where(valid_p, _dot_nt(q, kp_ref[:, hs]) * scale, NEG_INF)
            m = jnp.maximum(jnp.max(s_c, axis=1, keepdims=True), jnp.max(s_p, axis=1, keepdims=True))
            p_c, p_p = jnp.exp(s_c - m), jnp.exp(s_p - m)
            den = jnp.sum(p_c, axis=1, keepdims=True) + jnp.sum(p_p, axis=1, keepdims=True)
            o = _dot(p_c.astype(BF16), vc_ref[:, hs]) + _dot(p_p.astype(BF16), vp_ref[:, hs])
            o_ref[:, hs] = (o / den).astype(o_ref.dtype)
            lse = _set_col(lse, h, m + jnp.log(den))
        lse_ref[...] = lse

    cur = pl.BlockSpec((None, blk, cfg.aw), lambda g, b: (g, b, 0))
    prev = pl.BlockSpec((None, blk, cfg.aw), lambda g, b: (g, jnp.maximum(b - 1, 0), 0))
    return pl.pallas_call(
        body, name="attn_fwd", grid=(3, nb), in_specs=[cur, prev, cur, prev, cur],
        out_specs=[cur, pl.BlockSpec((None, blk, cfg.ahg), lambda g, b: (g, b, 0))],
        out_shape=[jax.ShapeDtypeStruct((3, cfg.seq, cfg.aw), BF16), jax.ShapeDtypeStruct((3, cfg.seq, cfg.ahg), F32)],
        compiler_params=_params(("parallel", "parallel")),
    )(q3, k3, k3, v3, v3)


def _attn_combine(cfg, o3, lse3):
    tm = 256

    def body(o_ref, lse_ref, oa_ref, l_ref):
        l0, l1, l2 = lse_ref[0], lse_ref[1], lse_ref[2]
        m = jnp.maximum(jnp.maximum(l0, l1), l2)
        big = m + jnp.log(jnp.exp(l0 - m) + jnp.exp(l1 - m) + jnp.exp(l2 - m))
        ws = [jnp.exp(l0 - big), jnp.exp(l1 - big), jnp.exp(l2 - big)]
        for h in range(cfg.ahg):
            hs = slice(h * cfg.ahd, (h + 1) * cfg.ahd)
            acc = _col(ws[0], h) * o_ref[0, :, hs].astype(F32)
            acc += _col(ws[1], h) * o_ref[1, :, hs].astype(F32)
            acc += _col(ws[2], h) * o_ref[2, :, hs].astype(F32)
            oa_ref[:, hs] = acc.astype(oa_ref.dtype)
        l_ref[...] = big

    return pl.pallas_call(
        body, name="attn_combine", grid=(cfg.seq // tm,),
        in_specs=[pl.BlockSpec((3, tm, cfg.aw), lambda i: (0, i, 0)), pl.BlockSpec((3, tm, cfg.ahg), lambda i: (0, i, 0))],
        out_specs=[pl.BlockSpec((tm, cfg.aw), lambda i: (i, 0)), pl.BlockSpec((tm, cfg.ahg), lambda i: (i, 0))],
        out_shape=[jax.ShapeDtypeStruct((cfg.seq, cfg.aw), BF16), jax.ShapeDtypeStruct((cfg.seq, cfg.ahg), F32)],
        compiler_params=_params(("parallel",)),
    )(o3, lse3)


def _attn_delta(cfg, do, o):
    tm = 256

    def body(do_ref, o_ref, d_ref):
        out = jnp.zeros((tm, cfg.ahg), F32)
        for h in range(cfg.ahg):
            hs = slice(h * cfg.ahd, (h + 1) * cfg.ahd)
            prod = do_ref[:, hs].astype(F32) * o_ref[:, hs].astype(F32)
            out = _set_col(out, h, jnp.sum(prod, axis=1, keepdims=True))
        d_ref[...] = out

    row = pl.BlockSpec((tm, cfg.aw), lambda i: (i, 0))
    return pl.pallas_call(
        body, name="attn_delta", grid=(cfg.seq // tm,), in_specs=[row, row],
        out_specs=pl.BlockSpec((tm, cfg.ahg), lambda i: (i, 0)),
        out_shape=jax.ShapeDtypeStruct((cfg.seq, cfg.ahg), F32), compiler_params=_params(("parallel",)),
    )(do, o)


def _attn_bwd(cfg, q3, k3, v3, do3, l3, d3):
    nb = cfg.seq // ATT_BLOCK
    segs = _seg_blocks(cfg)
    scale = cfg.ahd ** -0.5
    blk = ATT_BLOCK

    def body(q_ref, qn_ref, do_ref, don_ref, l_ref, ln_ref, d_ref, dn_ref, kp_ref, kc_ref, vp_ref, vc_ref,
             dq_ref, dk_ref, dv_ref):
        g, b = pl.program_id(0), pl.program_id(1)
        seg_mask = _group_select(g, segs) - 1
        has_prev = (b & seg_mask) != 0
        has_next = jnp.logical_and(b + 1 < nb, ((b + 1) & seg_mask) != 0)
        qi = lax.broadcasted_iota(jnp.int32, (blk, blk), 0)
        kj = lax.broadcasted_iota(jnp.int32, (blk, blk), 1)
        valid_c = kj <= qi
        valid_p = jnp.logical_and(kj >= qi, has_prev)
        valid_n = jnp.logical_and(kj >= qi, has_next)
        lse, lse_n, dlt, dlt_n = l_ref[...], ln_ref[...], d_ref[...], dn_ref[...]
        for h in range(cfg.ahg):
            hs = slice(h * cfg.ahd, (h + 1) * cfg.ahd)
            q, qn, do, don = q_ref[:, hs], qn_ref[:, hs], do_ref[:, hs], don_ref[:, hs]
            kc, kp, vc, vp = kc_ref[:, hs], kp_ref[:, hs], vc_ref[:, hs], vp_ref[:, hs]
            lh, lnh, dh, dnh = _col(lse, h), _col(lse_n, h), _col(dlt, h), _col(dlt_n, h)
            p_c = jnp.where(valid_c, jnp.exp(_dot_nt(q, kc) * scale - lh), 0.0)
            p_p = jnp.where(valid_p, jnp.exp(_dot_nt(q, kp) * scale - lh), 0.0)
            ds_c = (p_c * (_dot_nt(do, vc) - dh)).astype(BF16)
            ds_p = (p_p * (_dot_nt(do, vp) - dh)).astype(BF16)
            dq_ref[:, hs] = ((_dot(ds_c, kc) + _dot(ds_p, kp)) * scale).astype(dq_ref.dtype)
            p_n = jnp.where(valid_n, jnp.exp(_dot_nt(qn, kc) * scale - lnh), 0.0)
            ds_n = (p_n * (_dot_nt(don, vc) - dnh)).astype(BF16)
            dk = _dot_tn(ds_c, q) + _dot_tn(ds_n, qn)
            dv = _dot_tn(p_c.astype(BF16), do) + _dot_tn(p_n.astype(BF16), don)
            dk_ref[:, hs] = (dk * scale).astype(dk_ref.dtype)
            dv_ref[:, hs] = dv.astype(dv_ref.dtype)

    def spec(width, shift):
        if shift == 0:
            return pl.BlockSpec((None, blk, width), lambda g, b: (g, b, 0))
        if shift > 0:
            return pl.BlockSpec((None, blk, width), lambda g, b: (g, jnp.minimum(b + 1, nb - 1), 0))
        return pl.BlockSpec((None, blk, width), lambda g, b: (g, jnp.maximum(b - 1, 0), 0))

    w, hw = cfg.aw, cfg.ahg
    out = jax.ShapeDtypeStruct((3, cfg.seq, w), BF16)
    return pl.pallas_call(
        body, name="attn_bwd", grid=(3, nb),
        in_specs=[spec(w, 0), spec(w, 1), spec(w, 0), spec(w, 1), spec(hw, 0), spec(hw, 1), spec(hw, 0), spec(hw, 1),
                  spec(w, -1), spec(w, 0), spec(w, -1), spec(w, 0)],
        out_specs=[spec(w, 0)] * 3, out_shape=[out, out, out], compiler_params=_params(("parallel", "parallel")),
    )(q3, q3, do3, do3, l3, l3, d3, d3, k3, k3, v3, v3)


def _deinterleave(t, dil):
    s, w = t.shape
    return t if dil == 1 else t.reshape(s // dil, dil, w).transpose(1, 0, 2).reshape(s, w)


def _interleave(t, dil):
    s, w = t.shape
    return t if dil == 1 else t.reshape(dil, s // dil, w).transpose(1, 0, 2).reshape(s, w)


def _to_groups(cfg, t):
    return jnp.stack([_deinterleave(t[:, g * cfg.aw:(g + 1) * cfg.aw], dil) for g, (_, dil) in enumerate(ATT_GROUPS)])


def _from_groups(t3):
    return [_interleave(t3[g], dil) for g, (_, dil) in enumerate(ATT_GROUPS)]


def _rope_tables(cfg):
    half = cfg.rqk // 2
    pos = jnp.arange(cfg.seq, dtype=F32)
    inv_freq = ROPE_BASE ** (-jnp.arange(half, dtype=F32) / half)
    ang = pos[:, None] * inv_freq[None, :]
    return jnp.cos(ang), jnp.sin(ang)


def _log_gamma(cfg):
    lg = jnp.log(1.0 - 2.0 ** (-5.0 - jnp.arange(cfg.rh, dtype=F32)))
    return jnp.broadcast_to(lg[:, None, None], (cfg.rh, 1, max(cfg.rqk, cfg.rv)))


def _rot(t, c, s, half):
    t1, t2 = t[:, :half], t[:, half:]
    return jnp.concatenate([t1 * c - t2 * s, t1 * s + t2 * c], axis=1)


def _unrot(d, c, s, half):
    d1, d2 = d[:, :half], d[:, half:]
    return jnp.concatenate([d1 * c + d2 * s, d2 * c - d1 * s], axis=1)


def _decays(lg_ref, cfg):
    blk = ATT_BLOCK
    lg_v, lg_k = lg_ref[:, :cfg.rv], lg_ref[:, :cfg.rqk]
    qi = lax.broadcasted_iota(jnp.int32, (blk, blk), 0)
    kj = lax.broadcasted_iota(jnp.int32, (blk, blk), 1)
    diff = (qi - kj).astype(F32)
    intra = jnp.where(diff >= 0, jnp.exp(jnp.maximum(diff, 0.0) * lg_ref[:, :blk]), 0.0)
    idx_v = lax.broadcasted_iota(jnp.int32, (blk, cfg.rv), 0).astype(F32)
    idx_k = lax.broadcasted_iota(jnp.int32, (blk, cfg.rqk), 0).astype(F32)
    cross = jnp.exp((idx_v + 1.0) * lg_v)
    state = jnp.exp((blk - 1.0 - idx_k) * lg_k)
    chunk = jnp.exp(float(blk) * lg_v)
    return intra, cross, state, chunk


def _ret_fwd(cfg, proj, cos, sin, lg):
    blk, nc, half = ATT_BLOCK, cfg.seq // ATT_BLOCK, cfg.rqk // 2
    kscale = cfg.rqk ** -0.5

    def body(q_ref, k_ref, v_ref, cos_ref, sin_ref, lg_ref, rn_ref, rstd_ref, st_ref, state):
        n = pl.program_id(1)

        @pl.when(n == 0)
        def _():
            state[...] = jnp.zeros_like(state)

        intra, cross_d, state_d, chunk_d = _decays(lg_ref, cfg)
        c, s = cos_ref[...], sin_ref[...]
        qb = _rot(q_ref[...].astype(F32), c, s, half).astype(BF16)
        kf = _rot(k_ref[...].astype(F32), c, s, half) * kscale
        vb = v_ref[...]
        prev = state[...]
        st_ref[...] = prev.astype(BF16)
        att = _dot_nt(qb, kf.astype(BF16)) * intra
        out = _dot(att.astype(BF16), vb) + _dot(qb, prev.astype(BF16)) * cross_d
        state[...] = chunk_d * prev + _dot_tn((kf * state_d).astype(BF16), vb)
        mu = jnp.mean(out, axis=1, keepdims=True)
        cen = out - mu
        rstd = lax.rsqrt(jnp.mean(cen * cen, axis=1, keepdims=True) + GN_EPS)
        rn_ref[...] = (cen * rstd).astype(rn_ref.dtype)
        rstd_ref[...] = rstd

    oq, ok, ov = cfg.o_qr // cfg.rqk, cfg.o_kr // cfg.rqk, cfg.o_vr // cfg.rv
    tab = pl.BlockSpec((blk, half), lambda h, n: (n, 0))
    return pl.pallas_call(
        body, name="ret_fwd", grid=(cfg.rh, nc),
        in_specs=[pl.BlockSpec((blk, cfg.rqk), lambda h, n: (n, oq + h)), pl.BlockSpec((blk, cfg.rqk), lambda h, n: (n, ok + h)),
                  pl.BlockSpec((blk, cfg.rv), lambda h, n: (n, ov + h)), tab, tab,
                  pl.BlockSpec((None, 1, lg.shape[2]), lambda h, n: (h, 0, 0))],
        out_specs=[pl.BlockSpec((blk, cfg.rv), lambda h, n: (n, h)), pl.BlockSpec((None, blk, 1), lambda h, n: (h, n, 0)),
                   pl.BlockSpec((None, None, cfg.rqk, cfg.rv), lambda h, n: (h, n, 0, 0))],
        out_shape=[jax.ShapeDtypeStruct((cfg.seq, cfg.rv_w), BF16), jax.ShapeDtypeStruct((cfg.rh, cfg.seq, 1), F32),
                   jax.ShapeDtypeStruct((cfg.rh, nc, cfg.rqk, cfg.rv), BF16)],
        scratch_shapes=[pltpu.VMEM((cfg.rqk, cfg.rv), F32)], compiler_params=_params(("parallel", "arbitrary")),
    )(proj, proj, proj, cos, sin, lg)


def _ret_bwd(cfg, proj, d_r, states, cos, sin, lg):
    blk, nc, half = ATT_BLOCK, cfg.seq // ATT_BLOCK, cfg.rqk // 2
    kscale = cfg.rqk ** -0.5

    def body(q_ref, k_ref, v_ref, do_ref, st_ref, cos_ref, sin_ref, lg_ref, dq_ref, dk_ref, dv_ref, dstate):
        n = pl.program_id(1)

        @pl.when(n == 0)
        def _():
            dstate[...] = jnp.zeros_like(dstate)

        intra, cross_d, state_d, chunk_d = _decays(lg_ref, cfg)
        c, s = cos_ref[...], sin_ref[...]
        qb = _rot(q_ref[...].astype(F32), c, s, half).astype(BF16)
        kf = _rot(k_ref[...].astype(F32), c, s, half) * kscale
        kb, ksb = kf.astype(BF16), (kf * state_d).astype(BF16)
        vb, prev = v_ref[...], st_ref[...]
        do = do_ref[...].astype(F32)
        dob, docb = do.astype(BF16), (do * cross_d).astype(BF16)
        dsb = dstate[...].astype(BF16)
        att = (_dot_nt(qb, kb) * intra).astype(BF16)
        datt = (_dot_nt(dob, vb) * intra).astype(BF16)
        d_q = _dot(datt, kb) + _dot_nt(docb, prev)
        d_k = _dot_tn(datt, qb) + _dot_nt(vb, dsb) * state_d
        d_v = _dot_tn(att, dob) + _dot(ksb, dsb)
        dstate[...] = chunk_d * dstate[...] + _dot_tn(qb, docb)
        dq_ref[...] = _unrot(d_q, c, s, half).astype(dq_ref.dtype)
        dk_ref[...] = _unrot(d_k * kscale, c, s, half).astype(dk_ref.dtype)
        dv_ref[...] = d_v.astype(dv_ref.dtype)

    oq, ok, ov = cfg.o_qr // cfg.rqk, cfg.o_kr // cfg.rqk, cfg.o_vr // cfg.rv
    last = nc - 1
    tab = pl.BlockSpec((blk, half), lambda h, n: (last - n, 0))
    qk_out = pl.BlockSpec((blk, cfg.rqk), lambda h, n: (last - n, h))
    v_out = pl.BlockSpec((blk, cfg.rv), lambda h, n: (last - n, h))
    return pl.pallas_call(
        body, name="ret_bwd", grid=(cfg.rh, nc),
        in_specs=[pl.BlockSpec((blk, cfg.rqk), lambda h, n: (last - n, oq + h)),
                  pl.BlockSpec((blk, cfg.rqk), lambda h, n: (last - n, ok + h)),
                  pl.BlockSpec((blk, cfg.rv), lambda h, n: (last - n, ov + h)), v_out,
                  pl.BlockSpec((None, None, cfg.rqk, cfg.rv), lambda h, n: (h, last - n, 0, 0)), tab, tab,
                  pl.BlockSpec((None, 1, lg.shape[2]), lambda h, n: (h, 0, 0))],
        out_specs=[qk_out, qk_out, v_out],
        out_shape=[jax.ShapeDtypeStruct((cfg.seq, cfg.rqk_w), BF16), jax.ShapeDtypeStruct((cfg.seq, cfg.rqk_w), BF16),
                   jax.ShapeDtypeStruct((cfg.seq, cfg.rv_w), BF16)],
        scratch_shapes=[pltpu.VMEM((cfg.rqk, cfg.rv), F32)], compiler_params=_params(("parallel", "arbitrary")),
    )(proj, proj, proj, d_r, states, cos, sin, lg)


def _ret_gate(cfg, proj, rn, gn_g):
    tm, cw = 256, cfg.cw
    og = cfg.o_gr // cw

    def body(g_ref, rn_ref, w_ref, o_ref):
        g = g_ref[...].astype(F32)
        o_ref[...] = (g * _sigmoid(g) * (rn_ref[...].astype(F32) * w_ref[...])).astype(o_ref.dtype)

    blk = pl.BlockSpec((tm, cw), lambda i, j: (i, j))
    return pl.pallas_call(
        body, name="ret_gate", grid=(cfg.seq // tm, cfg.rv_w // cw),
        in_specs=[pl.BlockSpec((tm, cw), lambda i, j: (i, og + j)), blk, pl.BlockSpec((1, cw), lambda i, j: (0, j))],
        out_specs=blk, out_shape=jax.ShapeDtypeStruct((cfg.seq, cfg.rv_w), BF16),
        compiler_params=_params(("parallel", "parallel")),
    )(proj, rn, gn_g)


def _mix(cfg, proj, y_att, y_ret):
    tm, cw = 256, cfg.cw
    oa, o2 = cfg.o_ga // cw, cfg.o_g2 // cw

    def body(ga_ref, g2_ref, ya_ref, yr_ref, o_ref):
        r = _sigmoid(ga_ref[...].astype(F32)) * ya_ref[...].astype(F32)
        r += _sigmoid(g2_ref[...].astype(F32)) * yr_ref[...].astype(F32)
        o_ref[...] = r.astype(o_ref.dtype)

    blk = pl.BlockSpec((tm, cw), lambda i, j: (i, j))
    return pl.pallas_call(
        body, name="mix", grid=(cfg.seq // tm, cfg.d // cw),
        in_specs=[pl.BlockSpec((tm, cw), lambda i, j: (i, oa + j)), pl.BlockSpec((tm, cw), lambda i, j: (i, o2 + j)), blk, blk],
        out_specs=blk, out_shape=jax.ShapeDtypeStruct((cfg.seq, cfg.d), BF16),
        compiler_params=_params(("parallel", "parallel")),
    )(proj, proj, y_att, y_ret)


def _mix_bwd(cfg, proj, y_att, y_ret, d_mixed):
    tm, cw = 256, cfg.cw
    oa, o2 = cfg.o_ga // cw, cfg.o_g2 // cw

    def body(ga_ref, g2_ref, ya_ref, yr_ref, dm_ref, dya_ref, dyr_ref, dga_ref, dg2_ref):
        dm = dm_ref[...].astype(F32)
        sa, s2 = _sigmoid(ga_ref[...].astype(F32)), _sigmoid(g2_ref[...].astype(F32))
        dya_ref[...] = (dm * sa).astype(dya_ref.dtype)
        dyr_ref[...] = (dm * s2).astype(dyr_ref.dtype)
        dga_ref[...] = (dm * ya_ref[...].astype(F32) * sa * (1.0 - sa)).astype(dga_ref.dtype)
        dg2_ref[...] = (dm * yr_ref[...].astype(F32) * s2 * (1.0 - s2)).astype(dg2_ref.dtype)

    blk = pl.BlockSpec((tm, cw), lambda i, j: (i, j))
    out = jax.ShapeDtypeStruct((cfg.seq, cfg.d), BF16)
    return pl.pallas_call(
        body, name="mix_bwd", grid=(cfg.seq // tm, cfg.d // cw),
        in_specs=[pl.BlockSpec((tm, cw), lambda i, j: (i, oa + j)), pl.BlockSpec((tm, cw), lambda i, j: (i, o2 + j)), blk, blk, blk],
        out_specs=[blk] * 4, out_shape=[out] * 4, compiler_params=_params(("parallel", "parallel")),
    )(proj, proj, y_att, y_ret, d_mixed)


def _ret_gate_bwd(cfg, proj, rn, gn_g, rstd, d_rg):
    tm, rv = 256, cfg.rv
    og = cfg.o_gr // rv

    def body(g_ref, rn_ref, w_ref, rstd_ref, drg_ref, dg_ref, dr_ref, gw_ref):
        i = pl.program_id(1)
        g, rn, w = g_ref[...].astype(F32), rn_ref[...].astype(F32), w_ref[...]
        drg = drg_ref[...].astype(F32)
        sg = _sigmoid(g)
        silu = g * sg
        dg_ref[...] = (drg * (rn * w) * (sg * (1.0 + g * (1.0 - sg)))).astype(dg_ref.dtype)
        drn = drg * silu * w
        part = jnp.sum(drg * silu * rn, axis=0, keepdims=True)

        @pl.when(i == 0)
        def _():
            gw_ref[...] = part

        @pl.when(i > 0)
        def _():
            gw_ref[...] += part

        m1 = jnp.mean(drn, axis=1, keepdims=True)
        m2 = jnp.mean(drn * rn, axis=1, keepdims=True)
        dr_ref[...] = (rstd_ref[...] * (drn - m1 - rn * m2)).astype(dr_ref.dtype)

    blk = pl.BlockSpec((tm, rv), lambda h, i: (i, h))
    vec = pl.BlockSpec((1, rv), lambda h, i: (0, h))
    out = jax.ShapeDtypeStruct((cfg.seq, cfg.rv_w), BF16)
    return pl.pallas_call(
        body, name="ret_gate_bwd", grid=(cfg.rh, cfg.seq // tm),
        in_specs=[pl.BlockSpec((tm, rv), lambda h, i: (i, og + h)), blk, vec, pl.BlockSpec((None, tm, 1), lambda h, i: (h, i, 0)), blk],
        out_specs=[blk, blk, vec], out_shape=[out, out, jax.ShapeDtypeStruct((1, cfg.rv_w), F32)],
        compiler_params=_params(("parallel", "arbitrary")),
    )(proj, rn, gn_g, rstd, d_rg)


def _ln1(cfg, x, mo, g, b):
    tm, d = 128, cfg.d

    def body(x_ref, mo_ref, g_ref, b_ref, xh_ref, h_ref, rstd_ref):
        z = cfg.alpha * x_ref[...] + mo_ref[...]
        cen = z - jnp.mean(z, axis=1, keepdims=True)
        rstd = lax.rsqrt(jnp.mean(cen * cen, axis=1, keepdims=True) + LN_EPS)
        xh = cen * rstd
        xh_ref[...] = xh
        h_ref[...] = (xh * g_ref[...] + b_ref[...]).astype(h_ref.dtype)
        rstd_ref[...] = rstd

    row = pl.BlockSpec((tm, d), lambda i: (i, 0))
    vec = pl.BlockSpec((1, d), lambda i: (0, 0))
    col = pl.BlockSpec((tm, 1), lambda i: (i, 0))
    return pl.pallas_call(
        body, name="ln1", grid=(cfg.seq // tm,), in_specs=[row, row, vec, vec], out_specs=[row, row, col],
        out_shape=[jax.ShapeDtypeStruct((cfg.seq, d), F32), jax.ShapeDtypeStruct((cfg.seq, d), BF16),
                   jax.ShapeDtypeStruct((cfg.seq, 1), F32)],
        compiler_params=_params(("parallel",)),
    )(x, mo, g, b)


def _ffn_act(cfg, u, t):
    tm, cf = 256, _tile(cfg.ffn_p, 2048)

    def body(u_ref, t_ref, a_ref):
        uu = u_ref[...].astype(F32)
        a_ref[...] = (uu * _sigmoid(uu) * t_ref[...].astype(F32)).astype(a_ref.dtype)

    blk = pl.BlockSpec((tm, cf), lambda i, j: (i, j))
    return pl.pallas_call(
        body, name="ffn_act", grid=(cfg.seq // tm, cfg.ffn_p // cf), in_specs=[blk, blk], out_specs=blk,
        out_shape=jax.ShapeDtypeStruct((cfg.seq, cfg.ffn_p), BF16), compiler_params=_params(("parallel", "parallel")),
    )(u, t)


def _ffn_act_bwd(cfg, u, t, d_a):
    tm, cf = 256, _tile(cfg.ffn_p, 2048)

    def body(u_ref, t_ref, da_ref, du_ref, dt_ref):
        uu, da = u_ref[...].astype(F32), da_ref[...].astype(F32)
        sg = _sigmoid(uu)
        du_ref[...] = (da * t_ref[...].astype(F32) * (sg * (1.0 + uu * (1.0 - sg)))).astype(du_ref.dtype)
        dt_ref[...] = (da * uu * sg).astype(dt_ref.dtype)

    blk = pl.BlockSpec((tm, cf), lambda i, j: (i, j))
    out = jax.ShapeDtypeStruct((cfg.seq, cfg.ffn_p), BF16)
    return pl.pallas_call(
        body, name="ffn_act_bwd", grid=(cfg.seq // tm, cfg.ffn_p // cf), in_specs=[blk, blk, blk], out_specs=[blk, blk],
        out_shape=[out, out], compiler_params=_params(("parallel", "parallel")),
    )(u, t, d_a)


def _head(cfg, xhat1, ffn, hp, pu, tgt, g1, b1, g2, b2):
    tm, d = 64, cfg.d

    def body(xh_ref, ffn_ref, hp_ref, pu_ref, tgt_ref, g1_ref, b1_ref, g2_ref, b2_ref,
             loss_ref, dzf_ref, dzb_ref, dpg_ref, dpu_ref, gg_ref, gb_ref):
        i = pl.program_id(0)
        h1 = xh_ref[...] * g1_ref[...] + b1_ref[...]
        sg, pu = _sigmoid(hp_ref[...].astype(F32)), pu_ref[...].astype(F32)
        z = cfg.alpha * h1 + ffn_ref[...] + sg * pu
        cen = z - jnp.mean(z, axis=1, keepdims=True)
        rstd = lax.rsqrt(jnp.mean(cen * cen, axis=1, keepdims=True) + LN_EPS)
        xh2 = cen * rstd
        err = xh2 * g2_ref[...] + b2_ref[...] - tgt_ref[...]
        dy = err * (1.0 / d)
        part_l = jnp.sum(jnp.sum(err * err, axis=1, keepdims=True), axis=0, keepdims=True) * (0.5 / d)
        part_g = jnp.sum(dy * xh2, axis=0, keepdims=True)
        part_b = jnp.sum(dy, axis=0, keepdims=True)

        @pl.when(i == 0)
        def _():
            loss_ref[...] = jnp.zeros_like(loss_ref)
            gg_ref[...] = jnp.zeros_like(gg_ref)
            gb_ref[...] = jnp.zeros_like(gb_ref)

        loss_ref[...] += jnp.broadcast_to(part_l, loss_ref.shape)
        gg_ref[...] += part_g
        gb_ref[...] += part_b
        dxh = dy * g2_ref[...]
        m1 = jnp.mean(dxh, axis=1, keepdims=True)
        m2 = jnp.mean(dxh * xh2, axis=1, keepdims=True)
        dz = rstd * (dxh - m1 - xh2 * m2)
        dzf_ref[...] = dz
        dzb_ref[...] = dz.astype(dzb_ref.dtype)
        dpg_ref[...] = (dz * pu * sg * (1.0 - sg)).astype(dpg_ref.dtype)
        dpu_ref[...] = (dz * sg).astype(dpu_ref.dtype)

    row = pl.BlockSpec((tm, d), lambda i: (i, 0))
    vec = pl.BlockSpec((1, d), lambda i: (0, 0))
    bf = jax.ShapeDtypeStruct((cfg.seq, d), BF16)
    vec_out = jax.ShapeDtypeStruct((1, d), F32)
    return pl.pallas_call(
        body, name="head", grid=(cfg.seq // tm,), in_specs=[row] * 5 + [vec] * 4,
        out_specs=[pl.BlockSpec((1, 128), lambda i: (0, 0)), row, row, row, row, vec, vec],
        out_shape=[jax.ShapeDtypeStruct((1, 128), F32), jax.ShapeDtypeStruct((cfg.seq, d), F32), bf, bf, bf, vec_out, vec_out],
        compiler_params=_params(("arbitrary",)),
    )(xhat1, ffn, hp, pu, tgt, g1, b1, g2, b2)


def _ln1_bwd(cfg, dh_mm, dz2, xhat1, rstd1, g1):
    tm, d = 128, cfg.d

    def body(dh_ref, dz_ref, xh_ref, rstd_ref, g_ref, dzf_ref, dzb_ref, gg_ref, gb_ref):
        i = pl.program_id(0)
        dh = cfg.alpha * dz_ref[...] + dh_ref[...]
        xh = xh_ref[...]

        @pl.when(i == 0)
        def _():
            gg_ref[...] = jnp.zeros_like(gg_ref)
            gb_ref[...] = jnp.zeros_like(gb_ref)

        gg_ref[...] += jnp.sum(dh * xh, axis=0, keepdims=True)
        gb_ref[...] += jnp.sum(dh, axis=0, keepdims=True)
        dxh = dh * g_ref[...]
        m1 = jnp.mean(dxh, axis=1, keepdims=True)
        m2 = jnp.mean(dxh * xh, axis=1, keepdims=True)
        dz = rstd_ref[...] * (dxh - m1 - xh * m2)
        dzf_ref[...] = dz
        dzb_ref[...] = dz.astype(dzb_ref.dtype)

    row = pl.BlockSpec((tm, d), lambda i: (i, 0))
    vec = pl.BlockSpec((1, d), lambda i: (0, 0))
    vec_out = jax.ShapeDtypeStruct((1, d), F32)
    return pl.pallas_call(
        body, name="ln1_bwd", grid=(cfg.seq // tm,),
        in_specs=[row, row, row, pl.BlockSpec((tm, 1), lambda i: (i, 0)), vec], out_specs=[row, row, vec, vec],
        out_shape=[jax.ShapeDtypeStruct((cfg.seq, d), F32), jax.ShapeDtypeStruct((cfg.seq, d), BF16), vec_out, vec_out],
        compiler_params=_params(("arbitrary",)),
    )(dh_mm, dz2, xhat1, rstd1, g1)


def _adamw(w, g, m, v, name):
    r, c = w.shape
    tr, tc = _row_tile(r, c * 4, 2 * 1024 * 1024), c
    bc1, bc2 = 1.0 - ADAM_B1 ** ADAM_STEP, 1.0 - ADAM_B2 ** ADAM_STEP

    def body(w_ref, g_ref, m_ref, v_ref, d_ref, mo_ref, vo_ref):
        gg = g_ref[...]
        mn = ADAM_B1 * m_ref[...] + (1.0 - ADAM_B1) * gg
        vn = ADAM_B2 * v_ref[...] + (1.0 - ADAM_B2) * (gg * gg)
        d_ref[...] = -ADAM_LR * ((mn / bc1) / (jnp.sqrt(vn / bc2) + ADAM_EPS) + ADAM_WD * w_ref[...])
        mo_ref[...] = mn
        vo_ref[...] = vn

    blk = pl.BlockSpec((tr, tc), lambda i, j: (i, j))
    out = jax.ShapeDtypeStruct((r, c), F32)
    return pl.pallas_call(
        body, name=name, grid=(r // tr, c // tc), in_specs=[blk] * 4, out_specs=[blk] * 3, out_shape=[out] * 3,
        compiler_params=_params(("parallel", "parallel")),
    )(w, g, m, v)


def _place():
    x, y, c = lax.axis_index("x"), lax.axis_index("y"), lax.axis_index("c")
    others = [(1 - x, y), (x, 1 - y), (1 - x, 1 - y)]
    return x, y, c, others


def _all_gather(ws, ax, name, landed=None):
    r, cdim = ws.shape
    full_shape = (4 * r, cdim) if ax == 0 else (r, 4 * cdim)
    n_shard = r if ax == 0 else cdim
    n_half = r // 2
    tr = _row_tile(n_half, cdim * 2, STAGE_BYTES, 16)
    nq = n_half // tr
    slots = 3

    over_ici = landed is None

    def body(*refs):
        ws_ref = refs[0]
        full_ref, buf, load_sems, write_sems, send_sems, pass_sems, recv_sems, sib_sem = refs[1 if over_ici else 2:]
        x, y, c, others = _place()
        me = 2 * x + y
        sibling = (x, y, 1 - c)

        def region(chip, half, row0, rows):
            if ax == 0:
                return full_ref.at[pl.ds(chip * n_shard + half * n_half + row0, rows), :]
            return full_ref.at[pl.ds(half * n_half + row0, rows), pl.ds(chip * n_shard, n_shard)]

        pending = {}

        def free(slot):
            for cp, remote in pending.pop(slot, []):
                if remote:
                    cp.wait_send()
                else:
                    cp.wait()

        def load(src, slot):
            free(slot)
            cp = pltpu.make_async_copy(src, buf.at[slot], load_sems.at[slot])
            cp.start()
            cp.wait()

        step = 0
        for mine in (True, False):
            half = c if mine else 1 - c
            for q in range(nq):
                slot = step % slots
                load(ws_ref.at[pl.ds(half * n_half + q * tr, tr), :], slot)
                dst = region(me, half, q * tr, tr)
                cp = pltpu.make_async_copy(buf.at[slot], dst, write_sems.at[slot])
                cp.start()
                pending[slot] = [(cp, False)]
                if mine and over_ici:
                    for k, chip in enumerate(others):
                        cp = pltpu.make_async_remote_copy(
                            src_ref=buf.at[slot], dst_ref=dst, send_sem=send_sems.at[3 * slot + k], recv_sem=recv_sems.at[k],
                            device_id=(*chip, c), device_id_type=MESH)
                        cp.start()
                        pending[slot].append((cp, True))
                step += 1
        for k, chip in enumerate(others):
            j = 2 * chip[0] + chip[1]
            if over_ici:
                whole = region(j, c, 0, n_half)
                pltpu.make_async_remote_copy(src_ref=whole, dst_ref=whole, send_sem=send_sems.at[0], recv_sem=recv_sems.at[k],
                                             device_id=(x, y, c), device_id_type=MESH).wait_recv()
            for q in range(nq):
                slot = step % slots
                part = region(j, c, q * tr, tr)
                load(part, slot)
                cp = pltpu.make_async_remote_copy(src_ref=buf.at[slot], dst_ref=part, send_sem=pass_sems.at[slot],
                                                  recv_sem=sib_sem, device_id=sibling, device_id_type=MESH)
                cp.start()
                pending[slot] = [(cp, True)]
                step += 1
        for slot in list(pending):
            free(slot)
        if ax == 0:
            three = full_ref.at[pl.ds(0, 3 * n_half), :]
        else:
            three = full_ref.at[pl.ds(0, n_half), pl.ds(0, 3 * n_shard)]
        pltpu.make_async_remote_copy(src_ref=three, dst_ref=three, send_sem=send_sems.at[0], recv_sem=sib_sem,
                                     device_id=(x, y, c), device_id_type=MESH).wait_recv()

    return pl.pallas_call(
        body, name=name, in_specs=[ANY] if over_ici else [ANY, ANY], out_specs=ANY,
        out_shape=jax.ShapeDtypeStruct(full_shape, ws.dtype), input_output_aliases={} if over_ici else {1: 0},
        scratch_shapes=[pltpu.VMEM((slots, tr, cdim), ws.dtype), pltpu.SemaphoreType.DMA((slots,)),
                        pltpu.SemaphoreType.DMA((slots,)), pltpu.SemaphoreType.DMA((3 * slots,)),
                        pltpu.SemaphoreType.DMA((slots,)), pltpu.SemaphoreType.DMA((3,)), pltpu.SemaphoreType.DMA],
        compiler_params=pltpu.CompilerParams(vmem_limit_bytes=VMEM_LIMIT),
    )(*((ws,) if over_ici else (ws, landed)))


def _gather_job(ws, ax):
    r, cdim = ws.shape
    full_shape = (4 * r, cdim) if ax == 0 else (r, 4 * cdim)
    n_shard = r if ax == 0 else cdim
    n_half = r // 2

    def copies(ins, outs, sems):
        (ws_ref,), (full_ref,) = ins, outs
        x, y, c, others = _place()
        me = 2 * x + y
        if ax == 0:
            dst = full_ref.at[pl.ds(me * n_shard + c * n_half, n_half), :]
        else:
            dst = full_ref.at[pl.ds(c * n_half, n_half), pl.ds(me * n_shard, n_shard)]
        return [(pltpu.make_async_remote_copy(
            src_ref=ws_ref.at[pl.ds(c * n_half, n_half), :], dst_ref=dst, send_sem=sems.at[k], recv_sem=sems.at[3 + k],
            device_id=(*chip, c), device_id_type=MESH), "both") for k, chip in enumerate(others)]

    return _Job([ws], [jax.ShapeDtypeStruct(full_shape, ws.dtype)], 6, copies)


def _scatter_job(pair):
    _, n_half, ncol = pair.shape

    def copies(ins, outs, sems):
        (p_ref,), (got_ref,) = ins, outs
        x, y, c, others = _place()
        return [(pltpu.make_async_remote_copy(
            src_ref=p_ref.at[k], dst_ref=got_ref.at[k], send_sem=sems.at[k], recv_sem=sems.at[3 + k],
            device_id=(*chip, c), device_id_type=MESH), "both") for k, chip in enumerate(others)]

    return _Job([pair], [jax.ShapeDtypeStruct((3, n_half, ncol), pair.dtype)], 6, copies)


def _shard_dims(shape, ax):
    rf, cf = shape
    n_shard = (rf if ax == 0 else cf) // 4
    return n_shard, (n_shard if ax == 0 else rf) // 2, (cf if ax == 0 else n_shard)


def _mm_tn_pair(a, b, ax, name):
    (k, m), (k2, n) = a.shape, b.shape
    assert k == k2
    n_shard, n_half, _ = _shard_dims((m, n), ax)
    tm = _tile(n_half, 1024)
    if tm < 512:
        tm = n_half
    tn, tk = _tile(n, 1024), _tile(k, 512)
    ni, nj, nk = m // tm, n // tn, k // tk
    ntiles = ni * nj
    assert n_half % tm == 0 and ntiles >= 2

    def body(a_ref, b_ref, own_ref, recv_ref, acc_ref, stage, local_sems, send_sems, recv_sem):
        i, j, kk = pl.program_id(0), pl.program_id(1), pl.program_id(2)
        x, y, c, _ = _place()
        sibling = (x, y, 1 - c)
        t = i * nj + j

        def is_mine(ii):
            row0 = ii * tm
            half = row0 // n_half if ax == 1 else (row0 % n_shard) // n_half
            return half == c

        def copies(tt):
            ii, jj, slot = tt // nj, tt % nj, tt % 2
            where = (pl.ds(ii * tm, tm), pl.ds(jj * tn, tn))
            local = pltpu.make_async_copy(stage.at[slot], own_ref.at[where], local_sems.at[slot])
            remote = pltpu.make_async_remote_copy(src_ref=stage.at[slot], dst_ref=recv_ref.at[where], send_sem=send_sems.at[slot],
                                                  recv_sem=recv_sem, device_id=sibling, device_id_type=MESH)
            return is_mine(ii), local, remote

        def wait_tile(tt):
            mine, local, remote = copies(tt)

            @pl.when(mine)
            def _():
                local.wait()

            @pl.when(jnp.logical_not(mine))
            def _():
                remote.wait_send()

        @pl.when(kk == 0)
        def _():
            acc_ref[...] = jnp.zeros_like(acc_ref)

        acc_ref[...] += lax.dot_general(a_ref[...], b_ref[...], _TN, preferred_element_type=F32)

        @pl.when(kk == nk - 1)
        def _():
            @pl.when(t >= 2)
            def _():
                wait_tile(t - 2)

            stage[t % 2] = acc_ref[...].astype(stage.dtype)
            mine, local, remote = copies(t)

            @pl.when(mine)
            def _():
                local.start()

            @pl.when(jnp.logical_not(mine))
            def _():
                remote.start()

            @pl.when(t == ntiles - 1)
            def _():
                wait_tile(t - 1)
                wait_tile(t)
                half_rows = recv_ref.at[pl.ds(0, m // 2), :]
                pltpu.make_async_remote_copy(src_ref=half_rows, dst_ref=half_rows, send_sem=send_sems.at[0], recv_sem=recv_sem,
                                             device_id=sibling, device_id_type=MESH).wait_recv()

    out = jax.ShapeDtypeStruct((m, n), BF16)
    return pl.pallas_call(
        body, name=name, grid=(ni, nj, nk),
        in_specs=[pl.BlockSpec((tk, tm), lambda i, j, kk: (kk, i)), pl.BlockSpec((tk, tn), lambda i, j, kk: (kk, j))],
        out_specs=[ANY, ANY], out_shape=[out, out],
        scratch_shapes=[pltpu.VMEM((tm, tn), F32), pltpu.VMEM((2, tm, tn), BF16), pltpu.SemaphoreType.DMA((2,)),
                        pltpu.SemaphoreType.DMA((2,)), pltpu.SemaphoreType.DMA],
        compiler_params=_params(("arbitrary", "arbitrary", "arbitrary")),
    )(a, b)


def _other_chip(k, x, y):
    return jnp.where(k == 1, x, 1 - x), jnp.where(k == 0, y, 1 - y)


def _pairsum(own, recv, ax, coords, name):
    n_shard, n_half, ncol = _shard_dims(own.shape, ax)
    tr = _row_tile(n_half, ncol * 2, STAGE_BYTES // 2, 16)
    nr = n_half // tr

    def src_map(k):
        def index(r, co):
            jx, jy = _other_chip(k, co[0], co[1])
            j = 2 * jx + jy
            if ax == 0:
                return ((j * n_shard + co[2] * n_half) // tr + r, 0)
            return (co[2] * nr + r, j)
        return index

    def body(co_ref, o0, r0, o1, r1, o2, r2, out_ref):
        for k, (own_ref, recv_ref) in enumerate(((o0, r0), (o1, r1), (o2, r2))):
            out_ref[k] = (own_ref[...].astype(F32) + recv_ref[...].astype(F32)).astype(out_ref.dtype)

    blks = [pl.BlockSpec((tr, ncol), src_map(k)) for k in range(3) for _ in range(2)]
    return pl.pallas_call(
        body, name=name,
        grid_spec=pltpu.PrefetchScalarGridSpec(
            num_scalar_prefetch=1, grid=(nr,), in_specs=blks, out_specs=pl.BlockSpec((3, tr, ncol), lambda r, co: (0, r, 0))),
        out_shape=jax.ShapeDtypeStruct((3, n_half, ncol), BF16),
        compiler_params=_params(("parallel",)),
    )(coords, own, recv, own, recv, own, recv)


def _sum_share(own, recv, got, ax, coords, name):
    n_shard, n_half, ncol = _shard_dims(own.shape, ax)
    tr = _row_tile(n_half, ncol * 4, STAGE_BYTES, 16)
    nr = n_half // tr

    def src_map(r, co):
        me = 2 * co[0] + co[1]
        if ax == 0:
            return ((me * n_shard + co[2] * n_half) // tr + r, 0)
        return (co[2] * nr + r, me)

    def body(co_ref, own_ref, recv_ref, got_ref, out_ref, sbuf, local_sems, send_sems, recv_sem):
        r = pl.program_id(0)
        x, y, c, _ = _place()

        def copies(slot):
            rows = out_ref.at[pl.ds(c * n_half + r * tr, tr), :]
            local = pltpu.make_async_copy(sbuf.at[slot], rows, local_sems.at[slot])
            remote = pltpu.make_async_remote_copy(src_ref=sbuf.at[slot], dst_ref=rows, send_sem=send_sems.at[slot],
                                                  recv_sem=recv_sem, device_id=(x, y, 1 - c), device_id_type=MESH)
            return local, remote

        def wait_slot(slot):
            local, remote = copies(slot)
            local.wait()
            remote.wait_send()

        @pl.when(r >= 2)
        def _():
            wait_slot(r % 2)

        total = own_ref[...].astype(F32) + recv_ref[...].astype(F32)
        for kk in range(3):
            total = total + got_ref[kk].astype(F32)
        sbuf[r % 2] = total
        local, remote = copies(r % 2)
        local.start()
        remote.start()

        @pl.when(r == nr - 1)
        def _():
            wait_slot(r % 2)
            if nr >= 2:
                wait_slot(1 - r % 2)
            half = out_ref.at[pl.ds(0, n_half), :]
            pltpu.make_async_remote_copy(src_ref=half, dst_ref=half, send_sem=send_sems.at[0], recv_sem=recv_sem,
                                         device_id=(x, y, c), device_id_type=MESH).wait_recv()

    blk = pl.BlockSpec((tr, ncol), src_map)
    return pl.pallas_call(
        body, name=name,
        grid_spec=pltpu.PrefetchScalarGridSpec(
            num_scalar_prefetch=1, grid=(nr,), in_specs=[blk, blk, pl.BlockSpec((3, tr, ncol), lambda r, co: (0, r, 0))],
            out_specs=ANY,
            scratch_shapes=[pltpu.VMEM((2, tr, ncol), F32), pltpu.SemaphoreType.DMA((2,)), pltpu.SemaphoreType.DMA((2,)),
                            pltpu.SemaphoreType.DMA]),
        out_shape=jax.ShapeDtypeStruct((2 * n_half, ncol), F32),
        compiler_params=_params(("arbitrary",)),
    )(coords, own, recv, got)


def _vec_all_reduce(v, name):
    rows, d = v.shape

    def body(v_ref, out_ref, gath_ref, send_sems, recv_sems):
        x, y, c, _ = _place()
        me = 4 * x + 2 * y + c
        gath_ref[me] = v_ref[...]
        flips = [(fx, fy, fc) for fx in (0, 1) for fy in (0, 1) for fc in (0, 1)][1:]
        sends = []
        for k, (fx, fy, fc) in enumerate(flips):
            peer = (1 - x if fx else x, 1 - y if fy else y, 1 - c if fc else c)
            sends.append(pltpu.make_async_remote_copy(
                src_ref=v_ref, dst_ref=gath_ref.at[me], send_sem=send_sems.at[k], recv_sem=recv_sems.at[k],
                device_id=peer, device_id_type=MESH))
        for cp in sends:
            cp.start()
        for cp in sends:
            cp.wait()
        acc = gath_ref[0]
        for dev in range(1, 8):
            acc = acc + gath_ref[dev]
        out_ref[...] = acc

    vm = pl.BlockSpec(memory_space=pltpu.VMEM)
    return pl.pallas_call(
        body, name=name, in_specs=[vm], out_specs=vm, out_shape=jax.ShapeDtypeStruct((rows, d), F32),
        scratch_shapes=[pltpu.VMEM((8, rows, d), F32), pltpu.SemaphoreType.DMA((7,)), pltpu.SemaphoreType.DMA((7,))],
    )(v)


_SHARD_AXIS = {"w_in": 1, "w_attn_out": 1, "w_ret_out": 0, "w_o": 0, "w_ffn_gate": 1, "w_ffn_up": 1, "w_ffn_down": 0,
               "w_ple_gate": 0, "w_ple_up": 1}
_FFN_HIDDEN_AXIS = {"w_ffn_gate": 1, "w_ffn_up": 1, "w_ffn_down": 0}
_VECTORS = ("ret_gn_g", "ln1_g", "ln1_b", "ln2_g", "ln2_b")
_WEIGHTS = ("w_in", "w_attn_out", "w_ret_out", "ret_gn_g", "w_o", "ln1_g", "ln1_b", "w_ffn_gate", "w_ffn_up", "w_ffn_down",
            "w_ple_gate", "w_ple_up", "ln2_g", "ln2_b")


def _local_grads(cfg, x, p, tgt, w, vec, coords=None):
    dist = coords is not None
    xb, pb = x.astype(BF16), p.astype(BF16)
    cos, sin = _rope_tables(cfg)
    lg = _log_gamma(cfg)
    wf = {"w_in": _all_gather(w["w_in"], _SHARD_AXIS["w_in"], "ag_w_in")} if dist else dict(w)
    grads, pairs, got, waiting = {}, {}, {}, []

    def gather_behind(names, matmul):
        if not dist:
            return matmul(())
        out, *landed = matmul([_gather_job(w[n], _SHARD_AXIS[n]) for n in names])
        for n, full in zip(names, landed):
            wf[n] = _all_gather(w[n], _SHARD_AXIS[n], "ag_" + n, landed=full)
        return out

    def grad(a, b, n, name):
        if not dist:
            grads[n] = _mm(a, b, "tn", BF16, name)
            return
        own, recv = _mm_tn_pair(a, b, _SHARD_AXIS[n], name)
        pairs[n] = (own, recv, _pairsum(own, recv, _SHARD_AXIS[n], coords, "rs_" + n + "_pairsum"))
        waiting.append(n)

    def scatter_behind(matmul):
        if not waiting:
            return matmul(())
        names = list(waiting)
        del waiting[:]
        out, *landed = matmul([_scatter_job(pairs[n][2]) for n in names])
        got.update(zip(names, landed))
        return out

    proj = gather_behind(["w_attn_out", "w_ret_out", "w_o", "w_ffn_gate", "w_ffn_up"],
                         lambda jobs: _mm(xb, wf["w_in"], "nn", BF16, "proj", jobs=jobs))
    q3 = _to_groups(cfg, proj[:, cfg.o_qa:cfg.o_qa + cfg.att_w])
    k3 = _to_groups(cfg, proj[:, cfg.o_ka:cfg.o_ka + cfg.att_w])
    v3 = _to_groups(cfg, proj[:, cfg.o_va:cfg.o_va + cfg.att_w])
    o3, lse3 = _attn_fwd(cfg, q3, k3, v3)
    o_att, lse = _attn_combine(cfg, jnp.stack(_from_groups(o3)), jnp.stack(_from_groups(lse3)))
    y_att = _mm(o_att, wf["w_attn_out"], "nn", BF16, "y_att")
    rn, rstd_r, states = _ret_fwd(cfg, proj, cos, sin, lg)
    rg = _ret_gate(cfg, proj, rn, vec["ret_gn_g"])
    y_ret = _mm(rg, wf["w_ret_out"], "nn", BF16, "y_ret")
    mixed = _mix(cfg, proj, y_att, y_ret)
    mo = _mm(mixed, wf["w_o"], "nn", F32, "mixed_out")
    xhat1, h1, rstd1 = _ln1(cfg, x, mo, vec["ln1_g"], vec["ln1_b"])
    u = gather_behind(["w_ffn_down", "w_ple_gate", "w_ple_up"],
                      lambda jobs: _mm(h1, wf["w_ffn_gate"], "nn", BF16, "ffn_u", jobs=jobs))
    t = _mm(h1, wf["w_ffn_up"], "nn", BF16, "ffn_t")
    a = _ffn_act(cfg, u, t)
    ffn = _mm(a, wf["w_ffn_down"], "nn", F32, "ffn_down")
    hp = _mm(h1, wf["w_ple_gate"], "nn", BF16, "ple_gate")
    pu = _mm(pb, wf["w_ple_up"], "nn", BF16, "ple_up")
    loss, dz2, dz2b, d_pg, d_pu, g_ln2g, g_ln2b = _head(cfg, xhat1, ffn, hp, pu, tgt, vec["ln1_g"], vec["ln1_b"],
                                                        vec["ln2_g"], vec["ln2_b"])

    d_a = _mm(dz2b, wf["w_ffn_down"], "nt", BF16, "d_a")
    d_u, d_t = _ffn_act_bwd(cfg, u, t, d_a)
    grad(a, dz2b, "w_ffn_down", "g_ffn_down")
    dh = scatter_behind(lambda jobs: _mm(d_u, wf["w_ffn_gate"], "nt", F32, "dh_u", jobs=jobs))
    grad(h1, d_u, "w_ffn_gate", "g_ffn_gate")
    dh = scatter_behind(lambda jobs: _mm(d_t, wf["w_ffn_up"], "nt", F32, "dh_t", add=dh, jobs=jobs))
    grad(h1, d_t, "w_ffn_up", "g_ffn_up")
    dh = scatter_behind(lambda jobs: _mm(d_pg, wf["w_ple_gate"], "nt", F32, "dh_pg", add=dh, jobs=jobs))
    grad(h1, d_pg, "w_ple_gate", "g_ple_gate")
    grad(pb, d_pu, "w_ple_up", "g_ple_up")
    dz1, dz1b, g_ln1g, g_ln1b = _ln1_bwd(cfg, dh, dz2, xhat1, rstd1, vec["ln1_g"])
    d_mixed = scatter_behind(lambda jobs: _mm(dz1b, wf["w_o"], "nt", BF16, "d_mixed", jobs=jobs))
    grad(mixed, dz1b, "w_o", "g_o")
    d_ya, d_yr, d_ga, d_g2 = _mix_bwd(cfg, proj, y_att, y_ret, d_mixed)
    d_oatt = _mm(d_ya, wf["w_attn_out"], "nt", BF16, "d_oatt")
    d_rg = scatter_behind(lambda jobs: _mm(d_yr, wf["w_ret_out"], "nt", BF16, "d_rg", jobs=jobs))
    grad(o_att, d_ya, "w_attn_out", "g_attn_out")
    grad(rg, d_yr, "w_ret_out", "g_ret_out")
    d_gr, d_r, g_gn = _ret_gate_bwd(cfg, proj, rn, vec["ret_gn_g"], rstd_r, d_rg)
    d_qr, d_kr, d_vr = _ret_bwd(cfg, proj, d_r, states, cos, sin, lg)
    delta = _attn_delta(cfg, d_oatt, o_att)
    do3 = jnp.stack([_deinterleave(d_oatt, dil) for _, dil in ATT_GROUPS])
    l3 = jnp.stack([_deinterleave(lse, dil) for _, dil in ATT_GROUPS])
    dl3 = jnp.stack([_deinterleave(delta, dil) for _, dil in ATT_GROUPS])
    dq3, dk3, dv3 = _attn_bwd(cfg, q3, k3, v3, do3, l3, dl3)
    dproj = jnp.concatenate(_from_groups(dq3) + _from_groups(dk3) + _from_groups(dv3) + [d_qr, d_kr, d_vr, d_gr, d_ga, d_g2], axis=1)
    grad(xb, dproj, "w_in", "g_in")
    grad_x = scatter_behind(lambda jobs: _mm(dproj, wf["w_in"], "nt", F32, "grad_x", add=dz1, add_scale=cfg.alpha, jobs=jobs))
    vgrads = {"ret_gn_g": g_gn, "ln1_g": g_ln1g, "ln1_b": g_ln1b, "ln2_g": g_ln2g, "ln2_b": g_ln2b}
    if dist:
        grads = {n: _sum_share(pairs[n][0], pairs[n][1], got[n], _SHARD_AXIS[n], coords, "rs_" + n + "_share") for n in pairs}
    return loss, grad_x, grads, vgrads


def _step(cfg, x, p, tgt, w, m, v):
    mats = [n for n in _WEIGHTS if n in _SHARD_AXIS]
    ffn_pad = cfg.ffn_shard - cfg.ffn // 4

    def shard_bf16(n):
        ws = w[n][0].astype(BF16)
        if n in _FFN_HIDDEN_AXIS and ffn_pad:
            ws = jnp.pad(ws, [(0, ffn_pad) if a == _FFN_HIDDEN_AXIS[n] else (0, 0) for a in (0, 1)])
        return ws

    def unpad(n, t):
        if n in _FFN_HIDDEN_AXIS and ffn_pad:
            return t[:cfg.ffn // 4] if _FFN_HIDDEN_AXIS[n] == 0 else t[:, :cfg.ffn // 4]
        return t

    vec = {n: w[n] for n in _VECTORS}
    coords = jnp.stack([lax.axis_index("x"), lax.axis_index("y"), lax.axis_index("c")]).astype(jnp.int32)
    loss, grad_x, grads, vgrads = _local_grads(cfg, x[0], p[0, 0], tgt[0], {n: shard_bf16(n) for n in mats}, vec, coords)
    loss = lax.psum(loss[0, 0], ("x", "y", "c"))
    red = {n: unpad(n, grads[n]) for n in mats}
    stacked = jnp.concatenate([vgrads[n] for n in _VECTORS] + [jnp.zeros((3, cfg.d), F32)], axis=0)
    vsum = _vec_all_reduce(stacked, "vec_all_reduce")
    for i, n in enumerate(_VECTORS):
        red[n] = vsum[i:i + 1]
    g_out, d_out, m_out, v_out = [], [], [], []
    for n in _WEIGHTS:
        dlt, mn, vn = _adamw(w[n][0] if n in _SHARD_AXIS else w[n], red[n], m[n][0] if n in _SHARD_AXIS else m[n],
                             v[n][0] if n in _SHARD_AXIS else v[n], "adamw_" + n)
        lead = (lambda t: t[None]) if n in _SHARD_AXIS else (lambda t: t)
        g_out.append(lead(red[n]))
        d_out.append(lead(dlt))
        m_out.append(lead(mn))
        v_out.append(lead(vn))
    return (loss, grad_x[None], *g_out, *d_out, *m_out, *v_out)


def kernel(x, p, w_in, w_attn_out, w_ret_out, ret_gn_g, w_o, ln1_g, ln1_b, w_ffn_gate, w_ffn_up, w_ffn_down, w_ple_gate, w_ple_up, ln2_g, ln2_b, loss_target, m_w_in, m_w_attn_out, m_w_ret_out, m_ret_gn_g, m_w_o, m_ln1_g, m_ln1_b, m_w_ffn_gate, m_w_ffn_up, m_w_ffn_down, m_w_ple_gate, m_w_ple_up, m_ln2_g, m_ln2_b, v_w_in, v_w_attn_out, v_w_ret_out, v_ret_gn_g, v_w_o, v_ln1_g, v_ln1_b, v_w_ffn_gate, v_w_ffn_up, v_w_ffn_down, v_w_ple_gate, v_w_ple_up, v_ln2_g, v_ln2_b):
    w = dict(zip(_WEIGHTS, (w_in, w_attn_out, w_ret_out, ret_gn_g, w_o, ln1_g, ln1_b, w_ffn_gate, w_ffn_up, w_ffn_down,
                            w_ple_gate, w_ple_up, ln2_g, ln2_b)))
    m = dict(zip(_WEIGHTS, (m_w_in, m_w_attn_out, m_w_ret_out, m_ret_gn_g, m_w_o, m_ln1_g, m_ln1_b, m_w_ffn_gate, m_w_ffn_up,
                            m_w_ffn_down, m_w_ple_gate, m_w_ple_up, m_ln2_g, m_ln2_b)))
    v = dict(zip(_WEIGHTS, (v_w_in, v_w_attn_out, v_w_ret_out, v_ret_gn_g, v_w_o, v_ln1_g, v_ln1_b, v_w_ffn_gate, v_w_ffn_up,
                            v_w_ffn_down, v_w_ple_gate, v_w_ple_up, v_ln2_g, v_ln2_b)))
    return _step(_FULL, x, p, loss_target, w, m, v)
```

```python
import functools
import math

import jax
import jax.numpy as jnp
import numpy as np
from jax import lax
from jax.experimental import pallas as pl
from jax.experimental.pallas import tpu as pltpu

F32 = jnp.float32
BF16 = jnp.bfloat16
MESH = pl.DeviceIdType.MESH
ANY = pl.BlockSpec(memory_space=pl.ANY)

ATT_BLOCK = 128
ATT_GROUPS = ((128, 1), (512, 4), (2048, 16))
LN_EPS = 1e-5
GN_EPS = 1e-6
NEG_INF = -1e30
ROPE_BASE = 10000.0
ADAM_LR, ADAM_B1, ADAM_B2, ADAM_EPS, ADAM_WD, ADAM_STEP = 0.001, 0.9, 0.999, 1e-08, 0.01, 10
VMEM_LIMIT = 56 * 1024 * 1024
STAGE_BYTES = 4 * 1024 * 1024
MM_VMEM_BYTES = 46 * 1024 * 1024 + 512 * 1024


class _Cfg:
    def __init__(self, seq, d, ple, ahd, ahg, rh, rqk, rv, ffn, cw):
        self.seq, self.d, self.ple, self.ahd, self.ahg = seq, d, ple, ahd, ahg
        self.rh, self.rqk, self.rv, self.ffn, self.cw = rh, rqk, rv, ffn, cw
        self.aw = ahg * ahd
        self.att_w = 3 * self.aw
        self.rqk_w = rh * rqk
        self.rv_w = rh * rv
        offs = np.cumsum([0] + [self.att_w] * 3 + [self.rqk_w] * 2 + [self.rv_w] * 2 + [d] * 2)
        (self.o_qa, self.o_ka, self.o_va, self.o_qr, self.o_kr, self.o_vr, self.o_gr, self.o_ga, self.o_g2,
         self.in_w) = [int(v) for v in offs]
        self.alpha = 2.0 ** 0.25
        self.ffn_shard = -(-(ffn // 4) // 128) * 128
        self.ffn_p = 4 * self.ffn_shard
        assert self.rv_w == d and d % cw == 0
        for o in (self.o_gr, self.o_ga, self.o_g2):
            assert o % cw == 0
        assert self.o_qr % rqk == 0 and self.o_kr % rqk == 0 and self.o_vr % rv == 0 and self.o_gr % rv == 0
        assert rqk // 2 % 128 == 0 and seq % (ATT_BLOCK * 16) == 0


_FULL = _Cfg(seq=4096, d=4096, ple=256, ahd=128, ahg=8, rh=8, rqk=256, rv=512, ffn=11008, cw=1024)


def _tile(n, target, q=128):
    t = min(n, target) // q * q
    while t >= q:
        if n % t == 0:
            return t
        t -= q
    return n


def _row_tile(n, row_bytes, target_bytes, q=8):
    best = None
    for t in range(q, n + 1, q):
        if n % t == 0 and (best is None or t * row_bytes <= target_bytes):
            best = t
            if t * row_bytes > target_bytes:
                break
    return n if best is None else best


def _mm_tiles(m, n, k, out_bytes_per_elem, tm=None):
    tm, tn = _tile(m, 1024) if tm is None else tm, _tile(n, 1024)
    tk = k if k <= 4096 else _tile(k, 2048)

    def need(tk_):
        acc = 4 if tk_ < k else 0
        return 4 * (tm + tn) * tk_ + tm * tn * (out_bytes_per_elem + acc + 4)

    while need(tk) > MM_VMEM_BYTES and tk > 256:
        tk = _tile(k, tk - 128)
    return tm, tn, tk


def _params(sem=None):
    return pltpu.CompilerParams(dimension_semantics=sem, vmem_limit_bytes=VMEM_LIMIT)


def _sigmoid(x):
    return 1.0 / (1.0 + jnp.exp(-x))


class _Job:
    def __init__(self, inputs, out_shapes, n_sems, copies):
        self.inputs, self.out_shapes, self.n_sems, self.copies = list(inputs), list(out_shapes), n_sems, copies


def _mm(a, b, mode, out_dtype, name, add=None, add_scale=1.0, jobs=()):
    if mode == "nn":
        (m, k), (k2, n) = a.shape, b.shape
    elif mode == "nt":
        (m, k), (n, k2) = a.shape, b.shape
    else:
        (k, m), (k2, n) = a.shape, b.shape
    assert k == k2, (name, a.shape, b.shape)
    has_add = add is not None
    tm, tn, tk = _mm_tiles(m, n, k, 2 * jnp.dtype(out_dtype).itemsize + (8 if has_add else 0))
    nk = k // tk
    if mode == "tn":
        a_spec = pl.BlockSpec((tk, tm), lambda i, j, kk: (kk, i))
        dims = (((0,), (0,)), ((), ()))
    else:
        a_spec = pl.BlockSpec((tm, tk), lambda i, j, kk: (i, kk))
        dims = (((1,), (1,)), ((), ())) if mode == "nt" else (((1,), (0,)), ((), ()))
    if mode == "nt":
        b_spec = pl.BlockSpec((tn, tk), lambda i, j, kk: (j, kk))
    else:
        b_spec = pl.BlockSpec((tk, tn), lambda i, j, kk: (kk, j))
    o_spec = pl.BlockSpec((tm, tn), lambda i, j, kk: (i, j))
    ni, nj = m // tm, n // tn
    jobs = list(jobs)
    n_main_in = 3 if has_add else 2
    n_job_in = sum(len(jb.inputs) for jb in jobs)
    n_job_out = sum(len(jb.out_shapes) for jb in jobs)
    n_acc = 1 if nk > 1 else 0

    def body(*refs):
        a_ref, b_ref = refs[0], refs[1]
        add_ref = refs[2] if has_add else None
        job_in = refs[n_main_in:n_main_in + n_job_in]
        o_ref = refs[n_main_in + n_job_in]
        job_out = refs[n_main_in + n_job_in + 1:n_main_in + n_job_in + 1 + n_job_out]
        acc_ref = refs[n_main_in + n_job_in + 1 + n_job_out] if n_acc else None
        job_sems = refs[n_main_in + n_job_in + 1 + n_acc + n_job_out:]
        i, j, kk = pl.program_id(0), pl.program_id(1), pl.program_id(2)

        def finish(r):
            if has_add:
                r = r + add_scale * add_ref[...]
            o_ref[...] = r.astype(o_ref.dtype)

        def job_copies():
            found, pi, po = [], 0, 0
            for jb, sems in zip(jobs, job_sems):
                found += jb.copies(job_in[pi:pi + len(jb.inputs)], job_out[po:po + len(jb.out_shapes)], sems)
                pi, po = pi + len(jb.inputs), po + len(jb.out_shapes)
            return found

        if jobs:
            @pl.when(jnp.logical_and(jnp.logical_and(i == 0, j == 0), kk == 0))
            def _():
                for cp, _ in job_copies():
                    cp.start()

        part = lax.dot_general(a_ref[...], b_ref[...], dims, preferred_element_type=F32)
        if n_acc:
            @pl.when(kk == 0)
            def _():
                acc_ref[...] = part

            @pl.when(kk > 0)
            def _():
                acc_ref[...] += part

            @pl.when(kk == nk - 1)
            def _():
                finish(acc_ref[...])
        else:
            finish(part)

        if jobs:
            @pl.when(jnp.logical_and(jnp.logical_and(i == ni - 1, j == nj - 1), kk == nk - 1))
            def _():
                for cp, _ in job_copies():
                    cp.wait()

    job_inputs = [t for jb in jobs for t in jb.inputs]
    outs = pl.pallas_call(
        body, name=name, grid=(ni, nj, nk),
        in_specs=[a_spec, b_spec] + ([o_spec] if has_add else []) + [ANY] * n_job_in, out_specs=[o_spec] + [ANY] * n_job_out,
        out_shape=[jax.ShapeDtypeStruct((m, n), out_dtype)] + [s for jb in jobs for s in jb.out_shapes],
        scratch_shapes=[pltpu.VMEM((tm, tn), F32)] * n_acc + [pltpu.SemaphoreType.DMA((jb.n_sems,)) for jb in jobs],
        compiler_params=_params(("arbitrary",) * 3 if jobs else ("parallel", "parallel", "arbitrary")),
    )(*((a, b, add) if has_add else (a, b)), *job_inputs)
    return outs if jobs else outs[0]


def _seg_blocks(cfg):
    return [cfg.seq // ATT_BLOCK // dil for _, dil in ATT_GROUPS]


def _group_select(g, vals):
    out = jnp.int32(vals[-1])
    for i in range(len(vals) - 2, -1, -1):
        out = jnp.where(g == i, jnp.int32(vals[i]), out)
    return out


def _col(tile, h):
    lane = lax.broadcasted_iota(jnp.int32, tile.shape, 1)
    return jnp.sum(jnp.where(lane == h, tile, 0.0), axis=1, keepdims=True)


def _set_col(tile, h, col):
    lane = lax.broadcasted_iota(jnp.int32, tile.shape, 1)
    return jnp.where(lane == h, col, tile)


_NT = (((1,), (1,)), ((), ()))
_TN = (((0,), (0,)), ((), ()))


def _dot(a, b):
    return jnp.dot(a, b, preferred_element_type=F32)


def _dot_nt(a, b):
    return lax.dot_general(a, b, _NT, preferred_element_type=F32)


def _dot_tn(a, b):
    return lax.dot_general(a, b, _TN, preferred_element_type=F32)


def _attn_fwd(cfg, q3, k3, v3):
    nb = cfg.seq // ATT_BLOCK
    segs = _seg_blocks(cfg)
    scale = cfg.ahd ** -0.5
    blk = ATT_BLOCK

    def body(q_ref, kp_ref, kc_ref, vp_ref, vc_ref, o_ref, lse_ref):
        g, b = pl.program_id(0), pl.program_id(1)
        has_prev = (b & (_group_select(g, segs) - 1)) != 0
        qi = lax.broadcasted_iota(jnp.int32, (blk, blk), 0)
        kj = lax.broadcasted_iota(jnp.int32, (blk, blk), 1)
        valid_c = kj <= qi
        valid_p = jnp.logical_and(kj >= qi, has_prev)
        lse = jnp.zeros((blk, cfg.ahg), F32)
        for h in range(cfg.ahg):
            hs = slice(h * cfg.ahd, (h + 1) * cfg.ahd)
            q = q_ref[:, hs]
            s_c = jnp.where(valid_c, _dot_nt(q, kc_ref[:, hs]) * scale, NEG_INF)
            s_p = jnp.where(valid_p, _dot_nt(q, kp_ref[:, hs]) * scale, NEG_INF)
            m = jnp.maximum(jnp.max(s_c, axis=1, keepdims=True), jnp.max(s_p, axis=1, keepdims=True))
            p_c, p_p = jnp.exp(s_c - m), jnp.exp(s_p - m)
            den = jnp.sum(p_c, axis=1, keepdims=True) + jnp.sum(p_p, axis=1, keepdims=True)
            o = _dot(p_c.astype(BF16), vc_ref[:, hs]) + _dot(p_p.astype(BF16), vp_ref[:, hs])
            o_ref[:, hs] = (o / den).astype(o_ref.dtype)
            lse = _set_col(lse, h, m + jnp.log(den))
        lse_ref[...] = lse

    cur = pl.BlockSpec((None, blk, cfg.aw), lambda g, b: (g, b, 0))
    prev = pl.BlockSpec((None, blk, cfg.aw), lambda g, b: (g, jnp.maximum(b - 1, 0), 0))
    return pl.pallas_call(
        body, name="attn_fwd", grid=(3, nb), in_specs=[cur, prev, cur, prev, cur],
        out_specs=[cur, pl.BlockSpec((None, blk, cfg.ahg), lambda g, b: (g, b, 0))],
        out_shape=[jax.ShapeDtypeStruct((3, cfg.seq, cfg.aw), BF16), jax.ShapeDtypeStruct((3, cfg.seq, cfg.ahg), F32)],
        compiler_params=_params(("parallel", "parallel")),
    )(q3, k3, k3, v3, v3)


def _attn_combine(cfg, o3, lse3):
    tm = 256

    def body(o_ref, lse_ref, oa_ref, l_ref):
        l0, l1, l2 = lse_ref[0], lse_ref[1], lse_ref[2]
        m = jnp.maximum(jnp.maximum(l0, l1), l2)
        big = m + jnp.log(jnp.exp(l0 - m) + jnp.exp(l1 - m) + jnp.exp(l2 - m))
        ws = [jnp.exp(l0 - big), jnp.exp(l1 - big), jnp.exp(l2 - big)]
        for h in range(cfg.ahg):
            hs = slice(h * cfg.ahd, (h + 1) * cfg.ahd)
            acc = _col(ws[0], h) * o_ref[0, :, hs].astype(F32)
            acc += _col(ws[1], h) * o_ref[1, :, hs].astype(F32)
            acc += _col(ws[2], h) * o_ref[2, :, hs].astype(F32)
            oa_ref[:, hs] = acc.astype(oa_ref.dtype)
        l_ref[...] = big

    return pl.pallas_call(
        body, name="attn_combine", grid=(cfg.seq // tm,),
        in_specs=[pl.BlockSpec((3, tm, cfg.aw), lambda i: (0, i, 0)), pl.BlockSpec((3, tm, cfg.ahg), lambda i: (0, i, 0))],
        out_specs=[pl.BlockSpec((tm, cfg.aw), lambda i: (i, 0)), pl.BlockSpec((tm, cfg.ahg), lambda i: (i, 0))],
        out_shape=[jax.ShapeDtypeStruct((cfg.seq, cfg.aw), BF16), jax.ShapeDtypeStruct((cfg.seq, cfg.ahg), F32)],
        compiler_params=_params(("parallel",)),
    )(o3, lse3)


def _attn_delta(cfg, do, o):
    tm = 256

    def body(do_ref, o_ref, d_ref):
        out = jnp.zeros((tm, cfg.ahg), F32)
        for h in range(cfg.ahg):
            hs = slice(h * cfg.ahd, (h + 1) * cfg.ahd)
            prod = do_ref[:, hs].astype(F32) * o_ref[:, hs].astype(F32)
            out = _set_col(out, h, jnp.sum(prod, axis=1, keepdims=True))
        d_ref[...] = out

    row = pl.BlockSpec((tm, cfg.aw), lambda i: (i, 0))
    return pl.pallas_call(
        body, name="attn_delta", grid=(cfg.seq // tm,), in_specs=[row, row],
        out_specs=pl.BlockSpec((tm, cfg.ahg), lambda i: (i, 0)),
        out_shape=jax.ShapeDtypeStruct((cfg.seq, cfg.ahg), F32), compiler_params=_params(("parallel",)),
    )(do, o)


def _attn_bwd(cfg, q3, k3, v3, do3, l3, d3):
    nb = cfg.seq // ATT_BLOCK
    segs = _seg_blocks(cfg)
    scale = cfg.ahd ** -0.5
    blk = ATT_BLOCK

    def body(q_ref, qn_ref, do_ref, don_ref, l_ref, ln_ref, d_ref, dn_ref, kp_ref, kc_ref, vp_ref, vc_ref,
             dq_ref, dk_ref, dv_ref):
        g, b = pl.program_id(0), pl.program_id(1)
        seg_mask = _group_select(g, segs) - 1
        has_prev = (b & seg_mask) != 0
        has_next = jnp.logical_and(b + 1 < nb, ((b + 1) & seg_mask) != 0)
        qi = lax.broadcasted_iota(jnp.int32, (blk, blk), 0)
        kj = lax.broadcasted_iota(jnp.int32, (blk, blk), 1)
        valid_c = kj <= qi
        valid_p = jnp.logical_and(kj >= qi, has_prev)
        valid_n = jnp.logical_and(kj >= qi, has_next)
        lse, lse_n, dlt, dlt_n = l_ref[...], ln_ref[...], d_ref[...], dn_ref[...]
        for h in range(cfg.ahg):
            hs = slice(h * cfg.ahd, (h + 1) * cfg.ahd)
            q, qn, do, don = q_ref[:, hs], qn_ref[:, hs], do_ref[:, hs], don_ref[:, hs]
            kc, kp, vc, vp = kc_ref[:, hs], kp_ref[:, hs], vc_ref[:, hs], vp_ref[:, hs]
            lh, lnh, dh, dnh = _col(lse, h), _col(lse_n, h), _col(dlt, h), _col(dlt_n, h)
            p_c = jnp.where(valid_c, jnp.exp(_dot_nt(q, kc) * scale - lh), 0.0)
            p_p = jnp.where(valid_p, jnp.exp(_dot_nt(q, kp) * scale - lh), 0.0)
            ds_c = (p_c * (_dot_nt(do, vc) - dh)).astype(BF16)
            ds_p = (p_p * (_dot_nt(do, vp) - dh)).astype(BF16)
            dq_ref[:, hs] = ((_dot(ds_c, kc) + _dot(ds_p, kp)) * scale).astype(dq_ref.dtype)
            p_n = jnp.where(valid_n, jnp.exp(_dot_nt(qn, kc) * scale - lnh), 0.0)
            ds_n = (p_n * (_dot_nt(don, vc) - dnh)).astype(BF16)
            dk = _dot_tn(ds_c, q) + _dot_tn(ds_n, qn)
            dv = _dot_tn(p_c.astype(BF16), do) + _dot_tn(p_n.astype(BF16), don)
            dk_ref[:, hs] = (dk * scale).astype(dk_ref.dtype)
            dv_ref[:, hs] = dv.astype(dv_ref.dtype)

    def spec(width, shift):
        if shift == 0:
            return pl.BlockSpec((None, blk, width), lambda g, b: (g, b, 0))
        if shift > 0:
            return pl.BlockSpec((None, blk, width), lambda g, b: (g, jnp.minimum(b + 1, nb - 1), 0))
        return pl.BlockSpec((None, blk, width), lambda g, b: (g, jnp.maximum(b - 1, 0), 0))

    w, hw = cfg.aw, cfg.ahg
    out = jax.ShapeDtypeStruct((3, cfg.seq, w), BF16)
    return pl.pallas_call(
        body, name="attn_bwd", grid=(3, nb),
        in_specs=[spec(w, 0), spec(w, 1), spec(w, 0), spec(w, 1), spec(hw, 0), spec(hw, 1), spec(hw, 0), spec(hw, 1),
                  spec(w, -1), spec(w, 0), spec(w, -1), spec(w, 0)],
        out_specs=[spec(w, 0)] * 3, out_shape=[out, out, out], compiler_params=_params(("parallel", "parallel")),
    )(q3, q3, do3, do3, l3, l3, d3, d3, k3, k3, v3, v3)


def _deinterleave(t, dil):
    s, w = t.shape
    return t if dil == 1 else t.reshape(s // dil, dil, w).transpose(1, 0, 2).reshape(s, w)


def _interleave(t, dil):
    s, w = t.shape
    return t if dil == 1 else t.reshape(dil, s // dil, w).transpose(1, 0, 2).reshape(s, w)


def _to_groups(cfg, t):
    return jnp.stack([_deinterleave(t[:, g * cfg.aw:(g + 1) * cfg.aw], dil) for g, (_, dil) in enumerate(ATT_GROUPS)])


def _from_groups(t3):
    return [_interleave(t3[g], dil) for g, (_, dil) in enumerate(ATT_GROUPS)]


def _rope_tables(cfg):
    half = cfg.rqk // 2
    pos = jnp.arange(cfg.seq, dtype=F32)
    inv_freq = ROPE_BASE ** (-jnp.arange(half, dtype=F32) / half)
    ang = pos[:, None] * inv_freq[None, :]
    return jnp.cos(ang), jnp.sin(ang)


def _log_gamma(cfg):
    lg = jnp.log(1.0 - 2.0 ** (-5.0 - jnp.arange(cfg.rh, dtype=F32)))
    return jnp.broadcast_to(lg[:, None, None], (cfg.rh, 1, max(cfg.rqk, cfg.rv)))


def _rot(t, c, s, half):
    t1, t2 = t[:, :half], t[:, half:]
    return jnp.concatenate([t1 * c - t2 * s, t1 * s + t2 * c], axis=1)


def _unrot(d, c, s, half):
    d1, d2 = d[:, :half], d[:, half:]
    return jnp.concatenate([d1 * c + d2 * s, d2 * c - d1 * s], axis=1)


def _decays(lg_ref, cfg):
    blk = ATT_BLOCK
    lg_v, lg_k = lg_ref[:, :cfg.rv], lg_ref[:, :cfg.rqk]
    qi = lax.broadcasted_iota(jnp.int32, (blk, blk), 0)
    kj = lax.broadcasted_iota(jnp.int32, (blk, blk), 1)
    diff = (qi - kj).astype(F32)
    intra = jnp.where(diff >= 0, jnp.exp(jnp.maximum(diff, 0.0) * lg_ref[:, :blk]), 0.0)
    idx_v = lax.broadcasted_iota(jnp.int32, (blk, cfg.rv), 0).astype(F32)
    idx_k = lax.broadcasted_iota(jnp.int32, (blk, cfg.rqk), 0).astype(F32)
    cross = jnp.exp((idx_v + 1.0) * lg_v)
    state = jnp.exp((blk - 1.0 - idx_k) * lg_k)
    chunk = jnp.exp(float(blk) * lg_v)
    return intra, cross, state, chunk


def _ret_fwd(cfg, proj, cos, sin, lg):
    blk, nc, half = ATT_BLOCK, cfg.seq // ATT_BLOCK, cfg.rqk // 2
    kscale = cfg.rqk ** -0.5

    def body(q_ref, k_ref, v_ref, cos_ref, sin_ref, lg_ref, rn_ref, rstd_ref, st_ref, state):
        n = pl.program_id(1)

        @pl.when(n == 0)
        def _():
            state[...] = jnp.zeros_like(state)

        intra, cross_d, state_d, chunk_d = _decays(lg_ref, cfg)
        c, s = cos_ref[...], sin_ref[...]
        qb = _rot(q_ref[...].astype(F32), c, s, half).astype(BF16)
        kf = _rot(k_ref[...].astype(F32), c, s, half) * kscale
        vb = v_ref[...]
        prev = state[...]
        st_ref[...] = prev.astype(BF16)
        att = _dot_nt(qb, kf.astype(BF16)) * intra
        out = _dot(att.astype(BF16), vb) + _dot(qb, prev.astype(BF16)) * cross_d
        state[...] = chunk_d * prev + _dot_tn((kf * state_d).astype(BF16), vb)
        mu = jnp.mean(out, axis=1, keepdims=True)
        cen = out - mu
        rstd = lax.rsqrt(jnp.mean(cen * cen, axis=1, keepdims=True) + GN_EPS)
        rn_ref[...] = (cen * rstd).astype(rn_ref.dtype)
        rstd_ref[...] = rstd

    oq, ok, ov = cfg.o_qr // cfg.rqk, cfg.o_kr // cfg.rqk, cfg.o_vr // cfg.rv
    tab = pl.BlockSpec((blk, half), lambda h, n: (n, 0))
    return pl.pallas_call(
        body, name="ret_fwd", grid=(cfg.rh, nc),
        in_specs=[pl.BlockSpec((blk, cfg.rqk), lambda h, n: (n, oq + h)), pl.BlockSpec((blk, cfg.rqk), lambda h, n: (n, ok + h)),
                  pl.BlockSpec((blk, cfg.rv), lambda h, n: (n, ov + h)), tab, tab,
                  pl.BlockSpec((None, 1, lg.shape[2]), lambda h, n: (h, 0, 0))],
        out_specs=[pl.BlockSpec((blk, cfg.rv), lambda h, n: (n, h)), pl.BlockSpec((None, blk, 1), lambda h, n: (h, n, 0)),
                   pl.BlockSpec((None, None, cfg.rqk, cfg.rv), lambda h, n: (h, n, 0, 0))],
        out_shape=[jax.ShapeDtypeStruct((cfg.seq, cfg.rv_w), BF16), jax.ShapeDtypeStruct((cfg.rh, cfg.seq, 1), F32),
                   jax.ShapeDtypeStruct((cfg.rh, nc, cfg.rqk, cfg.rv), BF16)],
        scratch_shapes=[pltpu.VMEM((cfg.rqk, cfg.rv), F32)], compiler_params=_params(("parallel", "arbitrary")),
    )(proj, proj, proj, cos, sin, lg)


def _ret_bwd(cfg, proj, d_r, states, cos, sin, lg):
    blk, nc, half = ATT_BLOCK, cfg.seq // ATT_BLOCK, cfg.rqk // 2
    kscale = cfg.rqk ** -0.5

    def body(q_ref, k_ref, v_ref, do_ref, st_ref, cos_ref, sin_ref, lg_ref, dq_ref, dk_ref, dv_ref, dstate):
        n = pl.program_id(1)

        @pl.when(n == 0)
        def _():
            dstate[...] = jnp.zeros_like(dstate)

        intra, cross_d, state_d, chunk_d = _decays(lg_ref, cfg)
        c, s = cos_ref[...], sin_ref[...]
        qb = _rot(q_ref[...].astype(F32), c, s, half).astype(BF16)
        kf = _rot(k_ref[...].astype(F32), c, s, half) * kscale
        kb, ksb = kf.astype(BF16), (kf * state_d).astype(BF16)
        vb, prev = v_ref[...], st_ref[...]
        do = do_ref[...].astype(F32)
        dob, docb = do.astype(BF16), (do * cross_d).astype(BF16)
        dsb = dstate[...].astype(BF16)
        att = (_dot_nt(qb, kb) * intra).astype(BF16)
        datt = (_dot_nt(dob, vb) * intra).astype(BF16)
        d_q = _dot(datt, kb) + _dot_nt(docb, prev)
        d_k = _dot_tn(datt, qb) + _dot_nt(vb, dsb) * state_d
        d_v = _dot_tn(att, dob) + _dot(ksb, dsb)
        dstate[...] = chunk_d * dstate[...] + _dot_tn(qb, docb)
        dq_ref[...] = _unrot(d_q, c, s, half).astype(dq_ref.dtype)
        dk_ref[...] = _unrot(d_k * kscale, c, s, half).astype(dk_ref.dtype)
        dv_ref[...] = d_v.astype(dv_ref.dtype)

    oq, ok, ov = cfg.o_qr // cfg.rqk, cfg.o_kr // cfg.rqk, cfg.o_vr // cfg.rv
    last = nc - 1
    tab = pl.BlockSpec((blk, half), lambda h, n: (last - n, 0))
    qk_out = pl.BlockSpec((blk, cfg.rqk), lambda h, n: (last - n, h))
    v_out = pl.BlockSpec((blk, cfg.rv), lambda h, n: (last - n, h))
    return pl.pallas_call(
        body, name="ret_bwd", grid=(cfg.rh, nc),
        in_specs=[pl.BlockSpec((blk, cfg.rqk), lambda h, n: (last - n, oq + h)),
                  pl.BlockSpec((blk, cfg.rqk), lambda h, n: (last - n, ok + h)),
                  pl.BlockSpec((blk, cfg.rv), lambda h, n: (last - n, ov + h)), v_out,
                  pl.BlockSpec((None, None, cfg.rqk, cfg.rv), lambda h, n: (h, last - n, 0, 0)), tab, tab,
                  pl.BlockSpec((None, 1, lg.shape[2]), lambda h, n: (h, 0, 0))],
        out_specs=[qk_out, qk_out, v_out],
        out_shape=[jax.ShapeDtypeStruct((cfg.seq, cfg.rqk_w), BF16), jax.ShapeDtypeStruct((cfg.seq, cfg.rqk_w), BF16),
                   jax.ShapeDtypeStruct((cfg.seq, cfg.rv_w), BF16)],
        scratch_shapes=[pltpu.VMEM((cfg.rqk, cfg.rv), F32)], compiler_params=_params(("parallel", "arbitrary")),
    )(proj, proj, proj, d_r, states, cos, sin, lg)


def _ret_gate(cfg, proj, rn, gn_g):
    tm, cw = 256, cfg.cw
    og = cfg.o_gr // cw

    def body(g_ref, rn_ref, w_ref, o_ref):
        g = g_ref[...].astype(F32)
        o_ref[...] = (g * _sigmoid(g) * (rn_ref[...].astype(F32) * w_ref[...])).astype(o_ref.dtype)

    blk = pl.BlockSpec((tm, cw), lambda i, j: (i, j))
    return pl.pallas_call(
        body, name="ret_gate", grid=(cfg.seq // tm, cfg.rv_w // cw),
        in_specs=[pl.BlockSpec((tm, cw), lambda i, j: (i, og + j)), blk, pl.BlockSpec((1, cw), lambda i, j: (0, j))],
        out_specs=blk, out_shape=jax.ShapeDtypeStruct((cfg.seq, cfg.rv_w), BF16),
        compiler_params=_params(("parallel", "parallel")),
    )(proj, rn, gn_g)


def _mix(cfg, proj, y_att, y_ret):
    tm, cw = 256, cfg.cw
    oa, o2 = cfg.o_ga // cw, cfg.o_g2 // cw

    def body(ga_ref, g2_ref, ya_ref, yr_ref, o_ref):
        r = _sigmoid(ga_ref[...].astype(F32)) * ya_ref[...].astype(F32)
        r += _sigmoid(g2_ref[...].astype(F32)) * yr_ref[...].astype(F32)
        o_ref[...] = r.astype(o_ref.dtype)

    blk = pl.BlockSpec((tm, cw), lambda i, j: (i, j))
    return pl.pallas_call(
        body, name="mix", grid=(cfg.seq // tm, cfg.d // cw),
        in_specs=[pl.BlockSpec((tm, cw), lambda i, j: (i, oa + j)), pl.BlockSpec((tm, cw), lambda i, j: (i, o2 + j)), blk, blk],
        out_specs=blk, out_shape=jax.ShapeDtypeStruct((cfg.seq, cfg.d), BF16),
        compiler_params=_params(("parallel", "parallel")),
    )(proj, proj, y_att, y_ret)


def _mix_bwd(cfg, proj, y_att, y_ret, d_mixed):
    tm, cw = 256, cfg.cw
    oa, o2 = cfg.o_ga // cw, cfg.o_g2 // cw

    def body(ga_ref, g2_ref, ya_ref, yr_ref, dm_ref, dya_ref, dyr_ref, dga_ref, dg2_ref):
        dm = dm_ref[...].astype(F32)
        sa, s2 = _sigmoid(ga_ref[...].astype(F32)), _sigmoid(g2_ref[...].astype(F32))
        dya_ref[...] = (dm * sa).astype(dya_ref.dtype)
        dyr_ref[...] = (dm * s2).astype(dyr_ref.dtype)
        dga_ref[...] = (dm * ya_ref[...].astype(F32) * sa * (1.0 - sa)).astype(dga_ref.dtype)
        dg2_ref[...] = (dm * yr_ref[...].astype(F32) * s2 * (1.0 - s2)).astype(dg2_ref.dtype)

    blk = pl.BlockSpec((tm, cw), lambda i, j: (i, j))
    out = jax.ShapeDtypeStruct((cfg.seq, cfg.d), BF16)
    return pl.pallas_call(
        body, name="mix_bwd", grid=(cfg.seq // tm, cfg.d // cw),
        in_specs=[pl.BlockSpec((tm, cw), lambda i, j: (i, oa + j)), pl.BlockSpec((tm, cw), lambda i, j: (i, o2 + j)), blk, blk, blk],
        out_specs=[blk] * 4, out_shape=[out] * 4, compiler_params=_params(("parallel", "parallel")),
    )(proj, proj, y_att, y_ret, d_mixed)


def _ret_gate_bwd(cfg, proj, rn, gn_g, rstd, d_rg):
    tm, rv = 256, cfg.rv
    og = cfg.o_gr // rv

    def body(g_ref, rn_ref, w_ref, rstd_ref, drg_ref, dg_ref, dr_ref, gw_ref):
        i = pl.program_id(1)
        g, rn, w = g_ref[...].astype(F32), rn_ref[...].astype(F32), w_ref[...]
        drg = drg_ref[...].astype(F32)
        sg = _sigmoid(g)
        silu = g * sg
        dg_ref[...] = (drg * (rn * w) * (sg * (1.0 + g * (1.0 - sg)))).astype(dg_ref.dtype)
        drn = drg * silu * w
        part = jnp.sum(drg * silu * rn, axis=0, keepdims=True)

        @pl.when(i == 0)
        def _():
            gw_ref[...] = part

        @pl.when(i > 0)
        def _():
            gw_ref[...] += part

        m1 = jnp.mean(drn, axis=1, keepdims=True)
        m2 = jnp.mean(drn * rn, axis=1, keepdims=True)
        dr_ref[...] = (rstd_ref[...] * (drn - m1 - rn * m2)).astype(dr_ref.dtype)

    blk = pl.BlockSpec((tm, rv), lambda h, i: (i, h))
    vec = pl.BlockSpec((1, rv), lambda h, i: (0, h))
    out = jax.ShapeDtypeStruct((cfg.seq, cfg.rv_w), BF16)
    return pl.pallas_call(
        body, name="ret_gate_bwd", grid=(cfg.rh, cfg.seq // tm),
        in_specs=[pl.BlockSpec((tm, rv), lambda h, i: (i, og + h)), blk, vec, pl.BlockSpec((None, tm, 1), lambda h, i: (h, i, 0)), blk],
        out_specs=[blk, blk, vec], out_shape=[out, out, jax.ShapeDtypeStruct((1, cfg.rv_w), F32)],
        compiler_params=_params(("parallel", "arbitrary")),
    )(proj, rn, gn_g, rstd, d_rg)


def _ln1(cfg, x, mo, g, b):
    tm, d = 128, cfg.d

    def body(x_ref, mo_ref, g_ref, b_ref, xh_ref, h_ref, rstd_ref):
        z = cfg.alpha * x_ref[...] + mo_ref[...]
        cen = z - jnp.mean(z, axis=1, keepdims=True)
        rstd = lax.rsqrt(jnp.mean(cen * cen, axis=1, keepdims=True) + LN_EPS)
        xh = cen * rstd
        xh_ref[...] = xh
        h_ref[...] = (xh * g_ref[...] + b_ref[...]).astype(h_ref.dtype)
        rstd_ref[...] = rstd

    row = pl.BlockSpec((tm, d), lambda i: (i, 0))
    vec = pl.BlockSpec((1, d), lambda i: (0, 0))
    col = pl.BlockSpec((tm, 1), lambda i: (i, 0))
    return pl.pallas_call(
        body, name="ln1", grid=(cfg.seq // tm,), in_specs=[row, row, vec, vec], out_specs=[row, row, col],
        out_shape=[jax.ShapeDtypeStruct((cfg.seq, d), F32), jax.ShapeDtypeStruct((cfg.seq, d), BF16),
                   jax.ShapeDtypeStruct((cfg.seq, 1), F32)],
        compiler_params=_params(("parallel",)),
    )(x, mo, g, b)


def _ffn_act(cfg, u, t):
    tm, cf = 256, _tile(cfg.ffn_p, 2048)

    def body(u_ref, t_ref, a_ref):
        uu = u_ref[...].astype(F32)
        a_ref[...] = (uu * _sigmoid(uu) * t_ref[...].astype(F32)).astype(a_ref.dtype)

    blk = pl.BlockSpec((tm, cf), lambda i, j: (i, j))
    return pl.pallas_call(
        body, name="ffn_act", grid=(cfg.seq // tm, cfg.ffn_p // cf), in_specs=[blk, blk], out_specs=blk,
        out_shape=jax.ShapeDtypeStruct((cfg.seq, cfg.ffn_p), BF16), compiler_params=_params(("parallel", "parallel")),
    )(u, t)


def _ffn_act_bwd(cfg, u, t, d_a):
    tm, cf = 256, _tile(cfg.ffn_p, 2048)

    def body(u_ref, t_ref, da_ref, du_ref, dt_ref):
        uu, da = u_ref[...].astype(F32), da_ref[...].astype(F32)
        sg = _sigmoid(uu)
        du_ref[...] = (da * t_ref[...].astype(F32) * (sg * (1.0 + uu * (1.0 - sg)))).astype(du_ref.dtype)
        dt_ref[...] = (da * uu * sg).astype(dt_ref.dtype)

    blk = pl.BlockSpec((tm, cf), lambda i, j: (i, j))
    out = jax.ShapeDtypeStruct((cfg.seq, cfg.ffn_p), BF16)
    return pl.pallas_call(
        body, name="ffn_act_bwd", grid=(cfg.seq // tm, cfg.ffn_p // cf), in_specs=[blk, blk, blk], out_specs=[blk, blk],
        out_shape=[out, out], compiler_params=_params(("parallel", "parallel")),
    )(u, t, d_a)


def _head(cfg, xhat1, ffn, hp, pu, tgt, g1, b1, g2, b2):
    tm, d = 64, cfg.d

    def body(xh_ref, ffn_ref, hp_ref, pu_ref, tgt_ref, g1_ref, b1_ref, g2_ref, b2_ref,
             loss_ref, dzf_ref, dzb_ref, dpg_ref, dpu_ref, gg_ref, gb_ref):
        i = pl.program_id(0)
        h1 = xh_ref[...] * g1_ref[...] + b1_ref[...]
        sg, pu = _sigmoid(hp_ref[...].astype(F32)), pu_ref[...].astype(F32)
        z = cfg.alpha * h1 + ffn_ref[...] + sg * pu
        cen = z - jnp.mean(z, axis=1, keepdims=True)
        rstd = lax.rsqrt(jnp.mean(cen * cen, axis=1, keepdims=True) + LN_EPS)
        xh2 = cen * rstd
        err = xh2 * g2_ref[...] + b2_ref[...] - tgt_ref[...]
        dy = err * (1.0 / d)
        part_l = jnp.sum(jnp.sum(err * err, axis=1, keepdims=True), axis=0, keepdims=True) * (0.5 / d)
        part_g = jnp.sum(dy * xh2, axis=0, keepdims=True)
        part_b = jnp.sum(dy, axis=0, keepdims=True)

        @pl.when(i == 0)
        def _():
            loss_ref[...] = jnp.zeros_like(loss_ref)
            gg_ref[...] = jnp.zeros_like(gg_ref)
            gb_ref[...] = jnp.zeros_like(gb_ref)

        loss_ref[...] += jnp.broadcast_to(part_l, loss_ref.shape)
        gg_ref[...] += part_g
        gb_ref[...] += part_b
        dxh = dy * g2_ref[...]
        m1 = jnp.mean(dxh, axis=1, keepdims=True)
        m2 = jnp.mean(dxh * xh2, axis=1, keepdims=True)
        dz = rstd * (dxh - m1 - xh2 * m2)
        dzf_ref[...] = dz
        dzb_ref[...] = dz.astype(dzb_ref.dtype)
        dpg_ref[...] = (dz * pu * sg * (1.0 - sg)).astype(dpg_ref.dtype)
        dpu_ref[...] = (dz * sg).astype(dpu_ref.dtype)

    row = pl.BlockSpec((tm, d), lambda i: (i, 0))
    vec = pl.BlockSpec((1, d), lambda i: (0, 0))
    bf = jax.ShapeDtypeStruct((cfg.seq, d), BF16)
    vec_out = jax.ShapeDtypeStruct((1, d), F32)
    return pl.pallas_call(
        body, name="head", grid=(cfg.seq // tm,), in_specs=[row] * 5 + [vec] * 4,
        out_specs=[pl.BlockSpec((1, 128), lambda i: (0, 0)), row, row, row, row, vec, vec],
        out_shape=[jax.ShapeDtypeStruct((1, 128), F32), jax.ShapeDtypeStruct((cfg.seq, d), F32), bf, bf, bf, vec_out, vec_out],
        compiler_params=_params(("arbitrary",)),
    )(xhat1, ffn, hp, pu, tgt, g1, b1, g2, b2)


def _ln1_bwd(cfg, dh_mm, dz2, xhat1, rstd1, g1):
    tm, d = 128, cfg.d

    def body(dh_ref, dz_ref, xh_ref, rstd_ref, g_ref, dzf_ref, dzb_ref, gg_ref, gb_ref):
        i = pl.program_id(0)
        dh = cfg.alpha * dz_ref[...] + dh_ref[...]
        xh = xh_ref[...]

        @pl.when(i == 0)
        def _():
            gg_ref[...] = jnp.zeros_like(gg_ref)
            gb_ref[...] = jnp.zeros_like(gb_ref)

        gg_ref[...] += jnp.sum(dh * xh, axis=0, keepdims=True)
        gb_ref[...] += jnp.sum(dh, axis=0, keepdims=True)
        dxh = dh * g_ref[...]
        m1 = jnp.mean(dxh, axis=1, keepdims=True)
        m2 = jnp.mean(dxh * xh, axis=1, keepdims=True)
        dz = rstd_ref[...] * (dxh - m1 - xh * m2)
        dzf_ref[...] = dz
        dzb_ref[...] = dz.astype(dzb_ref.dtype)

    row = pl.BlockSpec((tm, d), lambda i: (i, 0))
    vec = pl.BlockSpec((1, d), lambda i: (0, 0))
    vec_out = jax.ShapeDtypeStruct((1, d), F32)
    return pl.pallas_call(
        body, name="ln1_bwd", grid=(cfg.seq // tm,),
        in_specs=[row, row, row, pl.BlockSpec((tm, 1), lambda i: (i, 0)), vec], out_specs=[row, row, vec, vec],
        out_shape=[jax.ShapeDtypeStruct((cfg.seq, d), F32), jax.ShapeDtypeStruct((cfg.seq, d), BF16), vec_out, vec_out],
        compiler_params=_params(("arbitrary",)),
    )(dh_mm, dz2, xhat1, rstd1, g1)


def _adamw(w, g, m, v, name):
    r, c = w.shape
    tr, tc = _row_tile(r, c * 4, 2 * 1024 * 1024), c
    bc1, bc2 = 1.0 - ADAM_B1 ** ADAM_STEP, 1.0 - ADAM_B2 ** ADAM_STEP

    def body(w_ref, g_ref, m_ref, v_ref, d_ref, mo_ref, vo_ref):
        gg = g_ref[...]
        mn = ADAM_B1 * m_ref[...] + (1.0 - ADAM_B1) * gg
        vn = ADAM_B2 * v_ref[...] + (1.0 - ADAM_B2) * (gg * gg)
        d_ref[...] = -ADAM_LR * ((mn / bc1) / (jnp.sqrt(vn / bc2) + ADAM_EPS) + ADAM_WD * w_ref[...])
        mo_ref[...] = mn
        vo_ref[...] = vn

    blk = pl.BlockSpec((tr, tc), lambda i, j: (i, j))
    out = jax.ShapeDtypeStruct((r, c), F32)
    return pl.pallas_call(
        body, name=name, grid=(r // tr, c // tc), in_specs=[blk] * 4, out_specs=[blk] * 3, out_shape=[out] * 3,
        compiler_params=_params(("parallel", "parallel")),
    )(w, g, m, v)


def _place():
    x, y, c = lax.axis_index("x"), lax.axis_index("y"), lax.axis_index("c")
    others = [(1 - x, y), (x, 1 - y), (1 - x, 1 - y)]
    return x, y, c, others


def _all_gather(ws, ax, name, landed=None):
    r, cdim = ws.shape
    full_shape = (4 * r, cdim) if ax == 0 else (r, 4 * cdim)
    n_shard = r if ax == 0 else cdim
    n_half = r // 2
    tr = _row_tile(n_half, cdim * 2, STAGE_BYTES, 16)
    nq = n_half // tr
    slots = 3

    over_ici = landed is None

    def body(*refs):
        ws_ref = refs[0]
        full_ref, buf, load_sems, write_sems, send_sems, pass_sems, recv_sems, sib_sem = refs[1 if over_ici else 2:]
        x, y, c, others = _place()
        me = 2 * x + y
        sibling = (x, y, 1 - c)

        def region(chip, half, row0, rows):
            if ax == 0:
                return full_ref.at[pl.ds(chip * n_shard + half * n_half + row0, rows), :]
            return full_ref.at[pl.ds(half * n_half + row0, rows), pl.ds(chip * n_shard, n_shard)]

        pending = {}

        def free(slot):
            for cp, remote in pending.pop(slot, []):
                if remote:
                    cp.wait_send()
                else:
                    cp.wait()

        def load(src, slot):
            free(slot)
            cp = pltpu.make_async_copy(src, buf.at[slot], load_sems.at[slot])
            cp.start()
            cp.wait()

        step = 0
        for mine in (True, False):
            half = c if mine else 1 - c
            for q in range(nq):
                slot = step % slots
                load(ws_ref.at[pl.ds(half * n_half + q * tr, tr), :], slot)
                dst = region(me, half, q * tr, tr)
                cp = pltpu.make_async_copy(buf.at[slot], dst, write_sems.at[slot])
                cp.start()
                pending[slot] = [(cp, False)]
                if mine and over_ici:
                    for k, chip in enumerate(others):
                        cp = pltpu.make_async_remote_copy(
                            src_ref=buf.at[slot], dst_ref=dst, send_sem=send_sems.at[3 * slot + k], recv_sem=recv_sems.at[k],
                            device_id=(*chip, c), device_id_type=MESH)
                        cp.start()
                        pending[slot].append((cp, True))
                step += 1
        for k, chip in enumerate(others):
            j = 2 * chip[0] + chip[1]
            if over_ici:
                whole = region(j, c, 0, n_half)
                pltpu.make_async_remote_copy(src_ref=whole, dst_ref=whole, send_sem=send_sems.at[0], recv_sem=recv_sems.at[k],
                                             device_id=(x, y, c), device_id_type=MESH).wait_recv()
            for q in range(nq):
                slot = step % slots
                part = region(j, c, q * tr, tr)
                load(part, slot)
                cp = pltpu.make_async_remote_copy(src_ref=buf.at[slot], dst_ref=part, send_sem=pass_sems.at[slot],
                                                  recv_sem=sib_sem, device_id=sibling, device_id_type=MESH)
                cp.start()
                pending[slot] = [(cp, True)]
                step += 1
        for slot in list(pending):
            free(slot)
        if ax == 0:
            three = full_ref.at[pl.ds(0, 3 * n_half), :]
        else:
            three = full_ref.at[pl.ds(0, n_half), pl.ds(0, 3 * n_shard)]
        pltpu.make_async_remote_copy(src_ref=three, dst_ref=three, send_sem=send_sems.at[0], recv_sem=sib_sem,
                                     device_id=(x, y, c), device_id_type=MESH).wait_recv()

    return pl.pallas_call(
        body, name=name, in_specs=[ANY] if over_ici else [ANY, ANY], out_specs=ANY,
        out_shape=jax.ShapeDtypeStruct(full_shape, ws.dtype), input_output_aliases={} if over_ici else {1: 0},
        scratch_shapes=[pltpu.VMEM((slots, tr, cdim), ws.dtype), pltpu.SemaphoreType.DMA((slots,)),
                        pltpu.SemaphoreType.DMA((slots,)), pltpu.SemaphoreType.DMA((3 * slots,)),
                        pltpu.SemaphoreType.DMA((slots,)), pltpu.SemaphoreType.DMA((3,)), pltpu.SemaphoreType.DMA],
        compiler_params=pltpu.CompilerParams(vmem_limit_bytes=VMEM_LIMIT),
    )(*((ws,) if over_ici else (ws, landed)))


def _gather_job(ws, ax):
    r, cdim = ws.shape
    full_shape = (4 * r, cdim) if ax == 0 else (r, 4 * cdim)
    n_shard = r if ax == 0 else cdim
    n_half = r // 2

    def copies(ins, outs, sems):
        (ws_ref,), (full_ref,) = ins, outs
        x, y, c, others = _place()
        me = 2 * x + y
        if ax == 0:
            dst = full_ref.at[pl.ds(me * n_shard + c * n_half, n_half), :]
        else:
            dst = full_ref.at[pl.ds(c * n_half, n_half), pl.ds(me * n_shard, n_shard)]
        return [(pltpu.make_async_remote_copy(
            src_ref=ws_ref.at[pl.ds(c * n_half, n_half), :], dst_ref=dst, send_sem=sems.at[k], recv_sem=sems.at[3 + k],
            device_id=(*chip, c), device_id_type=MESH), "both") for k, chip in enumerate(others)]

    return _Job([ws], [jax.ShapeDtypeStruct(full_shape, ws.dtype)], 6, copies)


def _scatter_job(pair):
    _, n_half, ncol = pair.shape

    def copies(ins, outs, sems):
        (p_ref,), (got_ref,) = ins, outs
        x, y, c, others = _place()
        return [(pltpu.make_async_remote_copy(
            src_ref=p_ref.at[k], dst_ref=got_ref.at[k], send_sem=sems.at[k], recv_sem=sems.at[3 + k],
            device_id=(*chip, c), device_id_type=MESH), "both") for k, chip in enumerate(others)]

    return _Job([pair], [jax.ShapeDtypeStruct((3, n_half, ncol), pair.dtype)], 6, copies)


def _shard_dims(shape, ax):
    rf, cf = shape
    n_shard = (rf if ax == 0 else cf) // 4
    return n_shard, (n_shard if ax == 0 else rf) // 2, (cf if ax == 0 else n_shard)


def _mm_tn_pair(a, b, ax, name):
    (k, m), (k2, n) = a.shape, b.shape
    assert k == k2
    n_shard, n_half, _ = _shard_dims((m, n), ax)
    tm = _tile(n_half, 1024)
    if tm < 512:
        tm = n_half
    tm, tn, tk = _mm_tiles(m, n, k, 4, tm=tm)
    ni, nj, nk = m // tm, n // tn, k // tk
    ntiles = ni * nj
    n_acc = 1 if nk > 1 else 0
    assert n_half % tm == 0 and ntiles >= 2

    def body(a_ref, b_ref, own_ref, recv_ref, *scratch):
        acc_ref = scratch[0] if n_acc else None
        stage, local_sems, send_sems, recv_sem = scratch[n_acc:]
        i, j, kk = pl.program_id(0), pl.program_id(1), pl.program_id(2)
        x, y, c, _ = _place()
        sibling = (x, y, 1 - c)
        t = i * nj + j

        def is_mine(ii):
            row0 = ii * tm
            half = row0 // n_half if ax == 1 else (row0 % n_shard) // n_half
            return half == c

        def copies(tt):
            ii, jj, slot = tt // nj, tt % nj, tt % 2
            where = (pl.ds(ii * tm, tm), pl.ds(jj * tn, tn))
            local = pltpu.make_async_copy(stage.at[slot], own_ref.at[where], local_sems.at[slot])
            remote = pltpu.make_async_remote_copy(src_ref=stage.at[slot], dst_ref=recv_ref.at[where], send_sem=send_sems.at[slot],
                                                  recv_sem=recv_sem, device_id=sibling, device_id_type=MESH)
            return is_mine(ii), local, remote

        def wait_tile(tt):
            mine, local, remote = copies(tt)

            @pl.when(mine)
            def _():
                local.wait()

            @pl.when(jnp.logical_not(mine))
            def _():
                remote.wait_send()

        part = lax.dot_general(a_ref[...], b_ref[...], _TN, preferred_element_type=F32)
        if n_acc:
            @pl.when(kk == 0)
            def _():
                acc_ref[...] = part

            @pl.when(kk > 0)
            def _():
                acc_ref[...] += part

        @pl.when(kk == nk - 1)
        def _():
            @pl.when(t >= 2)
            def _():
                wait_tile(t - 2)

            stage[t % 2] = (acc_ref[...] if n_acc else part).astype(stage.dtype)
            mine, local, remote = copies(t)

            @pl.when(mine)
            def _():
                local.start()

            @pl.when(jnp.logical_not(mine))
            def _():
                remote.start()

            @pl.when(t == ntiles - 1)
            def _():
                wait_tile(t - 1)
                wait_tile(t)
                half_rows = recv_ref.at[pl.ds(0, m // 2), :]
                pltpu.make_async_remote_copy(src_ref=half_rows, dst_ref=half_rows, send_sem=send_sems.at[0], recv_sem=recv_sem,
                                             device_id=sibling, device_id_type=MESH).wait_recv()

    out = jax.ShapeDtypeStruct((m, n), BF16)
    return pl.pallas_call(
        body, name=name, grid=(ni, nj, nk),
        in_specs=[pl.BlockSpec((tk, tm), lambda i, j, kk: (kk, i)), pl.BlockSpec((tk, tn), lambda i, j, kk: (kk, j))],
        out_specs=[ANY, ANY], out_shape=[out, out],
        scratch_shapes=[pltpu.VMEM((tm, tn), F32)] * n_acc + [pltpu.VMEM((2, tm, tn), BF16), pltpu.SemaphoreType.DMA((2,)),
                                                               pltpu.SemaphoreType.DMA((2,)), pltpu.SemaphoreType.DMA],
        compiler_params=_params(("arbitrary", "arbitrary", "arbitrary")),
    )(a, b)


def _other_chip(k, x, y):
    return jnp.where(k == 1, x, 1 - x), jnp.where(k == 0, y, 1 - y)


def _pairsum(own, recv, ax, coords, name):
    n_shard, n_half, ncol = _shard_dims(own.shape, ax)
    tr = _row_tile(n_half, ncol * 2, STAGE_BYTES // 2, 16)
    nr = n_half // tr

    def src_map(k):
        def index(r, co):
            jx, jy = _other_chip(k, co[0], co[1])
            j = 2 * jx + jy
            if ax == 0:
                return ((j * n_shard + co[2] * n_half) // tr + r, 0)
            return (co[2] * nr + r, j)
        return index

    def body(co_ref, o0, r0, o1, r1, o2, r2, out_ref):
        for k, (own_ref, recv_ref) in enumerate(((o0, r0), (o1, r1), (o2, r2))):
            out_ref[k] = (own_ref[...].astype(F32) + recv_ref[...].astype(F32)).astype(out_ref.dtype)

    blks = [pl.BlockSpec((tr, ncol), src_map(k)) for k in range(3) for _ in range(2)]
    return pl.pallas_call(
        body, name=name,
        grid_spec=pltpu.PrefetchScalarGridSpec(
            num_scalar_prefetch=1, grid=(nr,), in_specs=blks, out_specs=pl.BlockSpec((3, tr, ncol), lambda r, co: (0, r, 0))),
        out_shape=jax.ShapeDtypeStruct((3, n_half, ncol), BF16),
        compiler_params=_params(("parallel",)),
    )(coords, own, recv, own, recv, own, recv)


def _sum_share(own, recv, got, ax, coords, name):
    n_shard, n_half, ncol = _shard_dims(own.shape, ax)
    tr = _row_tile(n_half, ncol * 4, STAGE_BYTES, 16)
    nr = n_half // tr

    def src_map(r, co):
        me = 2 * co[0] + co[1]
        if ax == 0:
            return ((me * n_shard + co[2] * n_half) // tr + r, 0)
        return (co[2] * nr + r, me)

    def body(co_ref, own_ref, recv_ref, got_ref, out_ref, sbuf, local_sems, send_sems, recv_sem):
        r = pl.program_id(0)
        x, y, c, _ = _place()

        def copies(slot):
            rows = out_ref.at[pl.ds(c * n_half + r * tr, tr), :]
            local = pltpu.make_async_copy(sbuf.at[slot], rows, local_sems.at[slot])
            remote = pltpu.make_async_remote_copy(src_ref=sbuf.at[slot], dst_ref=rows, send_sem=send_sems.at[slot],
                                                  recv_sem=recv_sem, device_id=(x, y, 1 - c), device_id_type=MESH)
            return local, remote

        def wait_slot(slot):
            local, remote = copies(slot)
            local.wait()
            remote.wait_send()

        @pl.when(r >= 2)
        def _():
            wait_slot(r % 2)

        total = own_ref[...].astype(F32) + recv_ref[...].astype(F32)
        for kk in range(3):
            total = total + got_ref[kk].astype(F32)
        sbuf[r % 2] = total
        local, remote = copies(r % 2)
        local.start()
        remote.start()

        @pl.when(r == nr - 1)
        def _():
            wait_slot(r % 2)
            if nr >= 2:
                wait_slot(1 - r % 2)
            half = out_ref.at[pl.ds(0, n_half), :]
            pltpu.make_async_remote_copy(src_ref=half, dst_ref=half, send_sem=send_sems.at[0], recv_sem=recv_sem,
                                         device_id=(x, y, c), device_id_type=MESH).wait_recv()

    blk = pl.BlockSpec((tr, ncol), src_map)
    return pl.pallas_call(
        body, name=name,
        grid_spec=pltpu.PrefetchScalarGridSpec(
            num_scalar_prefetch=1, grid=(nr,), in_specs=[blk, blk, pl.BlockSpec((3, tr, ncol), lambda r, co: (0, r, 0))],
            out_specs=ANY,
            scratch_shapes=[pltpu.VMEM((2, tr, ncol), F32), pltpu.SemaphoreType.DMA((2,)), pltpu.SemaphoreType.DMA((2,)),
                            pltpu.SemaphoreType.DMA]),
        out_shape=jax.ShapeDtypeStruct((2 * n_half, ncol), F32),
        compiler_params=_params(("arbitrary",)),
    )(coords, own, recv, got)


def _vec_all_reduce(v, name):
    rows, d = v.shape

    def body(v_ref, out_ref, gath_ref, send_sems, recv_sems):
        x, y, c, _ = _place()
        me = 4 * x + 2 * y + c
        gath_ref[me] = v_ref[...]
        flips = [(fx, fy, fc) for fx in (0, 1) for fy in (0, 1) for fc in (0, 1)][1:]
        sends = []
        for k, (fx, fy, fc) in enumerate(flips):
            peer = (1 - x if fx else x, 1 - y if fy else y, 1 - c if fc else c)
            sends.append(pltpu.make_async_remote_copy(
                src_ref=v_ref, dst_ref=gath_ref.at[me], send_sem=send_sems.at[k], recv_sem=recv_sems.at[k],
                device_id=peer, device_id_type=MESH))
        for cp in sends:
            cp.start()
        for cp in sends:
            cp.wait()
        acc = gath_ref[0]
        for dev in range(1, 8):
            acc = acc + gath_ref[dev]
        out_ref[...] = acc

    vm = pl.BlockSpec(memory_space=pltpu.VMEM)
    return pl.pallas_call(
        body, name=name, in_specs=[vm], out_specs=vm, out_shape=jax.ShapeDtypeStruct((rows, d), F32),
        scratch_shapes=[pltpu.VMEM((8, rows, d), F32), pltpu.SemaphoreType.DMA((7,)), pltpu.SemaphoreType.DMA((7,))],
    )(v)


_SHARD_AXIS = {"w_in": 1, "w_attn_out": 1, "w_ret_out": 0, "w_o": 0, "w_ffn_gate": 1, "w_ffn_up": 1, "w_ffn_down": 0,
               "w_ple_gate": 0, "w_ple_up": 1}
_FFN_HIDDEN_AXIS = {"w_ffn_gate": 1, "w_ffn_up": 1, "w_ffn_down": 0}
_VECTORS = ("ret_gn_g", "ln1_g", "ln1_b", "ln2_g", "ln2_b")
_WEIGHTS = ("w_in", "w_attn_out", "w_ret_out", "ret_gn_g", "w_o", "ln1_g", "ln1_b", "w_ffn_gate", "w_ffn_up", "w_ffn_down",
            "w_ple_gate", "w_ple_up", "ln2_g", "ln2_b")


def _local_grads(cfg, x, p, tgt, w, vec, coords=None):
    dist = coords is not None
    xb, pb = x.astype(BF16), p.astype(BF16)
    cos, sin = _rope_tables(cfg)
    lg = _log_gamma(cfg)
    wf = {"w_in": _all_gather(w["w_in"], _SHARD_AXIS["w_in"], "ag_w_in")} if dist else dict(w)
    grads, pairs, got, waiting = {}, {}, {}, []

    def gather_behind(names, matmul):
        if not dist:
            return matmul(())
        out, *landed = matmul([_gather_job(w[n], _SHARD_AXIS[n]) for n in names])
        for n, full in zip(names, landed):
            wf[n] = _all_gather(w[n], _SHARD_AXIS[n], "ag_" + n, landed=full)
        return out

    def grad(a, b, n, name):
        if not dist:
            grads[n] = _mm(a, b, "tn", BF16, name)
            return
        own, recv = _mm_tn_pair(a, b, _SHARD_AXIS[n], name)
        pairs[n] = (own, recv, _pairsum(own, recv, _SHARD_AXIS[n], coords, "rs_" + n + "_pairsum"))
        waiting.append(n)

    def scatter_behind(matmul):
        if not waiting:
            return matmul(())
        names = list(waiting)
        del waiting[:]
        out, *landed = matmul([_scatter_job(pairs[n][2]) for n in names])
        got.update(zip(names, landed))
        return out

    proj = gather_behind(["w_attn_out", "w_ret_out", "w_o", "w_ffn_gate", "w_ffn_up"],
                         lambda jobs: _mm(xb, wf["w_in"], "nn", BF16, "proj", jobs=jobs))
    q3 = _to_groups(cfg, proj[:, cfg.o_qa:cfg.o_qa + cfg.att_w])
    k3 = _to_groups(cfg, proj[:, cfg.o_ka:cfg.o_ka + cfg.att_w])
    v3 = _to_groups(cfg, proj[:, cfg.o_va:cfg.o_va + cfg.att_w])
    o3, lse3 = _attn_fwd(cfg, q3, k3, v3)
    o_att, lse = _attn_combine(cfg, jnp.stack(_from_groups(o3)), jnp.stack(_from_groups(lse3)))
    y_att = _mm(o_att, wf["w_attn_out"], "nn", BF16, "y_att")
    rn, rstd_r, states = _ret_fwd(cfg, proj, cos, sin, lg)
    rg = _ret_gate(cfg, proj, rn, vec["ret_gn_g"])
    y_ret = _mm(rg, wf["w_ret_out"], "nn", BF16, "y_ret")
    mixed = _mix(cfg, proj, y_att, y_ret)
    mo = _mm(mixed, wf["w_o"], "nn", F32, "mixed_out")
    xhat1, h1, rstd1 = _ln1(cfg, x, mo, vec["ln1_g"], vec["ln1_b"])
    u = gather_behind(["w_ffn_down", "w_ple_gate", "w_ple_up"],
                      lambda jobs: _mm(h1, wf["w_ffn_gate"], "nn", BF16, "ffn_u", jobs=jobs))
    t = _mm(h1, wf["w_ffn_up"], "nn", BF16, "ffn_t")
    a = _ffn_act(cfg, u, t)
    ffn = _mm(a, wf["w_ffn_down"], "nn", F32, "ffn_down")
    hp = _mm(h1, wf["w_ple_gate"], "nn", BF16, "ple_gate")
    pu = _mm(pb, wf["w_ple_up"], "nn", BF16, "ple_up")
    loss, dz2, dz2b, d_pg, d_pu, g_ln2g, g_ln2b = _head(cfg, xhat1, ffn, hp, pu, tgt, vec["ln1_g"], vec["ln1_b"],
                                                        vec["ln2_g"], vec["ln2_b"])

    d_a = _mm(dz2b, wf["w_ffn_down"], "nt", BF16, "d_a")
    d_u, d_t = _ffn_act_bwd(cfg, u, t, d_a)
    grad(a, dz2b, "w_ffn_down", "g_ffn_down")
    dh = scatter_behind(lambda jobs: _mm(d_u, wf["w_ffn_gate"], "nt", F32, "dh_u", jobs=jobs))
    grad(h1, d_u, "w_ffn_gate", "g_ffn_gate")
    dh = scatter_behind(lambda jobs: _mm(d_t, wf["w_ffn_up"], "nt", F32, "dh_t", add=dh, jobs=jobs))
    grad(h1, d_t, "w_ffn_up", "g_ffn_up")
    dh = scatter_behind(lambda jobs: _mm(d_pg, wf["w_ple_gate"], "nt", F32, "dh_pg", add=dh, jobs=jobs))
    grad(h1, d_pg, "w_ple_gate", "g_ple_gate")
    grad(pb, d_pu, "w_ple_up", "g_ple_up")
    dz1, dz1b, g_ln1g, g_ln1b = _ln1_bwd(cfg, dh, dz2, xhat1, rstd1, vec["ln1_g"])
    d_mixed = scatter_behind(lambda jobs: _mm(dz1b, wf["w_o"], "nt", BF16, "d_mixed", jobs=jobs))
    grad(mixed, dz1b, "w_o", "g_o")
    d_ya, d_yr, d_ga, d_g2 = _mix_bwd(cfg, proj, y_att, y_ret, d_mixed)
    d_oatt = _mm(d_ya, wf["w_attn_out"], "nt", BF16, "d_oatt")
    d_rg = scatter_behind(lambda jobs: _mm(d_yr, wf["w_ret_out"], "nt", BF16, "d_rg", jobs=jobs))
    grad(o_att, d_ya, "w_attn_out", "g_attn_out")
    grad(rg, d_yr, "w_ret_out", "g_ret_out")
    d_gr, d_r, g_gn = _ret_gate_bwd(cfg, proj, rn, vec["ret_gn_g"], rstd_r, d_rg)
    d_qr, d_kr, d_vr = _ret_bwd(cfg, proj, d_r, states, cos, sin, lg)
    delta = _attn_delta(cfg, d_oatt, o_att)
    do3 = jnp.stack([_deinterleave(d_oatt, dil) for _, dil in ATT_GROUPS])
    l3 = jnp.stack([_deinterleave(lse, dil) for _, dil in ATT_GROUPS])
    dl3 = jnp.stack([_deinterleave(delta, dil) for _, dil in ATT_GROUPS])
    dq3, dk3, dv3 = _attn_bwd(cfg, q3, k3, v3, do3, l3, dl3)
    dproj = jnp.concatenate(_from_groups(dq3) + _from_groups(dk3) + _from_groups(dv3) + [d_qr, d_kr, d_vr, d_gr, d_ga, d_g2], axis=1)
    grad(xb, dproj, "w_in", "g_in")
    grad_x = scatter_behind(lambda jobs: _mm(dproj, wf["w_in"], "nt", F32, "grad_x", add=dz1, add_scale=cfg.alpha, jobs=jobs))
    vgrads = {"ret_gn_g": g_gn, "ln1_g": g_ln1g, "ln1_b": g_ln1b, "ln2_g": g_ln2g, "ln2_b": g_ln2b}
    if dist:
        grads = {n: _sum_share(pairs[n][0], pairs[n][1], got[n], _SHARD_AXIS[n], coords, "rs_" + n + "_share") for n in pairs}
    return loss, grad_x, grads, vgrads


def _step(cfg, x, p, tgt, w, m, v):
    mats = [n for n in _WEIGHTS if n in _SHARD_AXIS]
    ffn_pad = cfg.ffn_shard - cfg.ffn // 4

    def shard_bf16(n):
        ws = w[n][0].astype(BF16)
        if n in _FFN_HIDDEN_AXIS and ffn_pad:
            ws = jnp.pad(ws, [(0, ffn_pad) if a == _FFN_HIDDEN_AXIS[n] else (0, 0) for a in (0, 1)])
        return ws

    def unpad(n, t):
        if n in _FFN_HIDDEN_AXIS and ffn_pad:
            return t[:cfg.ffn // 4] if _FFN_HIDDEN_AXIS[n] == 0 else t[:, :cfg.ffn // 4]
        return t

    vec = {n: w[n] for n in _VECTORS}
    coords = jnp.stack([lax.axis_index("x"), lax.axis_index("y"), lax.axis_index("c")]).astype(jnp.int32)
    loss, grad_x, grads, vgrads = _local_grads(cfg, x[0], p[0, 0], tgt[0], {n: shard_bf16(n) for n in mats}, vec, coords)
    loss = lax.psum(loss[0, 0], ("x", "y", "c"))
    red = {n: unpad(n, grads[n]) for n in mats}
    stacked = jnp.concatenate([vgrads[n] for n in _VECTORS] + [jnp.zeros((3, cfg.d), F32)], axis=0)
    vsum = _vec_all_reduce(stacked, "vec_all_reduce")
    for i, n in enumerate(_VECTORS):
        red[n] = vsum[i:i + 1]
    g_out, d_out, m_out, v_out = [], [], [], []
    for n in _WEIGHTS:
        dlt, mn, vn = _adamw(w[n][0] if n in _SHARD_AXIS else w[n], red[n], m[n][0] if n in _SHARD_AXIS else m[n],
                             v[n][0] if n in _SHARD_AXIS else v[n], "adamw_" + n)
        lead = (lambda t: t[None]) if n in _SHARD_AXIS else (lambda t: t)
        g_out.append(lead(red[n]))
        d_out.append(lead(dlt))
        m_out.append(lead(mn))
        v_out.append(lead(vn))
    return (loss, grad_x[None], *g_out, *d_out, *m_out, *v_out)


def kernel(x, p, w_in, w_attn_out, w_ret_out, ret_gn_g, w_o, ln1_g, ln1_b, w_ffn_gate, w_ffn_up, w_ffn_down, w_ple_gate, w_ple_up, ln2_g, ln2_b, loss_target, m_w_in, m_w_attn_out, m_w_ret_out, m_ret_gn_g, m_w_o, m_ln1_g, m_ln1_b, m_w_ffn_gate, m_w_ffn_up, m_w_ffn_down, m_w_ple_gate, m_w_ple_up, m_ln2_g, m_ln2_b, v_w_in, v_w_attn_out, v_w_ret_out, v_ret_gn_g, v_w_o, v_ln1_g, v_ln1_b, v_w_ffn_gate, v_w_ffn_up, v_w_ffn_down, v_w_ple_gate, v_w_ple_up, v_ln2_g, v_ln2_b):
    w = dict(zip(_WEIGHTS, (w_in, w_attn_out, w_ret_out, ret_gn_g, w_o, ln1_g, ln1_b, w_ffn_gate, w_ffn_up, w_ffn_down,
                            w_ple_gate, w_ple_up, ln2_g, ln2_b)))
    m = dict(zip(_WEIGHTS, (m_w_in, m_w_attn_out, m_w_ret_out, m_ret_gn_g, m_w_o, m_ln1_g, m_ln1_b, m_w_ffn_gate, m_w_ffn_up,
                            m_w_ffn_down, m_w_ple_gate, m_w_ple_up, m_ln2_g, m_ln2_b)))
    v = dict(zip(_WEIGHTS, (v_w_in, v_w_attn_out, v_w_ret_out, v_ret_gn_g, v_w_o, v_ln1_g, v_ln1_b, v_w_ffn_gate, v_w_ffn_up,
                            v_w_ffn_down, v_w_ple_gate, v_w_ple_up, v_ln2_g, v_ln2_b)))
    return _step(_FULL, x, p, loss_target, w, m, v)
```

```python
import functools
import math

import jax
import jax.numpy as jnp
import numpy as np
from jax import lax
from jax.experimental import pallas as pl
from jax.experimental.pallas import tpu as pltpu

F32 = jnp.float32
BF16 = jnp.bfloat16
MESH = pl.DeviceIdType.MESH
ANY = pl.BlockSpec(memory_space=pl.ANY)

ATT_BLOCK = 128
ATT_GROUPS = ((128, 1), (512, 4), (2048, 16))
LN_EPS = 1e-5
GN_EPS = 1e-6
NEG_INF = -1e30
ROPE_BASE = 10000.0
ADAM_LR, ADAM_B1, ADAM_B2, ADAM_EPS, ADAM_WD, ADAM_STEP = 0.001, 0.9, 0.999, 1e-08, 0.01, 10
VMEM_LIMIT = 56 * 1024 * 1024
STAGE_BYTES = 4 * 1024 * 1024
MM_VMEM_BYTES = 46 * 1024 * 1024 + 512 * 1024


class _Cfg:
    def __init__(self, seq, d, ple, ahd, ahg, rh, rqk, rv, ffn, cw):
        self.seq, self.d, self.ple, self.ahd, self.ahg = seq, d, ple, ahd, ahg
        self.rh, self.rqk, self.rv, self.ffn, self.cw = rh, rqk, rv, ffn, cw
        self.aw = ahg * ahd
        self.att_w = 3 * self.aw
        self.rqk_w = rh * rqk
        self.rv_w = rh * rv
        offs = np.cumsum([0] + [self.att_w] * 3 + [self.rqk_w] * 2 + [self.rv_w] * 2 + [d] * 2)
        (self.o_qa, self.o_ka, self.o_va, self.o_qr, self.o_kr, self.o_vr, self.o_gr, self.o_ga, self.o_g2,
         self.in_w) = [int(v) for v in offs]
        self.alpha = 2.0 ** 0.25
        self.ffn_shard = -(-(ffn // 4) // 128) * 128
        self.ffn_p = 4 * self.ffn_shard
        assert self.rv_w == d and d % cw == 0
        for o in (self.o_gr, self.o_ga, self.o_g2):
            assert o % cw == 0
        assert self.o_qr % rqk == 0 and self.o_kr % rqk == 0 and self.o_vr % rv == 0 and self.o_gr % rv == 0
        assert rqk // 2 % 128 == 0 and seq % (ATT_BLOCK * 16) == 0


_FULL = _Cfg(seq=4096, d=4096, ple=256, ahd=128, ahg=8, rh=8, rqk=256, rv=512, ffn=11008, cw=1024)


def _tile(n, target, q=128):
    t = min(n, target) // q * q
    while t >= q:
        if n % t == 0:
            return t
        t -= q
    return n


def _row_tile(n, row_bytes, target_bytes, q=8):
    best = None
    for t in range(q, n + 1, q):
        if n % t == 0 and (best is None or t * row_bytes <= target_bytes):
            best = t
            if t * row_bytes > target_bytes:
                break
    return n if best is None else best


def _mm_tiles(m, n, k, out_bytes_per_elem, tm=None):
    tm, tn = _tile(m, 1024) if tm is None else tm, _tile(n, 1024)
    tk = k if k <= 4096 else _tile(k, 2048)

    def need(tk_):
        acc = 4 if tk_ < k else 0
        return 4 * (tm + tn) * tk_ + tm * tn * (out_bytes_per_elem + acc + 4)

    while need(tk) > MM_VMEM_BYTES and tk > 256:
        tk = _tile(k, tk - 128)
    return tm, tn, tk


def _params(sem=None):
    return pltpu.CompilerParams(dimension_semantics=sem, vmem_limit_bytes=VMEM_LIMIT)


def _sigmoid(x):
    return 1.0 / (1.0 + jnp.exp(-x))


class _Job:
    def __init__(self, inputs, out_shapes, n_sems, copies):
        self.inputs, self.out_shapes, self.n_sems, self.copies = list(inputs), list(out_shapes), n_sems, copies


def _mm(a, b, mode, out_dtype, name, add=None, add_scale=1.0, jobs=()):
    if mode == "nn":
        (m, k), (k2, n) = a.shape, b.shape
    elif mode == "nt":
        (m, k), (n, k2) = a.shape, b.shape
    else:
        (k, m), (k2, n) = a.shape, b.shape
    assert k == k2, (name, a.shape, b.shape)
    has_add = add is not None
    tm, tn, tk = _mm_tiles(m, n, k, 2 * jnp.dtype(out_dtype).itemsize + (8 if has_add else 0))
    nk = k // tk
    if mode == "tn":
        a_spec = pl.BlockSpec((tk, tm), lambda i, j, kk: (kk, i))
        dims = (((0,), (0,)), ((), ()))
    else:
        a_spec = pl.BlockSpec((tm, tk), lambda i, j, kk: (i, kk))
        dims = (((1,), (1,)), ((), ())) if mode == "nt" else (((1,), (0,)), ((), ()))
    if mode == "nt":
        b_spec = pl.BlockSpec((tn, tk), lambda i, j, kk: (j, kk))
    else:
        b_spec = pl.BlockSpec((tk, tn), lambda i, j, kk: (kk, j))
    o_spec = pl.BlockSpec((tm, tn), lambda i, j, kk: (i, j))
    ni, nj = m // tm, n // tn
    jobs = list(jobs)
    n_main_in = 3 if has_add else 2
    n_job_in = sum(len(jb.inputs) for jb in jobs)
    n_job_out = sum(len(jb.out_shapes) for jb in jobs)
    n_acc = 1 if nk > 1 else 0

    def body(*refs):
        a_ref, b_ref = refs[0], refs[1]
        add_ref = refs[2] if has_add else None
        job_in = refs[n_main_in:n_main_in + n_job_in]
        o_ref = refs[n_main_in + n_job_in]
        job_out = refs[n_main_in + n_job_in + 1:n_main_in + n_job_in + 1 + n_job_out]
        acc_ref = refs[n_main_in + n_job_in + 1 + n_job_out] if n_acc else None
        job_sems = refs[n_main_in + n_job_in + 1 + n_acc + n_job_out:]
        i, j, kk = pl.program_id(0), pl.program_id(1), pl.program_id(2)

        def finish(r):
            if has_add:
                r = r + add_scale * add_ref[...]
            o_ref[...] = r.astype(o_ref.dtype)

        def job_copies():
            found, pi, po = [], 0, 0
            for jb, sems in zip(jobs, job_sems):
                found += jb.copies(job_in[pi:pi + len(jb.inputs)], job_out[po:po + len(jb.out_shapes)], sems)
                pi, po = pi + len(jb.inputs), po + len(jb.out_shapes)
            return found

        if jobs:
            @pl.when(jnp.logical_and(jnp.logical_and(i == 0, j == 0), kk == 0))
            def _():
                for cp, _ in job_copies():
                    cp.start()

        part = lax.dot_general(a_ref[...], b_ref[...], dims, preferred_element_type=F32)
        if n_acc:
            @pl.when(kk == 0)
            def _():
                acc_ref[...] = part

            @pl.when(kk > 0)
            def _():
                acc_ref[...] += part

            @pl.when(kk == nk - 1)
            def _():
                finish(acc_ref[...])
        else:
            finish(part)

        if jobs:
            @pl.when(jnp.logical_and(jnp.logical_and(i == ni - 1, j == nj - 1), kk == nk - 1))
            def _():
                for cp, _ in job_copies():
                    cp.wait()

    job_inputs = [t for jb in jobs for t in jb.inputs]
    outs = pl.pallas_call(
        body, name=name, grid=(ni, nj, nk),
        in_specs=[a_spec, b_spec] + ([o_spec] if has_add else []) + [ANY] * n_job_in, out_specs=[o_spec] + [ANY] * n_job_out,
        out_shape=[jax.ShapeDtypeStruct((m, n), out_dtype)] + [s for jb in jobs for s in jb.out_shapes],
        scratch_shapes=[pltpu.VMEM((tm, tn), F32)] * n_acc + [pltpu.SemaphoreType.DMA((jb.n_sems,)) for jb in jobs],
        compiler_params=_params(("arbitrary",) * 3 if jobs else ("parallel", "parallel", "arbitrary")),
    )(*((a, b, add) if has_add else (a, b)), *job_inputs)
    return outs if jobs else outs[0]


def _seg_blocks(cfg):
    return [cfg.seq // ATT_BLOCK // dil for _, dil in ATT_GROUPS]


def _group_select(g, vals):
    out = jnp.int32(vals[-1])
    for i in range(len(vals) - 2, -1, -1):
        out = jnp.where(g == i, jnp.int32(vals[i]), out)
    return out


def _col(tile, h):
    lane = lax.broadcasted_iota(jnp.int32, tile.shape, 1)
    return jnp.sum(jnp.where(lane == h, tile, 0.0), axis=1, keepdims=True)


def _set_col(tile, h, col):
    lane = lax.broadcasted_iota(jnp.int32, tile.shape, 1)
    return jnp.where(lane == h, col, tile)


_NT = (((1,), (1,)), ((), ()))
_TN = (((0,), (0,)), ((), ()))


def _dot(a, b):
    return jnp.dot(a, b, preferred_element_type=F32)


def _dot_nt(a, b):
    return lax.dot_general(a, b, _NT, preferred_element_type=F32)


def _dot_tn(a, b):
    return lax.dot_general(a, b, _TN, preferred_element_type=F32)


def _attn_fwd(cfg, q3, k3, v3):
    nb = cfg.seq // ATT_BLOCK
    segs = _seg_blocks(cfg)
    scale = cfg.ahd ** -0.5
    blk = ATT_BLOCK

    def body(q_ref, kp_ref, kc_ref, vp_ref, vc_ref, o_ref, lse_ref):
        g, b = pl.program_id(0), pl.program_id(1)
        has_prev = (b & (_group_select(g, segs) - 1)) != 0
        qi = lax.broadcasted_iota(jnp.int32, (blk, blk), 0)
        kj = lax.broadcasted_iota(jnp.int32, (blk, blk), 1)
        valid_c = kj <= qi
        valid_p = jnp.logical_and(kj >= qi, has_prev)
        lse = jnp.zeros((blk, cfg.ahg), F32)
        for h in range(cfg.ahg):
            hs = slice(h * cfg.ahd, (h + 1) * cfg.ahd)
            q = q_ref[:, hs]
            s_c = jnp.where(valid_c, _dot_nt(q, kc_ref[:, hs]) * scale, NEG_INF)
            s_p = jnp.where(valid_p, _dot_nt(q, kp_ref[:, hs]) * scale, NEG_INF)
            m = jnp.maximum(jnp.max(s_c, axis=1, keepdims=True), jnp.max(s_p, axis=1, keepdims=True))
            p_c, p_p = jnp.exp(s_c - m), jnp.exp(s_p - m)
            den = jnp.sum(p_c, axis=1, keepdims=True) + jnp.sum(p_p, axis=1, keepdims=True)
            o = _dot(p_c.astype(BF16), vc_ref[:, hs]) + _dot(p_p.astype(BF16), vp_ref[:, hs])
            o_ref[:, hs] = (o / den).astype(o_ref.dtype)
            lse = _set_col(lse, h, m + jnp.log(den))
        lse_ref[...] = lse

    cur = pl.BlockSpec((None, blk, cfg.aw), lambda g, b: (g, b, 0))
    prev = pl.BlockSpec((None, blk, cfg.aw), lambda g, b: (g, jnp.maximum(b - 1, 0), 0))
    return pl.pallas_call(
        body, name="attn_fwd", grid=(3, nb), in_specs=[cur, prev, cur, prev, cur],
        out_specs=[cur, pl.BlockSpec((None, blk, cfg.ahg), lambda g, b: (g, b, 0))],
        out_shape=[jax.ShapeDtypeStruct((3, cfg.seq, cfg.aw), BF16), jax.ShapeDtypeStruct((3, cfg.seq, cfg.ahg), F32)],
        compiler_params=_params(("parallel", "parallel")),
    )(q3, k3, k3, v3, v3)


def _attn_combine(cfg, o3, lse3):
    tm = 256

    def body(o_ref, lse_ref, oa_ref, l_ref):
        l0, l1, l2 = lse_ref[0], lse_ref[1], lse_ref[2]
        m = jnp.maximum(jnp.maximum(l0, l1), l2)
        big = m + jnp.log(jnp.exp(l0 - m) + jnp.exp(l1 - m) + jnp.exp(l2 - m))
        ws = [jnp.exp(l0 - big), jnp.exp(l1 - big), jnp.exp(l2 - big)]
        for h in range(cfg.ahg):
            hs = slice(h * cfg.ahd, (h + 1) * cfg.ahd)
            acc = _col(ws[0], h) * o_ref[0, :, hs].astype(F32)
            acc += _col(ws[1], h) * o_ref[1, :, hs].astype(F32)
            acc += _col(ws[2], h) * o_ref[2, :, hs].astype(F32)
            oa_ref[:, hs] = acc.astype(oa_ref.dtype)
        l_ref[...] = big

    return pl.pallas_call(
        body, name="attn_combine", grid=(cfg.seq // tm,),
        in_specs=[pl.BlockSpec((3, tm, cfg.aw), lambda i: (0, i, 0)), pl.BlockSpec((3, tm, cfg.ahg), lambda i: (0, i, 0))],
        out_specs=[pl.BlockSpec((tm, cfg.aw), lambda i: (i, 0)), pl.BlockSpec((tm, cfg.ahg), lambda i: (i, 0))],
        out_shape=[jax.ShapeDtypeStruct((cfg.seq, cfg.aw), BF16), jax.ShapeDtypeStruct((cfg.seq, cfg.ahg), F32)],
        compiler_params=_params(("parallel",)),
    )(o3, lse3)


def _attn_delta(cfg, do, o):
    tm = 256

    def body(do_ref, o_ref, d_ref):
        out = jnp.zeros((tm, cfg.ahg), F32)
        for h in range(cfg.ahg):
            hs = slice(h * cfg.ahd, (h + 1) * cfg.ahd)
            prod = do_ref[:, hs].astype(F32) * o_ref[:, hs].astype(F32)
            out = _set_col(out, h, jnp.sum(prod, axis=1, keepdims=True))
        d_ref[...] = out

    row = pl.BlockSpec((tm, cfg.aw), lambda i: (i, 0))
    return pl.pallas_call(
        body, name="attn_delta", grid=(cfg.seq // tm,), in_specs=[row, row],
        out_specs=pl.BlockSpec((tm, cfg.ahg), lambda i: (i, 0)),
        out_shape=jax.ShapeDtypeStruct((cfg.seq, cfg.ahg), F32), compiler_params=_params(("parallel",)),
    )(do, o)


def _attn_bwd(cfg, q3, k3, v3, do3, l3, d3):
    nb = cfg.seq // ATT_BLOCK
    segs = _seg_blocks(cfg)
    scale = cfg.ahd ** -0.5
    blk = ATT_BLOCK

    def body(q_ref, qn_ref, do_ref, don_ref, l_ref, ln_ref, d_ref, dn_ref, kp_ref, kc_ref, vp_ref, vc_ref,
             dq_ref, dk_ref, dv_ref):
        g, b = pl.program_id(0), pl.program_id(1)
        seg_mask = _group_select(g, segs) - 1
        has_prev = (b & seg_mask) != 0
        has_next = jnp.logical_and(b + 1 < nb, ((b + 1) & seg_mask) != 0)
        qi = lax.broadcasted_iota(jnp.int32, (blk, blk), 0)
        kj = lax.broadcasted_iota(jnp.int32, (blk, blk), 1)
        valid_c = kj <= qi
        valid_p = jnp.logical_and(kj >= qi, has_prev)
        valid_n = jnp.logical_and(kj >= qi, has_next)
        lse, lse_n, dlt, dlt_n = l_ref[...], ln_ref[...], d_ref[...], dn_ref[...]
        for h in range(cfg.ahg):
            hs = slice(h * cfg.ahd, (h + 1) * cfg.ahd)
            q, qn, do, don = q_ref[:, hs], qn_ref[:, hs], do_ref[:, hs], don_ref[:, hs]
            kc, kp, vc, vp = kc_ref[:, hs], kp_ref[:, hs], vc_ref[:, hs], vp_ref[:, hs]
            lh, lnh, dh, dnh = _col(lse, h), _col(lse_n, h), _col(dlt, h), _col(dlt_n, h)
            p_c = jnp.where(valid_c, jnp.exp(_dot_nt(q, kc) * scale - lh), 0.0)
            p_p = jnp.where(valid_p, jnp.exp(_dot_nt(q, kp) * scale - lh), 0.0)
            ds_c = (p_c * (_dot_nt(do, vc) - dh)).astype(BF16)
            ds_p = (p_p * (_dot_nt(do, vp) - dh)).astype(BF16)
            dq_ref[:, hs] = ((_dot(ds_c, kc) + _dot(ds_p, kp)) * scale).astype(dq_ref.dtype)
            p_n = jnp.where(valid_n, jnp.exp(_dot_nt(qn, kc) * scale - lnh), 0.0)
            ds_n = (p_n * (_dot_nt(don, vc) - dnh)).astype(BF16)
            dk = _dot_tn(ds_c, q) + _dot_tn(ds_n, qn)
            dv = _dot_tn(p_c.astype(BF16), do) + _dot_tn(p_n.astype(BF16), don)
            dk_ref[:, hs] = (dk * scale).astype(dk_ref.dtype)
            dv_ref[:, hs] = dv.astype(dv_ref.dtype)

    def spec(width, shift):
        if shift == 0:
            return pl.BlockSpec((None, blk, width), lambda g, b: (g, b, 0))
        if shift > 0:
            return pl.BlockSpec((None, blk, width), lambda g, b: (g, jnp.minimum(b + 1, nb - 1), 0))
        return pl.BlockSpec((None, blk, width), lambda g, b: (g, jnp.maximum(b - 1, 0), 0))

    w, hw = cfg.aw, cfg.ahg
    out = jax.ShapeDtypeStruct((3, cfg.seq, w), BF16)
    return pl.pallas_call(
        body, name="attn_bwd", grid=(3, nb),
        in_specs=[spec(w, 0), spec(w, 1), spec(w, 0), spec(w, 1), spec(hw, 0), spec(hw, 1), spec(hw, 0), spec(hw, 1),
                  spec(w, -1), spec(w, 0), spec(w, -1), spec(w, 0)],
        out_specs=[spec(w, 0)] * 3, out_shape=[out, out, out], compiler_params=_params(("parallel", "parallel")),
    )(q3, q3, do3, do3, l3, l3, d3, d3, k3, k3, v3, v3)


def _deinterleave(t, dil):
    s, w = t.shape
    return t if dil == 1 else t.reshape(s // dil, dil, w).transpose(1, 0, 2).reshape(s, w)


def _interleave(t, dil):
    s, w = t.shape
    return t if dil == 1 else t.reshape(dil, s // dil, w).transpose(1, 0, 2).reshape(s, w)


def _to_groups(cfg, t):
    return jnp.stack([_deinterleave(t[:, g * cfg.aw:(g + 1) * cfg.aw], dil) for g, (_, dil) in enumerate(ATT_GROUPS)])


def _from_groups(t3):
    return [_interleave(t3[g], dil) for g, (_, dil) in enumerate(ATT_GROUPS)]


def _rope_tables(cfg):
    half = cfg.rqk // 2
    pos = jnp.arange(cfg.seq, dtype=F32)
    inv_freq = ROPE_BASE ** (-jnp.arange(half, dtype=F32) / half)
    ang = pos[:, None] * inv_freq[None, :]
    return jnp.cos(ang), jnp.sin(ang)


def _log_gamma(cfg):
    lg = jnp.log(1.0 - 2.0 ** (-5.0 - jnp.arange(cfg.rh, dtype=F32)))
    return jnp.broadcast_to(lg[:, None, None], (cfg.rh, 1, max(cfg.rqk, cfg.rv)))


def _rot(t, c, s, half):
    t1, t2 = t[:, :half], t[:, half:]
    return jnp.concatenate([t1 * c - t2 * s, t1 * s + t2 * c], axis=1)


def _unrot(d, c, s, half):
    d1, d2 = d[:, :half], d[:, half:]
    return jnp.concatenate([d1 * c + d2 * s, d2 * c - d1 * s], axis=1)


def _decays(lg_ref, cfg):
    blk = ATT_BLOCK
    lg_v, lg_k = lg_ref[:, :cfg.rv], lg_ref[:, :cfg.rqk]
    qi = lax.broadcasted_iota(jnp.int32, (blk, blk), 0)
    kj = lax.broadcasted_iota(jnp.int32, (blk, blk), 1)
    diff = (qi - kj).astype(F32)
    intra = jnp.where(diff >= 0, jnp.exp(jnp.maximum(diff, 0.0) * lg_ref[:, :blk]), 0.0)
    idx_v = lax.broadcasted_iota(jnp.int32, (blk, cfg.rv), 0).astype(F32)
    idx_k = lax.broadcasted_iota(jnp.int32, (blk, cfg.rqk), 0).astype(F32)
    cross = jnp.exp((idx_v + 1.0) * lg_v)
    state = jnp.exp((blk - 1.0 - idx_k) * lg_k)
    chunk = jnp.exp(float(blk) * lg_v)
    return intra, cross, state, chunk


def _ret_fwd(cfg, proj, cos, sin, lg):
    blk, nc, half = ATT_BLOCK, cfg.seq // ATT_BLOCK, cfg.rqk // 2
    kscale = cfg.rqk ** -0.5

    def body(q_ref, k_ref, v_ref, cos_ref, sin_ref, lg_ref, rn_ref, rstd_ref, st_ref, state):
        n = pl.program_id(1)

        @pl.when(n == 0)
        def _():
            state[...] = jnp.zeros_like(state)

        intra, cross_d, state_d, chunk_d = _decays(lg_ref, cfg)
        c, s = cos_ref[...], sin_ref[...]
        qb = _rot(q_ref[...].astype(F32), c, s, half).astype(BF16)
        kf = _rot(k_ref[...].astype(F32), c, s, half) * kscale
        vb = v_ref[...]
        prev = state[...]
        st_ref[...] = prev.astype(BF16)
        att = _dot_nt(qb, kf.astype(BF16)) * intra
        out = _dot(att.astype(BF16), vb) + _dot(qb, prev.astype(BF16)) * cross_d
        state[...] = chunk_d * prev + _dot_tn((kf * state_d).astype(BF16), vb)
        mu = jnp.mean(out, axis=1, keepdims=True)
        cen = out - mu
        rstd = lax.rsqrt(jnp.mean(cen * cen, axis=1, keepdims=True) + GN_EPS)
        rn_ref[...] = (cen * rstd).astype(rn_ref.dtype)
        rstd_ref[...] = rstd

    oq, ok, ov = cfg.o_qr // cfg.rqk, cfg.o_kr // cfg.rqk, cfg.o_vr // cfg.rv
    tab = pl.BlockSpec((blk, half), lambda h, n: (n, 0))
    return pl.pallas_call(
        body, name="ret_fwd", grid=(cfg.rh, nc),
        in_specs=[pl.BlockSpec((blk, cfg.rqk), lambda h, n: (n, oq + h)), pl.BlockSpec((blk, cfg.rqk), lambda h, n: (n, ok + h)),
                  pl.BlockSpec((blk, cfg.rv), lambda h, n: (n, ov + h)), tab, tab,
                  pl.BlockSpec((None, 1, lg.shape[2]), lambda h, n: (h, 0, 0))],
        out_specs=[pl.BlockSpec((blk, cfg.rv), lambda h, n: (n, h)), pl.BlockSpec((None, blk, 1), lambda h, n: (h, n, 0)),
                   pl.BlockSpec((None, None, cfg.rqk, cfg.rv), lambda h, n: (h, n, 0, 0))],
        out_shape=[jax.ShapeDtypeStruct((cfg.seq, cfg.rv_w), BF16), jax.ShapeDtypeStruct((cfg.rh, cfg.seq, 1), F32),
                   jax.ShapeDtypeStruct((cfg.rh, nc, cfg.rqk, cfg.rv), BF16)],
        scratch_shapes=[pltpu.VMEM((cfg.rqk, cfg.rv), F32)], compiler_params=_params(("parallel", "arbitrary")),
    )(proj, proj, proj, cos, sin, lg)


def _ret_bwd(cfg, proj, d_r, states, cos, sin, lg):
    blk, nc, half = ATT_BLOCK, cfg.seq // ATT_BLOCK, cfg.rqk // 2
    kscale = cfg.rqk ** -0.5

    def body(q_ref, k_ref, v_ref, do_ref, st_ref, cos_ref, sin_ref, lg_ref, dq_ref, dk_ref, dv_ref, dstate):
        n = pl.program_id(1)

        @pl.when(n == 0)
        def _():
            dstate[...] = jnp.zeros_like(dstate)

        intra, cross_d, state_d, chunk_d = _decays(lg_ref, cfg)
        c, s = cos_ref[...], sin_ref[...]
        qb = _rot(q_ref[...].astype(F32), c, s, half).astype(BF16)
        kf = _rot(k_ref[...].astype(F32), c, s, half) * kscale
        kb, ksb = kf.astype(BF16), (kf * state_d).astype(BF16)
        vb, prev = v_ref[...], st_ref[...]
        do = do_ref[...].astype(F32)
        dob, docb = do.astype(BF16), (do * cross_d).astype(BF16)
        dsb = dstate[...].astype(BF16)
        att = (_dot_nt(qb, kb) * intra).astype(BF16)
        datt = (_dot_nt(dob, vb) * intra).astype(BF16)
        d_q = _dot(datt, kb) + _dot_nt(docb, prev)
        d_k = _dot_tn(datt, qb) + _dot_nt(vb, dsb) * state_d
        d_v = _dot_tn(att, dob) + _dot(ksb, dsb)
        dstate[...] = chunk_d * dstate[...] + _dot_tn(qb, docb)
        dq_ref[...] = _unrot(d_q, c, s, half).astype(dq_ref.dtype)
        dk_ref[...] = _unrot(d_k * kscale, c, s, half).astype(dk_ref.dtype)
        dv_ref[...] = d_v.astype(dv_ref.dtype)

    oq, ok, ov = cfg.o_qr // cfg.rqk, cfg.o_kr // cfg.rqk, cfg.o_vr // cfg.rv
    last = nc - 1
    tab = pl.BlockSpec((blk, half), lambda h, n: (last - n, 0))
    qk_out = pl.BlockSpec((blk, cfg.rqk), lambda h, n: (last - n, h))
    v_out = pl.BlockSpec((blk, cfg.rv), lambda h, n: (last - n, h))
    return pl.pallas_call(
        body, name="ret_bwd", grid=(cfg.rh, nc),
        in_specs=[pl.BlockSpec((blk, cfg.rqk), lambda h, n: (last - n, oq + h)),
                  pl.BlockSpec((blk, cfg.rqk), lambda h, n: (last - n, ok + h)),
                  pl.BlockSpec((blk, cfg.rv), lambda h, n: (last - n, ov + h)), v_out,
                  pl.BlockSpec((None, None, cfg.rqk, cfg.rv), lambda h, n: (h, last - n, 0, 0)), tab, tab,
                  pl.BlockSpec((None, 1, lg.shape[2]), lambda h, n: (h, 0, 0))],
        out_specs=[qk_out, qk_out, v_out],
        out_shape=[jax.ShapeDtypeStruct((cfg.seq, cfg.rqk_w), BF16), jax.ShapeDtypeStruct((cfg.seq, cfg.rqk_w), BF16),
                   jax.ShapeDtypeStruct((cfg.seq, cfg.rv_w), BF16)],
        scratch_shapes=[pltpu.VMEM((cfg.rqk, cfg.rv), F32)], compiler_params=_params(("parallel", "arbitrary")),
    )(proj, proj, proj, d_r, states, cos, sin, lg)


def _ret_gate(cfg, proj, rn, gn_g):
    tm, cw = 256, cfg.cw
    og = cfg.o_gr // cw

    def body(g_ref, rn_ref, w_ref, o_ref):
        g = g_ref[...].astype(F32)
        o_ref[...] = (g * _sigmoid(g) * (rn_ref[...].astype(F32) * w_ref[...])).astype(o_ref.dtype)

    blk = pl.BlockSpec((tm, cw), lambda i, j: (i, j))
    return pl.pallas_call(
        body, name="ret_gate", grid=(cfg.seq // tm, cfg.rv_w // cw),
        in_specs=[pl.BlockSpec((tm, cw), lambda i, j: (i, og + j)), blk, pl.BlockSpec((1, cw), lambda i, j: (0, j))],
        out_specs=blk, out_shape=jax.ShapeDtypeStruct((cfg.seq, cfg.rv_w), BF16),
        compiler_params=_params(("parallel", "parallel")),
    )(proj, rn, gn_g)


def _mix(cfg, proj, y_att, y_ret):
    tm, cw = 256, cfg.cw
    oa, o2 = cfg.o_ga // cw, cfg.o_g2 // cw

    def body(ga_ref, g2_ref, ya_ref, yr_ref, o_ref):
        r = _sigmoid(ga_ref[...].astype(F32)) * ya_ref[...].astype(F32)
        r += _sigmoid(g2_ref[...].astype(F32)) * yr_ref[...].astype(F32)
        o_ref[...] = r.astype(o_ref.dtype)

    blk = pl.BlockSpec((tm, cw), lambda i, j: (i, j))
    return pl.pallas_call(
        body, name="mix", grid=(cfg.seq // tm, cfg.d // cw),
        in_specs=[pl.BlockSpec((tm, cw), lambda i, j: (i, oa + j)), pl.BlockSpec((tm, cw), lambda i, j: (i, o2 + j)), blk, blk],
        out_specs=blk, out_shape=jax.ShapeDtypeStruct((cfg.seq, cfg.d), BF16),
        compiler_params=_params(("parallel", "parallel")),
    )(proj, proj, y_att, y_ret)


def _mix_bwd(cfg, proj, y_att, y_ret, d_mixed):
    tm, cw = 256, cfg.cw
    oa, o2 = cfg.o_ga // cw, cfg.o_g2 // cw

    def body(ga_ref, g2_ref, ya_ref, yr_ref, dm_ref, dya_ref, dyr_ref, dga_ref, dg2_ref):
        dm = dm_ref[...].astype(F32)
        sa, s2 = _sigmoid(ga_ref[...].astype(F32)), _sigmoid(g2_ref[...].astype(F32))
        dya_ref[...] = (dm * sa).astype(dya_ref.dtype)
        dyr_ref[...] = (dm * s2).astype(dyr_ref.dtype)
        dga_ref[...] = (dm * ya_ref[...].astype(F32) * sa * (1.0 - sa)).astype(dga_ref.dtype)
        dg2_ref[...] = (dm * yr_ref[...].astype(F32) * s2 * (1.0 - s2)).astype(dg2_ref.dtype)

    blk = pl.BlockSpec((tm, cw), lambda i, j: (i, j))
    out = jax.ShapeDtypeStruct((cfg.seq, cfg.d), BF16)
    return pl.pallas_call(
        body, name="mix_bwd", grid=(cfg.seq // tm, cfg.d // cw),
        in_specs=[pl.BlockSpec((tm, cw), lambda i, j: (i, oa + j)), pl.BlockSpec((tm, cw), lambda i, j: (i, o2 + j)), blk, blk, blk],
        out_specs=[blk] * 4, out_shape=[out] * 4, compiler_params=_params(("parallel", "parallel")),
    )(proj, proj, y_att, y_ret, d_mixed)


def _ret_gate_bwd(cfg, proj, rn, gn_g, rstd, d_rg):
    tm, rv = 256, cfg.rv
    og = cfg.o_gr // rv

    def body(g_ref, rn_ref, w_ref, rstd_ref, drg_ref, dg_ref, dr_ref, gw_ref):
        i = pl.program_id(1)
        g, rn, w = g_ref[...].astype(F32), rn_ref[...].astype(F32), w_ref[...]
        drg = drg_ref[...].astype(F32)
        sg = _sigmoid(g)
        silu = g * sg
        dg_ref[...] = (drg * (rn * w) * (sg * (1.0 + g * (1.0 - sg)))).astype(dg_ref.dtype)
        drn = drg * silu * w
        part = jnp.sum(drg * silu * rn, axis=0, keepdims=True)

        @pl.when(i == 0)
        def _():
            gw_ref[...] = part

        @pl.when(i > 0)
        def _():
            gw_ref[...] += part

        m1 = jnp.mean(drn, axis=1, keepdims=True)
        m2 = jnp.mean(drn * rn, axis=1, keepdims=True)
        dr_ref[...] = (rstd_ref[...] * (drn - m1 - rn * m2)).astype(dr_ref.dtype)

    blk = pl.BlockSpec((tm, rv), lambda h, i: (i, h))
    vec = pl.BlockSpec((1, rv), lambda h, i: (0, h))
    out = jax.ShapeDtypeStruct((cfg.seq, cfg.rv_w), BF16)
    return pl.pallas_call(
        body, name="ret_gate_bwd", grid=(cfg.rh, cfg.seq // tm),
        in_specs=[pl.BlockSpec((tm, rv), lambda h, i: (i, og + h)), blk, vec, pl.BlockSpec((None, tm, 1), lambda h, i: (h, i, 0)), blk],
        out_specs=[blk, blk, vec], out_shape=[out, out, jax.ShapeDtypeStruct((1, cfg.rv_w), F32)],
        compiler_params=_params(("parallel", "arbitrary")),
    )(proj, rn, gn_g, rstd, d_rg)


def _ln1(cfg, x, mo, g, b):
    tm, d = 128, cfg.d

    def body(x_ref, mo_ref, g_ref, b_ref, xh_ref, h_ref, rstd_ref):
        z = cfg.alpha * x_ref[...] + mo_ref[...]
        cen = z - jnp.mean(z, axis=1, keepdims=True)
        rstd = lax.rsqrt(jnp.mean(cen * cen, axis=1, keepdims=True) + LN_EPS)
        xh = cen * rstd
        xh_ref[...] = xh
        h_ref[...] = (xh * g_ref[...] + b_ref[...]).astype(h_ref.dtype)
        rstd_ref[...] = rstd

    row = pl.BlockSpec((tm, d), lambda i: (i, 0))
    vec = pl.BlockSpec((1, d), lambda i: (0, 0))
    col = pl.BlockSpec((tm, 1), lambda i: (i, 0))
    return pl.pallas_call(
        body, name="ln1", grid=(cfg.seq // tm,), in_specs=[row, row, vec, vec], out_specs=[row, row, col],
        out_shape=[jax.ShapeDtypeStruct((cfg.seq, d), F32), jax.ShapeDtypeStruct((cfg.seq, d), BF16),
                   jax.ShapeDtypeStruct((cfg.seq, 1), F32)],
        compiler_params=_params(("parallel",)),
    )(x, mo, g, b)


def _ffn_act(cfg, u, t):
    tm, cf = 256, _tile(cfg.ffn_p, 2048)

    def body(u_ref, t_ref, a_ref):
        uu = u_ref[...].astype(F32)
        a_ref[...] = (uu * _sigmoid(uu) * t_ref[...].astype(F32)).astype(a_ref.dtype)

    blk = pl.BlockSpec((tm, cf), lambda i, j: (i, j))
    return pl.pallas_call(
        body, name="ffn_act", grid=(cfg.seq // tm, cfg.ffn_p // cf), in_specs=[blk, blk], out_specs=blk,
        out_shape=jax.ShapeDtypeStruct((cfg.seq, cfg.ffn_p), BF16), compiler_params=_params(("parallel", "parallel")),
    )(u, t)


def _ffn_act_bwd(cfg, u, t, d_a):
    tm, cf = 256, _tile(cfg.ffn_p, 2048)

    def body(u_ref, t_ref, da_ref, du_ref, dt_ref):
        uu, da = u_ref[...].astype(F32), da_ref[...].astype(F32)
        sg = _sigmoid(uu)
        du_ref[...] = (da * t_ref[...].astype(F32) * (sg * (1.0 + uu * (1.0 - sg)))).astype(du_ref.dtype)
        dt_ref[...] = (da * uu * sg).astype(dt_ref.dtype)

    blk = pl.BlockSpec((tm, cf), lambda i, j: (i, j))
    out = jax.ShapeDtypeStruct((cfg.seq, cfg.ffn_p), BF16)
    return pl.pallas_call(
        body, name="ffn_act_bwd", grid=(cfg.seq // tm, cfg.ffn_p // cf), in_specs=[blk, blk, blk], out_specs=[blk, blk],
        out_shape=[out, out], compiler_params=_params(("parallel", "parallel")),
    )(u, t, d_a)


def _head(cfg, xhat1, ffn, hp, pu, tgt, g1, b1, g2, b2):
    tm, d = 64, cfg.d

    def body(xh_ref, ffn_ref, hp_ref, pu_ref, tgt_ref, g1_ref, b1_ref, g2_ref, b2_ref,
             loss_ref, dzf_ref, dzb_ref, dpg_ref, dpu_ref, gg_ref, gb_ref):
        i = pl.program_id(0)
        h1 = xh_ref[...] * g1_ref[...] + b1_ref[...]
        sg, pu = _sigmoid(hp_ref[...].astype(F32)), pu_ref[...].astype(F32)
        z = cfg.alpha * h1 + ffn_ref[...] + sg * pu
        cen = z - jnp.mean(z, axis=1, keepdims=True)
        rstd = lax.rsqrt(jnp.mean(cen * cen, axis=1, keepdims=True) + LN_EPS)
        xh2 = cen * rstd
        err = xh2 * g2_ref[...] + b2_ref[...] - tgt_ref[...]
        dy = err * (1.0 / d)
        part_l = jnp.sum(jnp.sum(err * err, axis=1, keepdims=True), axis=0, keepdims=True) * (0.5 / d)
        part_g = jnp.sum(dy * xh2, axis=0, keepdims=True)
        part_b = jnp.sum(dy, axis=0, keepdims=True)

        @pl.when(i == 0)
        def _():
            loss_ref[...] = jnp.zeros_like(loss_ref)
            gg_ref[...] = jnp.zeros_like(gg_ref)
            gb_ref[...] = jnp.zeros_like(gb_ref)

        loss_ref[...] += jnp.broadcast_to(part_l, loss_ref.shape)
        gg_ref[...] += part_g
        gb_ref[...] += part_b
        dxh = dy * g2_ref[...]
        m1 = jnp.mean(dxh, axis=1, keepdims=True)
        m2 = jnp.mean(dxh * xh2, axis=1, keepdims=True)
        dz = rstd * (dxh - m1 - xh2 * m2)
        dzf_ref[...] = dz
        dzb_ref[...] = dz.astype(dzb_ref.dtype)
        dpg_ref[...] = (dz * pu * sg * (1.0 - sg)).astype(dpg_ref.dtype)
        dpu_ref[...] = (dz * sg).astype(dpu_ref.dtype)

    row = pl.BlockSpec((tm, d), lambda i: (i, 0))
    vec = pl.BlockSpec((1, d), lambda i: (0, 0))
    bf = jax.ShapeDtypeStruct((cfg.seq, d), BF16)
    vec_out = jax.ShapeDtypeStruct((1, d), F32)
    return pl.pallas_call(
        body, name="head", grid=(cfg.seq // tm,), in_specs=[row] * 5 + [vec] * 4,
        out_specs=[pl.BlockSpec((1, 128), lambda i: (0, 0)), row, row, row, row, vec, vec],
        out_shape=[jax.ShapeDtypeStruct((1, 128), F32), jax.ShapeDtypeStruct((cfg.seq, d), F32), bf, bf, bf, vec_out, vec_out],
        compiler_params=_params(("arbitrary",)),
    )(xhat1, ffn, hp, pu, tgt, g1, b1, g2, b2)


def _ln1_bwd(cfg, dh_mm, dz2, xhat1, rstd1, g1):
    tm, d = 128, cfg.d

    def body(dh_ref, dz_ref, xh_ref, rstd_ref, g_ref, dzf_ref, dzb_ref, gg_ref, gb_ref):
        i = pl.program_id(0)
        dh = cfg.alpha * dz_ref[...] + dh_ref[...]
        xh = xh_ref[...]

        @pl.when(i == 0)
        def _():
            gg_ref[...] = jnp.zeros_like(gg_ref)
            gb_ref[...] = jnp.zeros_like(gb_ref)

        gg_ref[...] += jnp.sum(dh * xh, axis=0, keepdims=True)
        gb_ref[...] += jnp.sum(dh, axis=0, keepdims=True)
        dxh = dh * g_ref[...]
        m1 = jnp.mean(dxh, axis=1, keepdims=True)
        m2 = jnp.mean(dxh * xh, axis=1, keepdims=True)
        dz = rstd_ref[...] * (dxh - m1 - xh * m2)
        dzf_ref[...] = dz
        dzb_ref[...] = dz.astype(dzb_ref.dtype)

    row = pl.BlockSpec((tm, d), lambda i: (i, 0))
    vec = pl.BlockSpec((1, d), lambda i: (0, 0))
    vec_out = jax.ShapeDtypeStruct((1, d), F32)
    return pl.pallas_call(
        body, name="ln1_bwd", grid=(cfg.seq // tm,),
        in_specs=[row, row, row, pl.BlockSpec((tm, 1), lambda i: (i, 0)), vec], out_specs=[row, row, vec, vec],
        out_shape=[jax.ShapeDtypeStruct((cfg.seq, d), F32), jax.ShapeDtypeStruct((cfg.seq, d), BF16), vec_out, vec_out],
        compiler_params=_params(("arbitrary",)),
    )(dh_mm, dz2, xhat1, rstd1, g1)


def _adamw(w, g, m, v, name):
    r, c = w.shape
    assert g.shape[0] >= r and g.shape[1] == c
    tr, tc = _row_tile(r, c * 4, 2 * 1024 * 1024), c
    bc1, bc2 = 1.0 - ADAM_B1 ** ADAM_STEP, 1.0 - ADAM_B2 ** ADAM_STEP

    def body(w_ref, g_ref, m_ref, v_ref, go_ref, d_ref, mo_ref, vo_ref):
        gg = g_ref[...]
        mn = ADAM_B1 * m_ref[...] + (1.0 - ADAM_B1) * gg
        vn = ADAM_B2 * v_ref[...] + (1.0 - ADAM_B2) * (gg * gg)
        go_ref[...] = gg
        d_ref[...] = -ADAM_LR * ((mn / bc1) / (jnp.sqrt(vn / bc2) + ADAM_EPS) + ADAM_WD * w_ref[...])
        mo_ref[...] = mn
        vo_ref[...] = vn

    blk = pl.BlockSpec((tr, tc), lambda i, j: (i, j))
    out = jax.ShapeDtypeStruct((r, c), F32)
    return pl.pallas_call(
        body, name=name, grid=(r // tr, c // tc), in_specs=[blk] * 4, out_specs=[blk] * 4, out_shape=[out] * 4,
        compiler_params=_params(("parallel", "parallel")),
    )(w, g, m, v)


def _place():
    x, y, c = lax.axis_index("x"), lax.axis_index("y"), lax.axis_index("c")
    others = [(1 - x, y), (x, 1 - y), (1 - x, 1 - y)]
    return x, y, c, others


def _all_gather(ws, ax, name, landed=None):
    r, cdim = ws.shape
    full_shape = (4 * r, cdim) if ax == 0 else (r, 4 * cdim)
    n_shard = r if ax == 0 else cdim
    n_half = r // 2
    tr = _row_tile(n_half, cdim * 2, STAGE_BYTES, 16)
    nq = n_half // tr
    slots = 3

    over_ici = landed is None

    def body(*refs):
        ws_ref = refs[0]
        full_ref, buf, load_sems, write_sems, send_sems, pass_sems, recv_sems, sib_sem = refs[1 if over_ici else 2:]
        x, y, c, others = _place()
        me = 2 * x + y
        sibling = (x, y, 1 - c)

        def region(chip, half, row0, rows):
            if ax == 0:
                return full_ref.at[pl.ds(chip * n_shard + half * n_half + row0, rows), :]
            return full_ref.at[pl.ds(half * n_half + row0, rows), pl.ds(chip * n_shard, n_shard)]

        pending = {}

        def free(slot):
            for cp, remote in pending.pop(slot, []):
                if remote:
                    cp.wait_send()
                else:
                    cp.wait()

        def load(src, slot):
            free(slot)
            cp = pltpu.make_async_copy(src, buf.at[slot], load_sems.at[slot])
            cp.start()
            cp.wait()

        step = 0
        for mine in (True, False):
            half = c if mine else 1 - c
            for q in range(nq):
                slot = step % slots
                load(ws_ref.at[pl.ds(half * n_half + q * tr, tr), :], slot)
                dst = region(me, half, q * tr, tr)
                cp = pltpu.make_async_copy(buf.at[slot], dst, write_sems.at[slot])
                cp.start()
                pending[slot] = [(cp, False)]
                if mine and over_ici:
                    for k, chip in enumerate(others):
                        cp = pltpu.make_async_remote_copy(
                            src_ref=buf.at[slot], dst_ref=dst, send_sem=send_sems.at[3 * slot + k], recv_sem=recv_sems.at[k],
                            device_id=(*chip, c), device_id_type=MESH)
                        cp.start()
                        pending[slot].append((cp, True))
                step += 1
        for k, chip in enumerate(others):
            j = 2 * chip[0] + chip[1]
            if over_ici:
                whole = region(j, c, 0, n_half)
                pltpu.make_async_remote_copy(src_ref=whole, dst_ref=whole, send_sem=send_sems.at[0], recv_sem=recv_sems.at[k],
                                             device_id=(x, y, c), device_id_type=MESH).wait_recv()
            for q in range(nq):
                slot = step % slots
                part = region(j, c, q * tr, tr)
                load(part, slot)
                cp = pltpu.make_async_remote_copy(src_ref=buf.at[slot], dst_ref=part, send_sem=pass_sems.at[slot],
                                                  recv_sem=sib_sem, device_id=sibling, device_id_type=MESH)
                cp.start()
                pending[slot] = [(cp, True)]
                step += 1
        for slot in list(pending):
            free(slot)
        if ax == 0:
            three = full_ref.at[pl.ds(0, 3 * n_half), :]
        else:
            three = full_ref.at[pl.ds(0, n_half), pl.ds(0, 3 * n_shard)]
        pltpu.make_async_remote_copy(src_ref=three, dst_ref=three, send_sem=send_sems.at[0], recv_sem=sib_sem,
                                     device_id=(x, y, c), device_id_type=MESH).wait_recv()

    return pl.pallas_call(
        body, name=name, in_specs=[ANY] if over_ici else [ANY, ANY], out_specs=ANY,
        out_shape=jax.ShapeDtypeStruct(full_shape, ws.dtype), input_output_aliases={} if over_ici else {1: 0},
        scratch_shapes=[pltpu.VMEM((slots, tr, cdim), ws.dtype), pltpu.SemaphoreType.DMA((slots,)),
                        pltpu.SemaphoreType.DMA((slots,)), pltpu.SemaphoreType.DMA((3 * slots,)),
                        pltpu.SemaphoreType.DMA((slots,)), pltpu.SemaphoreType.DMA((3,)), pltpu.SemaphoreType.DMA],
        compiler_params=pltpu.CompilerParams(vmem_limit_bytes=VMEM_LIMIT),
    )(*((ws,) if over_ici else (ws, landed)))


def _gather_job(ws, ax):
    r, cdim = ws.shape
    full_shape = (4 * r, cdim) if ax == 0 else (r, 4 * cdim)
    n_shard = r if ax == 0 else cdim
    n_half = r // 2

    def copies(ins, outs, sems):
        (ws_ref,), (full_ref,) = ins, outs
        x, y, c, others = _place()
        me = 2 * x + y
        if ax == 0:
            dst = full_ref.at[pl.ds(me * n_shard + c * n_half, n_half), :]
        else:
            dst = full_ref.at[pl.ds(c * n_half, n_half), pl.ds(me * n_shard, n_shard)]
        return [(pltpu.make_async_remote_copy(
            src_ref=ws_ref.at[pl.ds(c * n_half, n_half), :], dst_ref=dst, send_sem=sems.at[k], recv_sem=sems.at[3 + k],
            device_id=(*chip, c), device_id_type=MESH), "both") for k, chip in enumerate(others)]

    return _Job([ws], [jax.ShapeDtypeStruct(full_shape, ws.dtype)], 6, copies)


def _scatter_job(pair):
    _, n_half, ncol = pair.shape

    def copies(ins, outs, sems):
        (p_ref,), (got_ref,) = ins, outs
        x, y, c, others = _place()
        return [(pltpu.make_async_remote_copy(
            src_ref=p_ref.at[k], dst_ref=got_ref.at[k], send_sem=sems.at[k], recv_sem=sems.at[3 + k],
            device_id=(*chip, c), device_id_type=MESH), "both") for k, chip in enumerate(others)]

    return _Job([pair], [jax.ShapeDtypeStruct((3, n_half, ncol), pair.dtype)], 6, copies)


def _shard_dims(shape, ax):
    rf, cf = shape
    n_shard = (rf if ax == 0 else cf) // 4
    return n_shard, (n_shard if ax == 0 else rf) // 2, (cf if ax == 0 else n_shard)


def _mm_tn_pair(a, b, ax, name, jobs=()):
    (k, m), (k2, n) = a.shape, b.shape
    assert k == k2
    n_shard, n_half, _ = _shard_dims((m, n), ax)
    tm = _tile(n_half, 1024)
    if tm < 512:
        tm = n_half
    tm, tn, tk = _mm_tiles(m, n, k, 4, tm=tm)
    ni, nj, nk = m // tm, n // tn, k // tk
    ntiles = ni * nj
    n_acc = 1 if nk > 1 else 0
    assert n_half % tm == 0 and ntiles >= 2
    jobs = list(jobs)
    n_job_in = sum(len(jb.inputs) for jb in jobs)
    n_job_out = sum(len(jb.out_shapes) for jb in jobs)

    def body(*refs):
        a_ref, b_ref = refs[0], refs[1]
        job_in = refs[2:2 + n_job_in]
        own_ref, recv_ref = refs[2 + n_job_in], refs[3 + n_job_in]
        job_out = refs[4 + n_job_in:4 + n_job_in + n_job_out]
        scratch = refs[4 + n_job_in + n_job_out:]
        acc_ref = scratch[0] if n_acc else None
        stage, local_sems, send_sems, recv_sem = scratch[n_acc:n_acc + 4]
        job_sems = scratch[n_acc + 4:]
        i, j, kk = pl.program_id(0), pl.program_id(1), pl.program_id(2)
        x, y, c, _ = _place()
        sibling = (x, y, 1 - c)
        t = i * nj + j

        def job_copies():
            found, pi, po = [], 0, 0
            for jb, sems in zip(jobs, job_sems):
                found += jb.copies(job_in[pi:pi + len(jb.inputs)], job_out[po:po + len(jb.out_shapes)], sems)
                pi, po = pi + len(jb.inputs), po + len(jb.out_shapes)
            return found

        if jobs:
            @pl.when(jnp.logical_and(t == 0, kk == 0))
            def _():
                for cp, _ in job_copies():
                    cp.start()

        def is_mine(ii):
            row0 = ii * tm
            half = row0 // n_half if ax == 1 else (row0 % n_shard) // n_half
            return half == c

        def copies(tt):
            ii, jj, slot = tt // nj, tt % nj, tt % 2
            where = (pl.ds(ii * tm, tm), pl.ds(jj * tn, tn))
            local = pltpu.make_async_copy(stage.at[slot], own_ref.at[where], local_sems.at[slot])
            remote = pltpu.make_async_remote_copy(src_ref=stage.at[slot], dst_ref=recv_ref.at[where], send_sem=send_sems.at[slot],
                                                  recv_sem=recv_sem, device_id=sibling, device_id_type=MESH)
            return is_mine(ii), local, remote

        def wait_tile(tt):
            mine, local, remote = copies(tt)

            @pl.when(mine)
            def _():
                local.wait()

            @pl.when(jnp.logical_not(mine))
            def _():
                remote.wait_send()

        part = lax.dot_general(a_ref[...], b_ref[...], _TN, preferred_element_type=F32)
        if n_acc:
            @pl.when(kk == 0)
            def _():
                acc_ref[...] = part

            @pl.when(kk > 0)
            def _():
                acc_ref[...] += part

        @pl.when(kk == nk - 1)
        def _():
            @pl.when(t >= 2)
            def _():
                wait_tile(t - 2)

            stage[t % 2] = (acc_ref[...] if n_acc else part).astype(stage.dtype)
            mine, local, remote = copies(t)

            @pl.when(mine)
            def _():
                local.start()

            @pl.when(jnp.logical_not(mine))
            def _():
                remote.start()

            @pl.when(t == ntiles - 1)
            def _():
                wait_tile(t - 1)
                wait_tile(t)
                half_rows = recv_ref.at[pl.ds(0, m // 2), :]
                pltpu.make_async_remote_copy(src_ref=half_rows, dst_ref=half_rows, send_sem=send_sems.at[0], recv_sem=recv_sem,
                                             device_id=sibling, device_id_type=MESH).wait_recv()
                for cp, _ in job_copies():
                    cp.wait()

    out = jax.ShapeDtypeStruct((m, n), BF16)
    return pl.pallas_call(
        body, name=name, grid=(ni, nj, nk),
        in_specs=[pl.BlockSpec((tk, tm), lambda i, j, kk: (kk, i)), pl.BlockSpec((tk, tn), lambda i, j, kk: (kk, j))]
        + [ANY] * n_job_in,
        out_specs=[ANY, ANY] + [ANY] * n_job_out, out_shape=[out, out] + [s for jb in jobs for s in jb.out_shapes],
        scratch_shapes=[pltpu.VMEM((tm, tn), F32)] * n_acc + [pltpu.VMEM((2, tm, tn), BF16), pltpu.SemaphoreType.DMA((2,)),
                                                               pltpu.SemaphoreType.DMA((2,)), pltpu.SemaphoreType.DMA]
        + [pltpu.SemaphoreType.DMA((jb.n_sems,)) for jb in jobs],
        compiler_params=_params(("arbitrary", "arbitrary", "arbitrary")),
    )(a, b, *[t for jb in jobs for t in jb.inputs])


def _other_chip(k, x, y):
    return jnp.where(k == 1, x, 1 - x), jnp.where(k == 0, y, 1 - y)


def _pairsum(own, recv, ax, coords, name):
    n_shard, n_half, ncol = _shard_dims(own.shape, ax)
    tr = _row_tile(n_half, ncol * 2, STAGE_BYTES // 2, 16)
    nr = n_half // tr

    def src_map(k):
        def index(r, co):
            jx, jy = _other_chip(k, co[0], co[1])
            j = 2 * jx + jy
            if ax == 0:
                return ((j * n_shard + co[2] * n_half) // tr + r, 0)
            return (co[2] * nr + r, j)
        return index

    def body(co_ref, o0, r0, o1, r1, o2, r2, out_ref):
        for k, (own_ref, recv_ref) in enumerate(((o0, r0), (o1, r1), (o2, r2))):
            out_ref[k] = (own_ref[...].astype(F32) + recv_ref[...].astype(F32)).astype(out_ref.dtype)

    blks = [pl.BlockSpec((tr, ncol), src_map(k)) for k in range(3) for _ in range(2)]
    return pl.pallas_call(
        body, name=name,
        grid_spec=pltpu.PrefetchScalarGridSpec(
            num_scalar_prefetch=1, grid=(nr,), in_specs=blks, out_specs=pl.BlockSpec((3, tr, ncol), lambda r, co: (0, r, 0))),
        out_shape=jax.ShapeDtypeStruct((3, n_half, ncol), BF16),
        compiler_params=_params(("parallel",)),
    )(coords, own, recv, own, recv, own, recv)


def _sum_share(own, recv, got, ax, coords, name):
    n_shard, n_half, ncol = _shard_dims(own.shape, ax)
    tr = _row_tile(n_half, ncol * 4, STAGE_BYTES, 16)
    nr = n_half // tr

    def src_map(r, co):
        me = 2 * co[0] + co[1]
        if ax == 0:
            return ((me * n_shard + co[2] * n_half) // tr + r, 0)
        return (co[2] * nr + r, me)

    def body(co_ref, own_ref, recv_ref, got_ref, out_ref, sbuf, local_sems, send_sems, recv_sem):
        r = pl.program_id(0)
        x, y, c, _ = _place()

        def copies(slot):
            rows = out_ref.at[pl.ds(c * n_half + r * tr, tr), :]
            local = pltpu.make_async_copy(sbuf.at[slot], rows, local_sems.at[slot])
            remote = pltpu.make_async_remote_copy(src_ref=sbuf.at[slot], dst_ref=rows, send_sem=send_sems.at[slot],
                                                  recv_sem=recv_sem, device_id=(x, y, 1 - c), device_id_type=MESH)
            return local, remote

        def wait_slot(slot):
            local, remote = copies(slot)
            local.wait()
            remote.wait_send()

        @pl.when(r >= 2)
        def _():
            wait_slot(r % 2)

        total = own_ref[...].astype(F32) + recv_ref[...].astype(F32)
        for kk in range(3):
            total = total + got_ref[kk].astype(F32)
        sbuf[r % 2] = total
        local, remote = copies(r % 2)
        local.start()
        remote.start()

        @pl.when(r == nr - 1)
        def _():
            wait_slot(r % 2)
            if nr >= 2:
                wait_slot(1 - r % 2)
            half = out_ref.at[pl.ds(0, n_half), :]
            pltpu.make_async_remote_copy(src_ref=half, dst_ref=half, send_sem=send_sems.at[0], recv_sem=recv_sem,
                                         device_id=(x, y, c), device_id_type=MESH).wait_recv()

    blk = pl.BlockSpec((tr, ncol), src_map)
    return pl.pallas_call(
        body, name=name,
        grid_spec=pltpu.PrefetchScalarGridSpec(
            num_scalar_prefetch=1, grid=(nr,), in_specs=[blk, blk, pl.BlockSpec((3, tr, ncol), lambda r, co: (0, r, 0))],
            out_specs=ANY,
            scratch_shapes=[pltpu.VMEM((2, tr, ncol), F32), pltpu.SemaphoreType.DMA((2,)), pltpu.SemaphoreType.DMA((2,)),
                            pltpu.SemaphoreType.DMA]),
        out_shape=jax.ShapeDtypeStruct((2 * n_half, ncol), F32),
        compiler_params=_params(("arbitrary",)),
    )(coords, own, recv, got)


def _vec_all_reduce(v, name):
    rows, d = v.shape

    def body(v_ref, out_ref, gath_ref, send_sems, recv_sems):
        x, y, c, _ = _place()
        me = 4 * x + 2 * y + c
        gath_ref[me] = v_ref[...]
        flips = [(fx, fy, fc) for fx in (0, 1) for fy in (0, 1) for fc in (0, 1)][1:]
        sends = []
        for k, (fx, fy, fc) in enumerate(flips):
            peer = (1 - x if fx else x, 1 - y if fy else y, 1 - c if fc else c)
            sends.append(pltpu.make_async_remote_copy(
                src_ref=v_ref, dst_ref=gath_ref.at[me], send_sem=send_sems.at[k], recv_sem=recv_sems.at[k],
                device_id=peer, device_id_type=MESH))
        for cp in sends:
            cp.start()
        for cp in sends:
            cp.wait()
        acc = gath_ref[0]
        for dev in range(1, 8):
            acc = acc + gath_ref[dev]
        out_ref[...] = acc

    vm = pl.BlockSpec(memory_space=pltpu.VMEM)
    return pl.pallas_call(
        body, name=name, in_specs=[vm], out_specs=vm, out_shape=jax.ShapeDtypeStruct((rows, d), F32),
        scratch_shapes=[pltpu.VMEM((8, rows, d), F32), pltpu.SemaphoreType.DMA((7,)), pltpu.SemaphoreType.DMA((7,))],
    )(v)


_TRANSPOSED = ("w_ffn_gate", "w_ffn_up")
_SHARD_AXIS = {"w_in": 1, "w_attn_out": 1, "w_ret_out": 0, "w_o": 0, "w_ffn_gate": 0, "w_ffn_up": 0, "w_ffn_down": 0,
               "w_ple_gate": 0, "w_ple_up": 1}
_FFN_HIDDEN_ROWS = ("w_ffn_gate", "w_ffn_up", "w_ffn_down")
_VECTORS = ("ret_gn_g", "ln1_g", "ln1_b", "ln2_g", "ln2_b")
_WEIGHTS = ("w_in", "w_attn_out", "w_ret_out", "ret_gn_g", "w_o", "ln1_g", "ln1_b", "w_ffn_gate", "w_ffn_up", "w_ffn_down",
            "w_ple_gate", "w_ple_up", "ln2_g", "ln2_b")


def _local_grads(cfg, x, p, tgt, w, vec, coords=None):
    dist = coords is not None
    xb, pb = x.astype(BF16), p.astype(BF16)
    cos, sin = _rope_tables(cfg)
    lg = _log_gamma(cfg)
    wf = {"w_in": _all_gather(w["w_in"], _SHARD_AXIS["w_in"], "ag_w_in")} if dist else dict(w)
    grads, pairs, got, waiting = {}, {}, {}, []

    def gather_behind(names, matmul):
        if not dist:
            return matmul(())
        out, *landed = matmul([_gather_job(w[n], _SHARD_AXIS[n]) for n in names])
        for n, full in zip(names, landed):
            wf[n] = _all_gather(w[n], _SHARD_AXIS[n], "ag_" + n, landed=full)
        return out

    def scatter_jobs():
        names = list(waiting)
        del waiting[:]
        return [_scatter_job(pairs[n][2]) for n in names], lambda landed: got.update(zip(names, landed))

    def grad(a, b, n, name, host=False):
        if not dist:
            grads[n] = _mm(a, b, "tn", BF16, name)
            return
        jobs, keep = scatter_jobs() if host else ((), None)
        own, recv, *landed = _mm_tn_pair(a, b, _SHARD_AXIS[n], name, jobs=jobs)
        if host:
            keep(landed)
        pairs[n] = (own, recv, _pairsum(own, recv, _SHARD_AXIS[n], coords, "rs_" + n + "_pairsum"))
        waiting.append(n)

    def scatter_behind(matmul):
        if not waiting:
            return matmul(())
        jobs, keep = scatter_jobs()
        out, *landed = matmul(jobs)
        keep(landed)
        return out

    proj = gather_behind(["w_attn_out", "w_ret_out", "w_o", "w_ffn_gate"],
                         lambda jobs: _mm(xb, wf["w_in"], "nn", BF16, "proj", jobs=jobs))
    q3 = _to_groups(cfg, proj[:, cfg.o_qa:cfg.o_qa + cfg.att_w])
    k3 = _to_groups(cfg, proj[:, cfg.o_ka:cfg.o_ka + cfg.att_w])
    v3 = _to_groups(cfg, proj[:, cfg.o_va:cfg.o_va + cfg.att_w])
    o3, lse3 = _attn_fwd(cfg, q3, k3, v3)
    o_att, lse = _attn_combine(cfg, jnp.stack(_from_groups(o3)), jnp.stack(_from_groups(lse3)))
    y_att = _mm(o_att, wf["w_attn_out"], "nn", BF16, "y_att")
    rn, rstd_r, states = _ret_fwd(cfg, proj, cos, sin, lg)
    rg = _ret_gate(cfg, proj, rn, vec["ret_gn_g"])
    y_ret = _mm(rg, wf["w_ret_out"], "nn", BF16, "y_ret")
    mixed = _mix(cfg, proj, y_att, y_ret)
    mo = _mm(mixed, wf["w_o"], "nn", F32, "mixed_out")
    xhat1, h1, rstd1 = _ln1(cfg, x, mo, vec["ln1_g"], vec["ln1_b"])
    u = gather_behind(["w_ffn_up"], lambda jobs: _mm(h1, wf["w_ffn_gate"], "nt", BF16, "ffn_u", jobs=jobs))
    t = gather_behind(["w_ffn_down"], lambda jobs: _mm(h1, wf["w_ffn_up"], "nt", BF16, "ffn_t", jobs=jobs))
    a = _ffn_act(cfg, u, t)
    ffn = gather_behind(["w_ple_gate", "w_ple_up"], lambda jobs: _mm(a, wf["w_ffn_down"], "nn", F32, "ffn_down", jobs=jobs))
    hp = _mm(h1, wf["w_ple_gate"], "nn", BF16, "ple_gate")
    pu = _mm(pb, wf["w_ple_up"], "nn", BF16, "ple_up")
    loss, dz2, dz2b, d_pg, d_pu, g_ln2g, g_ln2b = _head(cfg, xhat1, ffn, hp, pu, tgt, vec["ln1_g"], vec["ln1_b"],
                                                        vec["ln2_g"], vec["ln2_b"])

    d_a = _mm(dz2b, wf["w_ffn_down"], "nt", BF16, "d_a")
    d_u, d_t = _ffn_act_bwd(cfg, u, t, d_a)
    grad(a, dz2b, "w_ffn_down", "g_ffn_down")
    dh = scatter_behind(lambda jobs: _mm(d_u, wf["w_ffn_gate"], "nn", F32, "dh_u", jobs=jobs))
    grad(d_u, h1, "w_ffn_gate", "g_ffn_gate")
    dh = scatter_behind(lambda jobs: _mm(d_t, wf["w_ffn_up"], "nn", F32, "dh_t", add=dh, jobs=jobs))
    grad(d_t, h1, "w_ffn_up", "g_ffn_up")
    dh = _mm(d_pg, wf["w_ple_gate"], "nt", F32, "dh_pg", add=dh)
    grad(h1, d_pg, "w_ple_gate", "g_ple_gate")
    grad(pb, d_pu, "w_ple_up", "g_ple_up")
    dz1, dz1b, g_ln1g, g_ln1b = _ln1_bwd(cfg, dh, dz2, xhat1, rstd1, vec["ln1_g"])
    d_mixed = _mm(dz1b, wf["w_o"], "nt", BF16, "d_mixed")
    grad(mixed, dz1b, "w_o", "g_o")
    d_ya, d_yr, d_ga, d_g2 = _mix_bwd(cfg, proj, y_att, y_ret, d_mixed)
    d_oatt = _mm(d_ya, wf["w_attn_out"], "nt", BF16, "d_oatt")
    d_rg = _mm(d_yr, wf["w_ret_out"], "nt", BF16, "d_rg")
    grad(o_att, d_ya, "w_attn_out", "g_attn_out")
    grad(rg, d_yr, "w_ret_out", "g_ret_out")
    d_gr, d_r, g_gn = _ret_gate_bwd(cfg, proj, rn, vec["ret_gn_g"], rstd_r, d_rg)
    d_qr, d_kr, d_vr = _ret_bwd(cfg, proj, d_r, states, cos, sin, lg)
    delta = _attn_delta(cfg, d_oatt, o_att)
    do3 = jnp.stack([_deinterleave(d_oatt, dil) for _, dil in ATT_GROUPS])
    l3 = jnp.stack([_deinterleave(lse, dil) for _, dil in ATT_GROUPS])
    dl3 = jnp.stack([_deinterleave(delta, dil) for _, dil in ATT_GROUPS])
    dq3, dk3, dv3 = _attn_bwd(cfg, q3, k3, v3, do3, l3, dl3)
    dproj = jnp.concatenate(_from_groups(dq3) + _from_groups(dk3) + _from_groups(dv3) + [d_qr, d_kr, d_vr, d_gr, d_ga, d_g2], axis=1)
    grad(xb, dproj, "w_in", "g_in", host=True)
    grad_x = scatter_behind(lambda jobs: _mm(dproj, wf["w_in"], "nt", F32, "grad_x", add=dz1, add_scale=cfg.alpha, jobs=jobs))
    vgrads = {"ret_gn_g": g_gn, "ln1_g": g_ln1g, "ln1_b": g_ln1b, "ln2_g": g_ln2g, "ln2_b": g_ln2b}
    if dist:
        grads = {n: _sum_share(pairs[n][0], pairs[n][1], got[n], _SHARD_AXIS[n], coords, "rs_" + n + "_share") for n in pairs}
    return loss, grad_x, grads, vgrads


def _step(cfg, x, p, tgt, w, m, v):
    mats = [n for n in _WEIGHTS if n in _SHARD_AXIS]
    ffn_pad = cfg.ffn_shard - cfg.ffn // 4

    def stored(n, t):
        return t[0].T if n in _TRANSPOSED else t[0]

    def given(n, t):
        return (t.T if n in _TRANSPOSED else t)[None]

    def shard_bf16(n):
        ws = stored(n, w[n]).astype(BF16)
        if n in _FFN_HIDDEN_ROWS and ffn_pad:
            ws = jnp.pad(ws, [(0, ffn_pad), (0, 0)])
        return ws

    vec = {n: w[n] for n in _VECTORS}
    coords = jnp.stack([lax.axis_index("x"), lax.axis_index("y"), lax.axis_index("c")]).astype(jnp.int32)
    loss, grad_x, red, vgrads = _local_grads(cfg, x[0], p[0, 0], tgt[0], {n: shard_bf16(n) for n in mats}, vec, coords)
    loss = lax.psum(loss[0, 0], ("x", "y", "c"))
    stacked = jnp.concatenate([vgrads[n] for n in _VECTORS] + [jnp.zeros((3, cfg.d), F32)], axis=0)
    vsum = _vec_all_reduce(stacked, "vec_all_reduce")
    for i, n in enumerate(_VECTORS):
        red[n] = vsum[i:i + 1]
    outs = [[], [], [], []]
    for n in _WEIGHTS:
        if n in _SHARD_AXIS:
            res = _adamw(stored(n, w[n]), red[n], stored(n, m[n]), stored(n, v[n]), "adamw_" + n)
            res = [given(n, t) for t in res]
        else:
            res = _adamw(w[n], red[n], m[n], v[n], "adamw_" + n)
        for kept, t in zip(outs, res):
            kept.append(t)
    return (loss, grad_x[None], *outs[0], *outs[1], *outs[2], *outs[3])


def kernel(x, p, w_in, w_attn_out, w_ret_out, ret_gn_g, w_o, ln1_g, ln1_b, w_ffn_gate, w_ffn_up, w_ffn_down, w_ple_gate, w_ple_up, ln2_g, ln2_b, loss_target, m_w_in, m_w_attn_out, m_w_ret_out, m_ret_gn_g, m_w_o, m_ln1_g, m_ln1_b, m_w_ffn_gate, m_w_ffn_up, m_w_ffn_down, m_w_ple_gate, m_w_ple_up, m_ln2_g, m_ln2_b, v_w_in, v_w_attn_out, v_w_ret_out, v_ret_gn_g, v_w_o, v_ln1_g, v_ln1_b, v_w_ffn_gate, v_w_ffn_up, v_w_ffn_down, v_w_ple_gate, v_w_ple_up, v_ln2_g, v_ln2_b):
    w = dict(zip(_WEIGHTS, (w_in, w_attn_out, w_ret_out, ret_gn_g, w_o, ln1_g, ln1_b, w_ffn_gate, w_ffn_up, w_ffn_down,
                            w_ple_gate, w_ple_up, ln2_g, ln2_b)))
    m = dict(zip(_WEIGHTS, (m_w_in, m_w_attn_out, m_w_ret_out, m_ret_gn_g, m_w_o, m_ln1_g, m_ln1_b, m_w_ffn_gate, m_w_ffn_up,
                            m_w_ffn_down, m_w_ple_gate, m_w_ple_up, m_ln2_g, m_ln2_b)))
    v = dict(zip(_WEIGHTS, (v_w_in, v_w_attn_out, v_w_ret_out, v_ret_gn_g, v_w_o, v_ln1_g, v_ln1_b, v_w_ffn_gate, v_w_ffn_up,
                            v_w_ffn_down, v_w_ple_gate, v_w_ple_up, v_ln2_g, v_ln2_b)))
    return _step(_FULL, x, p, loss_target, w, m, v)
```

```python
import functools
import math

import jax
import jax.numpy as jnp
import numpy as np
from jax import lax
from jax.experimental import pallas as pl
from jax.experimental.pallas import tpu as pltpu

F32 = jnp.float32
BF16 = jnp.bfloat16
MESH = pl.DeviceIdType.MESH
ANY = pl.BlockSpec(memory_space=pl.ANY)

ATT_BLOCK = 128
ATT_GROUPS = ((128, 1), (512, 4), (2048, 16))
LN_EPS = 1e-5
GN_EPS = 1e-6
NEG_INF = -1e30
ROPE_BASE = 10000.0
ADAM_LR, ADAM_B1, ADAM_B2, ADAM_EPS, ADAM_WD, ADAM_STEP = 0.001, 0.9, 0.999, 1e-08, 0.01, 10
VMEM_LIMIT = 56 * 1024 * 1024
STAGE_BYTES = 4 * 1024 * 1024
MM_VMEM_BYTES = 46 * 1024 * 1024 + 512 * 1024


class _Cfg:
    def __init__(self, seq, d, ple, ahd, ahg, rh, rqk, rv, ffn, cw):
        self.seq, self.d, self.ple, self.ahd, self.ahg = seq, d, ple, ahd, ahg
        self.rh, self.rqk, self.rv, self.ffn, self.cw = rh, rqk, rv, ffn, cw
        self.aw = ahg * ahd
        self.att_w = 3 * self.aw
        self.rqk_w = rh * rqk
        self.rv_w = rh * rv
        offs = np.cumsum([0] + [self.att_w] * 3 + [self.rqk_w] * 2 + [self.rv_w] * 2 + [d] * 2)
        (self.o_qa, self.o_ka, self.o_va, self.o_qr, self.o_kr, self.o_vr, self.o_gr, self.o_ga, self.o_g2,
         self.in_w) = [int(v) for v in offs]
        self.alpha = 2.0 ** 0.25
        self.ffn_shard = -(-(ffn // 4) // 128) * 128
        self.ffn_p = 4 * self.ffn_shard
        assert self.rv_w == d and d % cw == 0
        for o in (self.o_gr, self.o_ga, self.o_g2):
            assert o % cw == 0
        assert self.o_qr % rqk == 0 and self.o_kr % rqk == 0 and self.o_vr % rv == 0 and self.o_gr % rv == 0
        assert rqk // 2 % 128 == 0 and seq % (ATT_BLOCK * 16) == 0


_FULL = _Cfg(seq=4096, d=4096, ple=256, ahd=128, ahg=8, rh=8, rqk=256, rv=512, ffn=11008, cw=1024)


def _tile(n, target, q=128):
    t = min(n, target) // q * q
    while t >= q:
        if n % t == 0:
            return t
        t -= q
    return n


def _row_tile(n, row_bytes, target_bytes, q=8):
    best = None
    for t in range(q, n + 1, q):
        if n % t == 0 and (best is None or t * row_bytes <= target_bytes):
            best = t
            if t * row_bytes > target_bytes:
                break
    return n if best is None else best


def _mm_tiles(m, n, k, out_bytes_per_elem, tm=None):
    fixed_tm = tm is not None
    tm, tn = tm if fixed_tm else _tile(m, 1024), _tile(n, 1024)
    tk_most = k if k <= 4096 else _tile(k, 4096)
    tk = tk_most

    def need(tm_, tk_):
        acc = 4 if tk_ < k else 0
        return 4 * (tm_ + tn) * tk_ + tm_ * tn * (out_bytes_per_elem + acc + 4)

    while need(tm, tk) > MM_VMEM_BYTES:
        if tk > 2048:
            tk = _tile(k, tk - 128)
        elif not fixed_tm and tm > 512 and m % 512 == 0:
            tm, tk = 512, tk_most
        elif tk > 256:
            tk = _tile(k, tk - 128)
        else:
            break
    return tm, tn, tk


def _params(sem=None):
    return pltpu.CompilerParams(dimension_semantics=sem, vmem_limit_bytes=VMEM_LIMIT)


def _sigmoid(x):
    return 1.0 / (1.0 + jnp.exp(-x))


class _Job:
    def __init__(self, inputs, out_shapes, n_sems, copies):
        self.inputs, self.out_shapes, self.n_sems, self.copies = list(inputs), list(out_shapes), n_sems, copies


class _Epilogue:
    def __init__(self, extras, out_dtypes, fn):
        self.extras, self.out_dtypes, self.fn = list(extras), list(out_dtypes), fn


def _mm(a, b, mode, out_dtype, name, add=None, add_scale=1.0, jobs=(), epi=None):
    if mode == "nn":
        (m, k), (k2, n) = a.shape, b.shape
    elif mode == "nt":
        (m, k), (n, k2) = a.shape, b.shape
    else:
        (k, m), (k2, n) = a.shape, b.shape
    assert k == k2, (name, a.shape, b.shape)
    has_add = add is not None
    if epi is None:
        epi = _Epilogue([], [out_dtype], lambda r: (r,))
    tile_bytes = sum(2 * jnp.dtype(dt).itemsize for dt in epi.out_dtypes) + (8 if has_add else 0)
    tile_bytes += sum(2 * t.dtype.itemsize for t, _ in epi.extras)
    tm, tn, tk = _mm_tiles(m, math.gcd(n, *[off for _, off in epi.extras]), k, tile_bytes)
    nk = k // tk
    if mode == "tn":
        a_spec = pl.BlockSpec((tk, tm), lambda i, j, kk: (kk, i))
        dims = (((0,), (0,)), ((), ()))
    else:
        a_spec = pl.BlockSpec((tm, tk), lambda i, j, kk: (i, kk))
        dims = (((1,), (1,)), ((), ())) if mode == "nt" else (((1,), (0,)), ((), ()))
    if mode == "nt":
        b_spec = pl.BlockSpec((tn, tk), lambda i, j, kk: (j, kk))
    else:
        b_spec = pl.BlockSpec((tk, tn), lambda i, j, kk: (kk, j))
    o_spec = pl.BlockSpec((tm, tn), lambda i, j, kk: (i, j))
    ni, nj = m // tm, n // tn
    jobs = list(jobs)
    n_extra, n_out = len(epi.extras), len(epi.out_dtypes)
    n_main_in = (3 if has_add else 2) + n_extra
    n_job_in = sum(len(jb.inputs) for jb in jobs)
    n_job_out = sum(len(jb.out_shapes) for jb in jobs)
    n_acc = 1 if nk > 1 else 0

    def body(*refs):
        a_ref, b_ref = refs[0], refs[1]
        add_ref = refs[2] if has_add else None
        extra_refs = refs[n_main_in - n_extra:n_main_in]
        job_in = refs[n_main_in:n_main_in + n_job_in]
        o_refs = refs[n_main_in + n_job_in:n_main_in + n_job_in + n_out]
        job_out = refs[n_main_in + n_job_in + n_out:n_main_in + n_job_in + n_out + n_job_out]
        acc_ref = refs[n_main_in + n_job_in + n_out + n_job_out] if n_acc else None
        job_sems = refs[n_main_in + n_job_in + n_out + n_acc + n_job_out:]
        i, j, kk = pl.program_id(0), pl.program_id(1), pl.program_id(2)

        def finish(r):
            if has_add:
                r = r + add_scale * add_ref[...]
            for o_ref, tile in zip(o_refs, epi.fn(r, *[ref[...] for ref in extra_refs])):
                o_ref[...] = tile.astype(o_ref.dtype)

        def job_copies():
            found, pi, po = [], 0, 0
            for jb, sems in zip(jobs, job_sems):
                found += jb.copies(job_in[pi:pi + len(jb.inputs)], job_out[po:po + len(jb.out_shapes)], sems)
                pi, po = pi + len(jb.inputs), po + len(jb.out_shapes)
            return found

        if jobs:
            @pl.when(jnp.logical_and(jnp.logical_and(i == 0, j == 0), kk == 0))
            def _():
                for cp, _ in job_copies():
                    cp.start()

        part = lax.dot_general(a_ref[...], b_ref[...], dims, preferred_element_type=F32)
        if n_acc:
            @pl.when(kk == 0)
            def _():
                acc_ref[...] = part

            @pl.when(kk > 0)
            def _():
                acc_ref[...] += part

            @pl.when(kk == nk - 1)
            def _():
                finish(acc_ref[...])
        else:
            finish(part)

        if jobs:
            @pl.when(jnp.logical_and(jnp.logical_and(i == ni - 1, j == nj - 1), kk == nk - 1))
            def _():
                for cp, _ in job_copies():
                    cp.wait()

    job_inputs = [t for jb in jobs for t in jb.inputs]
    extra_specs = [pl.BlockSpec((tm, tn), functools.partial(lambda i, j, kk, shift: (i, j + shift), shift=off // tn))
                   for _, off in epi.extras]
    outs = pl.pallas_call(
        body, name=name, grid=(ni, nj, nk),
        in_specs=[a_spec, b_spec] + ([o_spec] if has_add else []) + extra_specs + [ANY] * n_job_in,
        out_specs=[o_spec] * n_out + [ANY] * n_job_out,
        out_shape=[jax.ShapeDtypeStruct((m, n), dt) for dt in epi.out_dtypes] + [s for jb in jobs for s in jb.out_shapes],
        scratch_shapes=[pltpu.VMEM((tm, tn), F32)] * n_acc + [pltpu.SemaphoreType.DMA((jb.n_sems,)) for jb in jobs],
        compiler_params=_params(("arbitrary",) * 3 if jobs else ("parallel", "parallel", "arbitrary")),
    )(*((a, b, add) if has_add else (a, b)), *[t for t, _ in epi.extras], *job_inputs)
    return outs if jobs or n_out > 1 else outs[0]


def _seg_blocks(cfg):
    return [cfg.seq // ATT_BLOCK // dil for _, dil in ATT_GROUPS]


def _group_select(g, vals):
    out = jnp.int32(vals[-1])
    for i in range(len(vals) - 2, -1, -1):
        out = jnp.where(g == i, jnp.int32(vals[i]), out)
    return out


def _col(tile, h):
    lane = lax.broadcasted_iota(jnp.int32, tile.shape, 1)
    return jnp.sum(jnp.where(lane == h, tile, 0.0), axis=1, keepdims=True)


def _set_col(tile, h, col):
    lane = lax.broadcasted_iota(jnp.int32, tile.shape, 1)
    return jnp.where(lane == h, col, tile)


_NT = (((1,), (1,)), ((), ()))
_TN = (((0,), (0,)), ((), ()))


def _dot(a, b):
    return jnp.dot(a, b, preferred_element_type=F32)


def _dot_nt(a, b):
    return lax.dot_general(a, b, _NT, preferred_element_type=F32)


def _dot_tn(a, b):
    return lax.dot_general(a, b, _TN, preferred_element_type=F32)


def _attn_fwd(cfg, q3, k3, v3):
    nb = cfg.seq // ATT_BLOCK
    segs = _seg_blocks(cfg)
    scale = cfg.ahd ** -0.5
    blk = ATT_BLOCK

    def body(q_ref, kp_ref, kc_ref, vp_ref, vc_ref, o_ref, lse_ref):
        g, b = pl.program_id(0), pl.program_id(1)
        has_prev = (b & (_group_select(g, segs) - 1)) != 0
        qi = lax.broadcasted_iota(jnp.int32, (blk, blk), 0)
        kj = lax.broadcasted_iota(jnp.int32, (blk, blk), 1)
        valid_c = kj <= qi
        valid_p = jnp.logical_and(kj >= qi, has_prev)
        lse = jnp.zeros((blk, cfg.ahg), F32)
        for h in range(cfg.ahg):
            hs = slice(h * cfg.ahd, (h + 1) * cfg.ahd)
            q = q_ref[:, hs]
            s_c = jnp.where(valid_c, _dot_nt(q, kc_ref[:, hs]) * scale, NEG_INF)
            s_p = jnp.where(valid_p, _dot_nt(q, kp_ref[:, hs]) * scale, NEG_INF)
            m = jnp.maximum(jnp.max(s_c, axis=1, keepdims=True), jnp.max(s_p, axis=1, keepdims=True))
            p_c, p_p = jnp.exp(s_c - m), jnp.exp(s_p - m)
            den = jnp.sum(p_c, axis=1, keepdims=True) + jnp.sum(p_p, axis=1, keepdims=True)
            o = _dot(p_c.astype(BF16), vc_ref[:, hs]) + _dot(p_p.astype(BF16), vp_ref[:, hs])
            o_ref[:, hs] = (o / den).astype(o_ref.dtype)
            lse = _set_col(lse, h, m + jnp.log(den))
        lse_ref[...] = lse

    cur = pl.BlockSpec((None, blk, cfg.aw), lambda g, b: (g, b, 0))
    prev = pl.BlockSpec((None, blk, cfg.aw), lambda g, b: (g, jnp.maximum(b - 1, 0), 0))
    return pl.pallas_call(
        body, name="attn_fwd", grid=(3, nb), in_specs=[cur, prev, cur, prev, cur],
        out_specs=[cur, pl.BlockSpec((None, blk, cfg.ahg), lambda g, b: (g, b, 0))],
        out_shape=[jax.ShapeDtypeStruct((3, cfg.seq, cfg.aw), BF16), jax.ShapeDtypeStruct((3, cfg.seq, cfg.ahg), F32)],
        compiler_params=_params(("parallel", "parallel")),
    )(q3, k3, k3, v3, v3)


def _attn_combine(cfg, o3, lse3):
    tm = 256

    def body(o_ref, lse_ref, oa_ref, l_ref):
        l0, l1, l2 = lse_ref[0], lse_ref[1], lse_ref[2]
        m = jnp.maximum(jnp.maximum(l0, l1), l2)
        big = m + jnp.log(jnp.exp(l0 - m) + jnp.exp(l1 - m) + jnp.exp(l2 - m))
        ws = [jnp.exp(l0 - big), jnp.exp(l1 - big), jnp.exp(l2 - big)]
        for h in range(cfg.ahg):
            hs = slice(h * cfg.ahd, (h + 1) * cfg.ahd)
            acc = _col(ws[0], h) * o_ref[0, :, hs].astype(F32)
            acc += _col(ws[1], h) * o_ref[1, :, hs].astype(F32)
            acc += _col(ws[2], h) * o_ref[2, :, hs].astype(F32)
            oa_ref[:, hs] = acc.astype(oa_ref.dtype)
        l_ref[...] = big

    return pl.pallas_call(
        body, name="attn_combine", grid=(cfg.seq // tm,),
        in_specs=[pl.BlockSpec((3, tm, cfg.aw), lambda i: (0, i, 0)), pl.BlockSpec((3, tm, cfg.ahg), lambda i: (0, i, 0))],
        out_specs=[pl.BlockSpec((tm, cfg.aw), lambda i: (i, 0)), pl.BlockSpec((tm, cfg.ahg), lambda i: (i, 0))],
        out_shape=[jax.ShapeDtypeStruct((cfg.seq, cfg.aw), BF16), jax.ShapeDtypeStruct((cfg.seq, cfg.ahg), F32)],
        compiler_params=_params(("parallel",)),
    )(o3, lse3)


def _attn_delta(cfg, do, o):
    tm = 256

    def body(do_ref, o_ref, d_ref):
        out = jnp.zeros((tm, cfg.ahg), F32)
        for h in range(cfg.ahg):
            hs = slice(h * cfg.ahd, (h + 1) * cfg.ahd)
            prod = do_ref[:, hs].astype(F32) * o_ref[:, hs].astype(F32)
            out = _set_col(out, h, jnp.sum(prod, axis=1, keepdims=True))
        d_ref[...] = out

    row = pl.BlockSpec((tm, cfg.aw), lambda i: (i, 0))
    return pl.pallas_call(
        body, name="attn_delta", grid=(cfg.seq // tm,), in_specs=[row, row],
        out_specs=pl.BlockSpec((tm, cfg.ahg), lambda i: (i, 0)),
        out_shape=jax.ShapeDtypeStruct((cfg.seq, cfg.ahg), F32), compiler_params=_params(("parallel",)),
    )(do, o)


def _attn_bwd(cfg, q3, k3, v3, do3, l3, d3):
    nb = cfg.seq // ATT_BLOCK
    segs = _seg_blocks(cfg)
    scale = cfg.ahd ** -0.5
    blk = ATT_BLOCK

    def body(q_ref, qn_ref, do_ref, don_ref, l_ref, ln_ref, d_ref, dn_ref, kp_ref, kc_ref, vp_ref, vc_ref,
             dq_ref, dk_ref, dv_ref):
        g, b = pl.program_id(0), pl.program_id(1)
        seg_mask = _group_select(g, segs) - 1
        has_prev = (b & seg_mask) != 0
        has_next = jnp.logical_and(b + 1 < nb, ((b + 1) & seg_mask) != 0)
        qi = lax.broadcasted_iota(jnp.int32, (blk, blk), 0)
        kj = lax.broadcasted_iota(jnp.int32, (blk, blk), 1)
        valid_c = kj <= qi
        valid_p = jnp.logical_and(kj >= qi, has_prev)
        valid_n = jnp.logical_and(kj >= qi, has_next)
        lse, lse_n, dlt, dlt_n = l_ref[...], ln_ref[...], d_ref[...], dn_ref[...]
        for h in range(cfg.ahg):
            hs = slice(h * cfg.ahd, (h + 1) * cfg.ahd)
            q, qn, do, don = q_ref[:, hs], qn_ref[:, hs], do_ref[:, hs], don_ref[:, hs]
            kc, kp, vc, vp = kc_ref[:, hs], kp_ref[:, hs], vc_ref[:, hs], vp_ref[:, hs]
            lh, lnh, dh, dnh = _col(lse, h), _col(lse_n, h), _col(dlt, h), _col(dlt_n, h)
            p_c = jnp.where(valid_c, jnp.exp(_dot_nt(q, kc) * scale - lh), 0.0)
            p_p = jnp.where(valid_p, jnp.exp(_dot_nt(q, kp) * scale - lh), 0.0)
            ds_c = (p_c * (_dot_nt(do, vc) - dh)).astype(BF16)
            ds_p = (p_p * (_dot_nt(do, vp) - dh)).astype(BF16)
            dq_ref[:, hs] = ((_dot(ds_c, kc) + _dot(ds_p, kp)) * scale).astype(dq_ref.dtype)
            p_n = jnp.where(valid_n, jnp.exp(_dot_nt(qn, kc) * scale - lnh), 0.0)
            ds_n = (p_n * (_dot_nt(don, vc) - dnh)).astype(BF16)
            dk = _dot_tn(ds_c, q) + _dot_tn(ds_n, qn)
            dv = _dot_tn(p_c.astype(BF16), do) + _dot_tn(p_n.astype(BF16), don)
            dk_ref[:, hs] = (dk * scale).astype(dk_ref.dtype)
            dv_ref[:, hs] = dv.astype(dv_ref.dtype)

    def spec(width, shift):
        if shift == 0:
            return pl.BlockSpec((None, blk, width), lambda g, b: (g, b, 0))
        if shift > 0:
            return pl.BlockSpec((None, blk, width), lambda g, b: (g, jnp.minimum(b + 1, nb - 1), 0))
        return pl.BlockSpec((None, blk, width), lambda g, b: (g, jnp.maximum(b - 1, 0), 0))

    w, hw = cfg.aw, cfg.ahg
    out = jax.ShapeDtypeStruct((3, cfg.seq, w), BF16)
    return pl.pallas_call(
        body, name="attn_bwd", grid=(3, nb),
        in_specs=[spec(w, 0), spec(w, 1), spec(w, 0), spec(w, 1), spec(hw, 0), spec(hw, 1), spec(hw, 0), spec(hw, 1),
                  spec(w, -1), spec(w, 0), spec(w, -1), spec(w, 0)],
        out_specs=[spec(w, 0)] * 3, out_shape=[out, out, out], compiler_params=_params(("parallel", "parallel")),
    )(q3, q3, do3, do3, l3, l3, d3, d3, k3, k3, v3, v3)


def _deinterleave(t, dil):
    s, w = t.shape
    return t if dil == 1 else t.reshape(s // dil, dil, w).transpose(1, 0, 2).reshape(s, w)


def _interleave(t, dil):
    s, w = t.shape
    return t if dil == 1 else t.reshape(dil, s // dil, w).transpose(1, 0, 2).reshape(s, w)


def _to_groups(cfg, t):
    return jnp.stack([_deinterleave(t[:, g * cfg.aw:(g + 1) * cfg.aw], dil) for g, (_, dil) in enumerate(ATT_GROUPS)])


def _from_groups(t3):
    return [_interleave(t3[g], dil) for g, (_, dil) in enumerate(ATT_GROUPS)]


def _rope_tables(cfg):
    half = cfg.rqk // 2
    pos = jnp.arange(cfg.seq, dtype=F32)
    inv_freq = ROPE_BASE ** (-jnp.arange(half, dtype=F32) / half)
    ang = pos[:, None] * inv_freq[None, :]
    return jnp.cos(ang), jnp.sin(ang)


def _log_gamma(cfg):
    lg = jnp.log(1.0 - 2.0 ** (-5.0 - jnp.arange(cfg.rh, dtype=F32)))
    return jnp.broadcast_to(lg[:, None, None], (cfg.rh, 1, max(cfg.rqk, cfg.rv)))


def _rot(t, c, s, half):
    t1, t2 = t[:, :half], t[:, half:]
    return jnp.concatenate([t1 * c - t2 * s, t1 * s + t2 * c], axis=1)


def _unrot(d, c, s, half):
    d1, d2 = d[:, :half], d[:, half:]
    return jnp.concatenate([d1 * c + d2 * s, d2 * c - d1 * s], axis=1)


def _decays(lg_ref, cfg):
    blk = ATT_BLOCK
    lg_v, lg_k = lg_ref[:, :cfg.rv], lg_ref[:, :cfg.rqk]
    qi = lax.broadcasted_iota(jnp.int32, (blk, blk), 0)
    kj = lax.broadcasted_iota(jnp.int32, (blk, blk), 1)
    diff = (qi - kj).astype(F32)
    intra = jnp.where(diff >= 0, jnp.exp(jnp.maximum(diff, 0.0) * lg_ref[:, :blk]), 0.0)
    idx_v = lax.broadcasted_iota(jnp.int32, (blk, cfg.rv), 0).astype(F32)
    idx_k = lax.broadcasted_iota(jnp.int32, (blk, cfg.rqk), 0).astype(F32)
    cross = jnp.exp((idx_v + 1.0) * lg_v)
    state = jnp.exp((blk - 1.0 - idx_k) * lg_k)
    chunk = jnp.exp(float(blk) * lg_v)
    return intra, cross, state, chunk


def _ret_fwd(cfg, proj, cos, sin, lg):
    blk, nc, half = ATT_BLOCK, cfg.seq // ATT_BLOCK, cfg.rqk // 2
    kscale = cfg.rqk ** -0.5

    def body(q_ref, k_ref, v_ref, cos_ref, sin_ref, lg_ref, rn_ref, rstd_ref, st_ref, state):
        n = pl.program_id(1)

        @pl.when(n == 0)
        def _():
            state[...] = jnp.zeros_like(state)

        intra, cross_d, state_d, chunk_d = _decays(lg_ref, cfg)
        c, s = cos_ref[...], sin_ref[...]
        qb = _rot(q_ref[...].astype(F32), c, s, half).astype(BF16)
        kf = _rot(k_ref[...].astype(F32), c, s, half) * kscale
        vb = v_ref[...]
        prev = state[...]
        st_ref[...] = prev.astype(BF16)
        att = _dot_nt(qb, kf.astype(BF16)) * intra
        out = _dot(att.astype(BF16), vb) + _dot(qb, prev.astype(BF16)) * cross_d
        state[...] = chunk_d * prev + _dot_tn((kf * state_d).astype(BF16), vb)
        mu = jnp.mean(out, axis=1, keepdims=True)
        cen = out - mu
        rstd = lax.rsqrt(jnp.mean(cen * cen, axis=1, keepdims=True) + GN_EPS)
        rn_ref[...] = (cen * rstd).astype(rn_ref.dtype)
        rstd_ref[...] = rstd

    oq, ok, ov = cfg.o_qr // cfg.rqk, cfg.o_kr // cfg.rqk, cfg.o_vr // cfg.rv
    tab = pl.BlockSpec((blk, half), lambda h, n: (n, 0))
    return pl.pallas_call(
        body, name="ret_fwd", grid=(cfg.rh, nc),
        in_specs=[pl.BlockSpec((blk, cfg.rqk), lambda h, n: (n, oq + h)), pl.BlockSpec((blk, cfg.rqk), lambda h, n: (n, ok + h)),
                  pl.BlockSpec((blk, cfg.rv), lambda h, n: (n, ov + h)), tab, tab,
                  pl.BlockSpec((None, 1, lg.shape[2]), lambda h, n: (h, 0, 0))],
        out_specs=[pl.BlockSpec((blk, cfg.rv), lambda h, n: (n, h)), pl.BlockSpec((None, blk, 1), lambda h, n: (h, n, 0)),
                   pl.BlockSpec((None, None, cfg.rqk, cfg.rv), lambda h, n: (h, n, 0, 0))],
        out_shape=[jax.ShapeDtypeStruct((cfg.seq, cfg.rv_w), BF16), jax.ShapeDtypeStruct((cfg.rh, cfg.seq, 1), F32),
                   jax.ShapeDtypeStruct((cfg.rh, nc, cfg.rqk, cfg.rv), BF16)],
        scratch_shapes=[pltpu.VMEM((cfg.rqk, cfg.rv), F32)], compiler_params=_params(("parallel", "arbitrary")),
    )(proj, proj, proj, cos, sin, lg)


def _ret_bwd(cfg, proj, d_r, states, cos, sin, lg):
    blk, nc, half = ATT_BLOCK, cfg.seq // ATT_BLOCK, cfg.rqk // 2
    kscale = cfg.rqk ** -0.5

    def body(q_ref, k_ref, v_ref, do_ref, st_ref, cos_ref, sin_ref, lg_ref, dq_ref, dk_ref, dv_ref, dstate):
        n = pl.program_id(1)

        @pl.when(n == 0)
        def _():
            dstate[...] = jnp.zeros_like(dstate)

        intra, cross_d, state_d, chunk_d = _decays(lg_ref, cfg)
        c, s = cos_ref[...], sin_ref[...]
        qb = _rot(q_ref[...].astype(F32), c, s, half).astype(BF16)
        kf = _rot(k_ref[...].astype(F32), c, s, half) * kscale
        kb, ksb = kf.astype(BF16), (kf * state_d).astype(BF16)
        vb, prev = v_ref[...], st_ref[...]
        do = do_ref[...].astype(F32)
        dob, docb = do.astype(BF16), (do * cross_d).astype(BF16)
        dsb = dstate[...].astype(BF16)
        att = (_dot_nt(qb, kb) * intra).astype(BF16)
        datt = (_dot_nt(dob, vb) * intra).astype(BF16)
        d_q = _dot(datt, kb) + _dot_nt(docb, prev)
        d_k = _dot_tn(datt, qb) + _dot_nt(vb, dsb) * state_d
        d_v = _dot_tn(att, dob) + _dot(ksb, dsb)
        dstate[...] = chunk_d * dstate[...] + _dot_tn(qb, docb)
        dq_ref[...] = _unrot(d_q, c, s, half).astype(dq_ref.dtype)
        dk_ref[...] = _unrot(d_k * kscale, c, s, half).astype(dk_ref.dtype)
        dv_ref[...] = d_v.astype(dv_ref.dtype)

    oq, ok, ov = cfg.o_qr // cfg.rqk, cfg.o_kr // cfg.rqk, cfg.o_vr // cfg.rv
    last = nc - 1
    tab = pl.BlockSpec((blk, half), lambda h, n: (last - n, 0))
    qk_out = pl.BlockSpec((blk, cfg.rqk), lambda h, n: (last - n, h))
    v_out = pl.BlockSpec((blk, cfg.rv), lambda h, n: (last - n, h))
    return pl.pallas_call(
        body, name="ret_bwd", grid=(cfg.rh, nc),
        in_specs=[pl.BlockSpec((blk, cfg.rqk), lambda h, n: (last - n, oq + h)),
                  pl.BlockSpec((blk, cfg.rqk), lambda h, n: (last - n, ok + h)),
                  pl.BlockSpec((blk, cfg.rv), lambda h, n: (last - n, ov + h)), v_out,
                  pl.BlockSpec((None, None, cfg.rqk, cfg.rv), lambda h, n: (h, last - n, 0, 0)), tab, tab,
                  pl.BlockSpec((None, 1, lg.shape[2]), lambda h, n: (h, 0, 0))],
        out_specs=[qk_out, qk_out, v_out],
        out_shape=[jax.ShapeDtypeStruct((cfg.seq, cfg.rqk_w), BF16), jax.ShapeDtypeStruct((cfg.seq, cfg.rqk_w), BF16),
                   jax.ShapeDtypeStruct((cfg.seq, cfg.rv_w), BF16)],
        scratch_shapes=[pltpu.VMEM((cfg.rqk, cfg.rv), F32)], compiler_params=_params(("parallel", "arbitrary")),
    )(proj, proj, proj, d_r, states, cos, sin, lg)


def _ret_gate(cfg, proj, rn, gn_g):
    tm, cw = 256, cfg.cw
    og = cfg.o_gr // cw

    def body(g_ref, rn_ref, w_ref, o_ref):
        g = g_ref[...].astype(F32)
        o_ref[...] = (g * _sigmoid(g) * (rn_ref[...].astype(F32) * w_ref[...])).astype(o_ref.dtype)

    blk = pl.BlockSpec((tm, cw), lambda i, j: (i, j))
    return pl.pallas_call(
        body, name="ret_gate", grid=(cfg.seq // tm, cfg.rv_w // cw),
        in_specs=[pl.BlockSpec((tm, cw), lambda i, j: (i, og + j)), blk, pl.BlockSpec((1, cw), lambda i, j: (0, j))],
        out_specs=blk, out_shape=jax.ShapeDtypeStruct((cfg.seq, cfg.rv_w), BF16),
        compiler_params=_params(("parallel", "parallel")),
    )(proj, rn, gn_g)


def _ret_gate_bwd(cfg, proj, rn, gn_g, rstd, d_rg):
    tm, rv = 256, cfg.rv
    og = cfg.o_gr // rv

    def body(g_ref, rn_ref, w_ref, rstd_ref, drg_ref, dg_ref, dr_ref, gw_ref):
        i = pl.program_id(1)
        g, rn, w = g_ref[...].astype(F32), rn_ref[...].astype(F32), w_ref[...]
        drg = drg_ref[...].astype(F32)
        sg = _sigmoid(g)
        silu = g * sg
        dg_ref[...] = (drg * (rn * w) * (sg * (1.0 + g * (1.0 - sg)))).astype(dg_ref.dtype)
        drn = drg * silu * w
        part = jnp.sum(drg * silu * rn, axis=0, keepdims=True)

        @pl.when(i == 0)
        def _():
            gw_ref[...] = part

        @pl.when(i > 0)
        def _():
            gw_ref[...] += part

        m1 = jnp.mean(drn, axis=1, keepdims=True)
        m2 = jnp.mean(drn * rn, axis=1, keepdims=True)
        dr_ref[...] = (rstd_ref[...] * (drn - m1 - rn * m2)).astype(dr_ref.dtype)

    blk = pl.BlockSpec((tm, rv), lambda h, i: (i, h))
    vec = pl.BlockSpec((1, rv), lambda h, i: (0, h))
    out = jax.ShapeDtypeStruct((cfg.seq, cfg.rv_w), BF16)
    return pl.pallas_call(
        body, name="ret_gate_bwd", grid=(cfg.rh, cfg.seq // tm),
        in_specs=[pl.BlockSpec((tm, rv), lambda h, i: (i, og + h)), blk, vec, pl.BlockSpec((None, tm, 1), lambda h, i: (h, i, 0)), blk],
        out_specs=[blk, blk, vec], out_shape=[out, out, jax.ShapeDtypeStruct((1, cfg.rv_w), F32)],
        compiler_params=_params(("parallel", "arbitrary")),
    )(proj, rn, gn_g, rstd, d_rg)


def _ln1(cfg, x, mo, g, b):
    tm, d = 128, cfg.d

    def body(x_ref, mo_ref, g_ref, b_ref, xh_ref, h_ref, rstd_ref):
        z = cfg.alpha * x_ref[...] + mo_ref[...]
        cen = z - jnp.mean(z, axis=1, keepdims=True)
        rstd = lax.rsqrt(jnp.mean(cen * cen, axis=1, keepdims=True) + LN_EPS)
        xh = cen * rstd
        xh_ref[...] = xh
        h_ref[...] = (xh * g_ref[...] + b_ref[...]).astype(h_ref.dtype)
        rstd_ref[...] = rstd

    row = pl.BlockSpec((tm, d), lambda i: (i, 0))
    vec = pl.BlockSpec((1, d), lambda i: (0, 0))
    col = pl.BlockSpec((tm, 1), lambda i: (i, 0))
    return pl.pallas_call(
        body, name="ln1", grid=(cfg.seq // tm,), in_specs=[row, row, vec, vec], out_specs=[row, row, col],
        out_shape=[jax.ShapeDtypeStruct((cfg.seq, d), F32), jax.ShapeDtypeStruct((cfg.seq, d), BF16),
                   jax.ShapeDtypeStruct((cfg.seq, 1), F32)],
        compiler_params=_params(("parallel",)),
    )(x, mo, g, b)


def _head(cfg, xhat1, ffn, hp, pu, tgt, g1, b1, g2, b2):
    tm, d = 64, cfg.d

    def body(xh_ref, ffn_ref, hp_ref, pu_ref, tgt_ref, g1_ref, b1_ref, g2_ref, b2_ref,
             loss_ref, dzf_ref, dzb_ref, dpg_ref, dpu_ref, gg_ref, gb_ref):
        i = pl.program_id(0)
        h1 = xh_ref[...] * g1_ref[...] + b1_ref[...]
        sg, pu = _sigmoid(hp_ref[...].astype(F32)), pu_ref[...].astype(F32)
        z = cfg.alpha * h1 + ffn_ref[...] + sg * pu
        cen = z - jnp.mean(z, axis=1, keepdims=True)
        rstd = lax.rsqrt(jnp.mean(cen * cen, axis=1, keepdims=True) + LN_EPS)
        xh2 = cen * rstd
        err = xh2 * g2_ref[...] + b2_ref[...] - tgt_ref[...]
        dy = err * (1.0 / d)
        part_l = jnp.sum(jnp.sum(err * err, axis=1, keepdims=True), axis=0, keepdims=True) * (0.5 / d)
        part_g = jnp.sum(dy * xh2, axis=0, keepdims=True)
        part_b = jnp.sum(dy, axis=0, keepdims=True)

        @pl.when(i == 0)
        def _():
            loss_ref[...] = jnp.zeros_like(loss_ref)
            gg_ref[...] = jnp.zeros_like(gg_ref)
            gb_ref[...] = jnp.zeros_like(gb_ref)

        loss_ref[...] += jnp.broadcast_to(part_l, loss_ref.shape)
        gg_ref[...] += part_g
        gb_ref[...] += part_b
        dxh = dy * g2_ref[...]
        m1 = jnp.mean(dxh, axis=1, keepdims=True)
        m2 = jnp.mean(dxh * xh2, axis=1, keepdims=True)
        dz = rstd * (dxh - m1 - xh2 * m2)
        dzf_ref[...] = dz
        dzb_ref[...] = dz.astype(dzb_ref.dtype)
        dpg_ref[...] = (dz * pu * sg * (1.0 - sg)).astype(dpg_ref.dtype)
        dpu_ref[...] = (dz * sg).astype(dpu_ref.dtype)

    row = pl.BlockSpec((tm, d), lambda i: (i, 0))
    vec = pl.BlockSpec((1, d), lambda i: (0, 0))
    bf = jax.ShapeDtypeStruct((cfg.seq, d), BF16)
    vec_out = jax.ShapeDtypeStruct((1, d), F32)
    return pl.pallas_call(
        body, name="head", grid=(cfg.seq // tm,), in_specs=[row] * 5 + [vec] * 4,
        out_specs=[pl.BlockSpec((1, 128), lambda i: (0, 0)), row, row, row, row, vec, vec],
        out_shape=[jax.ShapeDtypeStruct((1, 128), F32), jax.ShapeDtypeStruct((cfg.seq, d), F32), bf, bf, bf, vec_out, vec_out],
        compiler_params=_params(("arbitrary",)),
    )(xhat1, ffn, hp, pu, tgt, g1, b1, g2, b2)


def _ln1_bwd(cfg, dh_mm, dz2, xhat1, rstd1, g1):
    tm, d = 128, cfg.d

    def body(dh_ref, dz_ref, xh_ref, rstd_ref, g_ref, dzf_ref, dzb_ref, gg_ref, gb_ref):
        i = pl.program_id(0)
        dh = cfg.alpha * dz_ref[...] + dh_ref[...]
        xh = xh_ref[...]

        @pl.when(i == 0)
        def _():
            gg_ref[...] = jnp.zeros_like(gg_ref)
            gb_ref[...] = jnp.zeros_like(gb_ref)

        gg_ref[...] += jnp.sum(dh * xh, axis=0, keepdims=True)
        gb_ref[...] += jnp.sum(dh, axis=0, keepdims=True)
        dxh = dh * g_ref[...]
        m1 = jnp.mean(dxh, axis=1, keepdims=True)
        m2 = jnp.mean(dxh * xh, axis=1, keepdims=True)
        dz = rstd_ref[...] * (dxh - m1 - xh * m2)
        dzf_ref[...] = dz
        dzb_ref[...] = dz.astype(dzb_ref.dtype)

    row = pl.BlockSpec((tm, d), lambda i: (i, 0))
    vec = pl.BlockSpec((1, d), lambda i: (0, 0))
    vec_out = jax.ShapeDtypeStruct((1, d), F32)
    return pl.pallas_call(
        body, name="ln1_bwd", grid=(cfg.seq // tm,),
        in_specs=[row, row, row, pl.BlockSpec((tm, 1), lambda i: (i, 0)), vec], out_specs=[row, row, vec, vec],
        out_shape=[jax.ShapeDtypeStruct((cfg.seq, d), F32), jax.ShapeDtypeStruct((cfg.seq, d), BF16), vec_out, vec_out],
        compiler_params=_params(("arbitrary",)),
    )(dh_mm, dz2, xhat1, rstd1, g1)


def _adamw(w, g, m, v, name):
    r, c = w.shape
    assert g.shape[0] >= r and g.shape[1] == c
    tr, tc = _row_tile(r, c * 4, 2 * 1024 * 1024), c
    bc1, bc2 = 1.0 - ADAM_B1 ** ADAM_STEP, 1.0 - ADAM_B2 ** ADAM_STEP

    def body(w_ref, g_ref, m_ref, v_ref, go_ref, d_ref, mo_ref, vo_ref):
        gg = g_ref[...]
        mn = ADAM_B1 * m_ref[...] + (1.0 - ADAM_B1) * gg
        vn = ADAM_B2 * v_ref[...] + (1.0 - ADAM_B2) * (gg * gg)
        go_ref[...] = gg
        d_ref[...] = -ADAM_LR * ((mn / bc1) / (jnp.sqrt(vn / bc2) + ADAM_EPS) + ADAM_WD * w_ref[...])
        mo_ref[...] = mn
        vo_ref[...] = vn

    blk = pl.BlockSpec((tr, tc), lambda i, j: (i, j))
    out = jax.ShapeDtypeStruct((r, c), F32)
    return pl.pallas_call(
        body, name=name, grid=(r // tr, c // tc), in_specs=[blk] * 4, out_specs=[blk] * 4, out_shape=[out] * 4,
        compiler_params=_params(("parallel", "parallel")),
    )(w, g, m, v)


def _place():
    x, y, c = lax.axis_index("x"), lax.axis_index("y"), lax.axis_index("c")
    others = [(1 - x, y), (x, 1 - y), (1 - x, 1 - y)]
    return x, y, c, others


def _all_gather(ws, ax, name, landed=None):
    r, cdim = ws.shape
    full_shape = (4 * r, cdim) if ax == 0 else (r, 4 * cdim)
    n_shard = r if ax == 0 else cdim
    n_half = r // 2
    tr = _row_tile(n_half, cdim * 2, STAGE_BYTES, 16)
    nq = n_half // tr
    slots = 3

    over_ici = landed is None

    def body(*refs):
        ws_ref = refs[0]
        full_ref, buf, load_sems, write_sems, send_sems, pass_sems, recv_sems, sib_sem = refs[1 if over_ici else 2:]
        x, y, c, others = _place()
        me = 2 * x + y
        sibling = (x, y, 1 - c)

        def region(chip, half, row0, rows):
            if ax == 0:
                return full_ref.at[pl.ds(chip * n_shard + half * n_half + row0, rows), :]
            return full_ref.at[pl.ds(half * n_half + row0, rows), pl.ds(chip * n_shard, n_shard)]

        pending = {}

        def free(slot):
            for cp, remote in pending.pop(slot, []):
                if remote:
                    cp.wait_send()
                else:
                    cp.wait()

        def load(src, slot):
            free(slot)
            cp = pltpu.make_async_copy(src, buf.at[slot], load_sems.at[slot])
            cp.start()
            cp.wait()

        step = 0
        for mine in (True, False):
            half = c if mine else 1 - c
            for q in range(nq):
                slot = step % slots
                load(ws_ref.at[pl.ds(half * n_half + q * tr, tr), :], slot)
                dst = region(me, half, q * tr, tr)
                cp = pltpu.make_async_copy(buf.at[slot], dst, write_sems.at[slot])
                cp.start()
                pending[slot] = [(cp, False)]
                if mine and over_ici:
                    for k, chip in enumerate(others):
                        cp = pltpu.make_async_remote_copy(
                            src_ref=buf.at[slot], dst_ref=dst, send_sem=send_sems.at[3 * slot + k], recv_sem=recv_sems.at[k],
                            device_id=(*chip, c), device_id_type=MESH)
                        cp.start()
                        pending[slot].append((cp, True))
                step += 1
        for k, chip in enumerate(others):
            j = 2 * chip[0] + chip[1]
            if over_ici:
                whole = region(j, c, 0, n_half)
                pltpu.make_async_remote_copy(src_ref=whole, dst_ref=whole, send_sem=send_sems.at[0], recv_sem=recv_sems.at[k],
                                             device_id=(x, y, c), device_id_type=MESH).wait_recv()
            for q in range(nq):
                slot = step % slots
                part = region(j, c, q * tr, tr)
                load(part, slot)
                cp = pltpu.make_async_remote_copy(src_ref=buf.at[slot], dst_ref=part, send_sem=pass_sems.at[slot],
                                                  recv_sem=sib_sem, device_id=sibling, device_id_type=MESH)
                cp.start()
                pending[slot] = [(cp, True)]
                step += 1
        for slot in list(pending):
            free(slot)
        if ax == 0:
            three = full_ref.at[pl.ds(0, 3 * n_half), :]
        else:
            three = full_ref.at[pl.ds(0, n_half), pl.ds(0, 3 * n_shard)]
        pltpu.make_async_remote_copy(src_ref=three, dst_ref=three, send_sem=send_sems.at[0], recv_sem=sib_sem,
                                     device_id=(x, y, c), device_id_type=MESH).wait_recv()

    return pl.pallas_call(
        body, name=name, in_specs=[ANY] if over_ici else [ANY, ANY], out_specs=ANY,
        out_shape=jax.ShapeDtypeStruct(full_shape, ws.dtype), input_output_aliases={} if over_ici else {1: 0},
        scratch_shapes=[pltpu.VMEM((slots, tr, cdim), ws.dtype), pltpu.SemaphoreType.DMA((slots,)),
                        pltpu.SemaphoreType.DMA((slots,)), pltpu.SemaphoreType.DMA((3 * slots,)),
                        pltpu.SemaphoreType.DMA((slots,)), pltpu.SemaphoreType.DMA((3,)), pltpu.SemaphoreType.DMA],
        compiler_params=pltpu.CompilerParams(vmem_limit_bytes=VMEM_LIMIT),
    )(*((ws,) if over_ici else (ws, landed)))


def _gather_job(ws, ax):
    r, cdim = ws.shape
    full_shape = (4 * r, cdim) if ax == 0 else (r, 4 * cdim)
    n_shard = r if ax == 0 else cdim
    n_half = r // 2

    def copies(ins, outs, sems):
        (ws_ref,), (full_ref,) = ins, outs
        x, y, c, others = _place()
        me = 2 * x + y
        if ax == 0:
            dst = full_ref.at[pl.ds(me * n_shard + c * n_half, n_half), :]
        else:
            dst = full_ref.at[pl.ds(c * n_half, n_half), pl.ds(me * n_shard, n_shard)]
        return [(pltpu.make_async_remote_copy(
            src_ref=ws_ref.at[pl.ds(c * n_half, n_half), :], dst_ref=dst, send_sem=sems.at[k], recv_sem=sems.at[3 + k],
            device_id=(*chip, c), device_id_type=MESH), "both") for k, chip in enumerate(others)]

    return _Job([ws], [jax.ShapeDtypeStruct(full_shape, ws.dtype)], 6, copies)


def _scatter_job(pair):
    _, n_half, ncol = pair.shape

    def copies(ins, outs, sems):
        (p_ref,), (got_ref,) = ins, outs
        x, y, c, others = _place()
        return [(pltpu.make_async_remote_copy(
            src_ref=p_ref.at[k], dst_ref=got_ref.at[k], send_sem=sems.at[k], recv_sem=sems.at[3 + k],
            device_id=(*chip, c), device_id_type=MESH), "both") for k, chip in enumerate(others)]

    return _Job([pair], [jax.ShapeDtypeStruct((3, n_half, ncol), pair.dtype)], 6, copies)


def _shard_dims(shape, ax):
    rf, cf = shape
    n_shard = (rf if ax == 0 else cf) // 4
    return n_shard, (n_shard if ax == 0 else rf) // 2, (cf if ax == 0 else n_shard)


def _mm_tn_pair(a, b, ax, name, jobs=()):
    (k, m), (k2, n) = a.shape, b.shape
    assert k == k2
    n_shard, n_half, _ = _shard_dims((m, n), ax)
    tm = _tile(n_half, 1024)
    if tm < 512:
        tm = n_half
    tm, tn, tk = _mm_tiles(m, n, k, 4, tm=tm)
    ni, nj, nk = m // tm, n // tn, k // tk
    ntiles = ni * nj
    n_acc = 1 if nk > 1 else 0
    assert n_half % tm == 0 and ntiles >= 2
    jobs = list(jobs)
    n_job_in = sum(len(jb.inputs) for jb in jobs)
    n_job_out = sum(len(jb.out_shapes) for jb in jobs)

    def body(*refs):
        a_ref, b_ref = refs[0], refs[1]
        job_in = refs[2:2 + n_job_in]
        own_ref, recv_ref = refs[2 + n_job_in], refs[3 + n_job_in]
        job_out = refs[4 + n_job_in:4 + n_job_in + n_job_out]
        scratch = refs[4 + n_job_in + n_job_out:]
        acc_ref = scratch[0] if n_acc else None
        stage, local_sems, send_sems, recv_sem = scratch[n_acc:n_acc + 4]
        job_sems = scratch[n_acc + 4:]
        i, j, kk = pl.program_id(0), pl.program_id(1), pl.program_id(2)
        x, y, c, _ = _place()
        sibling = (x, y, 1 - c)
        t = i * nj + j

        def job_copies():
            found, pi, po = [], 0, 0
            for jb, sems in zip(jobs, job_sems):
                found += jb.copies(job_in[pi:pi + len(jb.inputs)], job_out[po:po + len(jb.out_shapes)], sems)
                pi, po = pi + len(jb.inputs), po + len(jb.out_shapes)
            return found

        if jobs:
            @pl.when(jnp.logical_and(t == 0, kk == 0))
            def _():
                for cp, _ in job_copies():
                    cp.start()

        def is_mine(ii):
            row0 = ii * tm
            half = row0 // n_half if ax == 1 else (row0 % n_shard) // n_half
            return half == c

        def copies(tt):
            ii, jj, slot = tt // nj, tt % nj, tt % 2
            where = (pl.ds(ii * tm, tm), pl.ds(jj * tn, tn))
            local = pltpu.make_async_copy(stage.at[slot], own_ref.at[where], local_sems.at[slot])
            remote = pltpu.make_async_remote_copy(src_ref=stage.at[slot], dst_ref=recv_ref.at[where], send_sem=send_sems.at[slot],
                                                  recv_sem=recv_sem, device_id=sibling, device_id_type=MESH)
            return is_mine(ii), local, remote

        def wait_tile(tt):
            mine, local, remote = copies(tt)

            @pl.when(mine)
            def _():
                local.wait()

            @pl.when(jnp.logical_not(mine))
            def _():
                remote.wait_send()

        part = lax.dot_general(a_ref[...], b_ref[...], _TN, preferred_element_type=F32)
        if n_acc:
            @pl.when(kk == 0)
            def _():
                acc_ref[...] = part

            @pl.when(kk > 0)
            def _():
                acc_ref[...] += part

        @pl.when(kk == nk - 1)
        def _():
            @pl.when(t >= 2)
            def _():
                wait_tile(t - 2)

            stage[t % 2] = (acc_ref[...] if n_acc else part).astype(stage.dtype)
            mine, local, remote = copies(t)

            @pl.when(mine)
            def _():
                local.start()

            @pl.when(jnp.logical_not(mine))
            def _():
                remote.start()

            @pl.when(t == ntiles - 1)
            def _():
                wait_tile(t - 1)
                wait_tile(t)
                half_rows = recv_ref.at[pl.ds(0, m // 2), :]
                pltpu.make_async_remote_copy(src_ref=half_rows, dst_ref=half_rows, send_sem=send_sems.at[0], recv_sem=recv_sem,
                                             device_id=sibling, device_id_type=MESH).wait_recv()
                for cp, _ in job_copies():
                    cp.wait()

    out = jax.ShapeDtypeStruct((m, n), BF16)
    return pl.pallas_call(
        body, name=name, grid=(ni, nj, nk),
        in_specs=[pl.BlockSpec((tk, tm), lambda i, j, kk: (kk, i)), pl.BlockSpec((tk, tn), lambda i, j, kk: (kk, j))]
        + [ANY] * n_job_in,
        out_specs=[ANY, ANY] + [ANY] * n_job_out, out_shape=[out, out] + [s for jb in jobs for s in jb.out_shapes],
        scratch_shapes=[pltpu.VMEM((tm, tn), F32)] * n_acc + [pltpu.VMEM((2, tm, tn), BF16), pltpu.SemaphoreType.DMA((2,)),
                                                               pltpu.SemaphoreType.DMA((2,)), pltpu.SemaphoreType.DMA]
        + [pltpu.SemaphoreType.DMA((jb.n_sems,)) for jb in jobs],
        compiler_params=_params(("arbitrary", "arbitrary", "arbitrary")),
    )(a, b, *[t for jb in jobs for t in jb.inputs])


def _other_chip(k, x, y):
    return jnp.where(k == 1, x, 1 - x), jnp.where(k == 0, y, 1 - y)


def _pairsum(own, recv, ax, coords, name):
    n_shard, n_half, ncol = _shard_dims(own.shape, ax)
    tr = _row_tile(n_half, ncol * 2, STAGE_BYTES // 2, 16)
    nr = n_half // tr

    def src_map(k):
        def index(r, co):
            jx, jy = _other_chip(k, co[0], co[1])
            j = 2 * jx + jy
            if ax == 0:
                return ((j * n_shard + co[2] * n_half) // tr + r, 0)
            return (co[2] * nr + r, j)
        return index

    def body(co_ref, o0, r0, o1, r1, o2, r2, out_ref):
        for k, (own_ref, recv_ref) in enumerate(((o0, r0), (o1, r1), (o2, r2))):
            out_ref[k] = (own_ref[...].astype(F32) + recv_ref[...].astype(F32)).astype(out_ref.dtype)

    blks = [pl.BlockSpec((tr, ncol), src_map(k)) for k in range(3) for _ in range(2)]
    return pl.pallas_call(
        body, name=name,
        grid_spec=pltpu.PrefetchScalarGridSpec(
            num_scalar_prefetch=1, grid=(nr,), in_specs=blks, out_specs=pl.BlockSpec((3, tr, ncol), lambda r, co: (0, r, 0))),
        out_shape=jax.ShapeDtypeStruct((3, n_half, ncol), BF16),
        compiler_params=_params(("parallel",)),
    )(coords, own, recv, own, recv, own, recv)


def _sum_share(own, recv, got, ax, coords, name):
    n_shard, n_half, ncol = _shard_dims(own.shape, ax)
    tr = _row_tile(n_half, ncol * 4, STAGE_BYTES, 16)
    nr = n_half // tr

    def src_map(r, co):
        me = 2 * co[0] + co[1]
        if ax == 0:
            return ((me * n_shard + co[2] * n_half) // tr + r, 0)
        return (co[2] * nr + r, me)

    def body(co_ref, own_ref, recv_ref, got_ref, out_ref, sbuf, local_sems, send_sems, recv_sem):
        r = pl.program_id(0)
        x, y, c, _ = _place()

        def copies(slot):
            rows = out_ref.at[pl.ds(c * n_half + r * tr, tr), :]
            local = pltpu.make_async_copy(sbuf.at[slot], rows, local_sems.at[slot])
            remote = pltpu.make_async_remote_copy(src_ref=sbuf.at[slot], dst_ref=rows, send_sem=send_sems.at[slot],
                                                  recv_sem=recv_sem, device_id=(x, y, 1 - c), device_id_type=MESH)
            return local, remote

        def wait_slot(slot):
            local, remote = copies(slot)
            local.wait()
            remote.wait_send()

        @pl.when(r >= 2)
        def _():
            wait_slot(r % 2)

        total = own_ref[...].astype(F32) + recv_ref[...].astype(F32)
        for kk in range(3):
            total = total + got_ref[kk].astype(F32)
        sbuf[r % 2] = total
        local, remote = copies(r % 2)
        local.start()
        remote.start()

        @pl.when(r == nr - 1)
        def _():
            wait_slot(r % 2)
            if nr >= 2:
                wait_slot(1 - r % 2)
            half = out_ref.at[pl.ds(0, n_half), :]
            pltpu.make_async_remote_copy(src_ref=half, dst_ref=half, send_sem=send_sems.at[0], recv_sem=recv_sem,
                                         device_id=(x, y, c), device_id_type=MESH).wait_recv()

    blk = pl.BlockSpec((tr, ncol), src_map)
    return pl.pallas_call(
        body, name=name,
        grid_spec=pltpu.PrefetchScalarGridSpec(
            num_scalar_prefetch=1, grid=(nr,), in_specs=[blk, blk, pl.BlockSpec((3, tr, ncol), lambda r, co: (0, r, 0))],
            out_specs=ANY,
            scratch_shapes=[pltpu.VMEM((2, tr, ncol), F32), pltpu.SemaphoreType.DMA((2,)), pltpu.SemaphoreType.DMA((2,)),
                            pltpu.SemaphoreType.DMA]),
        out_shape=jax.ShapeDtypeStruct((2 * n_half, ncol), F32),
        compiler_params=_params(("arbitrary",)),
    )(coords, own, recv, got)


def _vec_all_reduce(v, name):
    rows, d = v.shape

    def body(v_ref, out_ref, gath_ref, send_sems, recv_sems):
        x, y, c, _ = _place()
        me = 4 * x + 2 * y + c
        gath_ref[me] = v_ref[...]
        flips = [(fx, fy, fc) for fx in (0, 1) for fy in (0, 1) for fc in (0, 1)][1:]
        sends = []
        for k, (fx, fy, fc) in enumerate(flips):
            peer = (1 - x if fx else x, 1 - y if fy else y, 1 - c if fc else c)
            sends.append(pltpu.make_async_remote_copy(
                src_ref=v_ref, dst_ref=gath_ref.at[me], send_sem=send_sems.at[k], recv_sem=recv_sems.at[k],
                device_id=peer, device_id_type=MESH))
        for cp in sends:
            cp.start()
        for cp in sends:
            cp.wait()
        acc = gath_ref[0]
        for dev in range(1, 8):
            acc = acc + gath_ref[dev]
        out_ref[...] = acc

    vm = pl.BlockSpec(memory_space=pltpu.VMEM)
    return pl.pallas_call(
        body, name=name, in_specs=[vm], out_specs=vm, out_shape=jax.ShapeDtypeStruct((rows, d), F32),
        scratch_shapes=[pltpu.VMEM((8, rows, d), F32), pltpu.SemaphoreType.DMA((7,)), pltpu.SemaphoreType.DMA((7,))],
    )(v)


_TRANSPOSED = ("w_ffn_gate", "w_ffn_up")
_SHARD_AXIS = {"w_in": 1, "w_attn_out": 1, "w_ret_out": 0, "w_o": 0, "w_ffn_gate": 0, "w_ffn_up": 0, "w_ffn_down": 0,
               "w_ple_gate": 0, "w_ple_up": 1}
_FFN_HIDDEN_ROWS = ("w_ffn_gate", "w_ffn_up", "w_ffn_down")
_VECTORS = ("ret_gn_g", "ln1_g", "ln1_b", "ln2_g", "ln2_b")
_WEIGHTS = ("w_in", "w_attn_out", "w_ret_out", "ret_gn_g", "w_o", "ln1_g", "ln1_b", "w_ffn_gate", "w_ffn_up", "w_ffn_down",
            "w_ple_gate", "w_ple_up", "ln2_g", "ln2_b")


def _local_grads(cfg, x, p, tgt, w, vec, coords=None):
    dist = coords is not None
    xb, pb = x.astype(BF16), p.astype(BF16)
    cos, sin = _rope_tables(cfg)
    lg = _log_gamma(cfg)
    wf = {"w_in": _all_gather(w["w_in"], _SHARD_AXIS["w_in"], "ag_w_in")} if dist else dict(w)
    grads, pairs, got, waiting = {}, {}, {}, []

    def gather_behind(names, matmul, n_main=1):
        if not dist:
            return matmul(())
        res = matmul([_gather_job(w[n], _SHARD_AXIS[n]) for n in names])
        for n, full in zip(names, res[n_main:]):
            wf[n] = _all_gather(w[n], _SHARD_AXIS[n], "ag_" + n, landed=full)
        return res[0] if n_main == 1 else res[:n_main]

    def scatter_jobs():
        names = list(waiting)
        del waiting[:]
        return [_scatter_job(pairs[n][2]) for n in names], lambda landed: got.update(zip(names, landed))

    def grad(a, b, n, name, host=False):
        if not dist:
            grads[n] = _mm(a, b, "tn", BF16, name)
            return
        jobs, keep = scatter_jobs() if host else ((), None)
        own, recv, *landed = _mm_tn_pair(a, b, _SHARD_AXIS[n], name, jobs=jobs)
        if host:
            keep(landed)
        pairs[n] = (own, recv, _pairsum(own, recv, _SHARD_AXIS[n], coords, "rs_" + n + "_pairsum"))
        waiting.append(n)

    def scatter_behind(matmul):
        if not waiting:
            return matmul(())
        jobs, keep = scatter_jobs()
        out, *landed = matmul(jobs)
        keep(landed)
        return out

    proj = gather_behind(["w_attn_out", "w_ret_out", "w_o", "w_ffn_gate"],
                         lambda jobs: _mm(xb, wf["w_in"], "nn", BF16, "proj", jobs=jobs))
    q3 = _to_groups(cfg, proj[:, cfg.o_qa:cfg.o_qa + cfg.att_w])
    k3 = _to_groups(cfg, proj[:, cfg.o_ka:cfg.o_ka + cfg.att_w])
    v3 = _to_groups(cfg, proj[:, cfg.o_va:cfg.o_va + cfg.att_w])
    o3, lse3 = _attn_fwd(cfg, q3, k3, v3)
    o_att, lse = _attn_combine(cfg, jnp.stack(_from_groups(o3)), jnp.stack(_from_groups(lse3)))
    y_att = _mm(o_att, wf["w_attn_out"], "nn", BF16, "y_att")
    rn, rstd_r, states = _ret_fwd(cfg, proj, cos, sin, lg)
    rg = _ret_gate(cfg, proj, rn, vec["ret_gn_g"])
    def mix_tiles(yr, ga, g2, ya):
        yr = yr.astype(BF16)
        return yr, _sigmoid(ga.astype(F32)) * ya.astype(F32) + _sigmoid(g2.astype(F32)) * yr.astype(F32)

    y_ret, mixed = _mm(rg, wf["w_ret_out"], "nn", None, "y_ret",
                       epi=_Epilogue([(proj, cfg.o_ga), (proj, cfg.o_g2), (y_att, 0)], [BF16, BF16], mix_tiles))
    mo = _mm(mixed, wf["w_o"], "nn", F32, "mixed_out")
    xhat1, h1, rstd1 = _ln1(cfg, x, mo, vec["ln1_g"], vec["ln1_b"])
    u = gather_behind(["w_ffn_up"], lambda jobs: _mm(h1, wf["w_ffn_gate"], "nt", BF16, "ffn_u", jobs=jobs))

    def act_tiles(t, u):
        t, u = t.astype(BF16), u.astype(F32)
        return t, u * _sigmoid(u) * t.astype(F32)

    t, a = gather_behind(["w_ffn_down"], lambda jobs: _mm(h1, wf["w_ffn_up"], "nt", None, "ffn_t", jobs=jobs,
                                                          epi=_Epilogue([(u, 0)], [BF16, BF16], act_tiles)), n_main=2)
    ffn = gather_behind(["w_ple_gate", "w_ple_up"], lambda jobs: _mm(a, wf["w_ffn_down"], "nn", F32, "ffn_down", jobs=jobs))
    hp = _mm(h1, wf["w_ple_gate"], "nn", BF16, "ple_gate")
    pu = _mm(pb, wf["w_ple_up"], "nn", BF16, "ple_up")
    loss, dz2, dz2b, d_pg, d_pu, g_ln2g, g_ln2b = _head(cfg, xhat1, ffn, hp, pu, tgt, vec["ln1_g"], vec["ln1_b"],
                                                        vec["ln2_g"], vec["ln2_b"])

    def act_bwd_tiles(da, u, t):
        da, u = da.astype(BF16).astype(F32), u.astype(F32)
        sg = _sigmoid(u)
        return da * t.astype(F32) * (sg * (1.0 + u * (1.0 - sg))), da * u * sg

    d_u, d_t = _mm(dz2b, wf["w_ffn_down"], "nt", None, "d_a", epi=_Epilogue([(u, 0), (t, 0)], [BF16, BF16], act_bwd_tiles))
    grad(a, dz2b, "w_ffn_down", "g_ffn_down")
    dh = scatter_behind(lambda jobs: _mm(d_u, wf["w_ffn_gate"], "nn", F32, "dh_u", jobs=jobs))
    grad(d_u, h1, "w_ffn_gate", "g_ffn_gate")
    dh = scatter_behind(lambda jobs: _mm(d_t, wf["w_ffn_up"], "nn", F32, "dh_t", add=dh, jobs=jobs))
    grad(d_t, h1, "w_ffn_up", "g_ffn_up")
    dh = _mm(d_pg, wf["w_ple_gate"], "nt", F32, "dh_pg", add=dh)
    grad(h1, d_pg, "w_ple_gate", "g_ple_gate")
    grad(pb, d_pu, "w_ple_up", "g_ple_up")
    dz1, dz1b, g_ln1g, g_ln1b = _ln1_bwd(cfg, dh, dz2, xhat1, rstd1, vec["ln1_g"])
    def mix_bwd_tiles(dm, ga, g2, ya, yr):
        dm = dm.astype(BF16).astype(F32)
        sa, s2 = _sigmoid(ga.astype(F32)), _sigmoid(g2.astype(F32))
        return dm * sa, dm * s2, dm * ya.astype(F32) * sa * (1.0 - sa), dm * yr.astype(F32) * s2 * (1.0 - s2)

    d_ya, d_yr, d_ga, d_g2 = _mm(dz1b, wf["w_o"], "nt", None, "d_mixed",
                                 epi=_Epilogue([(proj, cfg.o_ga), (proj, cfg.o_g2), (y_att, 0), (y_ret, 0)], [BF16] * 4,
                                               mix_bwd_tiles))
    grad(mixed, dz1b, "w_o", "g_o")
    d_oatt = _mm(d_ya, wf["w_attn_out"], "nt", BF16, "d_oatt")
    d_rg = _mm(d_yr, wf["w_ret_out"], "nt", BF16, "d_rg")
    grad(o_att, d_ya, "w_attn_out", "g_attn_out")
    grad(rg, d_yr, "w_ret_out", "g_ret_out")
    d_gr, d_r, g_gn = _ret_gate_bwd(cfg, proj, rn, vec["ret_gn_g"], rstd_r, d_rg)
    d_qr, d_kr, d_vr = _ret_bwd(cfg, proj, d_r, states, cos, sin, lg)
    delta = _attn_delta(cfg, d_oatt, o_att)
    do3 = jnp.stack([_deinterleave(d_oatt, dil) for _, dil in ATT_GROUPS])
    l3 = jnp.stack([_deinterleave(lse, dil) for _, dil in ATT_GROUPS])
    dl3 = jnp.stack([_deinterleave(delta, dil) for _, dil in ATT_GROUPS])
    dq3, dk3, dv3 = _attn_bwd(cfg, q3, k3, v3, do3, l3, dl3)
    dproj = jnp.concatenate(_from_groups(dq3) + _from_groups(dk3) + _from_groups(dv3) + [d_qr, d_kr, d_vr, d_gr, d_ga, d_g2], axis=1)
    grad(xb, dproj, "w_in", "g_in", host=True)
    grad_x = scatter_behind(lambda jobs: _mm(dproj, wf["w_in"], "nt", F32, "grad_x", add=dz1, add_scale=cfg.alpha, jobs=jobs))
    vgrads = {"ret_gn_g": g_gn, "ln1_g": g_ln1g, "ln1_b": g_ln1b, "ln2_g": g_ln2g, "ln2_b": g_ln2b}
    if dist:
        grads = {n: _sum_share(pairs[n][0], pairs[n][1], got[n], _SHARD_AXIS[n], coords, "rs_" + n + "_share") for n in pairs}
    return loss, grad_x, grads, vgrads


def _step(cfg, x, p, tgt, w, m, v):
    mats = [n for n in _WEIGHTS if n in _SHARD_AXIS]
    ffn_pad = cfg.ffn_shard - cfg.ffn // 4

    def stored(n, t):
        return t[0].T if n in _TRANSPOSED else t[0]

    def given(n, t):
        return (t.T if n in _TRANSPOSED else t)[None]

    def shard_bf16(n):
        ws = stored(n, w[n]).astype(BF16)
        if n in _FFN_HIDDEN_ROWS and ffn_pad:
            ws = jnp.pad(ws, [(0, ffn_pad), (0, 0)])
        return ws

    vec = {n: w[n] for n in _VECTORS}
    coords = jnp.stack([lax.axis_index("x"), lax.axis_index("y"), lax.axis_index("c")]).astype(jnp.int32)
    loss, grad_x, red, vgrads = _local_grads(cfg, x[0], p[0, 0], tgt[0], {n: shard_bf16(n) for n in mats}, vec, coords)
    loss = lax.psum(loss[0, 0], ("x", "y", "c"))
    stacked = jnp.concatenate([vgrads[n] for n in _VECTORS] + [jnp.zeros((3, cfg.d), F32)], axis=0)
    vsum = _vec_all_reduce(stacked, "vec_all_reduce")
    for i, n in enumerate(_VECTORS):
        red[n] = vsum[i:i + 1]
    outs = [[], [], [], []]
    for n in _WEIGHTS:
        if n in _SHARD_AXIS:
            res = _adamw(stored(n, w[n]), red[n], stored(n, m[n]), stored(n, v[n]), "adamw_" + n)
            res = [given(n, t) for t in res]
        else:
            res = _adamw(w[n], red[n], m[n], v[n], "adamw_" + n)
        for kept, t in zip(outs, res):
            kept.append(t)
    return (loss, grad_x[None], *outs[0], *outs[1], *outs[2], *outs[3])


def kernel(x, p, w_in, w_attn_out, w_ret_out, ret_gn_g, w_o, ln1_g, ln1_b, w_ffn_gate, w_ffn_up, w_ffn_down, w_ple_gate, w_ple_up, ln2_g, ln2_b, loss_target, m_w_in, m_w_attn_out, m_w_ret_out, m_ret_gn_g, m_w_o, m_ln1_g, m_ln1_b, m_w_ffn_gate, m_w_ffn_up, m_w_ffn_down, m_w_ple_gate, m_w_ple_up, m_ln2_g, m_ln2_b, v_w_in, v_w_attn_out, v_w_ret_out, v_ret_gn_g, v_w_o, v_ln1_g, v_ln1_b, v_w_ffn_gate, v_w_ffn_up, v_w_ffn_down, v_w_ple_gate, v_w_ple_up, v_ln2_g, v_ln2_b):
    w = dict(zip(_WEIGHTS, (w_in, w_attn_out, w_ret_out, ret_gn_g, w_o, ln1_g, ln1_b, w_ffn_gate, w_ffn_up, w_ffn_down,
                            w_ple_gate, w_ple_up, ln2_g, ln2_b)))
    m = dict(zip(_WEIGHTS, (m_w_in, m_w_attn_out, m_w_ret_out, m_ret_gn_g, m_w_o, m_ln1_g, m_ln1_b, m_w_ffn_gate, m_w_ffn_up,
                            m_w_ffn_down, m_w_ple_gate, m_w_ple_up, m_ln2_g, m_ln2_b)))
    v = dict(zip(_WEIGHTS, (v_w_in, v_w_attn_out, v_w_ret_out, v_ret_gn_g, v_w_o, v_ln1_g, v_ln1_b, v_w_ffn_gate, v_w_ffn_up,
                            v_w_ffn_down, v_w_ple_gate, v_w_ple_up, v_ln2_g, v_ln2_b)))
    return _step(_FULL, x, p, loss_target, w, m, v)
```

```python
import functools
import math

import jax
import jax.numpy as jnp
import numpy as np
from jax import lax
from jax.experimental import pallas as pl
from jax.experimental.pallas import tpu as pltpu

F32 = jnp.float32
BF16 = jnp.bfloat16
MESH = pl.DeviceIdType.MESH
ANY = pl.BlockSpec(memory_space=pl.ANY)

ATT_BLOCK = 128
ATT_GROUPS = ((128, 1), (512, 4), (2048, 16))
LN_EPS = 1e-5
GN_EPS = 1e-6
NEG_INF = -1e30
ROPE_BASE = 10000.0
ADAM_LR, ADAM_B1, ADAM_B2, ADAM_EPS, ADAM_WD, ADAM_STEP = 0.001, 0.9, 0.999, 1e-08, 0.01, 10
VMEM_LIMIT = 56 * 1024 * 1024
STAGE_BYTES = 4 * 1024 * 1024
MM_VMEM_BYTES = 46 * 1024 * 1024 + 512 * 1024


class _Cfg:
    def __init__(self, seq, d, ple, ahd, ahg, rh, rqk, rv, ffn, cw):
        self.seq, self.d, self.ple, self.ahd, self.ahg = seq, d, ple, ahd, ahg
        self.rh, self.rqk, self.rv, self.ffn, self.cw = rh, rqk, rv, ffn, cw
        self.aw = ahg * ahd
        self.att_w = 3 * self.aw
        self.rqk_w = rh * rqk
        self.rv_w = rh * rv
        offs = np.cumsum([0] + [self.att_w] * 3 + [self.rqk_w] * 2 + [self.rv_w] * 2 + [d] * 2)
        (self.o_qa, self.o_ka, self.o_va, self.o_qr, self.o_kr, self.o_vr, self.o_gr, self.o_ga, self.o_g2,
         self.in_w) = [int(v) for v in offs]
        self.alpha = 2.0 ** 0.25
        self.ffn_shard = -(-(ffn // 4) // 128) * 128
        self.ffn_p = 4 * self.ffn_shard
        assert self.rv_w == d and d % cw == 0
        for o in (self.o_gr, self.o_ga, self.o_g2):
            assert o % cw == 0
        assert self.o_qr % rqk == 0 and self.o_kr % rqk == 0 and self.o_vr % rv == 0 and self.o_gr % rv == 0
        assert rqk // 2 % 128 == 0 and seq % (ATT_BLOCK * 16) == 0


_FULL = _Cfg(seq=4096, d=4096, ple=256, ahd=128, ahg=8, rh=8, rqk=256, rv=512, ffn=11008, cw=1024)


def _tile(n, target, q=128):
    t = min(n, target) // q * q
    while t >= q:
        if n % t == 0:
            return t
        t -= q
    return n


def _row_tile(n, row_bytes, target_bytes, q=8):
    best = None
    for t in range(q, n + 1, q):
        if n % t == 0 and (best is None or t * row_bytes <= target_bytes):
            best = t
            if t * row_bytes > target_bytes:
                break
    return n if best is None else best


def _mm_tiles(m, n, k, out_bytes_per_elem, tm=None):
    fixed_tm = tm is not None
    tm, tn = tm if fixed_tm else _tile(m, 1024), _tile(n, 1024)
    tk_most = k if k <= 4096 else _tile(k, 4096)
    tk = tk_most

    def need(tm_, tk_):
        acc = 4 if tk_ < k else 0
        return 4 * (tm_ + tn) * tk_ + tm_ * tn * (out_bytes_per_elem + acc + 4)

    while need(tm, tk) > MM_VMEM_BYTES:
        if tk > 2048:
            tk = _tile(k, tk - 128)
        elif not fixed_tm and tm > 512 and m % 512 == 0:
            tm, tk = 512, tk_most
        elif tk > 256:
            tk = _tile(k, tk - 128)
        else:
            break
    return tm, tn, tk


def _params(sem=None):
    return pltpu.CompilerParams(dimension_semantics=sem, vmem_limit_bytes=VMEM_LIMIT)


def _sigmoid(x):
    return 1.0 / (1.0 + jnp.exp(-x))


class _Job:
    def __init__(self, inputs, out_shapes, n_sems, copies):
        self.inputs, self.out_shapes, self.n_sems, self.copies = list(inputs), list(out_shapes), n_sems, copies


class _Epilogue:
    def __init__(self, extras, out_dtypes, fn):
        self.extras, self.out_dtypes, self.fn = list(extras), list(out_dtypes), fn


def _mm(a, b, mode, out_dtype, name, add=None, add_scale=1.0, jobs=(), epi=None):
    if mode == "nn":
        (m, k), (k2, n) = a.shape, b.shape
    elif mode == "nt":
        (m, k), (n, k2) = a.shape, b.shape
    else:
        (k, m), (k2, n) = a.shape, b.shape
    assert k == k2, (name, a.shape, b.shape)
    has_add = add is not None
    if epi is None:
        epi = _Epilogue([], [out_dtype], lambda r: (r,))
    tile_bytes = sum(2 * jnp.dtype(dt).itemsize for dt in epi.out_dtypes) + (8 if has_add else 0)
    tile_bytes += sum(2 * t.dtype.itemsize for t, _ in epi.extras)
    tm, tn, tk = _mm_tiles(m, math.gcd(n, *[off for _, off in epi.extras]), k, tile_bytes)
    nk = k // tk
    if mode == "tn":
        a_spec = pl.BlockSpec((tk, tm), lambda i, j, kk: (kk, i))
        dims = (((0,), (0,)), ((), ()))
    else:
        a_spec = pl.BlockSpec((tm, tk), lambda i, j, kk: (i, kk))
        dims = (((1,), (1,)), ((), ())) if mode == "nt" else (((1,), (0,)), ((), ()))
    if mode == "nt":
        b_spec = pl.BlockSpec((tn, tk), lambda i, j, kk: (j, kk))
    else:
        b_spec = pl.BlockSpec((tk, tn), lambda i, j, kk: (kk, j))
    o_spec = pl.BlockSpec((tm, tn), lambda i, j, kk: (i, j))
    ni, nj = m // tm, n // tn
    jobs = list(jobs)
    n_extra, n_out = len(epi.extras), len(epi.out_dtypes)
    n_main_in = (3 if has_add else 2) + n_extra
    n_job_in = sum(len(jb.inputs) for jb in jobs)
    n_job_out = sum(len(jb.out_shapes) for jb in jobs)
    n_acc = 1 if nk > 1 else 0

    def body(*refs):
        a_ref, b_ref = refs[0], refs[1]
        add_ref = refs[2] if has_add else None
        extra_refs = refs[n_main_in - n_extra:n_main_in]
        job_in = refs[n_main_in:n_main_in + n_job_in]
        o_refs = refs[n_main_in + n_job_in:n_main_in + n_job_in + n_out]
        job_out = refs[n_main_in + n_job_in + n_out:n_main_in + n_job_in + n_out + n_job_out]
        acc_ref = refs[n_main_in + n_job_in + n_out + n_job_out] if n_acc else None
        job_sems = refs[n_main_in + n_job_in + n_out + n_acc + n_job_out:]
        i, j, kk = pl.program_id(0), pl.program_id(1), pl.program_id(2)

        def finish(r):
            if has_add:
                r = r + add_scale * add_ref[...]
            for o_ref, tile in zip(o_refs, epi.fn(r, *[ref[...] for ref in extra_refs])):
                o_ref[...] = tile.astype(o_ref.dtype)

        def job_copies():
            found, pi, po = [], 0, 0
            for jb, sems in zip(jobs, job_sems):
                found += jb.copies(job_in[pi:pi + len(jb.inputs)], job_out[po:po + len(jb.out_shapes)], sems)
                pi, po = pi + len(jb.inputs), po + len(jb.out_shapes)
            return found

        if jobs:
            @pl.when(jnp.logical_and(jnp.logical_and(i == 0, j == 0), kk == 0))
            def _():
                for cp, _ in job_copies():
                    cp.start()

        part = lax.dot_general(a_ref[...], b_ref[...], dims, preferred_element_type=F32)
        if n_acc:
            @pl.when(kk == 0)
            def _():
                acc_ref[...] = part

            @pl.when(kk > 0)
            def _():
                acc_ref[...] += part

            @pl.when(kk == nk - 1)
            def _():
                finish(acc_ref[...])
        else:
            finish(part)

        if jobs:
            @pl.when(jnp.logical_and(jnp.logical_and(i == ni - 1, j == nj - 1), kk == nk - 1))
            def _():
                for cp, _ in job_copies():
                    cp.wait()

    job_inputs = [t for jb in jobs for t in jb.inputs]
    extra_specs = [pl.BlockSpec((tm, tn), functools.partial(lambda i, j, kk, shift: (i, j + shift), shift=off // tn))
                   for _, off in epi.extras]
    outs = pl.pallas_call(
        body, name=name, grid=(ni, nj, nk),
        in_specs=[a_spec, b_spec] + ([o_spec] if has_add else []) + extra_specs + [ANY] * n_job_in,
        out_specs=[o_spec] * n_out + [ANY] * n_job_out,
        out_shape=[jax.ShapeDtypeStruct((m, n), dt) for dt in epi.out_dtypes] + [s for jb in jobs for s in jb.out_shapes],
        scratch_shapes=[pltpu.VMEM((tm, tn), F32)] * n_acc + [pltpu.SemaphoreType.DMA((jb.n_sems,)) for jb in jobs],
        compiler_params=_params(("arbitrary",) * 3 if jobs else ("parallel", "parallel", "arbitrary")),
    )(*((a, b, add) if has_add else (a, b)), *[t for t, _ in epi.extras], *job_inputs)
    return outs if jobs or n_out > 1 else outs[0]


def _seg_blocks(cfg):
    return [cfg.seq // ATT_BLOCK // dil for _, dil in ATT_GROUPS]


def _group_select(g, vals):
    out = jnp.int32(vals[-1])
    for i in range(len(vals) - 2, -1, -1):
        out = jnp.where(g == i, jnp.int32(vals[i]), out)
    return out


def _col(tile, h):
    lane = lax.broadcasted_iota(jnp.int32, tile.shape, 1)
    return jnp.sum(jnp.where(lane == h, tile, 0.0), axis=1, keepdims=True)


def _set_col(tile, h, col):
    lane = lax.broadcasted_iota(jnp.int32, tile.shape, 1)
    return jnp.where(lane == h, col, tile)


_NT = (((1,), (1,)), ((), ()))
_TN = (((0,), (0,)), ((), ()))


def _dot(a, b):
    return jnp.dot(a, b, preferred_element_type=F32)


def _dot_nt(a, b):
    return lax.dot_general(a, b, _NT, preferred_element_type=F32)


def _dot_tn(a, b):
    return lax.dot_general(a, b, _TN, preferred_element_type=F32)


def _attn_fwd(cfg, q3, k3, v3):
    nb = cfg.seq // ATT_BLOCK
    segs = _seg_blocks(cfg)
    scale = cfg.ahd ** -0.5
    blk = ATT_BLOCK

    def body(q_ref, kp_ref, kc_ref, vp_ref, vc_ref, o_ref, lse_ref):
        g, b = pl.program_id(0), pl.program_id(1)
        has_prev = (b & (_group_select(g, segs) - 1)) != 0
        qi = lax.broadcasted_iota(jnp.int32, (blk, blk), 0)
        kj = lax.broadcasted_iota(jnp.int32, (blk, blk), 1)
        valid_c = kj <= qi
        valid_p = jnp.logical_and(kj >= qi, has_prev)
        lse = jnp.zeros((blk, cfg.ahg), F32)
        for h in range(cfg.ahg):
            hs = slice(h * cfg.ahd, (h + 1) * cfg.ahd)
            q = q_ref[:, hs]
            s_c = jnp.where(valid_c, _dot_nt(q, kc_ref[:, hs]) * scale, NEG_INF)
            s_p = jnp.where(valid_p, _dot_nt(q, kp_ref[:, hs]) * scale, NEG_INF)
            m = jnp.maximum(jnp.max(s_c, axis=1, keepdims=True), jnp.max(s_p, axis=1, keepdims=True))
            p_c, p_p = jnp.exp(s_c - m), jnp.exp(s_p - m)
            den = jnp.sum(p_c, axis=1, keepdims=True) + jnp.sum(p_p, axis=1, keepdims=True)
            o = _dot(p_c.astype(BF16), vc_ref[:, hs]) + _dot(p_p.astype(BF16), vp_ref[:, hs])
            o_ref[:, hs] = (o / den).astype(o_ref.dtype)
            lse = _set_col(lse, h, m + jnp.log(den))
        lse_ref[...] = lse

    cur = pl.BlockSpec((None, blk, cfg.aw), lambda g, b: (g, b, 0))
    prev = pl.BlockSpec((None, blk, cfg.aw), lambda g, b: (g, jnp.maximum(b - 1, 0), 0))
    return pl.pallas_call(
        body, name="attn_fwd", grid=(3, nb), in_specs=[cur, prev, cur, prev, cur],
        out_specs=[cur, pl.BlockSpec((None, blk, cfg.ahg), lambda g, b: (g, b, 0))],
        out_shape=[jax.ShapeDtypeStruct((3, cfg.seq, cfg.aw), BF16), jax.ShapeDtypeStruct((3, cfg.seq, cfg.ahg), F32)],
        compiler_params=_params(("parallel", "parallel")),
    )(q3, k3, k3, v3, v3)


def _attn_combine(cfg, o3, lse3):
    tm = 256

    def body(o_ref, lse_ref, oa_ref, l_ref):
        l0, l1, l2 = lse_ref[0], lse_ref[1], lse_ref[2]
        m = jnp.maximum(jnp.maximum(l0, l1), l2)
        big = m + jnp.log(jnp.exp(l0 - m) + jnp.exp(l1 - m) + jnp.exp(l2 - m))
        ws = [jnp.exp(l0 - big), jnp.exp(l1 - big), jnp.exp(l2 - big)]
        for h in range(cfg.ahg):
            hs = slice(h * cfg.ahd, (h + 1) * cfg.ahd)
            acc = _col(ws[0], h) * o_ref[0, :, hs].astype(F32)
            acc += _col(ws[1], h) * o_ref[1, :, hs].astype(F32)
            acc += _col(ws[2], h) * o_ref[2, :, hs].astype(F32)
            oa_ref[:, hs] = acc.astype(oa_ref.dtype)
        l_ref[...] = big

    return pl.pallas_call(
        body, name="attn_combine", grid=(cfg.seq // tm,),
        in_specs=[pl.BlockSpec((3, tm, cfg.aw), lambda i: (0, i, 0)), pl.BlockSpec((3, tm, cfg.ahg), lambda i: (0, i, 0))],
        out_specs=[pl.BlockSpec((tm, cfg.aw), lambda i: (i, 0)), pl.BlockSpec((tm, cfg.ahg), lambda i: (i, 0))],
        out_shape=[jax.ShapeDtypeStruct((cfg.seq, cfg.aw), BF16), jax.ShapeDtypeStruct((cfg.seq, cfg.ahg), F32)],
        compiler_params=_params(("parallel",)),
    )(o3, lse3)


def _attn_delta(cfg, do, o):
    tm = 256

    def body(do_ref, o_ref, d_ref):
        out = jnp.zeros((tm, cfg.ahg), F32)
        for h in range(cfg.ahg):
            hs = slice(h * cfg.ahd, (h + 1) * cfg.ahd)
            prod = do_ref[:, hs].astype(F32) * o_ref[:, hs].astype(F32)
            out = _set_col(out, h, jnp.sum(prod, axis=1, keepdims=True))
        d_ref[...] = out

    row = pl.BlockSpec((tm, cfg.aw), lambda i: (i, 0))
    return pl.pallas_call(
        body, name="attn_delta", grid=(cfg.seq // tm,), in_specs=[row, row],
        out_specs=pl.BlockSpec((tm, cfg.ahg), lambda i: (i, 0)),
        out_shape=jax.ShapeDtypeStruct((cfg.seq, cfg.ahg), F32), compiler_params=_params(("parallel",)),
    )(do, o)


def _attn_bwd(cfg, q3, k3, v3, do3, l3, d3):
    nb = cfg.seq // ATT_BLOCK
    segs = _seg_blocks(cfg)
    scale = cfg.ahd ** -0.5
    blk = ATT_BLOCK

    def body(q_ref, qn_ref, do_ref, don_ref, l_ref, ln_ref, d_ref, dn_ref, kp_ref, kc_ref, vp_ref, vc_ref,
             dq_ref, dk_ref, dv_ref):
        g, b = pl.program_id(0), pl.program_id(1)
        seg_mask = _group_select(g, segs) - 1
        has_prev = (b & seg_mask) != 0
        has_next = jnp.logical_and(b + 1 < nb, ((b + 1) & seg_mask) != 0)
        qi = lax.broadcasted_iota(jnp.int32, (blk, blk), 0)
        kj = lax.broadcasted_iota(jnp.int32, (blk, blk), 1)
        valid_c = kj <= qi
        valid_p = jnp.logical_and(kj >= qi, has_prev)
        valid_n = jnp.logical_and(kj >= qi, has_next)
        lse, lse_n, dlt, dlt_n = l_ref[...], ln_ref[...], d_ref[...], dn_ref[...]
        for h in range(cfg.ahg):
            hs = slice(h * cfg.ahd, (h + 1) * cfg.ahd)
            q, qn, do, don = q_ref[:, hs], qn_ref[:, hs], do_ref[:, hs], don_ref[:, hs]
            kc, kp, vc, vp = kc_ref[:, hs], kp_ref[:, hs], vc_ref[:, hs], vp_ref[:, hs]
            lh, lnh, dh, dnh = _col(lse, h), _col(lse_n, h), _col(dlt, h), _col(dlt_n, h)
            p_c = jnp.where(valid_c, jnp.exp(_dot_nt(q, kc) * scale - lh), 0.0)
            p_p = jnp.where(valid_p, jnp.exp(_dot_nt(q, kp) * scale - lh), 0.0)
            ds_c = (p_c * (_dot_nt(do, vc) - dh)).astype(BF16)
            ds_p = (p_p * (_dot_nt(do, vp) - dh)).astype(BF16)
            dq_ref[:, hs] = ((_dot(ds_c, kc) + _dot(ds_p, kp)) * scale).astype(dq_ref.dtype)
            p_n = jnp.where(valid_n, jnp.exp(_dot_nt(qn, kc) * scale - lnh), 0.0)
            ds_n = (p_n * (_dot_nt(don, vc) - dnh)).astype(BF16)
            dk = _dot_tn(ds_c, q) + _dot_tn(ds_n, qn)
            dv = _dot_tn(p_c.astype(BF16), do) + _dot_tn(p_n.astype(BF16), don)
            dk_ref[:, hs] = (dk * scale).astype(dk_ref.dtype)
            dv_ref[:, hs] = dv.astype(dv_ref.dtype)

    def spec(width, shift):
        if shift == 0:
            return pl.BlockSpec((None, blk, width), lambda g, b: (g, b, 0))
        if shift > 0:
            return pl.BlockSpec((None, blk, width), lambda g, b: (g, jnp.minimum(b + 1, nb - 1), 0))
        return pl.BlockSpec((None, blk, width), lambda g, b: (g, jnp.maximum(b - 1, 0), 0))

    w, hw = cfg.aw, cfg.ahg
    out = jax.ShapeDtypeStruct((3, cfg.seq, w), BF16)
    return pl.pallas_call(
        body, name="attn_bwd", grid=(3, nb),
        in_specs=[spec(w, 0), spec(w, 1), spec(w, 0), spec(w, 1), spec(hw, 0), spec(hw, 1), spec(hw, 0), spec(hw, 1),
                  spec(w, -1), spec(w, 0), spec(w, -1), spec(w, 0)],
        out_specs=[spec(w, 0)] * 3, out_shape=[out, out, out], compiler_params=_params(("parallel", "parallel")),
    )(q3, q3, do3, do3, l3, l3, d3, d3, k3, k3, v3, v3)


def _deinterleave(t, dil):
    s, w = t.shape
    return t if dil == 1 else t.reshape(s // dil, dil, w).transpose(1, 0, 2).reshape(s, w)


def _interleave(t, dil):
    s, w = t.shape
    return t if dil == 1 else t.reshape(dil, s // dil, w).transpose(1, 0, 2).reshape(s, w)


def _to_groups(cfg, t):
    return jnp.stack([_deinterleave(t[:, g * cfg.aw:(g + 1) * cfg.aw], dil) for g, (_, dil) in enumerate(ATT_GROUPS)])


def _from_groups(t3):
    return [_interleave(t3[g], dil) for g, (_, dil) in enumerate(ATT_GROUPS)]


def _rope_tables(cfg):
    half = cfg.rqk // 2
    pos = jnp.arange(cfg.seq, dtype=F32)
    inv_freq = ROPE_BASE ** (-jnp.arange(half, dtype=F32) / half)
    ang = pos[:, None] * inv_freq[None, :]
    return jnp.cos(ang), jnp.sin(ang)


def _log_gamma(cfg):
    lg = jnp.log(1.0 - 2.0 ** (-5.0 - jnp.arange(cfg.rh, dtype=F32)))
    return jnp.broadcast_to(lg[:, None, None], (cfg.rh, 1, max(cfg.rqk, cfg.rv)))


def _rot(t, c, s, half):
    t1, t2 = t[:, :half], t[:, half:]
    return jnp.concatenate([t1 * c - t2 * s, t1 * s + t2 * c], axis=1)


def _unrot(d, c, s, half):
    d1, d2 = d[:, :half], d[:, half:]
    return jnp.concatenate([d1 * c + d2 * s, d2 * c - d1 * s], axis=1)


def _decays(lg_ref, cfg):
    blk = ATT_BLOCK
    lg_v, lg_k = lg_ref[:, :cfg.rv], lg_ref[:, :cfg.rqk]
    qi = lax.broadcasted_iota(jnp.int32, (blk, blk), 0)
    kj = lax.broadcasted_iota(jnp.int32, (blk, blk), 1)
    diff = (qi - kj).astype(F32)
    intra = jnp.where(diff >= 0, jnp.exp(jnp.maximum(diff, 0.0) * lg_ref[:, :blk]), 0.0)
    idx_v = lax.broadcasted_iota(jnp.int32, (blk, cfg.rv), 0).astype(F32)
    idx_k = lax.broadcasted_iota(jnp.int32, (blk, cfg.rqk), 0).astype(F32)
    cross = jnp.exp((idx_v + 1.0) * lg_v)
    state = jnp.exp((blk - 1.0 - idx_k) * lg_k)
    chunk = jnp.exp(float(blk) * lg_v)
    return intra, cross, state, chunk


def _ret_fwd(cfg, proj, cos, sin, lg):
    blk, nc, half = ATT_BLOCK, cfg.seq // ATT_BLOCK, cfg.rqk // 2
    kscale = cfg.rqk ** -0.5

    def body(q_ref, k_ref, v_ref, cos_ref, sin_ref, lg_ref, rn_ref, rstd_ref, st_ref, state):
        n = pl.program_id(1)

        @pl.when(n == 0)
        def _():
            state[...] = jnp.zeros_like(state)

        intra, cross_d, state_d, chunk_d = _decays(lg_ref, cfg)
        c, s = cos_ref[...], sin_ref[...]
        qb = _rot(q_ref[...].astype(F32), c, s, half).astype(BF16)
        kf = _rot(k_ref[...].astype(F32), c, s, half) * kscale
        vb = v_ref[...]
        prev = state[...]
        st_ref[...] = prev.astype(BF16)
        att = _dot_nt(qb, kf.astype(BF16)) * intra
        out = _dot(att.astype(BF16), vb) + _dot(qb, prev.astype(BF16)) * cross_d
        state[...] = chunk_d * prev + _dot_tn((kf * state_d).astype(BF16), vb)
        mu = jnp.mean(out, axis=1, keepdims=True)
        cen = out - mu
        rstd = lax.rsqrt(jnp.mean(cen * cen, axis=1, keepdims=True) + GN_EPS)
        rn_ref[...] = (cen * rstd).astype(rn_ref.dtype)
        rstd_ref[...] = rstd

    oq, ok, ov = cfg.o_qr // cfg.rqk, cfg.o_kr // cfg.rqk, cfg.o_vr // cfg.rv
    tab = pl.BlockSpec((blk, half), lambda h, n: (n, 0))
    return pl.pallas_call(
        body, name="ret_fwd", grid=(cfg.rh, nc),
        in_specs=[pl.BlockSpec((blk, cfg.rqk), lambda h, n: (n, oq + h)), pl.BlockSpec((blk, cfg.rqk), lambda h, n: (n, ok + h)),
                  pl.BlockSpec((blk, cfg.rv), lambda h, n: (n, ov + h)), tab, tab,
                  pl.BlockSpec((None, 1, lg.shape[2]), lambda h, n: (h, 0, 0))],
        out_specs=[pl.BlockSpec((blk, cfg.rv), lambda h, n: (n, h)), pl.BlockSpec((None, blk, 1), lambda h, n: (h, n, 0)),
                   pl.BlockSpec((None, None, cfg.rqk, cfg.rv), lambda h, n: (h, n, 0, 0))],
        out_shape=[jax.ShapeDtypeStruct((cfg.seq, cfg.rv_w), BF16), jax.ShapeDtypeStruct((cfg.rh, cfg.seq, 1), F32),
                   jax.ShapeDtypeStruct((cfg.rh, nc, cfg.rqk, cfg.rv), BF16)],
        scratch_shapes=[pltpu.VMEM((cfg.rqk, cfg.rv), F32)], compiler_params=_params(("parallel", "arbitrary")),
    )(proj, proj, proj, cos, sin, lg)


def _ret_bwd(cfg, proj, d_r, states, cos, sin, lg):
    blk, nc, half = ATT_BLOCK, cfg.seq // ATT_BLOCK, cfg.rqk // 2
    kscale = cfg.rqk ** -0.5

    def body(q_ref, k_ref, v_ref, do_ref, st_ref, cos_ref, sin_ref, lg_ref, dq_ref, dk_ref, dv_ref, dstate):
        n = pl.program_id(1)

        @pl.when(n == 0)
        def _():
            dstate[...] = jnp.zeros_like(dstate)

        intra, cross_d, state_d, chunk_d = _decays(lg_ref, cfg)
        c, s = cos_ref[...], sin_ref[...]
        qb = _rot(q_ref[...].astype(F32), c, s, half).astype(BF16)
        kf = _rot(k_ref[...].astype(F32), c, s, half) * kscale
        kb, ksb = kf.astype(BF16), (kf * state_d).astype(BF16)
        vb, prev = v_ref[...], st_ref[...]
        do = do_ref[...].astype(F32)
        dob, docb = do.astype(BF16), (do * cross_d).astype(BF16)
        dsb = dstate[...].astype(BF16)
        att = (_dot_nt(qb, kb) * intra).astype(BF16)
        datt = (_dot_nt(dob, vb) * intra).astype(BF16)
        d_q = _dot(datt, kb) + _dot_nt(docb, prev)
        d_k = _dot_tn(datt, qb) + _dot_nt(vb, dsb) * state_d
        d_v = _dot_tn(att, dob) + _dot(ksb, dsb)
        dstate[...] = chunk_d * dstate[...] + _dot_tn(qb, docb)
        dq_ref[...] = _unrot(d_q, c, s, half).astype(dq_ref.dtype)
        dk_ref[...] = _unrot(d_k * kscale, c, s, half).astype(dk_ref.dtype)
        dv_ref[...] = d_v.astype(dv_ref.dtype)

    oq, ok, ov = cfg.o_qr // cfg.rqk, cfg.o_kr // cfg.rqk, cfg.o_vr // cfg.rv
    last = nc - 1
    tab = pl.BlockSpec((blk, half), lambda h, n: (last - n, 0))
    qk_out = pl.BlockSpec((blk, cfg.rqk), lambda h, n: (last - n, h))
    v_out = pl.BlockSpec((blk, cfg.rv), lambda h, n: (last - n, h))
    return pl.pallas_call(
        body, name="ret_bwd", grid=(cfg.rh, nc),
        in_specs=[pl.BlockSpec((blk, cfg.rqk), lambda h, n: (last - n, oq + h)),
                  pl.BlockSpec((blk, cfg.rqk), lambda h, n: (last - n, ok + h)),
                  pl.BlockSpec((blk, cfg.rv), lambda h, n: (last - n, ov + h)), v_out,
                  pl.BlockSpec((None, None, cfg.rqk, cfg.rv), lambda h, n: (h, last - n, 0, 0)), tab, tab,
                  pl.BlockSpec((None, 1, lg.shape[2]), lambda h, n: (h, 0, 0))],
        out_specs=[qk_out, qk_out, v_out],
        out_shape=[jax.ShapeDtypeStruct((cfg.seq, cfg.rqk_w), BF16), jax.ShapeDtypeStruct((cfg.seq, cfg.rqk_w), BF16),
                   jax.ShapeDtypeStruct((cfg.seq, cfg.rv_w), BF16)],
        scratch_shapes=[pltpu.VMEM((cfg.rqk, cfg.rv), F32)], compiler_params=_params(("parallel", "arbitrary")),
    )(proj, proj, proj, d_r, states, cos, sin, lg)


def _ret_gate(cfg, proj, rn, gn_g):
    tm, cw = 256, cfg.cw
    og = cfg.o_gr // cw

    def body(g_ref, rn_ref, w_ref, o_ref):
        g = g_ref[...].astype(F32)
        o_ref[...] = (g * _sigmoid(g) * (rn_ref[...].astype(F32) * w_ref[...])).astype(o_ref.dtype)

    blk = pl.BlockSpec((tm, cw), lambda i, j: (i, j))
    return pl.pallas_call(
        body, name="ret_gate", grid=(cfg.seq // tm, cfg.rv_w // cw),
        in_specs=[pl.BlockSpec((tm, cw), lambda i, j: (i, og + j)), blk, pl.BlockSpec((1, cw), lambda i, j: (0, j))],
        out_specs=blk, out_shape=jax.ShapeDtypeStruct((cfg.seq, cfg.rv_w), BF16),
        compiler_params=_params(("parallel", "parallel")),
    )(proj, rn, gn_g)


def _ret_gate_bwd(cfg, proj, rn, gn_g, rstd, d_rg):
    tm, rv = 256, cfg.rv
    og = cfg.o_gr // rv

    def body(g_ref, rn_ref, w_ref, rstd_ref, drg_ref, dg_ref, dr_ref, gw_ref):
        i = pl.program_id(1)
        g, rn, w = g_ref[...].astype(F32), rn_ref[...].astype(F32), w_ref[...]
        drg = drg_ref[...].astype(F32)
        sg = _sigmoid(g)
        silu = g * sg
        dg_ref[...] = (drg * (rn * w) * (sg * (1.0 + g * (1.0 - sg)))).astype(dg_ref.dtype)
        drn = drg * silu * w
        part = jnp.sum(drg * silu * rn, axis=0, keepdims=True)

        @pl.when(i == 0)
        def _():
            gw_ref[...] = part

        @pl.when(i > 0)
        def _():
            gw_ref[...] += part

        m1 = jnp.mean(drn, axis=1, keepdims=True)
        m2 = jnp.mean(drn * rn, axis=1, keepdims=True)
        dr_ref[...] = (rstd_ref[...] * (drn - m1 - rn * m2)).astype(dr_ref.dtype)

    blk = pl.BlockSpec((tm, rv), lambda h, i: (i, h))
    vec = pl.BlockSpec((1, rv), lambda h, i: (0, h))
    out = jax.ShapeDtypeStruct((cfg.seq, cfg.rv_w), BF16)
    return pl.pallas_call(
        body, name="ret_gate_bwd", grid=(cfg.rh, cfg.seq // tm),
        in_specs=[pl.BlockSpec((tm, rv), lambda h, i: (i, og + h)), blk, vec, pl.BlockSpec((None, tm, 1), lambda h, i: (h, i, 0)), blk],
        out_specs=[blk, blk, vec], out_shape=[out, out, jax.ShapeDtypeStruct((1, cfg.rv_w), F32)],
        compiler_params=_params(("parallel", "arbitrary")),
    )(proj, rn, gn_g, rstd, d_rg)


def _ln1(cfg, x, mo, g, b):
    tm, d = 128, cfg.d

    def body(x_ref, mo_ref, g_ref, b_ref, xh_ref, h_ref, rstd_ref):
        z = cfg.alpha * x_ref[...] + mo_ref[...]
        cen = z - jnp.mean(z, axis=1, keepdims=True)
        rstd = lax.rsqrt(jnp.mean(cen * cen, axis=1, keepdims=True) + LN_EPS)
        xh = cen * rstd
        xh_ref[...] = xh
        h_ref[...] = (xh * g_ref[...] + b_ref[...]).astype(h_ref.dtype)
        rstd_ref[...] = rstd

    row = pl.BlockSpec((tm, d), lambda i: (i, 0))
    vec = pl.BlockSpec((1, d), lambda i: (0, 0))
    col = pl.BlockSpec((tm, 1), lambda i: (i, 0))
    return pl.pallas_call(
        body, name="ln1", grid=(cfg.seq // tm,), in_specs=[row, row, vec, vec], out_specs=[row, row, col],
        out_shape=[jax.ShapeDtypeStruct((cfg.seq, d), F32), jax.ShapeDtypeStruct((cfg.seq, d), BF16),
                   jax.ShapeDtypeStruct((cfg.seq, 1), F32)],
        compiler_params=_params(("parallel",)),
    )(x, mo, g, b)


def _head(cfg, xhat1, ffn, hp, pu, tgt, g1, b1, g2, b2):
    tm, d = 64, cfg.d

    def body(xh_ref, ffn_ref, hp_ref, pu_ref, tgt_ref, g1_ref, b1_ref, g2_ref, b2_ref,
             loss_ref, dzf_ref, dzb_ref, dpg_ref, dpu_ref, gg_ref, gb_ref):
        i = pl.program_id(0)
        h1 = xh_ref[...] * g1_ref[...] + b1_ref[...]
        sg, pu = _sigmoid(hp_ref[...].astype(F32)), pu_ref[...].astype(F32)
        z = cfg.alpha * h1 + ffn_ref[...] + sg * pu
        cen = z - jnp.mean(z, axis=1, keepdims=True)
        rstd = lax.rsqrt(jnp.mean(cen * cen, axis=1, keepdims=True) + LN_EPS)
        xh2 = cen * rstd
        err = xh2 * g2_ref[...] + b2_ref[...] - tgt_ref[...]
        dy = err * (1.0 / d)
        part_l = jnp.sum(jnp.sum(err * err, axis=1, keepdims=True), axis=0, keepdims=True) * (0.5 / d)
        part_g = jnp.sum(dy * xh2, axis=0, keepdims=True)
        part_b = jnp.sum(dy, axis=0, keepdims=True)

        @pl.when(i == 0)
        def _():
            loss_ref[...] = jnp.zeros_like(loss_ref)
            gg_ref[...] = jnp.zeros_like(gg_ref)
            gb_ref[...] = jnp.zeros_like(gb_ref)

        loss_ref[...] += jnp.broadcast_to(part_l, loss_ref.shape)
        gg_ref[...] += part_g
        gb_ref[...] += part_b
        dxh = dy * g2_ref[...]
        m1 = jnp.mean(dxh, axis=1, keepdims=True)
        m2 = jnp.mean(dxh * xh2, axis=1, keepdims=True)
        dz = rstd * (dxh - m1 - xh2 * m2)
        dzf_ref[...] = dz
        dzb_ref[...] = dz.astype(dzb_ref.dtype)
        dpg_ref[...] = (dz * pu * sg * (1.0 - sg)).astype(dpg_ref.dtype)
        dpu_ref[...] = (dz * sg).astype(dpu_ref.dtype)

    row = pl.BlockSpec((tm, d), lambda i: (i, 0))
    vec = pl.BlockSpec((1, d), lambda i: (0, 0))
    bf = jax.ShapeDtypeStruct((cfg.seq, d), BF16)
    vec_out = jax.ShapeDtypeStruct((1, d), F32)
    return pl.pallas_call(
        body, name="head", grid=(cfg.seq // tm,), in_specs=[row] * 5 + [vec] * 4,
        out_specs=[pl.BlockSpec((1, 128), lambda i: (0, 0)), row, row, row, row, vec, vec],
        out_shape=[jax.ShapeDtypeStruct((1, 128), F32), jax.ShapeDtypeStruct((cfg.seq, d), F32), bf, bf, bf, vec_out, vec_out],
        compiler_params=_params(("arbitrary",)),
    )(xhat1, ffn, hp, pu, tgt, g1, b1, g2, b2)


def _ln1_bwd(cfg, dh_mm, dz2, xhat1, rstd1, g1):
    tm, d = 128, cfg.d

    def body(dh_ref, dz_ref, xh_ref, rstd_ref, g_ref, dzf_ref, dzb_ref, gg_ref, gb_ref):
        i = pl.program_id(0)
        dh = cfg.alpha * dz_ref[...] + dh_ref[...]
        xh = xh_ref[...]

        @pl.when(i == 0)
        def _():
            gg_ref[...] = jnp.zeros_like(gg_ref)
            gb_ref[...] = jnp.zeros_like(gb_ref)

        gg_ref[...] += jnp.sum(dh * xh, axis=0, keepdims=True)
        gb_ref[...] += jnp.sum(dh, axis=0, keepdims=True)
        dxh = dh * g_ref[...]
        m1 = jnp.mean(dxh, axis=1, keepdims=True)
        m2 = jnp.mean(dxh * xh, axis=1, keepdims=True)
        dz = rstd_ref[...] * (dxh - m1 - xh * m2)
        dzf_ref[...] = dz
        dzb_ref[...] = dz.astype(dzb_ref.dtype)

    row = pl.BlockSpec((tm, d), lambda i: (i, 0))
    vec = pl.BlockSpec((1, d), lambda i: (0, 0))
    vec_out = jax.ShapeDtypeStruct((1, d), F32)
    return pl.pallas_call(
        body, name="ln1_bwd", grid=(cfg.seq // tm,),
        in_specs=[row, row, row, pl.BlockSpec((tm, 1), lambda i: (i, 0)), vec], out_specs=[row, row, vec, vec],
        out_shape=[jax.ShapeDtypeStruct((cfg.seq, d), F32), jax.ShapeDtypeStruct((cfg.seq, d), BF16), vec_out, vec_out],
        compiler_params=_params(("arbitrary",)),
    )(dh_mm, dz2, xhat1, rstd1, g1)


def _adamw(w, g, m, v, name):
    r, c = w.shape
    assert g.shape[0] >= r and g.shape[1] == c
    tr, tc = _row_tile(r, c * 4, 2 * 1024 * 1024), c
    bc1, bc2 = 1.0 - ADAM_B1 ** ADAM_STEP, 1.0 - ADAM_B2 ** ADAM_STEP

    def body(w_ref, g_ref, m_ref, v_ref, go_ref, d_ref, mo_ref, vo_ref):
        gg = g_ref[...]
        mn = ADAM_B1 * m_ref[...] + (1.0 - ADAM_B1) * gg
        vn = ADAM_B2 * v_ref[...] + (1.0 - ADAM_B2) * (gg * gg)
        go_ref[...] = gg
        d_ref[...] = -ADAM_LR * ((mn / bc1) / (jnp.sqrt(vn / bc2) + ADAM_EPS) + ADAM_WD * w_ref[...])
        mo_ref[...] = mn
        vo_ref[...] = vn

    blk = pl.BlockSpec((tr, tc), lambda i, j: (i, j))
    out = jax.ShapeDtypeStruct((r, c), F32)
    return pl.pallas_call(
        body, name=name, grid=(r // tr, c // tc), in_specs=[blk] * 4, out_specs=[blk] * 4, out_shape=[out] * 4,
        compiler_params=_params(("parallel", "parallel")),
    )(w, g, m, v)


def _place():
    x, y, c = lax.axis_index("x"), lax.axis_index("y"), lax.axis_index("c")
    others = [(1 - x, y), (x, 1 - y), (1 - x, 1 - y)]
    return x, y, c, others


def _all_gather(ws, ax, name, landed=None):
    r, cdim = ws.shape
    full_shape = (4 * r, cdim) if ax == 0 else (r, 4 * cdim)
    n_shard = r if ax == 0 else cdim
    n_half = r // 2
    tr = _row_tile(n_half, cdim * 2, STAGE_BYTES, 16)
    nq = n_half // tr
    slots = 3

    over_ici = landed is None

    def body(*refs):
        ws_ref = refs[0]
        (full_ref, buf, load_sems, write_sems, send_sems, pass_sems, recv_sems, sib_sem, relay_sems,
         diag_sems) = refs[1 if over_ici else 2:]
        x, y, c, others = _place()
        me = 2 * x + y
        sibling = (x, y, 1 - c)

        def region(chip, half, row0, rows):
            if ax == 0:
                return full_ref.at[pl.ds(chip * n_shard + half * n_half + row0, rows), :]
            return full_ref.at[pl.ds(half * n_half + row0, rows), pl.ds(chip * n_shard, n_shard)]

        pending = {}

        def free(slot):
            for cp, remote in pending.pop(slot, []):
                if remote:
                    cp.wait_send()
                else:
                    cp.wait()

        def load(src, slot):
            free(slot)
            cp = pltpu.make_async_copy(src, buf.at[slot], load_sems.at[slot])
            cp.start()
            cp.wait()

        step = 0
        for mine in (True, False):
            half = c if mine else 1 - c
            for q in range(nq):
                slot = step % slots
                load(ws_ref.at[pl.ds(half * n_half + q * tr, tr), :], slot)
                dst = region(me, half, q * tr, tr)
                cp = pltpu.make_async_copy(buf.at[slot], dst, write_sems.at[slot])
                cp.start()
                pending[slot] = [(cp, False)]
                if mine and over_ici:
                    for k, chip in enumerate(others[:2]):
                        cp = pltpu.make_async_remote_copy(
                            src_ref=buf.at[slot], dst_ref=dst, send_sem=send_sems.at[3 * slot + k], recv_sem=recv_sems.at[k],
                            device_id=(*chip, c), device_id_type=MESH)
                        cp.start()
                        pending[slot].append((cp, True))
                step += 1

        def landed_wait(part, sem):
            pltpu.make_async_remote_copy(src_ref=part, dst_ref=part, send_sem=send_sems.at[0], recv_sem=sem,
                                         device_id=(x, y, c), device_id_type=MESH).wait_recv()

        relays = []
        if over_ici:
            quarter = n_half // 2
            for k in (0, 1):
                landed_wait(region(2 * others[k][0] + others[k][1], c, 0, n_half), recv_sems.at[k])
            for k, (row0, to) in enumerate(((0, others[1]), (quarter, others[0]))):
                part = region(2 * others[k][0] + others[k][1], c, row0, quarter)
                cp = pltpu.make_async_remote_copy(src_ref=part, dst_ref=part, send_sem=relay_sems.at[k], recv_sem=diag_sems.at[k],
                                                  device_id=(*to, c), device_id_type=MESH)
                cp.start()
                relays.append(cp)
        for k, chip in enumerate(others):
            j = 2 * chip[0] + chip[1]
            if over_ici and k == 2:
                landed_wait(region(j, c, 0, quarter), diag_sems.at[0])
                landed_wait(region(j, c, quarter, quarter), diag_sems.at[1])
            for q in range(nq):
                slot = step % slots
                part = region(j, c, q * tr, tr)
                load(part, slot)
                cp = pltpu.make_async_remote_copy(src_ref=buf.at[slot], dst_ref=part, send_sem=pass_sems.at[slot],
                                                  recv_sem=sib_sem, device_id=sibling, device_id_type=MESH)
                cp.start()
                pending[slot] = [(cp, True)]
                step += 1
        for slot in list(pending):
            free(slot)
        for cp in relays:
            cp.wait_send()
        if ax == 0:
            three = full_ref.at[pl.ds(0, 3 * n_half), :]
        else:
            three = full_ref.at[pl.ds(0, n_half), pl.ds(0, 3 * n_shard)]
        pltpu.make_async_remote_copy(src_ref=three, dst_ref=three, send_sem=send_sems.at[0], recv_sem=sib_sem,
                                     device_id=(x, y, c), device_id_type=MESH).wait_recv()

    return pl.pallas_call(
        body, name=name, in_specs=[ANY] if over_ici else [ANY, ANY], out_specs=ANY,
        out_shape=jax.ShapeDtypeStruct(full_shape, ws.dtype), input_output_aliases={} if over_ici else {1: 0},
        scratch_shapes=[pltpu.VMEM((slots, tr, cdim), ws.dtype), pltpu.SemaphoreType.DMA((slots,)),
                        pltpu.SemaphoreType.DMA((slots,)), pltpu.SemaphoreType.DMA((3 * slots,)),
                        pltpu.SemaphoreType.DMA((slots,)), pltpu.SemaphoreType.DMA((3,)), pltpu.SemaphoreType.DMA,
                        pltpu.SemaphoreType.DMA((2,)), pltpu.SemaphoreType.DMA((2,))],
        compiler_params=pltpu.CompilerParams(vmem_limit_bytes=VMEM_LIMIT),
    )(*((ws,) if over_ici else (ws, landed)))


def _gather_job(ws, ax):
    r, cdim = ws.shape
    full_shape = (4 * r, cdim) if ax == 0 else (r, 4 * cdim)
    n_shard = r if ax == 0 else cdim
    n_half = r // 2

    def copies(ins, outs, sems):
        (ws_ref,), (full_ref,) = ins, outs
        x, y, c, others = _place()
        me = 2 * x + y
        if ax == 0:
            dst = full_ref.at[pl.ds(me * n_shard + c * n_half, n_half), :]
        else:
            dst = full_ref.at[pl.ds(c * n_half, n_half), pl.ds(me * n_shard, n_shard)]
        return [(pltpu.make_async_remote_copy(
            src_ref=ws_ref.at[pl.ds(c * n_half, n_half), :], dst_ref=dst, send_sem=sems.at[k], recv_sem=sems.at[3 + k],
            device_id=(*chip, c), device_id_type=MESH), "both") for k, chip in enumerate(others)]

    return _Job([ws], [jax.ShapeDtypeStruct(full_shape, ws.dtype)], 6, copies)


def _scatter_job(pair):
    _, n_half, ncol = pair.shape

    def copies(ins, outs, sems):
        (p_ref,), (got_ref,) = ins, outs
        x, y, c, others = _place()
        return [(pltpu.make_async_remote_copy(
            src_ref=p_ref.at[k], dst_ref=got_ref.at[k], send_sem=sems.at[k], recv_sem=sems.at[3 + k],
            device_id=(*chip, c), device_id_type=MESH), "both") for k, chip in enumerate(others)]

    return _Job([pair], [jax.ShapeDtypeStruct((3, n_half, ncol), pair.dtype)], 6, copies)


def _shard_dims(shape, ax):
    rf, cf = shape
    n_shard = (rf if ax == 0 else cf) // 4
    return n_shard, (n_shard if ax == 0 else rf) // 2, (cf if ax == 0 else n_shard)


def _mm_tn_pair(a, b, ax, name, jobs=()):
    (k, m), (k2, n) = a.shape, b.shape
    assert k == k2
    n_shard, n_half, _ = _shard_dims((m, n), ax)
    tm = _tile(n_half, 1024)
    if tm < 512:
        tm = n_half
    tm, tn, tk = _mm_tiles(m, n, k, 4, tm=tm)
    ni, nj, nk = m // tm, n // tn, k // tk
    ntiles = ni * nj
    n_acc = 1 if nk > 1 else 0
    assert n_half % tm == 0 and ntiles >= 2
    jobs = list(jobs)
    n_job_in = sum(len(jb.inputs) for jb in jobs)
    n_job_out = sum(len(jb.out_shapes) for jb in jobs)

    def body(*refs):
        a_ref, b_ref = refs[0], refs[1]
        job_in = refs[2:2 + n_job_in]
        own_ref, recv_ref = refs[2 + n_job_in], refs[3 + n_job_in]
        job_out = refs[4 + n_job_in:4 + n_job_in + n_job_out]
        scratch = refs[4 + n_job_in + n_job_out:]
        acc_ref = scratch[0] if n_acc else None
        stage, local_sems, send_sems, recv_sem = scratch[n_acc:n_acc + 4]
        job_sems = scratch[n_acc + 4:]
        i, j, kk = pl.program_id(0), pl.program_id(1), pl.program_id(2)
        x, y, c, _ = _place()
        sibling = (x, y, 1 - c)
        t = i * nj + j

        def job_copies():
            found, pi, po = [], 0, 0
            for jb, sems in zip(jobs, job_sems):
                found += jb.copies(job_in[pi:pi + len(jb.inputs)], job_out[po:po + len(jb.out_shapes)], sems)
                pi, po = pi + len(jb.inputs), po + len(jb.out_shapes)
            return found

        if jobs:
            @pl.when(jnp.logical_and(t == 0, kk == 0))
            def _():
                for cp, _ in job_copies():
                    cp.start()

        def is_mine(ii):
            row0 = ii * tm
            half = row0 // n_half if ax == 1 else (row0 % n_shard) // n_half
            return half == c

        def copies(tt):
            ii, jj, slot = tt // nj, tt % nj, tt % 2
            where = (pl.ds(ii * tm, tm), pl.ds(jj * tn, tn))
            local = pltpu.make_async_copy(stage.at[slot], own_ref.at[where], local_sems.at[slot])
            remote = pltpu.make_async_remote_copy(src_ref=stage.at[slot], dst_ref=recv_ref.at[where], send_sem=send_sems.at[slot],
                                                  recv_sem=recv_sem, device_id=sibling, device_id_type=MESH)
            return is_mine(ii), local, remote

        def wait_tile(tt):
            mine, local, remote = copies(tt)

            @pl.when(mine)
            def _():
                local.wait()

            @pl.when(jnp.logical_not(mine))
            def _():
                remote.wait_send()

        part = lax.dot_general(a_ref[...], b_ref[...], _TN, preferred_element_type=F32)
        if n_acc:
            @pl.when(kk == 0)
            def _():
                acc_ref[...] = part

            @pl.when(kk > 0)
            def _():
                acc_ref[...] += part

        @pl.when(kk == nk - 1)
        def _():
            @pl.when(t >= 2)
            def _():
                wait_tile(t - 2)

            stage[t % 2] = (acc_ref[...] if n_acc else part).astype(stage.dtype)
            mine, local, remote = copies(t)

            @pl.when(mine)
            def _():
                local.start()

            @pl.when(jnp.logical_not(mine))
            def _():
                remote.start()

            @pl.when(t == ntiles - 1)
            def _():
                wait_tile(t - 1)
                wait_tile(t)
                half_rows = recv_ref.at[pl.ds(0, m // 2), :]
                pltpu.make_async_remote_copy(src_ref=half_rows, dst_ref=half_rows, send_sem=send_sems.at[0], recv_sem=recv_sem,
                                             device_id=sibling, device_id_type=MESH).wait_recv()
                for cp, _ in job_copies():
                    cp.wait()

    out = jax.ShapeDtypeStruct((m, n), BF16)
    return pl.pallas_call(
        body, name=name, grid=(ni, nj, nk),
        in_specs=[pl.BlockSpec((tk, tm), lambda i, j, kk: (kk, i)), pl.BlockSpec((tk, tn), lambda i, j, kk: (kk, j))]
        + [ANY] * n_job_in,
        out_specs=[ANY, ANY] + [ANY] * n_job_out, out_shape=[out, out] + [s for jb in jobs for s in jb.out_shapes],
        scratch_shapes=[pltpu.VMEM((tm, tn), F32)] * n_acc + [pltpu.VMEM((2, tm, tn), BF16), pltpu.SemaphoreType.DMA((2,)),
                                                               pltpu.SemaphoreType.DMA((2,)), pltpu.SemaphoreType.DMA]
        + [pltpu.SemaphoreType.DMA((jb.n_sems,)) for jb in jobs],
        compiler_params=_params(("arbitrary", "arbitrary", "arbitrary")),
    )(a, b, *[t for jb in jobs for t in jb.inputs])


def _other_chip(k, x, y):
    return jnp.where(k == 1, x, 1 - x), jnp.where(k == 0, y, 1 - y)


def _pairsum(own, recv, ax, coords, name):
    n_shard, n_half, ncol = _shard_dims(own.shape, ax)
    tr = _row_tile(n_half, ncol * 2, STAGE_BYTES // 2, 16)
    nr = n_half // tr

    def src_map(k):
        def index(r, co):
            jx, jy = _other_chip(k, co[0], co[1])
            j = 2 * jx + jy
            if ax == 0:
                return ((j * n_shard + co[2] * n_half) // tr + r, 0)
            return (co[2] * nr + r, j)
        return index

    def body(co_ref, o0, r0, o1, r1, o2, r2, out_ref):
        for k, (own_ref, recv_ref) in enumerate(((o0, r0), (o1, r1), (o2, r2))):
            out_ref[k] = (own_ref[...].astype(F32) + recv_ref[...].astype(F32)).astype(out_ref.dtype)

    blks = [pl.BlockSpec((tr, ncol), src_map(k)) for k in range(3) for _ in range(2)]
    return pl.pallas_call(
        body, name=name,
        grid_spec=pltpu.PrefetchScalarGridSpec(
            num_scalar_prefetch=1, grid=(nr,), in_specs=blks, out_specs=pl.BlockSpec((3, tr, ncol), lambda r, co: (0, r, 0))),
        out_shape=jax.ShapeDtypeStruct((3, n_half, ncol), BF16),
        compiler_params=_params(("parallel",)),
    )(coords, own, recv, own, recv, own, recv)


def _sum_share(own, recv, got, ax, coords, name):
    n_shard, n_half, ncol = _shard_dims(own.shape, ax)
    tr = _row_tile(n_half, ncol * 4, STAGE_BYTES, 16)
    nr = n_half // tr

    def src_map(r, co):
        me = 2 * co[0] + co[1]
        if ax == 0:
            return ((me * n_shard + co[2] * n_half) // tr + r, 0)
        return (co[2] * nr + r, me)

    def body(co_ref, own_ref, recv_ref, got_ref, out_ref, sbuf, local_sems, send_sems, recv_sem):
        r = pl.program_id(0)
        x, y, c, _ = _place()

        def copies(slot):
            rows = out_ref.at[pl.ds(c * n_half + r * tr, tr), :]
            local = pltpu.make_async_copy(sbuf.at[slot], rows, local_sems.at[slot])
            remote = pltpu.make_async_remote_copy(src_ref=sbuf.at[slot], dst_ref=rows, send_sem=send_sems.at[slot],
                                                  recv_sem=recv_sem, device_id=(x, y, 1 - c), device_id_type=MESH)
            return local, remote

        def wait_slot(slot):
            local, remote = copies(slot)
            local.wait()
            remote.wait_send()

        @pl.when(r >= 2)
        def _():
            wait_slot(r % 2)

        total = own_ref[...].astype(F32) + recv_ref[...].astype(F32)
        for kk in range(3):
            total = total + got_ref[kk].astype(F32)
        sbuf[r % 2] = total
        local, remote = copies(r % 2)
        local.start()
        remote.start()

        @pl.when(r == nr - 1)
        def _():
            wait_slot(r % 2)
            if nr >= 2:
                wait_slot(1 - r % 2)
            half = out_ref.at[pl.ds(0, n_half), :]
            pltpu.make_async_remote_copy(src_ref=half, dst_ref=half, send_sem=send_sems.at[0], recv_sem=recv_sem,
                                         device_id=(x, y, c), device_id_type=MESH).wait_recv()

    blk = pl.BlockSpec((tr, ncol), src_map)
    return pl.pallas_call(
        body, name=name,
        grid_spec=pltpu.PrefetchScalarGridSpec(
            num_scalar_prefetch=1, grid=(nr,), in_specs=[blk, blk, pl.BlockSpec((3, tr, ncol), lambda r, co: (0, r, 0))],
            out_specs=ANY,
            scratch_shapes=[pltpu.VMEM((2, tr, ncol), F32), pltpu.SemaphoreType.DMA((2,)), pltpu.SemaphoreType.DMA((2,)),
                            pltpu.SemaphoreType.DMA]),
        out_shape=jax.ShapeDtypeStruct((2 * n_half, ncol), F32),
        compiler_params=_params(("arbitrary",)),
    )(coords, own, recv, got)


def _vec_all_reduce(v, name):
    rows, d = v.shape

    def body(v_ref, out_ref, gath_ref, send_sems, recv_sems):
        x, y, c, _ = _place()
        me = 4 * x + 2 * y + c
        gath_ref[me] = v_ref[...]
        flips = [(fx, fy, fc) for fx in (0, 1) for fy in (0, 1) for fc in (0, 1)][1:]
        sends = []
        for k, (fx, fy, fc) in enumerate(flips):
            peer = (1 - x if fx else x, 1 - y if fy else y, 1 - c if fc else c)
            sends.append(pltpu.make_async_remote_copy(
                src_ref=v_ref, dst_ref=gath_ref.at[me], send_sem=send_sems.at[k], recv_sem=recv_sems.at[k],
                device_id=peer, device_id_type=MESH))
        for cp in sends:
            cp.start()
        for cp in sends:
            cp.wait()
        acc = gath_ref[0]
        for dev in range(1, 8):
            acc = acc + gath_ref[dev]
        out_ref[...] = acc

    vm = pl.BlockSpec(memory_space=pltpu.VMEM)
    return pl.pallas_call(
        body, name=name, in_specs=[vm], out_specs=vm, out_shape=jax.ShapeDtypeStruct((rows, d), F32),
        scratch_shapes=[pltpu.VMEM((8, rows, d), F32), pltpu.SemaphoreType.DMA((7,)), pltpu.SemaphoreType.DMA((7,))],
    )(v)


_TRANSPOSED = ("w_ffn_gate", "w_ffn_up")
_SHARD_AXIS = {"w_in": 1, "w_attn_out": 1, "w_ret_out": 0, "w_o": 0, "w_ffn_gate": 0, "w_ffn_up": 0, "w_ffn_down": 0,
               "w_ple_gate": 0, "w_ple_up": 1}
_FFN_HIDDEN_ROWS = ("w_ffn_gate", "w_ffn_up", "w_ffn_down")
_VECTORS = ("ret_gn_g", "ln1_g", "ln1_b", "ln2_g", "ln2_b")
_WEIGHTS = ("w_in", "w_attn_out", "w_ret_out", "ret_gn_g", "w_o", "ln1_g", "ln1_b", "w_ffn_gate", "w_ffn_up", "w_ffn_down",
            "w_ple_gate", "w_ple_up", "ln2_g", "ln2_b")


def _local_grads(cfg, x, p, tgt, w, vec, coords=None):
    dist = coords is not None
    xb, pb = x.astype(BF16), p.astype(BF16)
    cos, sin = _rope_tables(cfg)
    lg = _log_gamma(cfg)
    wf = {"w_in": _all_gather(w["w_in"], _SHARD_AXIS["w_in"], "ag_w_in")} if dist else dict(w)
    grads, pairs, got, waiting = {}, {}, {}, []

    def gather_behind(names, matmul, n_main=1):
        if not dist:
            return matmul(())
        res = matmul([_gather_job(w[n], _SHARD_AXIS[n]) for n in names])
        for n, full in zip(names, res[n_main:]):
            wf[n] = _all_gather(w[n], _SHARD_AXIS[n], "ag_" + n, landed=full)
        return res[0] if n_main == 1 else res[:n_main]

    def scatter_jobs():
        names = list(waiting)
        del waiting[:]
        return [_scatter_job(pairs[n][2]) for n in names], lambda landed: got.update(zip(names, landed))

    def grad(a, b, n, name, host=False):
        if not dist:
            grads[n] = _mm(a, b, "tn", BF16, name)
            return
        jobs, keep = scatter_jobs() if host else ((), None)
        own, recv, *landed = _mm_tn_pair(a, b, _SHARD_AXIS[n], name, jobs=jobs)
        if host:
            keep(landed)
        pairs[n] = (own, recv, _pairsum(own, recv, _SHARD_AXIS[n], coords, "rs_" + n + "_pairsum"))
        waiting.append(n)

    def scatter_behind(matmul, n_main=1):
        if not waiting:
            return matmul(())
        jobs, keep = scatter_jobs()
        res = matmul(jobs)
        keep(res[n_main:])
        return res[0] if n_main == 1 else res[:n_main]

    proj = gather_behind(["w_attn_out", "w_ret_out", "w_o", "w_ffn_gate"],
                         lambda jobs: _mm(xb, wf["w_in"], "nn", BF16, "proj", jobs=jobs))
    q3 = _to_groups(cfg, proj[:, cfg.o_qa:cfg.o_qa + cfg.att_w])
    k3 = _to_groups(cfg, proj[:, cfg.o_ka:cfg.o_ka + cfg.att_w])
    v3 = _to_groups(cfg, proj[:, cfg.o_va:cfg.o_va + cfg.att_w])
    o3, lse3 = _attn_fwd(cfg, q3, k3, v3)
    o_att, lse = _attn_combine(cfg, jnp.stack(_from_groups(o3)), jnp.stack(_from_groups(lse3)))
    y_att = _mm(o_att, wf["w_attn_out"], "nn", BF16, "y_att")
    rn, rstd_r, states = _ret_fwd(cfg, proj, cos, sin, lg)
    rg = _ret_gate(cfg, proj, rn, vec["ret_gn_g"])
    def mix_tiles(yr, ga, g2, ya):
        yr = yr.astype(BF16)
        return yr, _sigmoid(ga.astype(F32)) * ya.astype(F32) + _sigmoid(g2.astype(F32)) * yr.astype(F32)

    y_ret, mixed = _mm(rg, wf["w_ret_out"], "nn", None, "y_ret",
                       epi=_Epilogue([(proj, cfg.o_ga), (proj, cfg.o_g2), (y_att, 0)], [BF16, BF16], mix_tiles))
    mo = _mm(mixed, wf["w_o"], "nn", F32, "mixed_out")
    xhat1, h1, rstd1 = _ln1(cfg, x, mo, vec["ln1_g"], vec["ln1_b"])
    u = gather_behind(["w_ffn_up"], lambda jobs: _mm(h1, wf["w_ffn_gate"], "nt", BF16, "ffn_u", jobs=jobs))

    def act_tiles(t, u):
        t, u = t.astype(BF16), u.astype(F32)
        return t, u * _sigmoid(u) * t.astype(F32)

    t, a = gather_behind(["w_ffn_down"], lambda jobs: _mm(h1, wf["w_ffn_up"], "nt", None, "ffn_t", jobs=jobs,
                                                          epi=_Epilogue([(u, 0)], [BF16, BF16], act_tiles)), n_main=2)
    ffn = gather_behind(["w_ple_gate", "w_ple_up"], lambda jobs: _mm(a, wf["w_ffn_down"], "nn", F32, "ffn_down", jobs=jobs))
    hp = _mm(h1, wf["w_ple_gate"], "nn", BF16, "ple_gate")
    pu = _mm(pb, wf["w_ple_up"], "nn", BF16, "ple_up")
    loss, dz2, dz2b, d_pg, d_pu, g_ln2g, g_ln2b = _head(cfg, xhat1, ffn, hp, pu, tgt, vec["ln1_g"], vec["ln1_b"],
                                                        vec["ln2_g"], vec["ln2_b"])

    def act_bwd_tiles(da, u, t):
        da, u = da.astype(BF16).astype(F32), u.astype(F32)
        sg = _sigmoid(u)
        return da * t.astype(F32) * (sg * (1.0 + u * (1.0 - sg))), da * u * sg

    grad(a, dz2b, "w_ffn_down", "g_ffn_down")
    d_u, d_t = scatter_behind(lambda jobs: _mm(dz2b, wf["w_ffn_down"], "nt", None, "d_a", jobs=jobs,
                                               epi=_Epilogue([(u, 0), (t, 0)], [BF16, BF16], act_bwd_tiles)), n_main=2)
    grad(d_u, h1, "w_ffn_gate", "g_ffn_gate")
    dh = scatter_behind(lambda jobs: _mm(d_u, wf["w_ffn_gate"], "nn", F32, "dh_u", jobs=jobs))
    grad(d_t, h1, "w_ffn_up", "g_ffn_up")
    dh = scatter_behind(lambda jobs: _mm(d_t, wf["w_ffn_up"], "nn", F32, "dh_t", add=dh, jobs=jobs))
    dh = _mm(d_pg, wf["w_ple_gate"], "nt", F32, "dh_pg", add=dh)
    grad(h1, d_pg, "w_ple_gate", "g_ple_gate")
    grad(pb, d_pu, "w_ple_up", "g_ple_up")
    dz1, dz1b, g_ln1g, g_ln1b = _ln1_bwd(cfg, dh, dz2, xhat1, rstd1, vec["ln1_g"])
    def mix_bwd_tiles(dm, ga, g2, ya, yr):
        dm = dm.astype(BF16).astype(F32)
        sa, s2 = _sigmoid(ga.astype(F32)), _sigmoid(g2.astype(F32))
        return dm * sa, dm * s2, dm * ya.astype(F32) * sa * (1.0 - sa), dm * yr.astype(F32) * s2 * (1.0 - s2)

    d_ya, d_yr, d_ga, d_g2 = _mm(dz1b, wf["w_o"], "nt", None, "d_mixed",
                                 epi=_Epilogue([(proj, cfg.o_ga), (proj, cfg.o_g2), (y_att, 0), (y_ret, 0)], [BF16] * 4,
                                               mix_bwd_tiles))
    grad(mixed, dz1b, "w_o", "g_o")
    d_oatt = _mm(d_ya, wf["w_attn_out"], "nt", BF16, "d_oatt")
    d_rg = _mm(d_yr, wf["w_ret_out"], "nt", BF16, "d_rg")
    grad(o_att, d_ya, "w_attn_out", "g_attn_out")
    grad(rg, d_yr, "w_ret_out", "g_ret_out")
    d_gr, d_r, g_gn = _ret_gate_bwd(cfg, proj, rn, vec["ret_gn_g"], rstd_r, d_rg)
    d_qr, d_kr, d_vr = _ret_bwd(cfg, proj, d_r, states, cos, sin, lg)
    delta = _attn_delta(cfg, d_oatt, o_att)
    do3 = jnp.stack([_deinterleave(d_oatt, dil) for _, dil in ATT_GROUPS])
    l3 = jnp.stack([_deinterleave(lse, dil) for _, dil in ATT_GROUPS])
    dl3 = jnp.stack([_deinterleave(delta, dil) for _, dil in ATT_GROUPS])
    dq3, dk3, dv3 = _attn_bwd(cfg, q3, k3, v3, do3, l3, dl3)
    dproj = jnp.concatenate(_from_groups(dq3) + _from_groups(dk3) + _from_groups(dv3) + [d_qr, d_kr, d_vr, d_gr, d_ga, d_g2], axis=1)
    grad(xb, dproj, "w_in", "g_in", host=True)
    grad_x = scatter_behind(lambda jobs: _mm(dproj, wf["w_in"], "nt", F32, "grad_x", add=dz1, add_scale=cfg.alpha, jobs=jobs))
    vgrads = {"ret_gn_g": g_gn, "ln1_g": g_ln1g, "ln1_b": g_ln1b, "ln2_g": g_ln2g, "ln2_b": g_ln2b}
    if dist:
        grads = {n: _sum_share(pairs[n][0], pairs[n][1], got[n], _SHARD_AXIS[n], coords, "rs_" + n + "_share") for n in pairs}
    return loss, grad_x, grads, vgrads


def _step(cfg, x, p, tgt, w, m, v):
    mats = [n for n in _WEIGHTS if n in _SHARD_AXIS]
    ffn_pad = cfg.ffn_shard - cfg.ffn // 4

    def stored(n, t):
        return t[0].T if n in _TRANSPOSED else t[0]

    def given(n, t):
        return (t.T if n in _TRANSPOSED else t)[None]

    def shard_bf16(n):
        ws = stored(n, w[n]).astype(BF16)
        if n in _FFN_HIDDEN_ROWS and ffn_pad:
            ws = jnp.pad(ws, [(0, ffn_pad), (0, 0)])
        return ws

    vec = {n: w[n] for n in _VECTORS}
    coords = jnp.stack([lax.axis_index("x"), lax.axis_index("y"), lax.axis_index("c")]).astype(jnp.int32)
    loss, grad_x, red, vgrads = _local_grads(cfg, x[0], p[0, 0], tgt[0], {n: shard_bf16(n) for n in mats}, vec, coords)
    loss = lax.psum(loss[0, 0], ("x", "y", "c"))
    stacked = jnp.concatenate([vgrads[n] for n in _VECTORS] + [jnp.zeros((3, cfg.d), F32)], axis=0)
    vsum = _vec_all_reduce(stacked, "vec_all_reduce")
    for i, n in enumerate(_VECTORS):
        red[n] = vsum[i:i + 1]
    outs = [[], [], [], []]
    for n in _WEIGHTS:
        if n in _SHARD_AXIS:
            res = _adamw(stored(n, w[n]), red[n], stored(n, m[n]), stored(n, v[n]), "adamw_" + n)
            res = [given(n, t) for t in res]
        else:
            res = _adamw(w[n], red[n], m[n], v[n], "adamw_" + n)
        for kept, t in zip(outs, res):
            kept.append(t)
    return (loss, grad_x[None], *outs[0], *outs[1], *outs[2], *outs[3])


def kernel(x, p, w_in, w_attn_out, w_ret_out, ret_gn_g, w_o, ln1_g, ln1_b, w_ffn_gate, w_ffn_up, w_ffn_down, w_ple_gate, w_ple_up, ln2_g, ln2_b, loss_target, m_w_in, m_w_attn_out, m_w_ret_out, m_ret_gn_g, m_w_o, m_ln1_g, m_ln1_b, m_w_ffn_gate, m_w_ffn_up, m_w_ffn_down, m_w_ple_gate, m_w_ple_up, m_ln2_g, m_ln2_b, v_w_in, v_w_attn_out, v_w_ret_out, v_ret_gn_g, v_w_o, v_ln1_g, v_ln1_b, v_w_ffn_gate, v_w_ffn_up, v_w_ffn_down, v_w_ple_gate, v_w_ple_up, v_ln2_g, v_ln2_b):
    w = dict(zip(_WEIGHTS, (w_in, w_attn_out, w_ret_out, ret_gn_g, w_o, ln1_g, ln1_b, w_ffn_gate, w_ffn_up, w_ffn_down,
                            w_ple_gate, w_ple_up, ln2_g, ln2_b)))
    m = dict(zip(_WEIGHTS, (m_w_in, m_w_attn_out, m_w_ret_out, m_ret_gn_g, m_w_o, m_ln1_g, m_ln1_b, m_w_ffn_gate, m_w_ffn_up,
                            m_w_ffn_down, m_w_ple_gate, m_w_ple_up, m_ln2_g, m_ln2_b)))
    v = dict(zip(_WEIGHTS, (v_w_in, v_w_attn_out, v_w_ret_out, v_ret_gn_g, v_w_o, v_ln1_g, v_ln1_b, v_w_ffn_gate, v_w_ffn_up,
                            v_w_ffn_down, v_w_ple_gate, v_w_ple_up, v_ln2_g, v_ln2_b)))
    return _step(_FULL, x, p, loss_target, w, m, v)
```

```python
import functools
import math

import jax
import jax.numpy as jnp
import numpy as np
from jax import lax
from jax.experimental import pallas as pl
from jax.experimental.pallas import tpu as pltpu

F32 = jnp.float32
BF16 = jnp.bfloat16
MESH = pl.DeviceIdType.MESH
ANY = pl.BlockSpec(memory_space=pl.ANY)

ATT_BLOCK = 128
ATT_GROUPS = ((128, 1), (512, 4), (2048, 16))
LN_EPS = 1e-5
GN_EPS = 1e-6
NEG_INF = -1e30
ROPE_BASE = 10000.0
ADAM_LR, ADAM_B1, ADAM_B2, ADAM_EPS, ADAM_WD, ADAM_STEP = 0.001, 0.9, 0.999, 1e-08, 0.01, 10
VMEM_LIMIT = 56 * 1024 * 1024
STAGE_BYTES = 4 * 1024 * 1024
MM_VMEM_BYTES = 46 * 1024 * 1024 + 512 * 1024


class _Cfg:
    def __init__(self, seq, d, ple, ahd, ahg, rh, rqk, rv, ffn, cw):
        self.seq, self.d, self.ple, self.ahd, self.ahg = seq, d, ple, ahd, ahg
        self.rh, self.rqk, self.rv, self.ffn, self.cw = rh, rqk, rv, ffn, cw
        self.aw = ahg * ahd
        self.att_w = 3 * self.aw
        self.rqk_w = rh * rqk
        self.rv_w = rh * rv
        offs = np.cumsum([0] + [self.att_w] * 3 + [self.rqk_w] * 2 + [self.rv_w] * 2 + [d] * 2)
        (self.o_qa, self.o_ka, self.o_va, self.o_qr, self.o_kr, self.o_vr, self.o_gr, self.o_ga, self.o_g2,
         self.in_w) = [int(v) for v in offs]
        self.alpha = 2.0 ** 0.25
        self.ffn_shard = -(-(ffn // 4) // 128) * 128
        self.ffn_p = 4 * self.ffn_shard
        assert self.rv_w == d and d % cw == 0
        for o in (self.o_gr, self.o_ga, self.o_g2):
            assert o % cw == 0
        assert self.o_qr % rqk == 0 and self.o_kr % rqk == 0 and self.o_vr % rv == 0 and self.o_gr % rv == 0
        assert rqk // 2 % 128 == 0 and seq % (ATT_BLOCK * 16) == 0


_FULL = _Cfg(seq=4096, d=4096, ple=256, ahd=128, ahg=8, rh=8, rqk=256, rv=512, ffn=11008, cw=1024)


def _tile(n, target, q=128):
    t = min(n, target) // q * q
    while t >= q:
        if n % t == 0:
            return t
        t -= q
    return n


def _row_tile(n, row_bytes, target_bytes, q=8):
    best = None
    for t in range(q, n + 1, q):
        if n % t == 0 and (best is None or t * row_bytes <= target_bytes):
            best = t
            if t * row_bytes > target_bytes:
                break
    return n if best is None else best


def _mm_tiles(m, n, k, out_bytes_per_elem, tm=None):
    fixed_tm = tm is not None
    tm, tn = tm if fixed_tm else _tile(m, 1024), _tile(n, 1024)
    tk_most = k if k <= 4096 else _tile(k, 4096)
    tk = tk_most

    def need(tm_, tk_):
        acc = 4 if tk_ < k else 0
        return 4 * (tm_ + tn) * tk_ + tm_ * tn * (out_bytes_per_elem + acc + 4)

    while need(tm, tk) > MM_VMEM_BYTES:
        if tk > 2048:
            tk = _tile(k, tk - 128)
        elif not fixed_tm and tm > 512 and m % 512 == 0:
            tm, tk = 512, tk_most
        elif tk > 256:
            tk = _tile(k, tk - 128)
        else:
            break
    return tm, tn, tk


def _params(sem=None):
    return pltpu.CompilerParams(dimension_semantics=sem, vmem_limit_bytes=VMEM_LIMIT)


def _sigmoid(x):
    return 1.0 / (1.0 + jnp.exp(-x))


class _Job:
    def __init__(self, inputs, out_shapes, n_sems, copies):
        self.inputs, self.out_shapes, self.n_sems, self.copies = list(inputs), list(out_shapes), n_sems, copies


class _Epilogue:
    def __init__(self, extras, out_dtypes, fn):
        self.extras, self.out_dtypes, self.fn = list(extras), list(out_dtypes), fn


def _mm(a, b, mode, out_dtype, name, add=None, add_scale=1.0, jobs=(), epi=None):
    if mode == "nn":
        (m, k), (k2, n) = a.shape, b.shape
    elif mode == "nt":
        (m, k), (n, k2) = a.shape, b.shape
    else:
        (k, m), (k2, n) = a.shape, b.shape
    assert k == k2, (name, a.shape, b.shape)
    has_add = add is not None
    if epi is None:
        epi = _Epilogue([], [out_dtype], lambda r: (r,))
    tile_bytes = sum(2 * jnp.dtype(dt).itemsize for dt in epi.out_dtypes) + (8 if has_add else 0)
    tile_bytes += sum(2 * t.dtype.itemsize for t, _ in epi.extras)
    tm, tn, tk = _mm_tiles(m, math.gcd(n, *[off for _, off in epi.extras]), k, tile_bytes)
    nk = k // tk
    if mode == "tn":
        a_spec = pl.BlockSpec((tk, tm), lambda i, j, kk: (kk, i))
        dims = (((0,), (0,)), ((), ()))
    else:
        a_spec = pl.BlockSpec((tm, tk), lambda i, j, kk: (i, kk))
        dims = (((1,), (1,)), ((), ())) if mode == "nt" else (((1,), (0,)), ((), ()))
    if mode == "nt":
        b_spec = pl.BlockSpec((tn, tk), lambda i, j, kk: (j, kk))
    else:
        b_spec = pl.BlockSpec((tk, tn), lambda i, j, kk: (kk, j))
    o_spec = pl.BlockSpec((tm, tn), lambda i, j, kk: (i, j))
    ni, nj = m // tm, n // tn
    jobs = list(jobs)
    n_extra, n_out = len(epi.extras), len(epi.out_dtypes)
    n_main_in = (3 if has_add else 2) + n_extra
    n_job_in = sum(len(jb.inputs) for jb in jobs)
    n_job_out = sum(len(jb.out_shapes) for jb in jobs)
    n_acc = 1 if nk > 1 else 0

    def body(*refs):
        a_ref, b_ref = refs[0], refs[1]
        add_ref = refs[2] if has_add else None
        extra_refs = refs[n_main_in - n_extra:n_main_in]
        job_in = refs[n_main_in:n_main_in + n_job_in]
        o_refs = refs[n_main_in + n_job_in:n_main_in + n_job_in + n_out]
        job_out = refs[n_main_in + n_job_in + n_out:n_main_in + n_job_in + n_out + n_job_out]
        acc_ref = refs[n_main_in + n_job_in + n_out + n_job_out] if n_acc else None
        job_sems = refs[n_main_in + n_job_in + n_out + n_acc + n_job_out:]
        i, j, kk = pl.program_id(0), pl.program_id(1), pl.program_id(2)

        def finish(r):
            if has_add:
                r = r + add_scale * add_ref[...]
            for o_ref, tile in zip(o_refs, epi.fn(r, *[ref[...] for ref in extra_refs])):
                o_ref[...] = tile.astype(o_ref.dtype)

        def job_copies():
            found, pi, po = [], 0, 0
            for jb, sems in zip(jobs, job_sems):
                found += jb.copies(job_in[pi:pi + len(jb.inputs)], job_out[po:po + len(jb.out_shapes)], sems)
                pi, po = pi + len(jb.inputs), po + len(jb.out_shapes)
            return found

        if jobs:
            @pl.when(jnp.logical_and(jnp.logical_and(i == 0, j == 0), kk == 0))
            def _():
                for cp, _ in job_copies():
                    cp.start()

        part = lax.dot_general(a_ref[...], b_ref[...], dims, preferred_element_type=F32)
        if n_acc:
            @pl.when(kk == 0)
            def _():
                acc_ref[...] = part

            @pl.when(kk > 0)
            def _():
                acc_ref[...] += part

            @pl.when(kk == nk - 1)
            def _():
                finish(acc_ref[...])
        else:
            finish(part)

        if jobs:
            @pl.when(jnp.logical_and(jnp.logical_and(i == ni - 1, j == nj - 1), kk == nk - 1))
            def _():
                for cp, _ in job_copies():
                    cp.wait()

    job_inputs = [t for jb in jobs for t in jb.inputs]
    extra_specs = [pl.BlockSpec((tm, tn), functools.partial(lambda i, j, kk, shift: (i, j + shift), shift=off // tn))
                   for _, off in epi.extras]
    outs = pl.pallas_call(
        body, name=name, grid=(ni, nj, nk),
        in_specs=[a_spec, b_spec] + ([o_spec] if has_add else []) + extra_specs + [ANY] * n_job_in,
        out_specs=[o_spec] * n_out + [ANY] * n_job_out,
        out_shape=[jax.ShapeDtypeStruct((m, n), dt) for dt in epi.out_dtypes] + [s for jb in jobs for s in jb.out_shapes],
        scratch_shapes=[pltpu.VMEM((tm, tn), F32)] * n_acc + [pltpu.SemaphoreType.DMA((jb.n_sems,)) for jb in jobs],
        compiler_params=_params(("arbitrary",) * 3 if jobs else ("parallel", "parallel", "arbitrary")),
    )(*((a, b, add) if has_add else (a, b)), *[t for t, _ in epi.extras], *job_inputs)
    return outs if jobs or n_out > 1 else outs[0]


def _seg_blocks(cfg):
    return [cfg.seq // ATT_BLOCK // dil for _, dil in ATT_GROUPS]


def _col(tile, h):
    lane = lax.broadcasted_iota(jnp.int32, tile.shape, 1)
    return jnp.sum(jnp.where(lane == h, tile, 0.0), axis=1, keepdims=True)


def _set_col(tile, h, col):
    lane = lax.broadcasted_iota(jnp.int32, tile.shape, 1)
    return jnp.where(lane == h, col, tile)


_NT = (((1,), (1,)), ((), ()))
_TN = (((0,), (0,)), ((), ()))


def _dot(a, b):
    return jnp.dot(a, b, preferred_element_type=F32)


def _dot_nt(a, b):
    return lax.dot_general(a, b, _NT, preferred_element_type=F32)


def _dot_tn(a, b):
    return lax.dot_general(a, b, _TN, preferred_element_type=F32)


def _attn_fwd(cfg, g, q, k, v, cols=(0, 0, 0)):
    nb = cfg.seq // ATT_BLOCK
    seg_mask = _seg_blocks(cfg)[g] - 1
    scale = cfg.ahd ** -0.5
    blk = ATT_BLOCK

    def body(q_ref, kp_ref, kc_ref, vp_ref, vc_ref, o_ref, lse_ref):
        b = pl.program_id(0)
        has_prev = (b & seg_mask) != 0
        qi = lax.broadcasted_iota(jnp.int32, (blk, blk), 0)
        kj = lax.broadcasted_iota(jnp.int32, (blk, blk), 1)
        valid_c = kj <= qi
        valid_p = jnp.logical_and(kj >= qi, has_prev)
        lse = jnp.zeros((blk, cfg.ahg), F32)
        for h in range(cfg.ahg):
            hs = slice(h * cfg.ahd, (h + 1) * cfg.ahd)
            q = q_ref[:, hs]
            s_c = jnp.where(valid_c, _dot_nt(q, kc_ref[:, hs]) * scale, NEG_INF)
            s_p = jnp.where(valid_p, _dot_nt(q, kp_ref[:, hs]) * scale, NEG_INF)
            m = jnp.maximum(jnp.max(s_c, axis=1, keepdims=True), jnp.max(s_p, axis=1, keepdims=True))
            p_c, p_p = jnp.exp(s_c - m), jnp.exp(s_p - m)
            den = jnp.sum(p_c, axis=1, keepdims=True) + jnp.sum(p_p, axis=1, keepdims=True)
            o = _dot(p_c.astype(BF16), vc_ref[:, hs]) + _dot(p_p.astype(BF16), vp_ref[:, hs])
            o_ref[:, hs] = (o / den).astype(o_ref.dtype)
            lse = _set_col(lse, h, m + jnp.log(den))
        lse_ref[...] = lse

    def spec(col, shift):
        if shift:
            return pl.BlockSpec((blk, cfg.aw), lambda b: (jnp.maximum(b - 1, 0), col))
        return pl.BlockSpec((blk, cfg.aw), lambda b: (b, col))

    return pl.pallas_call(
        body, name="attn_fwd_%d" % g, grid=(nb,),
        in_specs=[spec(cols[0], 0), spec(cols[1], 1), spec(cols[1], 0), spec(cols[2], 1), spec(cols[2], 0)],
        out_specs=[spec(0, 0), pl.BlockSpec((blk, cfg.ahg), lambda b: (b, 0))],
        out_shape=[jax.ShapeDtypeStruct((cfg.seq, cfg.aw), BF16), jax.ShapeDtypeStruct((cfg.seq, cfg.ahg), F32)],
        compiler_params=_params(("parallel",)),
    )(q, k, k, v, v)


def _attn_combine(cfg, outs, lses):
    tm = 256

    def body(o0, o1, o2, l0, l1, l2, oa_ref, l_ref):
        l0, l1, l2 = l0[...], l1[...], l2[...]
        m = jnp.maximum(jnp.maximum(l0, l1), l2)
        big = m + jnp.log(jnp.exp(l0 - m) + jnp.exp(l1 - m) + jnp.exp(l2 - m))
        ws = [jnp.exp(l0 - big), jnp.exp(l1 - big), jnp.exp(l2 - big)]
        for h in range(cfg.ahg):
            hs = slice(h * cfg.ahd, (h + 1) * cfg.ahd)
            acc = _col(ws[0], h) * o0[:, hs].astype(F32)
            acc += _col(ws[1], h) * o1[:, hs].astype(F32)
            acc += _col(ws[2], h) * o2[:, hs].astype(F32)
            oa_ref[:, hs] = acc.astype(oa_ref.dtype)
        l_ref[...] = big

    wide = pl.BlockSpec((tm, cfg.aw), lambda i: (i, 0))
    narrow = pl.BlockSpec((tm, cfg.ahg), lambda i: (i, 0))
    return pl.pallas_call(
        body, name="attn_combine", grid=(cfg.seq // tm,), in_specs=[wide] * 3 + [narrow] * 3, out_specs=[wide, narrow],
        out_shape=[jax.ShapeDtypeStruct((cfg.seq, cfg.aw), BF16), jax.ShapeDtypeStruct((cfg.seq, cfg.ahg), F32)],
        compiler_params=_params(("parallel",)),
    )(*outs, *lses)


def _attn_delta(cfg, do, o):
    tm = 256

    def body(do_ref, o_ref, d_ref):
        out = jnp.zeros((tm, cfg.ahg), F32)
        for h in range(cfg.ahg):
            hs = slice(h * cfg.ahd, (h + 1) * cfg.ahd)
            prod = do_ref[:, hs].astype(F32) * o_ref[:, hs].astype(F32)
            out = _set_col(out, h, jnp.sum(prod, axis=1, keepdims=True))
        d_ref[...] = out

    row = pl.BlockSpec((tm, cfg.aw), lambda i: (i, 0))
    return pl.pallas_call(
        body, name="attn_delta", grid=(cfg.seq // tm,), in_specs=[row, row],
        out_specs=pl.BlockSpec((tm, cfg.ahg), lambda i: (i, 0)),
        out_shape=jax.ShapeDtypeStruct((cfg.seq, cfg.ahg), F32), compiler_params=_params(("parallel",)),
    )(do, o)


def _attn_bwd(cfg, g, q, k, v, do, l, d, cols=(0, 0, 0)):
    nb = cfg.seq // ATT_BLOCK
    seg_mask = _seg_blocks(cfg)[g] - 1
    scale = cfg.ahd ** -0.5
    blk = ATT_BLOCK

    def body(q_ref, qn_ref, do_ref, don_ref, l_ref, ln_ref, d_ref, dn_ref, kp_ref, kc_ref, vp_ref, vc_ref,
             dq_ref, dk_ref, dv_ref):
        b = pl.program_id(0)
        has_prev = (b & seg_mask) != 0
        has_next = jnp.logical_and(b + 1 < nb, ((b + 1) & seg_mask) != 0)
        qi = lax.broadcasted_iota(jnp.int32, (blk, blk), 0)
        kj = lax.broadcasted_iota(jnp.int32, (blk, blk), 1)
        valid_c = kj <= qi
        valid_p = jnp.logical_and(kj >= qi, has_prev)
        valid_n = jnp.logical_and(kj >= qi, has_next)
        lse, lse_n, dlt, dlt_n = l_ref[...], ln_ref[...], d_ref[...], dn_ref[...]
        for h in range(cfg.ahg):
            hs = slice(h * cfg.ahd, (h + 1) * cfg.ahd)
            q, qn, do, don = q_ref[:, hs], qn_ref[:, hs], do_ref[:, hs], don_ref[:, hs]
            kc, kp, vc, vp = kc_ref[:, hs], kp_ref[:, hs], vc_ref[:, hs], vp_ref[:, hs]
            lh, lnh, dh, dnh = _col(lse, h), _col(lse_n, h), _col(dlt, h), _col(dlt_n, h)
            p_c = jnp.where(valid_c, jnp.exp(_dot_nt(q, kc) * scale - lh), 0.0)
            p_p = jnp.where(valid_p, jnp.exp(_dot_nt(q, kp) * scale - lh), 0.0)
            ds_c = (p_c * (_dot_nt(do, vc) - dh)).astype(BF16)
            ds_p = (p_p * (_dot_nt(do, vp) - dh)).astype(BF16)
            dq_ref[:, hs] = ((_dot(ds_c, kc) + _dot(ds_p, kp)) * scale).astype(dq_ref.dtype)
            p_n = jnp.where(valid_n, jnp.exp(_dot_nt(qn, kc) * scale - lnh), 0.0)
            ds_n = (p_n * (_dot_nt(don, vc) - dnh)).astype(BF16)
            dk = _dot_tn(ds_c, q) + _dot_tn(ds_n, qn)
            dv = _dot_tn(p_c.astype(BF16), do) + _dot_tn(p_n.astype(BF16), don)
            dk_ref[:, hs] = (dk * scale).astype(dk_ref.dtype)
            dv_ref[:, hs] = dv.astype(dv_ref.dtype)

    def spec(width, shift, col=0):
        if shift == 0:
            return pl.BlockSpec((blk, width), lambda b: (b, col))
        if shift > 0:
            return pl.BlockSpec((blk, width), lambda b: (jnp.minimum(b + 1, nb - 1), col))
        return pl.BlockSpec((blk, width), lambda b: (jnp.maximum(b - 1, 0), col))

    w, hw = cfg.aw, cfg.ahg
    out = jax.ShapeDtypeStruct((cfg.seq, w), BF16)
    return pl.pallas_call(
        body, name="attn_bwd_%d" % g, grid=(nb,),
        in_specs=[spec(w, 0, cols[0]), spec(w, 1, cols[0]), spec(w, 0), spec(w, 1), spec(hw, 0), spec(hw, 1), spec(hw, 0),
                  spec(hw, 1), spec(w, -1, cols[1]), spec(w, 0, cols[1]), spec(w, -1, cols[2]), spec(w, 0, cols[2])],
        out_specs=[spec(w, 0)] * 3, out_shape=[out, out, out], compiler_params=_params(("parallel",)),
    )(q, q, do, do, l, l, d, d, k, k, v, v)


def _deinterleave(t, dil):
    s, w = t.shape
    return t if dil == 1 else t.reshape(s // dil, dil, w).transpose(1, 0, 2).reshape(s, w)


def _interleave(t, dil):
    s, w = t.shape
    return t if dil == 1 else t.reshape(dil, s // dil, w).transpose(1, 0, 2).reshape(s, w)


def _rope_tables(cfg):
    half = cfg.rqk // 2
    pos = jnp.arange(cfg.seq, dtype=F32)
    inv_freq = ROPE_BASE ** (-jnp.arange(half, dtype=F32) / half)
    ang = pos[:, None] * inv_freq[None, :]
    return jnp.cos(ang), jnp.sin(ang)


def _log_gamma(cfg):
    lg = jnp.log(1.0 - 2.0 ** (-5.0 - jnp.arange(cfg.rh, dtype=F32)))
    return jnp.broadcast_to(lg[:, None, None], (cfg.rh, 1, max(cfg.rqk, cfg.rv)))


def _rot(t, c, s, half):
    t1, t2 = t[:, :half], t[:, half:]
    return jnp.concatenate([t1 * c - t2 * s, t1 * s + t2 * c], axis=1)


def _unrot(d, c, s, half):
    d1, d2 = d[:, :half], d[:, half:]
    return jnp.concatenate([d1 * c + d2 * s, d2 * c - d1 * s], axis=1)


def _decays(lg_ref, cfg):
    blk = ATT_BLOCK
    lg_v, lg_k = lg_ref[:, :cfg.rv], lg_ref[:, :cfg.rqk]
    qi = lax.broadcasted_iota(jnp.int32, (blk, blk), 0)
    kj = lax.broadcasted_iota(jnp.int32, (blk, blk), 1)
    diff = (qi - kj).astype(F32)
    intra = jnp.where(diff >= 0, jnp.exp(jnp.maximum(diff, 0.0) * lg_ref[:, :blk]), 0.0)
    idx_v = lax.broadcasted_iota(jnp.int32, (blk, cfg.rv), 0).astype(F32)
    idx_k = lax.broadcasted_iota(jnp.int32, (blk, cfg.rqk), 0).astype(F32)
    cross = jnp.exp((idx_v + 1.0) * lg_v)
    state = jnp.exp((blk - 1.0 - idx_k) * lg_k)
    chunk = jnp.exp(float(blk) * lg_v)
    return intra, cross, state, chunk


def _ret_fwd(cfg, proj, cos, sin, lg):
    blk, nc, half = ATT_BLOCK, cfg.seq // ATT_BLOCK, cfg.rqk // 2
    kscale = cfg.rqk ** -0.5

    def body(q_ref, k_ref, v_ref, cos_ref, sin_ref, lg_ref, rn_ref, rstd_ref, st_ref, state):
        n = pl.program_id(1)

        @pl.when(n == 0)
        def _():
            state[...] = jnp.zeros_like(state)

        intra, cross_d, state_d, chunk_d = _decays(lg_ref, cfg)
        c, s = cos_ref[...], sin_ref[...]
        qb = _rot(q_ref[...].astype(F32), c, s, half).astype(BF16)
        kf = _rot(k_ref[...].astype(F32), c, s, half) * kscale
        vb = v_ref[...]
        prev = state[...]
        st_ref[...] = prev.astype(BF16)
        att = _dot_nt(qb, kf.astype(BF16)) * intra
        out = _dot(att.astype(BF16), vb) + _dot(qb, prev.astype(BF16)) * cross_d
        state[...] = chunk_d * prev + _dot_tn((kf * state_d).astype(BF16), vb)
        mu = jnp.mean(out, axis=1, keepdims=True)
        cen = out - mu
        rstd = lax.rsqrt(jnp.mean(cen * cen, axis=1, keepdims=True) + GN_EPS)
        rn_ref[...] = (cen * rstd).astype(rn_ref.dtype)
        rstd_ref[...] = rstd

    oq, ok, ov = cfg.o_qr // cfg.rqk, cfg.o_kr // cfg.rqk, cfg.o_vr // cfg.rv
    tab = pl.BlockSpec((blk, half), lambda h, n: (n, 0))
    return pl.pallas_call(
        body, name="ret_fwd", grid=(cfg.rh, nc),
        in_specs=[pl.BlockSpec((blk, cfg.rqk), lambda h, n: (n, oq + h)), pl.BlockSpec((blk, cfg.rqk), lambda h, n: (n, ok + h)),
                  pl.BlockSpec((blk, cfg.rv), lambda h, n: (n, ov + h)), tab, tab,
                  pl.BlockSpec((None, 1, lg.shape[2]), lambda h, n: (h, 0, 0))],
        out_specs=[pl.BlockSpec((blk, cfg.rv), lambda h, n: (n, h)), pl.BlockSpec((None, blk, 1), lambda h, n: (h, n, 0)),
                   pl.BlockSpec((None, None, cfg.rqk, cfg.rv), lambda h, n: (h, n, 0, 0))],
        out_shape=[jax.ShapeDtypeStruct((cfg.seq, cfg.rv_w), BF16), jax.ShapeDtypeStruct((cfg.rh, cfg.seq, 1), F32),
                   jax.ShapeDtypeStruct((cfg.rh, nc, cfg.rqk, cfg.rv), BF16)],
        scratch_shapes=[pltpu.VMEM((cfg.rqk, cfg.rv), F32)], compiler_params=_params(("parallel", "arbitrary")),
    )(proj, proj, proj, cos, sin, lg)


def _ret_bwd(cfg, proj, d_r, states, cos, sin, lg):
    blk, nc, half = ATT_BLOCK, cfg.seq // ATT_BLOCK, cfg.rqk // 2
    kscale = cfg.rqk ** -0.5

    def body(q_ref, k_ref, v_ref, do_ref, st_ref, cos_ref, sin_ref, lg_ref, dq_ref, dk_ref, dv_ref, dstate):
        n = pl.program_id(1)

        @pl.when(n == 0)
        def _():
            dstate[...] = jnp.zeros_like(dstate)

        intra, cross_d, state_d, chunk_d = _decays(lg_ref, cfg)
        c, s = cos_ref[...], sin_ref[...]
        qb = _rot(q_ref[...].astype(F32), c, s, half).astype(BF16)
        kf = _rot(k_ref[...].astype(F32), c, s, half) * kscale
        kb, ksb = kf.astype(BF16), (kf * state_d).astype(BF16)
        vb, prev = v_ref[...], st_ref[...]
        do = do_ref[...].astype(F32)
        dob, docb = do.astype(BF16), (do * cross_d).astype(BF16)
        dsb = dstate[...].astype(BF16)
        att = (_dot_nt(qb, kb) * intra).astype(BF16)
        datt = (_dot_nt(dob, vb) * intra).astype(BF16)
        d_q = _dot(datt, kb) + _dot_nt(docb, prev)
        d_k = _dot_tn(datt, qb) + _dot_nt(vb, dsb) * state_d
        d_v = _dot_tn(att, dob) + _dot(ksb, dsb)
        dstate[...] = chunk_d * dstate[...] + _dot_tn(qb, docb)
        dq_ref[...] = _unrot(d_q, c, s, half).astype(dq_ref.dtype)
        dk_ref[...] = _unrot(d_k * kscale, c, s, half).astype(dk_ref.dtype)
        dv_ref[...] = d_v.astype(dv_ref.dtype)

    oq, ok, ov = cfg.o_qr // cfg.rqk, cfg.o_kr // cfg.rqk, cfg.o_vr // cfg.rv
    last = nc - 1
    tab = pl.BlockSpec((blk, half), lambda h, n: (last - n, 0))
    qk_out = pl.BlockSpec((blk, cfg.rqk), lambda h, n: (last - n, h))
    v_out = pl.BlockSpec((blk, cfg.rv), lambda h, n: (last - n, h))
    return pl.pallas_call(
        body, name="ret_bwd", grid=(cfg.rh, nc),
        in_specs=[pl.BlockSpec((blk, cfg.rqk), lambda h, n: (last - n, oq + h)),
                  pl.BlockSpec((blk, cfg.rqk), lambda h, n: (last - n, ok + h)),
                  pl.BlockSpec((blk, cfg.rv), lambda h, n: (last - n, ov + h)), v_out,
                  pl.BlockSpec((None, None, cfg.rqk, cfg.rv), lambda h, n: (h, last - n, 0, 0)), tab, tab,
                  pl.BlockSpec((None, 1, lg.shape[2]), lambda h, n: (h, 0, 0))],
        out_specs=[qk_out, qk_out, v_out],
        out_shape=[jax.ShapeDtypeStruct((cfg.seq, cfg.rqk_w), BF16), jax.ShapeDtypeStruct((cfg.seq, cfg.rqk_w), BF16),
                   jax.ShapeDtypeStruct((cfg.seq, cfg.rv_w), BF16)],
        scratch_shapes=[pltpu.VMEM((cfg.rqk, cfg.rv), F32)], compiler_params=_params(("parallel", "arbitrary")),
    )(proj, proj, proj, d_r, states, cos, sin, lg)


def _ret_gate(cfg, proj, rn, gn_g):
    tm, cw = 256, cfg.cw
    og = cfg.o_gr // cw

    def body(g_ref, rn_ref, w_ref, o_ref):
        g = g_ref[...].astype(F32)
        o_ref[...] = (g * _sigmoid(g) * (rn_ref[...].astype(F32) * w_ref[...])).astype(o_ref.dtype)

    blk = pl.BlockSpec((tm, cw), lambda i, j: (i, j))
    return pl.pallas_call(
        body, name="ret_gate", grid=(cfg.seq // tm, cfg.rv_w // cw),
        in_specs=[pl.BlockSpec((tm, cw), lambda i, j: (i, og + j)), blk, pl.BlockSpec((1, cw), lambda i, j: (0, j))],
        out_specs=blk, out_shape=jax.ShapeDtypeStruct((cfg.seq, cfg.rv_w), BF16),
        compiler_params=_params(("parallel", "parallel")),
    )(proj, rn, gn_g)


def _ret_gate_bwd(cfg, proj, rn, gn_g, rstd, d_rg):
    tm, rv = 256, cfg.rv
    og = cfg.o_gr // rv

    def body(g_ref, rn_ref, w_ref, rstd_ref, drg_ref, dg_ref, dr_ref, gw_ref):
        i = pl.program_id(1)
        g, rn, w = g_ref[...].astype(F32), rn_ref[...].astype(F32), w_ref[...]
        drg = drg_ref[...].astype(F32)
        sg = _sigmoid(g)
        silu = g * sg
        dg_ref[...] = (drg * (rn * w) * (sg * (1.0 + g * (1.0 - sg)))).astype(dg_ref.dtype)
        drn = drg * silu * w
        part = jnp.sum(drg * silu * rn, axis=0, keepdims=True)

        @pl.when(i == 0)
        def _():
            gw_ref[...] = part

        @pl.when(i > 0)
        def _():
            gw_ref[...] += part

        m1 = jnp.mean(drn, axis=1, keepdims=True)
        m2 = jnp.mean(drn * rn, axis=1, keepdims=True)
        dr_ref[...] = (rstd_ref[...] * (drn - m1 - rn * m2)).astype(dr_ref.dtype)

    blk = pl.BlockSpec((tm, rv), lambda h, i: (i, h))
    vec = pl.BlockSpec((1, rv), lambda h, i: (0, h))
    out = jax.ShapeDtypeStruct((cfg.seq, cfg.rv_w), BF16)
    return pl.pallas_call(
        body, name="ret_gate_bwd", grid=(cfg.rh, cfg.seq // tm),
        in_specs=[pl.BlockSpec((tm, rv), lambda h, i: (i, og + h)), blk, vec, pl.BlockSpec((None, tm, 1), lambda h, i: (h, i, 0)), blk],
        out_specs=[blk, blk, vec], out_shape=[out, out, jax.ShapeDtypeStruct((1, cfg.rv_w), F32)],
        compiler_params=_params(("parallel", "arbitrary")),
    )(proj, rn, gn_g, rstd, d_rg)


def _ln1(cfg, x, mo, g, b):
    tm, d = 128, cfg.d

    def body(x_ref, mo_ref, g_ref, b_ref, xh_ref, h_ref, rstd_ref):
        z = cfg.alpha * x_ref[...] + mo_ref[...]
        cen = z - jnp.mean(z, axis=1, keepdims=True)
        rstd = lax.rsqrt(jnp.mean(cen * cen, axis=1, keepdims=True) + LN_EPS)
        xh = cen * rstd
        xh_ref[...] = xh
        h_ref[...] = (xh * g_ref[...] + b_ref[...]).astype(h_ref.dtype)
        rstd_ref[...] = rstd

    row = pl.BlockSpec((tm, d), lambda i: (i, 0))
    vec = pl.BlockSpec((1, d), lambda i: (0, 0))
    col = pl.BlockSpec((tm, 1), lambda i: (i, 0))
    return pl.pallas_call(
        body, name="ln1", grid=(cfg.seq // tm,), in_specs=[row, row, vec, vec], out_specs=[row, row, col],
        out_shape=[jax.ShapeDtypeStruct((cfg.seq, d), F32), jax.ShapeDtypeStruct((cfg.seq, d), BF16),
                   jax.ShapeDtypeStruct((cfg.seq, 1), F32)],
        compiler_params=_params(("parallel",)),
    )(x, mo, g, b)


def _head(cfg, xhat1, ffn, hp, pu, tgt, g1, b1, g2, b2):
    tm, d = 64, cfg.d

    def body(xh_ref, ffn_ref, hp_ref, pu_ref, tgt_ref, g1_ref, b1_ref, g2_ref, b2_ref,
             loss_ref, dzf_ref, dzb_ref, dpg_ref, dpu_ref, gg_ref, gb_ref):
        i = pl.program_id(0)
        h1 = xh_ref[...] * g1_ref[...] + b1_ref[...]
        sg, pu = _sigmoid(hp_ref[...].astype(F32)), pu_ref[...].astype(F32)
        z = cfg.alpha * h1 + ffn_ref[...] + sg * pu
        cen = z - jnp.mean(z, axis=1, keepdims=True)
        rstd = lax.rsqrt(jnp.mean(cen * cen, axis=1, keepdims=True) + LN_EPS)
        xh2 = cen * rstd
        err = xh2 * g2_ref[...] + b2_ref[...] - tgt_ref[...]
        dy = err * (1.0 / d)
        part_l = jnp.sum(jnp.sum(err * err, axis=1, keepdims=True), axis=0, keepdims=True) * (0.5 / d)
        part_g = jnp.sum(dy * xh2, axis=0, keepdims=True)
        part_b = jnp.sum(dy, axis=0, keepdims=True)

        @pl.when(i == 0)
        def _():
            loss_ref[...] = jnp.zeros_like(loss_ref)
            gg_ref[...] = jnp.zeros_like(gg_ref)
            gb_ref[...] = jnp.zeros_like(gb_ref)

        loss_ref[...] += jnp.broadcast_to(part_l, loss_ref.shape)
        gg_ref[...] += part_g
        gb_ref[...] += part_b
        dxh = dy * g2_ref[...]
        m1 = jnp.mean(dxh, axis=1, keepdims=True)
        m2 = jnp.mean(dxh * xh2, axis=1, keepdims=True)
        dz = rstd * (dxh - m1 - xh2 * m2)
        dzf_ref[...] = dz
        dzb_ref[...] = dz.astype(dzb_ref.dtype)
        dpg_ref[...] = (dz * pu * sg * (1.0 - sg)).astype(dpg_ref.dtype)
        dpu_ref[...] = (dz * sg).astype(dpu_ref.dtype)

    row = pl.BlockSpec((tm, d), lambda i: (i, 0))
    vec = pl.BlockSpec((1, d), lambda i: (0, 0))
    bf = jax.ShapeDtypeStruct((cfg.seq, d), BF16)
    vec_out = jax.ShapeDtypeStruct((1, d), F32)
    return pl.pallas_call(
        body, name="head", grid=(cfg.seq // tm,), in_specs=[row] * 5 + [vec] * 4,
        out_specs=[pl.BlockSpec((1, 128), lambda i: (0, 0)), row, row, row, row, vec, vec],
        out_shape=[jax.ShapeDtypeStruct((1, 128), F32), jax.ShapeDtypeStruct((cfg.seq, d), F32), bf, bf, bf, vec_out, vec_out],
        compiler_params=_params(("arbitrary",)),
    )(xhat1, ffn, hp, pu, tgt, g1, b1, g2, b2)


def _ln1_bwd(cfg, dh_mm, dz2, xhat1, rstd1, g1):
    tm, d = 128, cfg.d

    def body(dh_ref, dz_ref, xh_ref, rstd_ref, g_ref, dzf_ref, dzb_ref, gg_ref, gb_ref):
        i = pl.program_id(0)
        dh = cfg.alpha * dz_ref[...] + dh_ref[...]
        xh = xh_ref[...]

        @pl.when(i == 0)
        def _():
            gg_ref[...] = jnp.zeros_like(gg_ref)
            gb_ref[...] = jnp.zeros_like(gb_ref)

        gg_ref[...] += jnp.sum(dh * xh, axis=0, keepdims=True)
        gb_ref[...] += jnp.sum(dh, axis=0, keepdims=True)
        dxh = dh * g_ref[...]
        m1 = jnp.mean(dxh, axis=1, keepdims=True)
        m2 = jnp.mean(dxh * xh, axis=1, keepdims=True)
        dz = rstd_ref[...] * (dxh - m1 - xh * m2)
        dzf_ref[...] = dz
        dzb_ref[...] = dz.astype(dzb_ref.dtype)

    row = pl.BlockSpec((tm, d), lambda i: (i, 0))
    vec = pl.BlockSpec((1, d), lambda i: (0, 0))
    vec_out = jax.ShapeDtypeStruct((1, d), F32)
    return pl.pallas_call(
        body, name="ln1_bwd", grid=(cfg.seq // tm,),
        in_specs=[row, row, row, pl.BlockSpec((tm, 1), lambda i: (i, 0)), vec], out_specs=[row, row, vec, vec],
        out_shape=[jax.ShapeDtypeStruct((cfg.seq, d), F32), jax.ShapeDtypeStruct((cfg.seq, d), BF16), vec_out, vec_out],
        compiler_params=_params(("arbitrary",)),
    )(dh_mm, dz2, xhat1, rstd1, g1)


def _adamw(w, g, m, v, name):
    r, c = w.shape
    assert g.shape[0] >= r and g.shape[1] == c
    tr, tc = _row_tile(r, c * 4, 2 * 1024 * 1024), c
    bc1, bc2 = 1.0 - ADAM_B1 ** ADAM_STEP, 1.0 - ADAM_B2 ** ADAM_STEP

    def body(w_ref, g_ref, m_ref, v_ref, go_ref, d_ref, mo_ref, vo_ref):
        gg = g_ref[...]
        mn = ADAM_B1 * m_ref[...] + (1.0 - ADAM_B1) * gg
        vn = ADAM_B2 * v_ref[...] + (1.0 - ADAM_B2) * (gg * gg)
        go_ref[...] = gg
        d_ref[...] = -ADAM_LR * ((mn / bc1) / (jnp.sqrt(vn / bc2) + ADAM_EPS) + ADAM_WD * w_ref[...])
        mo_ref[...] = mn
        vo_ref[...] = vn

    blk = pl.BlockSpec((tr, tc), lambda i, j: (i, j))
    out = jax.ShapeDtypeStruct((r, c), F32)
    return pl.pallas_call(
        body, name=name, grid=(r // tr, c // tc), in_specs=[blk] * 4, out_specs=[blk] * 4, out_shape=[out] * 4,
        compiler_params=_params(("parallel", "parallel")),
    )(w, g, m, v)


def _place():
    x, y, c = lax.axis_index("x"), lax.axis_index("y"), lax.axis_index("c")
    others = [(1 - x, y), (x, 1 - y), (1 - x, 1 - y)]
    return x, y, c, others


def _all_gather(ws, ax, name, landed=None):
    r, cdim = ws.shape
    full_shape = (4 * r, cdim) if ax == 0 else (r, 4 * cdim)
    n_shard = r if ax == 0 else cdim
    n_half = r // 2
    tr = _row_tile(n_half, cdim * 2, STAGE_BYTES, 16)
    nq = n_half // tr
    slots = 3

    over_ici = landed is None

    def body(*refs):
        ws_ref = refs[0]
        (full_ref, buf, load_sems, write_sems, send_sems, pass_sems, recv_sems, sib_sem, relay_sems,
         diag_sems) = refs[1 if over_ici else 2:]
        x, y, c, others = _place()
        me = 2 * x + y
        sibling = (x, y, 1 - c)

        def region(chip, half, row0, rows):
            if ax == 0:
                return full_ref.at[pl.ds(chip * n_shard + half * n_half + row0, rows), :]
            return full_ref.at[pl.ds(half * n_half + row0, rows), pl.ds(chip * n_shard, n_shard)]

        pending = {}

        def free(slot):
            for cp, remote in pending.pop(slot, []):
                if remote:
                    cp.wait_send()
                else:
                    cp.wait()

        def load(src, slot):
            free(slot)
            cp = pltpu.make_async_copy(src, buf.at[slot], load_sems.at[slot])
            cp.start()
            cp.wait()

        step = 0
        for mine in (True, False):
            half = c if mine else 1 - c
            for q in range(nq):
                slot = step % slots
                load(ws_ref.at[pl.ds(half * n_half + q * tr, tr), :], slot)
                dst = region(me, half, q * tr, tr)
                cp = pltpu.make_async_copy(buf.at[slot], dst, write_sems.at[slot])
                cp.start()
                pending[slot] = [(cp, False)]
                if mine and over_ici:
                    for k, chip in enumerate(others[:2]):
                        cp = pltpu.make_async_remote_copy(
                            src_ref=buf.at[slot], dst_ref=dst, send_sem=send_sems.at[3 * slot + k], recv_sem=recv_sems.at[k],
                            device_id=(*chip, c), device_id_type=MESH)
                        cp.start()
                        pending[slot].append((cp, True))
                step += 1

        def landed_wait(part, sem):
            pltpu.make_async_remote_copy(src_ref=part, dst_ref=part, send_sem=send_sems.at[0], recv_sem=sem,
                                         device_id=(x, y, c), device_id_type=MESH).wait_recv()

        relays = []
        if over_ici:
            quarter = n_half // 2
            for k in (0, 1):
                landed_wait(region(2 * others[k][0] + others[k][1], c, 0, n_half), recv_sems.at[k])
            for k, (row0, to) in enumerate(((0, others[1]), (quarter, others[0]))):
                part = region(2 * others[k][0] + others[k][1], c, row0, quarter)
                cp = pltpu.make_async_remote_copy(src_ref=part, dst_ref=part, send_sem=relay_sems.at[k], recv_sem=diag_sems.at[k],
                                                  device_id=(*to, c), device_id_type=MESH)
                cp.start()
                relays.append(cp)
        for k, chip in enumerate(others):
            j = 2 * chip[0] + chip[1]
            if over_ici and k == 2:
                landed_wait(region(j, c, 0, quarter), diag_sems.at[0])
                landed_wait(region(j, c, quarter, quarter), diag_sems.at[1])
            for q in range(nq):
                slot = step % slots
                part = region(j, c, q * tr, tr)
                load(part, slot)
                cp = pltpu.make_async_remote_copy(src_ref=buf.at[slot], dst_ref=part, send_sem=pass_sems.at[slot],
                                                  recv_sem=sib_sem, device_id=sibling, device_id_type=MESH)
                cp.start()
                pending[slot] = [(cp, True)]
                step += 1
        for slot in list(pending):
            free(slot)
        for cp in relays:
            cp.wait_send()
        if ax == 0:
            three = full_ref.at[pl.ds(0, 3 * n_half), :]
        else:
            three = full_ref.at[pl.ds(0, n_half), pl.ds(0, 3 * n_shard)]
        pltpu.make_async_remote_copy(src_ref=three, dst_ref=three, send_sem=send_sems.at[0], recv_sem=sib_sem,
                                     device_id=(x, y, c), device_id_type=MESH).wait_recv()

    return pl.pallas_call(
        body, name=name, in_specs=[ANY] if over_ici else [ANY, ANY], out_specs=ANY,
        out_shape=jax.ShapeDtypeStruct(full_shape, ws.dtype), input_output_aliases={} if over_ici else {1: 0},
        scratch_shapes=[pltpu.VMEM((slots, tr, cdim), ws.dtype), pltpu.SemaphoreType.DMA((slots,)),
                        pltpu.SemaphoreType.DMA((slots,)), pltpu.SemaphoreType.DMA((3 * slots,)),
                        pltpu.SemaphoreType.DMA((slots,)), pltpu.SemaphoreType.DMA((3,)), pltpu.SemaphoreType.DMA,
                        pltpu.SemaphoreType.DMA((2,)), pltpu.SemaphoreType.DMA((2,))],
        compiler_params=pltpu.CompilerParams(vmem_limit_bytes=VMEM_LIMIT),
    )(*((ws,) if over_ici else (ws, landed)))


def _gather_job(ws, ax):
    r, cdim = ws.shape
    full_shape = (4 * r, cdim) if ax == 0 else (r, 4 * cdim)
    n_shard = r if ax == 0 else cdim
    n_half = r // 2

    def copies(ins, outs, sems):
        (ws_ref,), (full_ref,) = ins, outs
        x, y, c, others = _place()
        me = 2 * x + y
        if ax == 0:
            dst = full_ref.at[pl.ds(me * n_shard + c * n_half, n_half), :]
        else:
            dst = full_ref.at[pl.ds(c * n_half, n_half), pl.ds(me * n_shard, n_shard)]
        return [(pltpu.make_async_remote_copy(
            src_ref=ws_ref.at[pl.ds(c * n_half, n_half), :], dst_ref=dst, send_sem=sems.at[k], recv_sem=sems.at[3 + k],
            device_id=(*chip, c), device_id_type=MESH), "both") for k, chip in enumerate(others)]

    return _Job([ws], [jax.ShapeDtypeStruct(full_shape, ws.dtype)], 6, copies)


def _scatter_job(pair):
    _, n_half, ncol = pair.shape

    def copies(ins, outs, sems):
        (p_ref,), (got_ref,) = ins, outs
        x, y, c, others = _place()
        return [(pltpu.make_async_remote_copy(
            src_ref=p_ref.at[k], dst_ref=got_ref.at[k], send_sem=sems.at[k], recv_sem=sems.at[3 + k],
            device_id=(*chip, c), device_id_type=MESH), "both") for k, chip in enumerate(others)]

    return _Job([pair], [jax.ShapeDtypeStruct((3, n_half, ncol), pair.dtype)], 6, copies)


def _shard_dims(shape, ax):
    rf, cf = shape
    n_shard = (rf if ax == 0 else cf) // 4
    return n_shard, (n_shard if ax == 0 else rf) // 2, (cf if ax == 0 else n_shard)


def _mm_tn_pair(a, b, ax, name, jobs=()):
    (k, m), (k2, n) = a.shape, b.shape
    assert k == k2
    n_shard, n_half, _ = _shard_dims((m, n), ax)
    tm = _tile(n_half, 1024)
    if tm < 512:
        tm = n_half
    tm, tn, tk = _mm_tiles(m, n, k, 4, tm=tm)
    ni, nj, nk = m // tm, n // tn, k // tk
    ntiles = ni * nj
    n_acc = 1 if nk > 1 else 0
    assert n_half % tm == 0 and ntiles >= 2
    jobs = list(jobs)
    n_job_in = sum(len(jb.inputs) for jb in jobs)
    n_job_out = sum(len(jb.out_shapes) for jb in jobs)

    def body(*refs):
        a_ref, b_ref = refs[0], refs[1]
        job_in = refs[2:2 + n_job_in]
        own_ref, recv_ref = refs[2 + n_job_in], refs[3 + n_job_in]
        job_out = refs[4 + n_job_in:4 + n_job_in + n_job_out]
        scratch = refs[4 + n_job_in + n_job_out:]
        acc_ref = scratch[0] if n_acc else None
        stage, local_sems, send_sems, recv_sem = scratch[n_acc:n_acc + 4]
        job_sems = scratch[n_acc + 4:]
        i, j, kk = pl.program_id(0), pl.program_id(1), pl.program_id(2)
        x, y, c, _ = _place()
        sibling = (x, y, 1 - c)
        t = i * nj + j

        def job_copies():
            found, pi, po = [], 0, 0
            for jb, sems in zip(jobs, job_sems):
                found += jb.copies(job_in[pi:pi + len(jb.inputs)], job_out[po:po + len(jb.out_shapes)], sems)
                pi, po = pi + len(jb.inputs), po + len(jb.out_shapes)
            return found

        if jobs:
            @pl.when(jnp.logical_and(t == 0, kk == 0))
            def _():
                for cp, _ in job_copies():
                    cp.start()

        def is_mine(ii):
            row0 = ii * tm
            half = row0 // n_half if ax == 1 else (row0 % n_shard) // n_half
            return half == c

        def copies(tt):
            ii, jj, slot = tt // nj, tt % nj, tt % 2
            where = (pl.ds(ii * tm, tm), pl.ds(jj * tn, tn))
            local = pltpu.make_async_copy(stage.at[slot], own_ref.at[where], local_sems.at[slot])
            remote = pltpu.make_async_remote_copy(src_ref=stage.at[slot], dst_ref=recv_ref.at[where], send_sem=send_sems.at[slot],
                                                  recv_sem=recv_sem, device_id=sibling, device_id_type=MESH)
            return is_mine(ii), local, remote

        def wait_tile(tt):
            mine, local, remote = copies(tt)

            @pl.when(mine)
            def _():
                local.wait()

            @pl.when(jnp.logical_not(mine))
            def _():
                remote.wait_send()

        part = lax.dot_general(a_ref[...], b_ref[...], _TN, preferred_element_type=F32)
        if n_acc:
            @pl.when(kk == 0)
            def _():
                acc_ref[...] = part

            @pl.when(kk > 0)
            def _():
                acc_ref[...] += part

        @pl.when(kk == nk - 1)
        def _():
            @pl.when(t >= 2)
            def _():
                wait_tile(t - 2)

            stage[t % 2] = (acc_ref[...] if n_acc else part).astype(stage.dtype)
            mine, local, remote = copies(t)

            @pl.when(mine)
            def _():
                local.start()

            @pl.when(jnp.logical_not(mine))
            def _():
                remote.start()

            @pl.when(t == ntiles - 1)
            def _():
                wait_tile(t - 1)
                wait_tile(t)
                half_rows = recv_ref.at[pl.ds(0, m // 2), :]
                pltpu.make_async_remote_copy(src_ref=half_rows, dst_ref=half_rows, send_sem=send_sems.at[0], recv_sem=recv_sem,
                                             device_id=sibling, device_id_type=MESH).wait_recv()
                for cp, _ in job_copies():
                    cp.wait()

    out = jax.ShapeDtypeStruct((m, n), BF16)
    return pl.pallas_call(
        body, name=name, grid=(ni, nj, nk),
        in_specs=[pl.BlockSpec((tk, tm), lambda i, j, kk: (kk, i)), pl.BlockSpec((tk, tn), lambda i, j, kk: (kk, j))]
        + [ANY] * n_job_in,
        out_specs=[ANY, ANY] + [ANY] * n_job_out, out_shape=[out, out] + [s for jb in jobs for s in jb.out_shapes],
        scratch_shapes=[pltpu.VMEM((tm, tn), F32)] * n_acc + [pltpu.VMEM((2, tm, tn), BF16), pltpu.SemaphoreType.DMA((2,)),
                                                               pltpu.SemaphoreType.DMA((2,)), pltpu.SemaphoreType.DMA]
        + [pltpu.SemaphoreType.DMA((jb.n_sems,)) for jb in jobs],
        compiler_params=_params(("arbitrary", "arbitrary", "arbitrary")),
    )(a, b, *[t for jb in jobs for t in jb.inputs])


def _other_chip(k, x, y):
    return jnp.where(k == 1, x, 1 - x), jnp.where(k == 0, y, 1 - y)


def _pairsum(own, recv, ax, coords, name):
    n_shard, n_half, ncol = _shard_dims(own.shape, ax)
    tr = _row_tile(n_half, ncol * 2, STAGE_BYTES // 2, 16)
    nr = n_half // tr

    def src_map(k):
        def index(r, co):
            jx, jy = _other_chip(k, co[0], co[1])
            j = 2 * jx + jy
            if ax == 0:
                return ((j * n_shard + co[2] * n_half) // tr + r, 0)
            return (co[2] * nr + r, j)
        return index

    def body(co_ref, o0, r0, o1, r1, o2, r2, out_ref):
        for k, (own_ref, recv_ref) in enumerate(((o0, r0), (o1, r1), (o2, r2))):
            out_ref[k] = (own_ref[...].astype(F32) + recv_ref[...].astype(F32)).astype(out_ref.dtype)

    blks = [pl.BlockSpec((tr, ncol), src_map(k)) for k in range(3) for _ in range(2)]
    return pl.pallas_call(
        body, name=name,
        grid_spec=pltpu.PrefetchScalarGridSpec(
            num_scalar_prefetch=1, grid=(nr,), in_specs=blks, out_specs=pl.BlockSpec((3, tr, ncol), lambda r, co: (0, r, 0))),
        out_shape=jax.ShapeDtypeStruct((3, n_half, ncol), BF16),
        compiler_params=_params(("parallel",)),
    )(coords, own, recv, own, recv, own, recv)


def _sum_share(own, recv, got, ax, coords, name):
    n_shard, n_half, ncol = _shard_dims(own.shape, ax)
    tr = _row_tile(n_half, ncol * 4, STAGE_BYTES, 16)
    nr = n_half // tr

    def src_map(r, co):
        me = 2 * co[0] + co[1]
        if ax == 0:
            return ((me * n_shard + co[2] * n_half) // tr + r, 0)
        return (co[2] * nr + r, me)

    def body(co_ref, own_ref, recv_ref, got_ref, out_ref, sbuf, local_sems, send_sems, recv_sem):
        r = pl.program_id(0)
        x, y, c, _ = _place()

        def copies(slot):
            rows = out_ref.at[pl.ds(c * n_half + r * tr, tr), :]
            local = pltpu.make_async_copy(sbuf.at[slot], rows, local_sems.at[slot])
            remote = pltpu.make_async_remote_copy(src_ref=sbuf.at[slot], dst_ref=rows, send_sem=send_sems.at[slot],
                                                  recv_sem=recv_sem, device_id=(x, y, 1 - c), device_id_type=MESH)
            return local, remote

        def wait_slot(slot):
            local, remote = copies(slot)
            local.wait()
            remote.wait_send()

        @pl.when(r >= 2)
        def _():
            wait_slot(r % 2)

        total = own_ref[...].astype(F32) + recv_ref[...].astype(F32)
        for kk in range(3):
            total = total + got_ref[kk].astype(F32)
        sbuf[r % 2] = total
        local, remote = copies(r % 2)
        local.start()
        remote.start()

        @pl.when(r == nr - 1)
        def _():
            wait_slot(r % 2)
            if nr >= 2:
                wait_slot(1 - r % 2)
            half = out_ref.at[pl.ds(0, n_half), :]
            pltpu.make_async_remote_copy(src_ref=half, dst_ref=half, send_sem=send_sems.at[0], recv_sem=recv_sem,
                                         device_id=(x, y, c), device_id_type=MESH).wait_recv()

    blk = pl.BlockSpec((tr, ncol), src_map)
    return pl.pallas_call(
        body, name=name,
        grid_spec=pltpu.PrefetchScalarGridSpec(
            num_scalar_prefetch=1, grid=(nr,), in_specs=[blk, blk, pl.BlockSpec((3, tr, ncol), lambda r, co: (0, r, 0))],
            out_specs=ANY,
            scratch_shapes=[pltpu.VMEM((2, tr, ncol), F32), pltpu.SemaphoreType.DMA((2,)), pltpu.SemaphoreType.DMA((2,)),
                            pltpu.SemaphoreType.DMA]),
        out_shape=jax.ShapeDtypeStruct((2 * n_half, ncol), F32),
        compiler_params=_params(("arbitrary",)),
    )(coords, own, recv, got)


def _vec_all_reduce(v, name):
    rows, d = v.shape

    def body(v_ref, out_ref, gath_ref, send_sems, recv_sems):
        x, y, c, _ = _place()
        me = 4 * x + 2 * y + c
        gath_ref[me] = v_ref[...]
        flips = [(fx, fy, fc) for fx in (0, 1) for fy in (0, 1) for fc in (0, 1)][1:]
        sends = []
        for k, (fx, fy, fc) in enumerate(flips):
            peer = (1 - x if fx else x, 1 - y if fy else y, 1 - c if fc else c)
            sends.append(pltpu.make_async_remote_copy(
                src_ref=v_ref, dst_ref=gath_ref.at[me], send_sem=send_sems.at[k], recv_sem=recv_sems.at[k],
                device_id=peer, device_id_type=MESH))
        for cp in sends:
            cp.start()
        for cp in sends:
            cp.wait()
        acc = gath_ref[0]
        for dev in range(1, 8):
            acc = acc + gath_ref[dev]
        out_ref[...] = acc

    vm = pl.BlockSpec(memory_space=pltpu.VMEM)
    return pl.pallas_call(
        body, name=name, in_specs=[vm], out_specs=vm, out_shape=jax.ShapeDtypeStruct((rows, d), F32),
        scratch_shapes=[pltpu.VMEM((8, rows, d), F32), pltpu.SemaphoreType.DMA((7,)), pltpu.SemaphoreType.DMA((7,))],
    )(v)


_TRANSPOSED = ("w_ffn_gate", "w_ffn_up")
_SHARD_AXIS = {"w_in": 1, "w_attn_out": 1, "w_ret_out": 0, "w_o": 0, "w_ffn_gate": 0, "w_ffn_up": 0, "w_ffn_down": 0,
               "w_ple_gate": 0, "w_ple_up": 1}
_FFN_HIDDEN_ROWS = ("w_ffn_gate", "w_ffn_up", "w_ffn_down")
_VECTORS = ("ret_gn_g", "ln1_g", "ln1_b", "ln2_g", "ln2_b")
_WEIGHTS = ("w_in", "w_attn_out", "w_ret_out", "ret_gn_g", "w_o", "ln1_g", "ln1_b", "w_ffn_gate", "w_ffn_up", "w_ffn_down",
            "w_ple_gate", "w_ple_up", "ln2_g", "ln2_b")


def _local_grads(cfg, x, p, tgt, w, vec, coords=None):
    dist = coords is not None
    xb, pb = x.astype(BF16), p.astype(BF16)
    cos, sin = _rope_tables(cfg)
    lg = _log_gamma(cfg)
    wf = {"w_in": _all_gather(w["w_in"], _SHARD_AXIS["w_in"], "ag_w_in")} if dist else dict(w)
    grads, pairs, got, waiting = {}, {}, {}, []

    def gather_behind(names, matmul, n_main=1):
        if not dist:
            return matmul(())
        res = matmul([_gather_job(w[n], _SHARD_AXIS[n]) for n in names])
        for n, full in zip(names, res[n_main:]):
            wf[n] = _all_gather(w[n], _SHARD_AXIS[n], "ag_" + n, landed=full)
        return res[0] if n_main == 1 else res[:n_main]

    def scatter_jobs():
        names = list(waiting)
        del waiting[:]
        return [_scatter_job(pairs[n][2]) for n in names], lambda landed: got.update(zip(names, landed))

    def grad(a, b, n, name, host=False):
        if not dist:
            grads[n] = _mm(a, b, "tn", BF16, name)
            return
        jobs, keep = scatter_jobs() if host else ((), None)
        own, recv, *landed = _mm_tn_pair(a, b, _SHARD_AXIS[n], name, jobs=jobs)
        if host:
            keep(landed)
        pairs[n] = (own, recv, _pairsum(own, recv, _SHARD_AXIS[n], coords, "rs_" + n + "_pairsum"))
        waiting.append(n)

    def scatter_behind(matmul, n_main=1):
        if not waiting:
            return matmul(())
        jobs, keep = scatter_jobs()
        res = matmul(jobs)
        keep(res[n_main:])
        return res[0] if n_main == 1 else res[:n_main]

    proj = gather_behind(["w_attn_out", "w_ret_out", "w_o", "w_ffn_gate"],
                         lambda jobs: _mm(xb, wf["w_in"], "nn", BF16, "proj", jobs=jobs))
    aw = cfg.aw
    att = []
    for g, (_, dil) in enumerate(ATT_GROUPS):
        starts = [o + g * aw for o in (cfg.o_qa, cfg.o_ka, cfg.o_va)]
        if dil == 1:
            qkv, cols = (proj, proj, proj), tuple(s // aw for s in starts)
        else:
            qkv, cols = tuple(_deinterleave(proj[:, s:s + aw], dil) for s in starts), (0, 0, 0)
        o_g, lse_g = _attn_fwd(cfg, g, *qkv, cols)
        att.append((qkv, cols, _interleave(o_g, dil), _interleave(lse_g, dil)))
    o_att, lse = _attn_combine(cfg, [t[2] for t in att], [t[3] for t in att])
    y_att = _mm(o_att, wf["w_attn_out"], "nn", BF16, "y_att")
    rn, rstd_r, states = _ret_fwd(cfg, proj, cos, sin, lg)
    rg = _ret_gate(cfg, proj, rn, vec["ret_gn_g"])
    def mix_tiles(yr, ga, g2, ya):
        yr = yr.astype(BF16)
        return yr, _sigmoid(ga.astype(F32)) * ya.astype(F32) + _sigmoid(g2.astype(F32)) * yr.astype(F32)

    y_ret, mixed = _mm(rg, wf["w_ret_out"], "nn", None, "y_ret",
                       epi=_Epilogue([(proj, cfg.o_ga), (proj, cfg.o_g2), (y_att, 0)], [BF16, BF16], mix_tiles))
    mo = _mm(mixed, wf["w_o"], "nn", F32, "mixed_out")
    xhat1, h1, rstd1 = _ln1(cfg, x, mo, vec["ln1_g"], vec["ln1_b"])
    u = gather_behind(["w_ffn_up"], lambda jobs: _mm(h1, wf["w_ffn_gate"], "nt", BF16, "ffn_u", jobs=jobs))

    def act_tiles(t, u):
        t, u = t.astype(BF16), u.astype(F32)
        return t, u * _sigmoid(u) * t.astype(F32)

    t, a = gather_behind(["w_ffn_down"], lambda jobs: _mm(h1, wf["w_ffn_up"], "nt", None, "ffn_t", jobs=jobs,
                                                          epi=_Epilogue([(u, 0)], [BF16, BF16], act_tiles)), n_main=2)
    ffn = gather_behind(["w_ple_gate", "w_ple_up"], lambda jobs: _mm(a, wf["w_ffn_down"], "nn", F32, "ffn_down", jobs=jobs))
    hp = _mm(h1, wf["w_ple_gate"], "nn", BF16, "ple_gate")
    pu = _mm(pb, wf["w_ple_up"], "nn", BF16, "ple_up")
    loss, dz2, dz2b, d_pg, d_pu, g_ln2g, g_ln2b = _head(cfg, xhat1, ffn, hp, pu, tgt, vec["ln1_g"], vec["ln1_b"],
                                                        vec["ln2_g"], vec["ln2_b"])

    def act_bwd_tiles(da, u, t):
        da, u = da.astype(BF16).astype(F32), u.astype(F32)
        sg = _sigmoid(u)
        return da * t.astype(F32) * (sg * (1.0 + u * (1.0 - sg))), da * u * sg

    grad(a, dz2b, "w_ffn_down", "g_ffn_down")
    d_u, d_t = scatter_behind(lambda jobs: _mm(dz2b, wf["w_ffn_down"], "nt", None, "d_a", jobs=jobs,
                                               epi=_Epilogue([(u, 0), (t, 0)], [BF16, BF16], act_bwd_tiles)), n_main=2)
    grad(d_u, h1, "w_ffn_gate", "g_ffn_gate")
    dh = scatter_behind(lambda jobs: _mm(d_u, wf["w_ffn_gate"], "nn", F32, "dh_u", jobs=jobs))
    grad(d_t, h1, "w_ffn_up", "g_ffn_up")
    dh = scatter_behind(lambda jobs: _mm(d_t, wf["w_ffn_up"], "nn", F32, "dh_t", add=dh, jobs=jobs))
    dh = _mm(d_pg, wf["w_ple_gate"], "nt", F32, "dh_pg", add=dh)
    grad(h1, d_pg, "w_ple_gate", "g_ple_gate")
    grad(pb, d_pu, "w_ple_up", "g_ple_up")
    dz1, dz1b, g_ln1g, g_ln1b = _ln1_bwd(cfg, dh, dz2, xhat1, rstd1, vec["ln1_g"])
    def mix_bwd_tiles(dm, ga, g2, ya, yr):
        dm = dm.astype(BF16).astype(F32)
        sa, s2 = _sigmoid(ga.astype(F32)), _sigmoid(g2.astype(F32))
        return dm * sa, dm * s2, dm * ya.astype(F32) * sa * (1.0 - sa), dm * yr.astype(F32) * s2 * (1.0 - s2)

    d_ya, d_yr, d_ga, d_g2 = _mm(dz1b, wf["w_o"], "nt", None, "d_mixed",
                                 epi=_Epilogue([(proj, cfg.o_ga), (proj, cfg.o_g2), (y_att, 0), (y_ret, 0)], [BF16] * 4,
                                               mix_bwd_tiles))
    grad(mixed, dz1b, "w_o", "g_o")
    d_oatt = _mm(d_ya, wf["w_attn_out"], "nt", BF16, "d_oatt")
    d_rg = _mm(d_yr, wf["w_ret_out"], "nt", BF16, "d_rg")
    grad(o_att, d_ya, "w_attn_out", "g_attn_out")
    grad(rg, d_yr, "w_ret_out", "g_ret_out")
    d_gr, d_r, g_gn = _ret_gate_bwd(cfg, proj, rn, vec["ret_gn_g"], rstd_r, d_rg)
    d_qr, d_kr, d_vr = _ret_bwd(cfg, proj, d_r, states, cos, sin, lg)
    delta = _attn_delta(cfg, d_oatt, o_att)
    d_qkv = ([], [], [])
    for g, (_, dil) in enumerate(ATT_GROUPS):
        parts = _attn_bwd(cfg, g, *att[g][0], _deinterleave(d_oatt, dil), _deinterleave(lse, dil), _deinterleave(delta, dil),
                          att[g][1])
        for kept, t in zip(d_qkv, parts):
            kept.append(_interleave(t, dil))
    dproj = jnp.concatenate(d_qkv[0] + d_qkv[1] + d_qkv[2] + [d_qr, d_kr, d_vr, d_gr, d_ga, d_g2], axis=1)
    grad(xb, dproj, "w_in", "g_in", host=True)
    grad_x = scatter_behind(lambda jobs: _mm(dproj, wf["w_in"], "nt", F32, "grad_x", add=dz1, add_scale=cfg.alpha, jobs=jobs))
    vgrads = {"ret_gn_g": g_gn, "ln1_g": g_ln1g, "ln1_b": g_ln1b, "ln2_g": g_ln2g, "ln2_b": g_ln2b}
    if dist:
        grads = {n: _sum_share(pairs[n][0], pairs[n][1], got[n], _SHARD_AXIS[n], coords, "rs_" + n + "_share") for n in pairs}
    return loss, grad_x, grads, vgrads


def _step(cfg, x, p, tgt, w, m, v):
    mats = [n for n in _WEIGHTS if n in _SHARD_AXIS]
    ffn_pad = cfg.ffn_shard - cfg.ffn // 4

    def stored(n, t):
        return t[0].T if n in _TRANSPOSED else t[0]

    def given(n, t):
        return (t.T if n in _TRANSPOSED else t)[None]

    def shard_bf16(n):
        ws = stored(n, w[n]).astype(BF16)
        if n in _FFN_HIDDEN_ROWS and ffn_pad:
            ws = jnp.pad(ws, [(0, ffn_pad), (0, 0)])
        return ws

    vec = {n: w[n] for n in _VECTORS}
    coords = jnp.stack([lax.axis_index("x"), lax.axis_index("y"), lax.axis_index("c")]).astype(jnp.int32)
    loss, grad_x, red, vgrads = _local_grads(cfg, x[0], p[0, 0], tgt[0], {n: shard_bf16(n) for n in mats}, vec, coords)
    loss = lax.psum(loss[0, 0], ("x", "y", "c"))
    stacked = jnp.concatenate([vgrads[n] for n in _VECTORS] + [jnp.zeros((3, cfg.d), F32)], axis=0)
    vsum = _vec_all_reduce(stacked, "vec_all_reduce")
    for i, n in enumerate(_VECTORS):
        red[n] = vsum[i:i + 1]
    outs = [[], [], [], []]
    for n in _WEIGHTS:
        if n in _SHARD_AXIS:
            res = _adamw(stored(n, w[n]), red[n], stored(n, m[n]), stored(n, v[n]), "adamw_" + n)
            res = [given(n, t) for t in res]
        else:
            res = _adamw(w[n], red[n], m[n], v[n], "adamw_" + n)
        for kept, t in zip(outs, res):
            kept.append(t)
    return (loss, grad_x[None], *outs[0], *outs[1], *outs[2], *outs[3])


def kernel(x, p, w_in, w_attn_out, w_ret_out, ret_gn_g, w_o, ln1_g, ln1_b, w_ffn_gate, w_ffn_up, w_ffn_down, w_ple_gate, w_ple_up, ln2_g, ln2_b, loss_target, m_w_in, m_w_attn_out, m_w_ret_out, m_ret_gn_g, m_w_o, m_ln1_g, m_ln1_b, m_w_ffn_gate, m_w_ffn_up, m_w_ffn_down, m_w_ple_gate, m_w_ple_up, m_ln2_g, m_ln2_b, v_w_in, v_w_attn_out, v_w_ret_out, v_ret_gn_g, v_w_o, v_ln1_g, v_ln1_b, v_w_ffn_gate, v_w_ffn_up, v_w_ffn_down, v_w_ple_gate, v_w_ple_up, v_ln2_g, v_ln2_b):
    w = dict(zip(_WEIGHTS, (w_in, w_attn_out, w_ret_out, ret_gn_g, w_o, ln1_g, ln1_b, w_ffn_gate, w_ffn_up, w_ffn_down,
                            w_ple_gate, w_ple_up, ln2_g, ln2_b)))
    m = dict(zip(_WEIGHTS, (m_w_in, m_w_attn_out, m_w_ret_out, m_ret_gn_g, m_w_o, m_ln1_g, m_ln1_b, m_w_ffn_gate, m_w_ffn_up,
                            m_w_ffn_down, m_w_ple_gate, m_w_ple_up, m_ln2_g, m_ln2_b)))
    v = dict(zip(_WEIGHTS, (v_w_in, v_w_attn_out, v_w_ret_out, v_ret_gn_g, v_w_o, v_ln1_g, v_ln1_b, v_w_ffn_gate, v_w_ffn_up,
                            v_w_ffn_down, v_w_ple_gate, v_w_ple_up, v_ln2_g, v_ln2_b)))
    return _step(_FULL, x, p, loss_target, w, m, v)
```

```python
import functools
import math

import jax
import jax.numpy as jnp
import numpy as np
from jax import lax
from jax.experimental import pallas as pl
from jax.experimental.pallas import tpu as pltpu

F32 = jnp.float32
BF16 = jnp.bfloat16
MESH = pl.DeviceIdType.MESH
ANY = pl.BlockSpec(memory_space=pl.ANY)

ATT_BLOCK = 128
ATT_GROUPS = ((128, 1), (512, 4), (2048, 16))
LN_EPS = 1e-5
GN_EPS = 1e-6
NEG_INF = -1e30
ROPE_BASE = 10000.0
ADAM_LR, ADAM_B1, ADAM_B2, ADAM_EPS, ADAM_WD, ADAM_STEP = 0.001, 0.9, 0.999, 1e-08, 0.01, 10
VMEM_LIMIT = 56 * 1024 * 1024
STAGE_BYTES = 4 * 1024 * 1024
MM_VMEM_BYTES = 46 * 1024 * 1024 + 512 * 1024


class _Cfg:
    def __init__(self, seq, d, ple, ahd, ahg, rh, rqk, rv, ffn, cw):
        self.seq, self.d, self.ple, self.ahd, self.ahg = seq, d, ple, ahd, ahg
        self.rh, self.rqk, self.rv, self.ffn, self.cw = rh, rqk, rv, ffn, cw
        self.aw = ahg * ahd
        self.att_w = 3 * self.aw
        self.rqk_w = rh * rqk
        self.rv_w = rh * rv
        offs = np.cumsum([0] + [self.att_w] * 3 + [self.rqk_w] * 2 + [self.rv_w] * 2 + [d] * 2)
        (self.o_qa, self.o_ka, self.o_va, self.o_qr, self.o_kr, self.o_vr, self.o_gr, self.o_ga, self.o_g2,
         self.in_w) = [int(v) for v in offs]
        self.alpha = 2.0 ** 0.25
        self.ffn_shard = -(-(ffn // 4) // 128) * 128
        self.ffn_p = 4 * self.ffn_shard
        assert self.rv_w == d and d % cw == 0
        for o in (self.o_gr, self.o_ga, self.o_g2):
            assert o % cw == 0
        assert self.o_qr % rqk == 0 and self.o_kr % rqk == 0 and self.o_vr % rv == 0 and self.o_gr % rv == 0
        assert rqk // 2 % 128 == 0 and seq % (ATT_BLOCK * 16) == 0


_FULL = _Cfg(seq=4096, d=4096, ple=256, ahd=128, ahg=8, rh=8, rqk=256, rv=512, ffn=11008, cw=1024)


def _tile(n, target, q=128):
    t = min(n, target) // q * q
    while t >= q:
        if n % t == 0:
            return t
        t -= q
    return n


def _row_tile(n, row_bytes, target_bytes, q=8):
    best = None
    for t in range(q, n + 1, q):
        if n % t == 0 and (best is None or t * row_bytes <= target_bytes):
            best = t
            if t * row_bytes > target_bytes:
                break
    return n if best is None else best


def _mm_tiles(m, n, k, out_bytes_per_elem, tm=None):
    fixed_tm = tm is not None
    tm, tn = tm if fixed_tm else _tile(m, 1024), _tile(n, 1024)
    tk_most = k if k <= 4096 else _tile(k, 4096)
    tk = tk_most

    def need(tm_, tk_):
        acc = 4 if tk_ < k else 0
        return 4 * (tm_ + tn) * tk_ + tm_ * tn * (out_bytes_per_elem + acc + 4)

    while need(tm, tk) > MM_VMEM_BYTES:
        if tk > 2048:
            tk = _tile(k, tk - 128)
        elif not fixed_tm and tm > 512 and m % 512 == 0:
            tm, tk = 512, tk_most
        elif tk > 256:
            tk = _tile(k, tk - 128)
        else:
            break
    return tm, tn, tk


def _params(sem=None):
    return pltpu.CompilerParams(dimension_semantics=sem, vmem_limit_bytes=VMEM_LIMIT)


def _sigmoid(x):
    return 1.0 / (1.0 + jnp.exp(-x))


class _Job:
    MID_STEP = 0.6

    def __init__(self, inputs, out_shapes, n_sems, copies, relays=False):
        self.inputs, self.out_shapes, self.n_sems, self.copies = list(inputs), list(out_shapes), n_sems, copies
        self.relays = relays

    @staticmethod
    def all_copies(jobs, in_refs, out_refs, sem_refs):
        found, pi, po = {"start": [], "mid_recv": [], "mid_start": [], "end_send": [], "end_recv": []}, 0, 0
        for jb, sems in zip(jobs, sem_refs):
            part = jb.copies(in_refs[pi:pi + len(jb.inputs)], out_refs[po:po + len(jb.out_shapes)], sems)
            for key, cps in part.items():
                found[key] += cps
            pi, po = pi + len(jb.inputs), po + len(jb.out_shapes)
        return found

    @staticmethod
    def run(jobs, in_refs, out_refs, sem_refs, step, n_steps):
        if not jobs:
            return

        @pl.when(step == 0)
        def _():
            for cp in _Job.all_copies(jobs, in_refs, out_refs, sem_refs)["start"]:
                cp.start()

        if any(jb.relays for jb in jobs):
            @pl.when(step == int(_Job.MID_STEP * n_steps))
            def _():
                found = _Job.all_copies(jobs, in_refs, out_refs, sem_refs)
                for cp in found["mid_recv"]:
                    cp.wait_recv()
                for cp in found["mid_start"]:
                    cp.start()

    @staticmethod
    def finish(jobs, in_refs, out_refs, sem_refs):
        found = _Job.all_copies(jobs, in_refs, out_refs, sem_refs)
        for cp in found["end_send"]:
            cp.wait_send()
        for cp in found["end_recv"]:
            cp.wait_recv()


class _Epilogue:
    def __init__(self, extras, out_dtypes, fn):
        self.extras, self.out_dtypes, self.fn = list(extras), list(out_dtypes), fn


def _mm(a, b, mode, out_dtype, name, add=None, add_scale=1.0, jobs=(), epi=None):
    if mode == "nn":
        (m, k), (k2, n) = a.shape, b.shape
    elif mode == "nt":
        (m, k), (n, k2) = a.shape, b.shape
    else:
        (k, m), (k2, n) = a.shape, b.shape
    assert k == k2, (name, a.shape, b.shape)
    has_add = add is not None
    if epi is None:
        epi = _Epilogue([], [out_dtype], lambda r: (r,))
    tile_bytes = sum(2 * jnp.dtype(dt).itemsize for dt in epi.out_dtypes) + (8 if has_add else 0)
    tile_bytes += sum(2 * t.dtype.itemsize for t, _ in epi.extras)
    tm, tn, tk = _mm_tiles(m, math.gcd(n, *[off for _, off in epi.extras]), k, tile_bytes)
    nk = k // tk
    if mode == "tn":
        a_spec = pl.BlockSpec((tk, tm), lambda i, j, kk: (kk, i))
        dims = (((0,), (0,)), ((), ()))
    else:
        a_spec = pl.BlockSpec((tm, tk), lambda i, j, kk: (i, kk))
        dims = (((1,), (1,)), ((), ())) if mode == "nt" else (((1,), (0,)), ((), ()))
    if mode == "nt":
        b_spec = pl.BlockSpec((tn, tk), lambda i, j, kk: (j, kk))
    else:
        b_spec = pl.BlockSpec((tk, tn), lambda i, j, kk: (kk, j))
    o_spec = pl.BlockSpec((tm, tn), lambda i, j, kk: (i, j))
    ni, nj = m // tm, n // tn
    jobs = list(jobs)
    n_extra, n_out = len(epi.extras), len(epi.out_dtypes)
    n_main_in = (3 if has_add else 2) + n_extra
    n_job_in = sum(len(jb.inputs) for jb in jobs)
    n_job_out = sum(len(jb.out_shapes) for jb in jobs)
    n_acc = 1 if nk > 1 else 0

    def body(*refs):
        a_ref, b_ref = refs[0], refs[1]
        add_ref = refs[2] if has_add else None
        extra_refs = refs[n_main_in - n_extra:n_main_in]
        job_in = refs[n_main_in:n_main_in + n_job_in]
        o_refs = refs[n_main_in + n_job_in:n_main_in + n_job_in + n_out]
        job_out = refs[n_main_in + n_job_in + n_out:n_main_in + n_job_in + n_out + n_job_out]
        acc_ref = refs[n_main_in + n_job_in + n_out + n_job_out] if n_acc else None
        job_sems = refs[n_main_in + n_job_in + n_out + n_acc + n_job_out:]
        i, j, kk = pl.program_id(0), pl.program_id(1), pl.program_id(2)

        def finish(r):
            if has_add:
                r = r + add_scale * add_ref[...]
            for o_ref, tile in zip(o_refs, epi.fn(r, *[ref[...] for ref in extra_refs])):
                o_ref[...] = tile.astype(o_ref.dtype)

        step = (i * nj + j) * nk + kk
        _Job.run(jobs, job_in, job_out, job_sems, step, ni * nj * nk)
        part = lax.dot_general(a_ref[...], b_ref[...], dims, preferred_element_type=F32)
        if n_acc:
            @pl.when(kk == 0)
            def _():
                acc_ref[...] = part

            @pl.when(kk > 0)
            def _():
                acc_ref[...] += part

            @pl.when(kk == nk - 1)
            def _():
                finish(acc_ref[...])
        else:
            finish(part)

        if jobs:
            @pl.when(step == ni * nj * nk - 1)
            def _():
                _Job.finish(jobs, job_in, job_out, job_sems)

    job_inputs = [t for jb in jobs for t in jb.inputs]
    extra_specs = [pl.BlockSpec((tm, tn), functools.partial(lambda i, j, kk, shift: (i, j + shift), shift=off // tn))
                   for _, off in epi.extras]
    outs = pl.pallas_call(
        body, name=name, grid=(ni, nj, nk),
        in_specs=[a_spec, b_spec] + ([o_spec] if has_add else []) + extra_specs + [ANY] * n_job_in,
        out_specs=[o_spec] * n_out + [ANY] * n_job_out,
        out_shape=[jax.ShapeDtypeStruct((m, n), dt) for dt in epi.out_dtypes] + [s for jb in jobs for s in jb.out_shapes],
        scratch_shapes=[pltpu.VMEM((tm, tn), F32)] * n_acc + [pltpu.SemaphoreType.DMA((jb.n_sems,)) for jb in jobs],
        compiler_params=_params(("arbitrary",) * 3 if jobs else ("parallel", "parallel", "arbitrary")),
    )(*((a, b, add) if has_add else (a, b)), *[t for t, _ in epi.extras], *job_inputs)
    return outs if jobs or n_out > 1 else outs[0]


def _seg_blocks(cfg):
    return [cfg.seq // ATT_BLOCK // dil for _, dil in ATT_GROUPS]


def _col(tile, h):
    lane = lax.broadcasted_iota(jnp.int32, tile.shape, 1)
    return jnp.sum(jnp.where(lane == h, tile, 0.0), axis=1, keepdims=True)


def _set_col(tile, h, col):
    lane = lax.broadcasted_iota(jnp.int32, tile.shape, 1)
    return jnp.where(lane == h, col, tile)


_NT = (((1,), (1,)), ((), ()))
_TN = (((0,), (0,)), ((), ()))


def _dot(a, b):
    return jnp.dot(a, b, preferred_element_type=F32)


def _dot_nt(a, b):
    return lax.dot_general(a, b, _NT, preferred_element_type=F32)


def _dot_tn(a, b):
    return lax.dot_general(a, b, _TN, preferred_element_type=F32)


def _attn_fwd(cfg, g, q, k, v, cols=(0, 0, 0)):
    nb = cfg.seq // ATT_BLOCK
    seg_mask = _seg_blocks(cfg)[g] - 1
    scale = cfg.ahd ** -0.5
    blk = ATT_BLOCK

    def body(q_ref, kp_ref, kc_ref, vp_ref, vc_ref, o_ref, lse_ref):
        b = pl.program_id(0)
        has_prev = (b & seg_mask) != 0
        qi = lax.broadcasted_iota(jnp.int32, (blk, blk), 0)
        kj = lax.broadcasted_iota(jnp.int32, (blk, blk), 1)
        valid_c = kj <= qi
        valid_p = jnp.logical_and(kj >= qi, has_prev)
        lse = jnp.zeros((blk, cfg.ahg), F32)
        for h in range(cfg.ahg):
            hs = slice(h * cfg.ahd, (h + 1) * cfg.ahd)
            q = q_ref[:, hs]
            s_c = jnp.where(valid_c, _dot_nt(q, kc_ref[:, hs]) * scale, NEG_INF)
            s_p = jnp.where(valid_p, _dot_nt(q, kp_ref[:, hs]) * scale, NEG_INF)
            m = jnp.maximum(jnp.max(s_c, axis=1, keepdims=True), jnp.max(s_p, axis=1, keepdims=True))
            p_c, p_p = jnp.exp(s_c - m), jnp.exp(s_p - m)
            den = jnp.sum(p_c, axis=1, keepdims=True) + jnp.sum(p_p, axis=1, keepdims=True)
            o = _dot(p_c.astype(BF16), vc_ref[:, hs]) + _dot(p_p.astype(BF16), vp_ref[:, hs])
            o_ref[:, hs] = (o / den).astype(o_ref.dtype)
            lse = _set_col(lse, h, m + jnp.log(den))
        lse_ref[...] = lse

    def spec(col, shift):
        if shift:
            return pl.BlockSpec((blk, cfg.aw), lambda b: (jnp.maximum(b - 1, 0), col))
        return pl.BlockSpec((blk, cfg.aw), lambda b: (b, col))

    return pl.pallas_call(
        body, name="attn_fwd_%d" % g, grid=(nb,),
        in_specs=[spec(cols[0], 0), spec(cols[1], 1), spec(cols[1], 0), spec(cols[2], 1), spec(cols[2], 0)],
        out_specs=[spec(0, 0), pl.BlockSpec((blk, cfg.ahg), lambda b: (b, 0))],
        out_shape=[jax.ShapeDtypeStruct((cfg.seq, cfg.aw), BF16), jax.ShapeDtypeStruct((cfg.seq, cfg.ahg), F32)],
        compiler_params=_params(("parallel",)),
    )(q, k, k, v, v)


def _attn_combine(cfg, outs, lses):
    tm = 256

    def body(o0, o1, o2, l0, l1, l2, oa_ref, l_ref):
        l0, l1, l2 = l0[...], l1[...], l2[...]
        m = jnp.maximum(jnp.maximum(l0, l1), l2)
        big = m + jnp.log(jnp.exp(l0 - m) + jnp.exp(l1 - m) + jnp.exp(l2 - m))
        ws = [jnp.exp(l0 - big), jnp.exp(l1 - big), jnp.exp(l2 - big)]
        for h in range(cfg.ahg):
            hs = slice(h * cfg.ahd, (h + 1) * cfg.ahd)
            acc = _col(ws[0], h) * o0[:, hs].astype(F32)
            acc += _col(ws[1], h) * o1[:, hs].astype(F32)
            acc += _col(ws[2], h) * o2[:, hs].astype(F32)
            oa_ref[:, hs] = acc.astype(oa_ref.dtype)
        l_ref[...] = big

    wide = pl.BlockSpec((tm, cfg.aw), lambda i: (i, 0))
    narrow = pl.BlockSpec((tm, cfg.ahg), lambda i: (i, 0))
    return pl.pallas_call(
        body, name="attn_combine", grid=(cfg.seq // tm,), in_specs=[wide] * 3 + [narrow] * 3, out_specs=[wide, narrow],
        out_shape=[jax.ShapeDtypeStruct((cfg.seq, cfg.aw), BF16), jax.ShapeDtypeStruct((cfg.seq, cfg.ahg), F32)],
        compiler_params=_params(("parallel",)),
    )(*outs, *lses)


def _attn_delta(cfg, do, o):
    tm = 256

    def body(do_ref, o_ref, d_ref):
        out = jnp.zeros((tm, cfg.ahg), F32)
        for h in range(cfg.ahg):
            hs = slice(h * cfg.ahd, (h + 1) * cfg.ahd)
            prod = do_ref[:, hs].astype(F32) * o_ref[:, hs].astype(F32)
            out = _set_col(out, h, jnp.sum(prod, axis=1, keepdims=True))
        d_ref[...] = out

    row = pl.BlockSpec((tm, cfg.aw), lambda i: (i, 0))
    return pl.pallas_call(
        body, name="attn_delta", grid=(cfg.seq // tm,), in_specs=[row, row],
        out_specs=pl.BlockSpec((tm, cfg.ahg), lambda i: (i, 0)),
        out_shape=jax.ShapeDtypeStruct((cfg.seq, cfg.ahg), F32), compiler_params=_params(("parallel",)),
    )(do, o)


def _attn_bwd(cfg, g, q, k, v, do, l, d, cols=(0, 0, 0)):
    nb = cfg.seq // ATT_BLOCK
    seg_mask = _seg_blocks(cfg)[g] - 1
    scale = cfg.ahd ** -0.5
    blk = ATT_BLOCK

    def body(q_ref, qn_ref, do_ref, don_ref, l_ref, ln_ref, d_ref, dn_ref, kp_ref, kc_ref, vp_ref, vc_ref,
             dq_ref, dk_ref, dv_ref):
        b = pl.program_id(0)
        has_prev = (b & seg_mask) != 0
        has_next = jnp.logical_and(b + 1 < nb, ((b + 1) & seg_mask) != 0)
        qi = lax.broadcasted_iota(jnp.int32, (blk, blk), 0)
        kj = lax.broadcasted_iota(jnp.int32, (blk, blk), 1)
        valid_c = kj <= qi
        valid_p = jnp.logical_and(kj >= qi, has_prev)
        valid_n = jnp.logical_and(kj >= qi, has_next)
        lse, lse_n, dlt, dlt_n = l_ref[...], ln_ref[...], d_ref[...], dn_ref[...]
        for h in range(cfg.ahg):
            hs = slice(h * cfg.ahd, (h + 1) * cfg.ahd)
            q, qn, do, don = q_ref[:, hs], qn_ref[:, hs], do_ref[:, hs], don_ref[:, hs]
            kc, kp, vc, vp = kc_ref[:, hs], kp_ref[:, hs], vc_ref[:, hs], vp_ref[:, hs]
            lh, lnh, dh, dnh = _col(lse, h), _col(lse_n, h), _col(dlt, h), _col(dlt_n, h)
            p_c = jnp.where(valid_c, jnp.exp(_dot_nt(q, kc) * scale - lh), 0.0)
            p_p = jnp.where(valid_p, jnp.exp(_dot_nt(q, kp) * scale - lh), 0.0)
            ds_c = (p_c * (_dot_nt(do, vc) - dh)).astype(BF16)
            ds_p = (p_p * (_dot_nt(do, vp) - dh)).astype(BF16)
            dq_ref[:, hs] = ((_dot(ds_c, kc) + _dot(ds_p, kp)) * scale).astype(dq_ref.dtype)
            p_n = jnp.where(valid_n, jnp.exp(_dot_nt(qn, kc) * scale - lnh), 0.0)
            ds_n = (p_n * (_dot_nt(don, vc) - dnh)).astype(BF16)
            dk = _dot_tn(ds_c, q) + _dot_tn(ds_n, qn)
            dv = _dot_tn(p_c.astype(BF16), do) + _dot_tn(p_n.astype(BF16), don)
            dk_ref[:, hs] = (dk * scale).astype(dk_ref.dtype)
            dv_ref[:, hs] = dv.astype(dv_ref.dtype)

    def spec(width, shift, col=0):
        if shift == 0:
            return pl.BlockSpec((blk, width), lambda b: (b, col))
        if shift > 0:
            return pl.BlockSpec((blk, width), lambda b: (jnp.minimum(b + 1, nb - 1), col))
        return pl.BlockSpec((blk, width), lambda b: (jnp.maximum(b - 1, 0), col))

    w, hw = cfg.aw, cfg.ahg
    out = jax.ShapeDtypeStruct((cfg.seq, w), BF16)
    return pl.pallas_call(
        body, name="attn_bwd_%d" % g, grid=(nb,),
        in_specs=[spec(w, 0, cols[0]), spec(w, 1, cols[0]), spec(w, 0), spec(w, 1), spec(hw, 0), spec(hw, 1), spec(hw, 0),
                  spec(hw, 1), spec(w, -1, cols[1]), spec(w, 0, cols[1]), spec(w, -1, cols[2]), spec(w, 0, cols[2])],
        out_specs=[spec(w, 0)] * 3, out_shape=[out, out, out], compiler_params=_params(("parallel",)),
    )(q, q, do, do, l, l, d, d, k, k, v, v)


def _deinterleave(t, dil):
    s, w = t.shape
    return t if dil == 1 else t.reshape(s // dil, dil, w).transpose(1, 0, 2).reshape(s, w)


def _interleave(t, dil):
    s, w = t.shape
    return t if dil == 1 else t.reshape(dil, s // dil, w).transpose(1, 0, 2).reshape(s, w)


def _rope_tables(cfg):
    half = cfg.rqk // 2
    pos = jnp.arange(cfg.seq, dtype=F32)
    inv_freq = ROPE_BASE ** (-jnp.arange(half, dtype=F32) / half)
    ang = pos[:, None] * inv_freq[None, :]
    return jnp.cos(ang), jnp.sin(ang)


def _log_gamma(cfg):
    lg = jnp.log(1.0 - 2.0 ** (-5.0 - jnp.arange(cfg.rh, dtype=F32)))
    return jnp.broadcast_to(lg[:, None, None], (cfg.rh, 1, max(cfg.rqk, cfg.rv)))


def _rot(t, c, s, half):
    t1, t2 = t[:, :half], t[:, half:]
    return jnp.concatenate([t1 * c - t2 * s, t1 * s + t2 * c], axis=1)


def _unrot(d, c, s, half):
    d1, d2 = d[:, :half], d[:, half:]
    return jnp.concatenate([d1 * c + d2 * s, d2 * c - d1 * s], axis=1)


def _decays(lg_ref, cfg):
    blk = ATT_BLOCK
    lg_v, lg_k = lg_ref[:, :cfg.rv], lg_ref[:, :cfg.rqk]
    qi = lax.broadcasted_iota(jnp.int32, (blk, blk), 0)
    kj = lax.broadcasted_iota(jnp.int32, (blk, blk), 1)
    diff = (qi - kj).astype(F32)
    intra = jnp.where(diff >= 0, jnp.exp(jnp.maximum(diff, 0.0) * lg_ref[:, :blk]), 0.0)
    idx_v = lax.broadcasted_iota(jnp.int32, (blk, cfg.rv), 0).astype(F32)
    idx_k = lax.broadcasted_iota(jnp.int32, (blk, cfg.rqk), 0).astype(F32)
    cross = jnp.exp((idx_v + 1.0) * lg_v)
    state = jnp.exp((blk - 1.0 - idx_k) * lg_k)
    chunk = jnp.exp(float(blk) * lg_v)
    return intra, cross, state, chunk


def _ret_fwd(cfg, proj, cos, sin, lg):
    blk, nc, half = ATT_BLOCK, cfg.seq // ATT_BLOCK, cfg.rqk // 2
    kscale = cfg.rqk ** -0.5

    def body(q_ref, k_ref, v_ref, cos_ref, sin_ref, lg_ref, rn_ref, rstd_ref, st_ref, state):
        n = pl.program_id(1)

        @pl.when(n == 0)
        def _():
            state[...] = jnp.zeros_like(state)

        intra, cross_d, state_d, chunk_d = _decays(lg_ref, cfg)
        c, s = cos_ref[...], sin_ref[...]
        qb = _rot(q_ref[...].astype(F32), c, s, half).astype(BF16)
        kf = _rot(k_ref[...].astype(F32), c, s, half) * kscale
        vb = v_ref[...]
        prev = state[...]
        st_ref[...] = prev.astype(BF16)
        att = _dot_nt(qb, kf.astype(BF16)) * intra
        out = _dot(att.astype(BF16), vb) + _dot(qb, prev.astype(BF16)) * cross_d
        state[...] = chunk_d * prev + _dot_tn((kf * state_d).astype(BF16), vb)
        mu = jnp.mean(out, axis=1, keepdims=True)
        cen = out - mu
        rstd = lax.rsqrt(jnp.mean(cen * cen, axis=1, keepdims=True) + GN_EPS)
        rn_ref[...] = (cen * rstd).astype(rn_ref.dtype)
        rstd_ref[...] = rstd

    oq, ok, ov = cfg.o_qr // cfg.rqk, cfg.o_kr // cfg.rqk, cfg.o_vr // cfg.rv
    tab = pl.BlockSpec((blk, half), lambda h, n: (n, 0))
    return pl.pallas_call(
        body, name="ret_fwd", grid=(cfg.rh, nc),
        in_specs=[pl.BlockSpec((blk, cfg.rqk), lambda h, n: (n, oq + h)), pl.BlockSpec((blk, cfg.rqk), lambda h, n: (n, ok + h)),
                  pl.BlockSpec((blk, cfg.rv), lambda h, n: (n, ov + h)), tab, tab,
                  pl.BlockSpec((None, 1, lg.shape[2]), lambda h, n: (h, 0, 0))],
        out_specs=[pl.BlockSpec((blk, cfg.rv), lambda h, n: (n, h)), pl.BlockSpec((None, blk, 1), lambda h, n: (h, n, 0)),
                   pl.BlockSpec((None, None, cfg.rqk, cfg.rv), lambda h, n: (h, n, 0, 0))],
        out_shape=[jax.ShapeDtypeStruct((cfg.seq, cfg.rv_w), BF16), jax.ShapeDtypeStruct((cfg.rh, cfg.seq, 1), F32),
                   jax.ShapeDtypeStruct((cfg.rh, nc, cfg.rqk, cfg.rv), BF16)],
        scratch_shapes=[pltpu.VMEM((cfg.rqk, cfg.rv), F32)], compiler_params=_params(("parallel", "arbitrary")),
    )(proj, proj, proj, cos, sin, lg)


def _ret_bwd(cfg, proj, d_r, states, cos, sin, lg):
    blk, nc, half = ATT_BLOCK, cfg.seq // ATT_BLOCK, cfg.rqk // 2
    kscale = cfg.rqk ** -0.5

    def body(q_ref, k_ref, v_ref, do_ref, st_ref, cos_ref, sin_ref, lg_ref, dq_ref, dk_ref, dv_ref, dstate):
        n = pl.program_id(1)

        @pl.when(n == 0)
        def _():
            dstate[...] = jnp.zeros_like(dstate)

        intra, cross_d, state_d, chunk_d = _decays(lg_ref, cfg)
        c, s = cos_ref[...], sin_ref[...]
        qb = _rot(q_ref[...].astype(F32), c, s, half).astype(BF16)
        kf = _rot(k_ref[...].astype(F32), c, s, half) * kscale
        kb, ksb = kf.astype(BF16), (kf * state_d).astype(BF16)
        vb, prev = v_ref[...], st_ref[...]
        do = do_ref[...].astype(F32)
        dob, docb = do.astype(BF16), (do * cross_d).astype(BF16)
        dsb = dstate[...].astype(BF16)
        att = (_dot_nt(qb, kb) * intra).astype(BF16)
        datt = (_dot_nt(dob, vb) * intra).astype(BF16)
        d_q = _dot(datt, kb) + _dot_nt(docb, prev)
        d_k = _dot_tn(datt, qb) + _dot_nt(vb, dsb) * state_d
        d_v = _dot_tn(att, dob) + _dot(ksb, dsb)
        dstate[...] = chunk_d * dstate[...] + _dot_tn(qb, docb)
        dq_ref[...] = _unrot(d_q, c, s, half).astype(dq_ref.dtype)
        dk_ref[...] = _unrot(d_k * kscale, c, s, half).astype(dk_ref.dtype)
        dv_ref[...] = d_v.astype(dv_ref.dtype)

    oq, ok, ov = cfg.o_qr // cfg.rqk, cfg.o_kr // cfg.rqk, cfg.o_vr // cfg.rv
    last = nc - 1
    tab = pl.BlockSpec((blk, half), lambda h, n: (last - n, 0))
    qk_out = pl.BlockSpec((blk, cfg.rqk), lambda h, n: (last - n, h))
    v_out = pl.BlockSpec((blk, cfg.rv), lambda h, n: (last - n, h))
    return pl.pallas_call(
        body, name="ret_bwd", grid=(cfg.rh, nc),
        in_specs=[pl.BlockSpec((blk, cfg.rqk), lambda h, n: (last - n, oq + h)),
                  pl.BlockSpec((blk, cfg.rqk), lambda h, n: (last - n, ok + h)),
                  pl.BlockSpec((blk, cfg.rv), lambda h, n: (last - n, ov + h)), v_out,
                  pl.BlockSpec((None, None, cfg.rqk, cfg.rv), lambda h, n: (h, last - n, 0, 0)), tab, tab,
                  pl.BlockSpec((None, 1, lg.shape[2]), lambda h, n: (h, 0, 0))],
        out_specs=[qk_out, qk_out, v_out],
        out_shape=[jax.ShapeDtypeStruct((cfg.seq, cfg.rqk_w), BF16), jax.ShapeDtypeStruct((cfg.seq, cfg.rqk_w), BF16),
                   jax.ShapeDtypeStruct((cfg.seq, cfg.rv_w), BF16)],
        scratch_shapes=[pltpu.VMEM((cfg.rqk, cfg.rv), F32)], compiler_params=_params(("parallel", "arbitrary")),
    )(proj, proj, proj, d_r, states, cos, sin, lg)


def _ret_gate(cfg, proj, rn, gn_g):
    tm, cw = 256, cfg.cw
    og = cfg.o_gr // cw

    def body(g_ref, rn_ref, w_ref, o_ref):
        g = g_ref[...].astype(F32)
        o_ref[...] = (g * _sigmoid(g) * (rn_ref[...].astype(F32) * w_ref[...])).astype(o_ref.dtype)

    blk = pl.BlockSpec((tm, cw), lambda i, j: (i, j))
    return pl.pallas_call(
        body, name="ret_gate", grid=(cfg.seq // tm, cfg.rv_w // cw),
        in_specs=[pl.BlockSpec((tm, cw), lambda i, j: (i, og + j)), blk, pl.BlockSpec((1, cw), lambda i, j: (0, j))],
        out_specs=blk, out_shape=jax.ShapeDtypeStruct((cfg.seq, cfg.rv_w), BF16),
        compiler_params=_params(("parallel", "parallel")),
    )(proj, rn, gn_g)


def _ret_gate_bwd(cfg, proj, rn, gn_g, rstd, d_rg):
    tm, rv = 256, cfg.rv
    og = cfg.o_gr // rv

    def body(g_ref, rn_ref, w_ref, rstd_ref, drg_ref, dg_ref, dr_ref, gw_ref):
        i = pl.program_id(1)
        g, rn, w = g_ref[...].astype(F32), rn_ref[...].astype(F32), w_ref[...]
        drg = drg_ref[...].astype(F32)
        sg = _sigmoid(g)
        silu = g * sg
        dg_ref[...] = (drg * (rn * w) * (sg * (1.0 + g * (1.0 - sg)))).astype(dg_ref.dtype)
        drn = drg * silu * w
        part = jnp.sum(drg * silu * rn, axis=0, keepdims=True)

        @pl.when(i == 0)
        def _():
            gw_ref[...] = part

        @pl.when(i > 0)
        def _():
            gw_ref[...] += part

        m1 = jnp.mean(drn, axis=1, keepdims=True)
        m2 = jnp.mean(drn * rn, axis=1, keepdims=True)
        dr_ref[...] = (rstd_ref[...] * (drn - m1 - rn * m2)).astype(dr_ref.dtype)

    blk = pl.BlockSpec((tm, rv), lambda h, i: (i, h))
    vec = pl.BlockSpec((1, rv), lambda h, i: (0, h))
    out = jax.ShapeDtypeStruct((cfg.seq, cfg.rv_w), BF16)
    return pl.pallas_call(
        body, name="ret_gate_bwd", grid=(cfg.rh, cfg.seq // tm),
        in_specs=[pl.BlockSpec((tm, rv), lambda h, i: (i, og + h)), blk, vec, pl.BlockSpec((None, tm, 1), lambda h, i: (h, i, 0)), blk],
        out_specs=[blk, blk, vec], out_shape=[out, out, jax.ShapeDtypeStruct((1, cfg.rv_w), F32)],
        compiler_params=_params(("parallel", "arbitrary")),
    )(proj, rn, gn_g, rstd, d_rg)


def _ln1(cfg, x, mo, g, b):
    tm, d = 128, cfg.d

    def body(x_ref, mo_ref, g_ref, b_ref, xh_ref, h_ref, rstd_ref):
        z = cfg.alpha * x_ref[...] + mo_ref[...]
        cen = z - jnp.mean(z, axis=1, keepdims=True)
        rstd = lax.rsqrt(jnp.mean(cen * cen, axis=1, keepdims=True) + LN_EPS)
        xh = cen * rstd
        xh_ref[...] = xh
        h_ref[...] = (xh * g_ref[...] + b_ref[...]).astype(h_ref.dtype)
        rstd_ref[...] = rstd

    row = pl.BlockSpec((tm, d), lambda i: (i, 0))
    vec = pl.BlockSpec((1, d), lambda i: (0, 0))
    col = pl.BlockSpec((tm, 1), lambda i: (i, 0))
    return pl.pallas_call(
        body, name="ln1", grid=(cfg.seq // tm,), in_specs=[row, row, vec, vec], out_specs=[row, row, col],
        out_shape=[jax.ShapeDtypeStruct((cfg.seq, d), F32), jax.ShapeDtypeStruct((cfg.seq, d), BF16),
                   jax.ShapeDtypeStruct((cfg.seq, 1), F32)],
        compiler_params=_params(("parallel",)),
    )(x, mo, g, b)


def _head(cfg, xhat1, ffn, hp, pu, tgt, g1, b1, g2, b2):
    tm, d = 64, cfg.d

    def body(xh_ref, ffn_ref, hp_ref, pu_ref, tgt_ref, g1_ref, b1_ref, g2_ref, b2_ref,
             loss_ref, dzf_ref, dzb_ref, dpg_ref, dpu_ref, gg_ref, gb_ref):
        i = pl.program_id(0)
        h1 = xh_ref[...] * g1_ref[...] + b1_ref[...]
        sg, pu = _sigmoid(hp_ref[...].astype(F32)), pu_ref[...].astype(F32)
        z = cfg.alpha * h1 + ffn_ref[...] + sg * pu
        cen = z - jnp.mean(z, axis=1, keepdims=True)
        rstd = lax.rsqrt(jnp.mean(cen * cen, axis=1, keepdims=True) + LN_EPS)
        xh2 = cen * rstd
        err = xh2 * g2_ref[...] + b2_ref[...] - tgt_ref[...]
        dy = err * (1.0 / d)
        part_l = jnp.sum(jnp.sum(err * err, axis=1, keepdims=True), axis=0, keepdims=True) * (0.5 / d)
        part_g = jnp.sum(dy * xh2, axis=0, keepdims=True)
        part_b = jnp.sum(dy, axis=0, keepdims=True)

        @pl.when(i == 0)
        def _():
            loss_ref[...] = jnp.zeros_like(loss_ref)
            gg_ref[...] = jnp.zeros_like(gg_ref)
            gb_ref[...] = jnp.zeros_like(gb_ref)

        loss_ref[...] += jnp.broadcast_to(part_l, loss_ref.shape)
        gg_ref[...] += part_g
        gb_ref[...] += part_b
        dxh = dy * g2_ref[...]
        m1 = jnp.mean(dxh, axis=1, keepdims=True)
        m2 = jnp.mean(dxh * xh2, axis=1, keepdims=True)
        dz = rstd * (dxh - m1 - xh2 * m2)
        dzf_ref[...] = dz
        dzb_ref[...] = dz.astype(dzb_ref.dtype)
        dpg_ref[...] = (dz * pu * sg * (1.0 - sg)).astype(dpg_ref.dtype)
        dpu_ref[...] = (dz * sg).astype(dpu_ref.dtype)

    row = pl.BlockSpec((tm, d), lambda i: (i, 0))
    vec = pl.BlockSpec((1, d), lambda i: (0, 0))
    bf = jax.ShapeDtypeStruct((cfg.seq, d), BF16)
    vec_out = jax.ShapeDtypeStruct((1, d), F32)
    return pl.pallas_call(
        body, name="head", grid=(cfg.seq // tm,), in_specs=[row] * 5 + [vec] * 4,
        out_specs=[pl.BlockSpec((1, 128), lambda i: (0, 0)), row, row, row, row, vec, vec],
        out_shape=[jax.ShapeDtypeStruct((1, 128), F32), jax.ShapeDtypeStruct((cfg.seq, d), F32), bf, bf, bf, vec_out, vec_out],
        compiler_params=_params(("arbitrary",)),
    )(xhat1, ffn, hp, pu, tgt, g1, b1, g2, b2)


def _ln1_bwd(cfg, dh_mm, dz2, xhat1, rstd1, g1):
    tm, d = 128, cfg.d

    def body(dh_ref, dz_ref, xh_ref, rstd_ref, g_ref, dzf_ref, dzb_ref, gg_ref, gb_ref):
        i = pl.program_id(0)
        dh = cfg.alpha * dz_ref[...] + dh_ref[...]
        xh = xh_ref[...]

        @pl.when(i == 0)
        def _():
            gg_ref[...] = jnp.zeros_like(gg_ref)
            gb_ref[...] = jnp.zeros_like(gb_ref)

        gg_ref[...] += jnp.sum(dh * xh, axis=0, keepdims=True)
        gb_ref[...] += jnp.sum(dh, axis=0, keepdims=True)
        dxh = dh * g_ref[...]
        m1 = jnp.mean(dxh, axis=1, keepdims=True)
        m2 = jnp.mean(dxh * xh, axis=1, keepdims=True)
        dz = rstd_ref[...] * (dxh - m1 - xh * m2)
        dzf_ref[...] = dz
        dzb_ref[...] = dz.astype(dzb_ref.dtype)

    row = pl.BlockSpec((tm, d), lambda i: (i, 0))
    vec = pl.BlockSpec((1, d), lambda i: (0, 0))
    vec_out = jax.ShapeDtypeStruct((1, d), F32)
    return pl.pallas_call(
        body, name="ln1_bwd", grid=(cfg.seq // tm,),
        in_specs=[row, row, row, pl.BlockSpec((tm, 1), lambda i: (i, 0)), vec], out_specs=[row, row, vec, vec],
        out_shape=[jax.ShapeDtypeStruct((cfg.seq, d), F32), jax.ShapeDtypeStruct((cfg.seq, d), BF16), vec_out, vec_out],
        compiler_params=_params(("arbitrary",)),
    )(dh_mm, dz2, xhat1, rstd1, g1)


def _adamw(w, g, m, v, name):
    r, c = w.shape
    assert g.shape[0] >= r and g.shape[1] == c
    tr, tc = _row_tile(r, c * 4, 2 * 1024 * 1024), c
    bc1, bc2 = 1.0 - ADAM_B1 ** ADAM_STEP, 1.0 - ADAM_B2 ** ADAM_STEP

    def body(w_ref, g_ref, m_ref, v_ref, go_ref, d_ref, mo_ref, vo_ref):
        gg = g_ref[...]
        mn = ADAM_B1 * m_ref[...] + (1.0 - ADAM_B1) * gg
        vn = ADAM_B2 * v_ref[...] + (1.0 - ADAM_B2) * (gg * gg)
        go_ref[...] = gg
        d_ref[...] = -ADAM_LR * ((mn / bc1) / (jnp.sqrt(vn / bc2) + ADAM_EPS) + ADAM_WD * w_ref[...])
        mo_ref[...] = mn
        vo_ref[...] = vn

    blk = pl.BlockSpec((tr, tc), lambda i, j: (i, j))
    out = jax.ShapeDtypeStruct((r, c), F32)
    return pl.pallas_call(
        body, name=name, grid=(r // tr, c // tc), in_specs=[blk] * 4, out_specs=[blk] * 4, out_shape=[out] * 4,
        compiler_params=_params(("parallel", "parallel")),
    )(w, g, m, v)


def _place():
    x, y, c = lax.axis_index("x"), lax.axis_index("y"), lax.axis_index("c")
    others = [(1 - x, y), (x, 1 - y), (1 - x, 1 - y)]
    return x, y, c, others


def _all_gather(ws, ax, name, landed=None):
    r, cdim = ws.shape
    full_shape = (4 * r, cdim) if ax == 0 else (r, 4 * cdim)
    n_shard = r if ax == 0 else cdim
    n_half = r // 2
    tr = _row_tile(n_half, cdim * 2, STAGE_BYTES, 16)
    nq = n_half // tr
    slots = 3

    over_ici = landed is None

    def body(*refs):
        ws_ref = refs[0]
        (full_ref, buf, load_sems, write_sems, send_sems, pass_sems, recv_sems, sib_sem, relay_sems,
         diag_sems) = refs[1 if over_ici else 2:]
        x, y, c, others = _place()
        me = 2 * x + y
        sibling = (x, y, 1 - c)

        def region(chip, half, row0, rows):
            if ax == 0:
                return full_ref.at[pl.ds(chip * n_shard + half * n_half + row0, rows), :]
            return full_ref.at[pl.ds(half * n_half + row0, rows), pl.ds(chip * n_shard, n_shard)]

        pending = {}

        def free(slot):
            for cp, remote in pending.pop(slot, []):
                if remote:
                    cp.wait_send()
                else:
                    cp.wait()

        def load(src, slot):
            free(slot)
            cp = pltpu.make_async_copy(src, buf.at[slot], load_sems.at[slot])
            cp.start()
            cp.wait()

        step = 0
        for mine in (True, False):
            half = c if mine else 1 - c
            for q in range(nq):
                slot = step % slots
                load(ws_ref.at[pl.ds(half * n_half + q * tr, tr), :], slot)
                dst = region(me, half, q * tr, tr)
                cp = pltpu.make_async_copy(buf.at[slot], dst, write_sems.at[slot])
                cp.start()
                pending[slot] = [(cp, False)]
                if mine and over_ici:
                    for k, chip in enumerate(others[:2]):
                        cp = pltpu.make_async_remote_copy(
                            src_ref=buf.at[slot], dst_ref=dst, send_sem=send_sems.at[3 * slot + k], recv_sem=recv_sems.at[k],
                            device_id=(*chip, c), device_id_type=MESH)
                        cp.start()
                        pending[slot].append((cp, True))
                step += 1

        def landed_wait(part, sem):
            pltpu.make_async_remote_copy(src_ref=part, dst_ref=part, send_sem=send_sems.at[0], recv_sem=sem,
                                         device_id=(x, y, c), device_id_type=MESH).wait_recv()

        relays = []
        if over_ici:
            quarter = n_half // 2
            for k in (0, 1):
                landed_wait(region(2 * others[k][0] + others[k][1], c, 0, n_half), recv_sems.at[k])
            for k, (row0, to) in enumerate(((0, others[1]), (quarter, others[0]))):
                part = region(2 * others[k][0] + others[k][1], c, row0, quarter)
                cp = pltpu.make_async_remote_copy(src_ref=part, dst_ref=part, send_sem=relay_sems.at[k], recv_sem=diag_sems.at[k],
                                                  device_id=(*to, c), device_id_type=MESH)
                cp.start()
                relays.append(cp)
        for k, chip in enumerate(others):
            j = 2 * chip[0] + chip[1]
            if over_ici and k == 2:
                landed_wait(region(j, c, 0, quarter), diag_sems.at[0])
                landed_wait(region(j, c, quarter, quarter), diag_sems.at[1])
            for q in range(nq):
                slot = step % slots
                part = region(j, c, q * tr, tr)
                load(part, slot)
                cp = pltpu.make_async_remote_copy(src_ref=buf.at[slot], dst_ref=part, send_sem=pass_sems.at[slot],
                                                  recv_sem=sib_sem, device_id=sibling, device_id_type=MESH)
                cp.start()
                pending[slot] = [(cp, True)]
                step += 1
        for slot in list(pending):
            free(slot)
        for cp in relays:
            cp.wait_send()
        if ax == 0:
            three = full_ref.at[pl.ds(0, 3 * n_half), :]
        else:
            three = full_ref.at[pl.ds(0, n_half), pl.ds(0, 3 * n_shard)]
        pltpu.make_async_remote_copy(src_ref=three, dst_ref=three, send_sem=send_sems.at[0], recv_sem=sib_sem,
                                     device_id=(x, y, c), device_id_type=MESH).wait_recv()

    return pl.pallas_call(
        body, name=name, in_specs=[ANY] if over_ici else [ANY, ANY], out_specs=ANY,
        out_shape=jax.ShapeDtypeStruct(full_shape, ws.dtype), input_output_aliases={} if over_ici else {1: 0},
        scratch_shapes=[pltpu.VMEM((slots, tr, cdim), ws.dtype), pltpu.SemaphoreType.DMA((slots,)),
                        pltpu.SemaphoreType.DMA((slots,)), pltpu.SemaphoreType.DMA((3 * slots,)),
                        pltpu.SemaphoreType.DMA((slots,)), pltpu.SemaphoreType.DMA((3,)), pltpu.SemaphoreType.DMA,
                        pltpu.SemaphoreType.DMA((2,)), pltpu.SemaphoreType.DMA((2,))],
        compiler_params=pltpu.CompilerParams(vmem_limit_bytes=VMEM_LIMIT),
    )(*((ws,) if over_ici else (ws, landed)))


def _gather_job(ws, ax):
    r, cdim = ws.shape
    full_shape = (4 * r, cdim) if ax == 0 else (r, 4 * cdim)
    n_shard = r if ax == 0 else cdim
    n_half = r // 2
    quarter = n_half // 2

    def copies(ins, outs, sems):
        (ws_ref,), (full_ref,) = ins, outs
        x, y, c, others = _place()
        me = 2 * x + y

        def region(chip, row0, rows):
            if ax == 0:
                return full_ref.at[pl.ds(chip * n_shard + c * n_half + row0, rows), :]
            return full_ref.at[pl.ds(c * n_half + row0, rows), pl.ds(chip * n_shard, n_shard)]

        def remote(src, dst, send, recv, chip):
            return pltpu.make_async_remote_copy(src_ref=src, dst_ref=dst, send_sem=sems.at[send], recv_sem=sems.at[recv],
                                                device_id=(*chip, c), device_id_type=MESH)

        near = [2 * chip[0] + chip[1] for chip in others[:2]]
        far = 2 * others[2][0] + others[2][1]
        direct = [remote(ws_ref.at[pl.ds(c * n_half, n_half), :], region(me, 0, n_half), k, 2 + k, others[k]) for k in (0, 1)]
        landed = [remote(region(near[k], 0, n_half), region(near[k], 0, n_half), k, 2 + k, others[k]) for k in (0, 1)]
        relay = [remote(region(near[0], 0, quarter), region(near[0], 0, quarter), 4, 6, others[1]),
                 remote(region(near[1], quarter, quarter), region(near[1], quarter, quarter), 5, 7, others[0])]
        last = [remote(region(far, 0, quarter), region(far, 0, quarter), 4, 6, others[1]),
                remote(region(far, quarter, quarter), region(far, quarter, quarter), 5, 7, others[0])]
        return {"start": direct, "mid_recv": landed, "mid_start": relay, "end_send": direct + relay, "end_recv": last}

    return _Job([ws], [jax.ShapeDtypeStruct(full_shape, ws.dtype)], 8, copies, relays=True)


def _scatter_job(pair):
    _, n_half, ncol = pair.shape

    def copies(ins, outs, sems):
        (p_ref,), (got_ref,) = ins, outs
        x, y, c, others = _place()
        cps = [pltpu.make_async_remote_copy(
            src_ref=p_ref.at[k], dst_ref=got_ref.at[k], send_sem=sems.at[k], recv_sem=sems.at[3 + k],
            device_id=(*chip, c), device_id_type=MESH) for k, chip in enumerate(others)]
        return {"start": cps, "end_send": cps, "end_recv": cps}

    return _Job([pair], [jax.ShapeDtypeStruct((3, n_half, ncol), pair.dtype)], 6, copies)


def _shard_dims(shape, ax):
    rf, cf = shape
    n_shard = (rf if ax == 0 else cf) // 4
    return n_shard, (n_shard if ax == 0 else rf) // 2, (cf if ax == 0 else n_shard)


def _mm_tn_pair(a, b, ax, name, jobs=()):
    (k, m), (k2, n) = a.shape, b.shape
    assert k == k2
    n_shard, n_half, _ = _shard_dims((m, n), ax)
    tm = _tile(n_half, 1024)
    if tm < 512:
        tm = n_half
    tm, tn, tk = _mm_tiles(m, n, k, 4, tm=tm)
    ni, nj, nk = m // tm, n // tn, k // tk
    ntiles = ni * nj
    n_acc = 1 if nk > 1 else 0
    assert n_half % tm == 0 and ntiles >= 2
    jobs = list(jobs)
    n_job_in = sum(len(jb.inputs) for jb in jobs)
    n_job_out = sum(len(jb.out_shapes) for jb in jobs)

    def body(*refs):
        a_ref, b_ref = refs[0], refs[1]
        job_in = refs[2:2 + n_job_in]
        own_ref, recv_ref = refs[2 + n_job_in], refs[3 + n_job_in]
        job_out = refs[4 + n_job_in:4 + n_job_in + n_job_out]
        scratch = refs[4 + n_job_in + n_job_out:]
        acc_ref = scratch[0] if n_acc else None
        stage, local_sems, send_sems, recv_sem = scratch[n_acc:n_acc + 4]
        job_sems = scratch[n_acc + 4:]
        i, j, kk = pl.program_id(0), pl.program_id(1), pl.program_id(2)
        x, y, c, _ = _place()
        sibling = (x, y, 1 - c)
        t = i * nj + j

        _Job.run(jobs, job_in, job_out, job_sems, t * nk + kk, ntiles * nk)

        def is_mine(ii):
            row0 = ii * tm
            half = row0 // n_half if ax == 1 else (row0 % n_shard) // n_half
            return half == c

        def copies(tt):
            ii, jj, slot = tt // nj, tt % nj, tt % 2
            where = (pl.ds(ii * tm, tm), pl.ds(jj * tn, tn))
            local = pltpu.make_async_copy(stage.at[slot], own_ref.at[where], local_sems.at[slot])
            remote = pltpu.make_async_remote_copy(src_ref=stage.at[slot], dst_ref=recv_ref.at[where], send_sem=send_sems.at[slot],
                                                  recv_sem=recv_sem, device_id=sibling, device_id_type=MESH)
            return is_mine(ii), local, remote

        def wait_tile(tt):
            mine, local, remote = copies(tt)

            @pl.when(mine)
            def _():
                local.wait()

            @pl.when(jnp.logical_not(mine))
            def _():
                remote.wait_send()

        part = lax.dot_general(a_ref[...], b_ref[...], _TN, preferred_element_type=F32)
        if n_acc:
            @pl.when(kk == 0)
            def _():
                acc_ref[...] = part

            @pl.when(kk > 0)
            def _():
                acc_ref[...] += part

        @pl.when(kk == nk - 1)
        def _():
            @pl.when(t >= 2)
            def _():
                wait_tile(t - 2)

            stage[t % 2] = (acc_ref[...] if n_acc else part).astype(stage.dtype)
            mine, local, remote = copies(t)

            @pl.when(mine)
            def _():
                local.start()

            @pl.when(jnp.logical_not(mine))
            def _():
                remote.start()

            @pl.when(t == ntiles - 1)
            def _():
                wait_tile(t - 1)
                wait_tile(t)
                half_rows = recv_ref.at[pl.ds(0, m // 2), :]
                pltpu.make_async_remote_copy(src_ref=half_rows, dst_ref=half_rows, send_sem=send_sems.at[0], recv_sem=recv_sem,
                                             device_id=sibling, device_id_type=MESH).wait_recv()
                _Job.finish(jobs, job_in, job_out, job_sems)

    out = jax.ShapeDtypeStruct((m, n), BF16)
    return pl.pallas_call(
        body, name=name, grid=(ni, nj, nk),
        in_specs=[pl.BlockSpec((tk, tm), lambda i, j, kk: (kk, i)), pl.BlockSpec((tk, tn), lambda i, j, kk: (kk, j))]
        + [ANY] * n_job_in,
        out_specs=[ANY, ANY] + [ANY] * n_job_out, out_shape=[out, out] + [s for jb in jobs for s in jb.out_shapes],
        scratch_shapes=[pltpu.VMEM((tm, tn), F32)] * n_acc + [pltpu.VMEM((2, tm, tn), BF16), pltpu.SemaphoreType.DMA((2,)),
                                                               pltpu.SemaphoreType.DMA((2,)), pltpu.SemaphoreType.DMA]
        + [pltpu.SemaphoreType.DMA((jb.n_sems,)) for jb in jobs],
        compiler_params=_params(("arbitrary", "arbitrary", "arbitrary")),
    )(a, b, *[t for jb in jobs for t in jb.inputs])


def _other_chip(k, x, y):
    return jnp.where(k == 1, x, 1 - x), jnp.where(k == 0, y, 1 - y)


def _pairsum(own, recv, ax, coords, name):
    n_shard, n_half, ncol = _shard_dims(own.shape, ax)
    tr = _row_tile(n_half, ncol * 2, STAGE_BYTES // 2, 16)
    nr = n_half // tr

    def src_map(k):
        def index(r, co):
            jx, jy = _other_chip(k, co[0], co[1])
            j = 2 * jx + jy
            if ax == 0:
                return ((j * n_shard + co[2] * n_half) // tr + r, 0)
            return (co[2] * nr + r, j)
        return index

    def body(co_ref, o0, r0, o1, r1, o2, r2, out_ref):
        for k, (own_ref, recv_ref) in enumerate(((o0, r0), (o1, r1), (o2, r2))):
            out_ref[k] = (own_ref[...].astype(F32) + recv_ref[...].astype(F32)).astype(out_ref.dtype)

    blks = [pl.BlockSpec((tr, ncol), src_map(k)) for k in range(3) for _ in range(2)]
    return pl.pallas_call(
        body, name=name,
        grid_spec=pltpu.PrefetchScalarGridSpec(
            num_scalar_prefetch=1, grid=(nr,), in_specs=blks, out_specs=pl.BlockSpec((3, tr, ncol), lambda r, co: (0, r, 0))),
        out_shape=jax.ShapeDtypeStruct((3, n_half, ncol), BF16),
        compiler_params=_params(("parallel",)),
    )(coords, own, recv, own, recv, own, recv)


def _sum_share(own, recv, got, ax, coords, name):
    n_shard, n_half, ncol = _shard_dims(own.shape, ax)
    tr = _row_tile(n_half, ncol * 4, STAGE_BYTES, 16)
    nr = n_half // tr

    def src_map(r, co):
        me = 2 * co[0] + co[1]
        if ax == 0:
            return ((me * n_shard + co[2] * n_half) // tr + r, 0)
        return (co[2] * nr + r, me)

    def body(co_ref, own_ref, recv_ref, got_ref, out_ref, sbuf, local_sems, send_sems, recv_sem):
        r = pl.program_id(0)
        x, y, c, _ = _place()

        def copies(slot):
            rows = out_ref.at[pl.ds(c * n_half + r * tr, tr), :]
            local = pltpu.make_async_copy(sbuf.at[slot], rows, local_sems.at[slot])
            remote = pltpu.make_async_remote_copy(src_ref=sbuf.at[slot], dst_ref=rows, send_sem=send_sems.at[slot],
                                                  recv_sem=recv_sem, device_id=(x, y, 1 - c), device_id_type=MESH)
            return local, remote

        def wait_slot(slot):
            local, remote = copies(slot)
            local.wait()
            remote.wait_send()

        @pl.when(r >= 2)
        def _():
            wait_slot(r % 2)

        total = own_ref[...].astype(F32) + recv_ref[...].astype(F32)
        for kk in range(3):
            total = total + got_ref[kk].astype(F32)
        sbuf[r % 2] = total
        local, remote = copies(r % 2)
        local.start()
        remote.start()

        @pl.when(r == nr - 1)
        def _():
            wait_slot(r % 2)
            if nr >= 2:
                wait_slot(1 - r % 2)
            half = out_ref.at[pl.ds(0, n_half), :]
            pltpu.make_async_remote_copy(src_ref=half, dst_ref=half, send_sem=send_sems.at[0], recv_sem=recv_sem,
                                         device_id=(x, y, c), device_id_type=MESH).wait_recv()

    blk = pl.BlockSpec((tr, ncol), src_map)
    return pl.pallas_call(
        body, name=name,
        grid_spec=pltpu.PrefetchScalarGridSpec(
            num_scalar_prefetch=1, grid=(nr,), in_specs=[blk, blk, pl.BlockSpec((3, tr, ncol), lambda r, co: (0, r, 0))],
            out_specs=ANY,
            scratch_shapes=[pltpu.VMEM((2, tr, ncol), F32), pltpu.SemaphoreType.DMA((2,)), pltpu.SemaphoreType.DMA((2,)),
                            pltpu.SemaphoreType.DMA]),
        out_shape=jax.ShapeDtypeStruct((2 * n_half, ncol), F32),
        compiler_params=_params(("arbitrary",)),
    )(coords, own, recv, got)


def _vec_all_reduce(v, name):
    rows, d = v.shape

    def body(v_ref, out_ref, gath_ref, send_sems, recv_sems):
        x, y, c, _ = _place()
        me = 4 * x + 2 * y + c
        gath_ref[me] = v_ref[...]
        flips = [(fx, fy, fc) for fx in (0, 1) for fy in (0, 1) for fc in (0, 1)][1:]
        sends = []
        for k, (fx, fy, fc) in enumerate(flips):
            peer = (1 - x if fx else x, 1 - y if fy else y, 1 - c if fc else c)
            sends.append(pltpu.make_async_remote_copy(
                src_ref=v_ref, dst_ref=gath_ref.at[me], send_sem=send_sems.at[k], recv_sem=recv_sems.at[k],
                device_id=peer, device_id_type=MESH))
        for cp in sends:
            cp.start()
        for cp in sends:
            cp.wait()
        acc = gath_ref[0]
        for dev in range(1, 8):
            acc = acc + gath_ref[dev]
        out_ref[...] = acc

    vm = pl.BlockSpec(memory_space=pltpu.VMEM)
    return pl.pallas_call(
        body, name=name, in_specs=[vm], out_specs=vm, out_shape=jax.ShapeDtypeStruct((rows, d), F32),
        scratch_shapes=[pltpu.VMEM((8, rows, d), F32), pltpu.SemaphoreType.DMA((7,)), pltpu.SemaphoreType.DMA((7,))],
    )(v)


_TRANSPOSED = ("w_ffn_gate", "w_ffn_up")
_SHARD_AXIS = {"w_in": 1, "w_attn_out": 1, "w_ret_out": 0, "w_o": 0, "w_ffn_gate": 0, "w_ffn_up": 0, "w_ffn_down": 0,
               "w_ple_gate": 0, "w_ple_up": 1}
_FFN_HIDDEN_ROWS = ("w_ffn_gate", "w_ffn_up", "w_ffn_down")
_VECTORS = ("ret_gn_g", "ln1_g", "ln1_b", "ln2_g", "ln2_b")
_WEIGHTS = ("w_in", "w_attn_out", "w_ret_out", "ret_gn_g", "w_o", "ln1_g", "ln1_b", "w_ffn_gate", "w_ffn_up", "w_ffn_down",
            "w_ple_gate", "w_ple_up", "ln2_g", "ln2_b")


def _local_grads(cfg, x, p, tgt, w, vec, coords=None):
    dist = coords is not None
    xb, pb = x.astype(BF16), p.astype(BF16)
    cos, sin = _rope_tables(cfg)
    lg = _log_gamma(cfg)
    wf = {"w_in": _all_gather(w["w_in"], _SHARD_AXIS["w_in"], "ag_w_in")} if dist else dict(w)
    grads, pairs, got, waiting = {}, {}, {}, []

    def gather_behind(names, matmul, n_main=1):
        if not dist:
            return matmul(())
        res = matmul([_gather_job(w[n], _SHARD_AXIS[n]) for n in names])
        for n, full in zip(names, res[n_main:]):
            wf[n] = _all_gather(w[n], _SHARD_AXIS[n], "ag_" + n, landed=full)
        return res[0] if n_main == 1 else res[:n_main]

    def scatter_jobs():
        names = list(waiting)
        del waiting[:]
        return [_scatter_job(pairs[n][2]) for n in names], lambda landed: got.update(zip(names, landed))

    def grad(a, b, n, name, host=False):
        if not dist:
            grads[n] = _mm(a, b, "tn", BF16, name)
            return
        jobs, keep = scatter_jobs() if host else ((), None)
        own, recv, *landed = _mm_tn_pair(a, b, _SHARD_AXIS[n], name, jobs=jobs)
        if host:
            keep(landed)
        pairs[n] = (own, recv, _pairsum(own, recv, _SHARD_AXIS[n], coords, "rs_" + n + "_pairsum"))
        waiting.append(n)

    def scatter_behind(matmul, n_main=1):
        if not waiting:
            return matmul(())
        jobs, keep = scatter_jobs()
        res = matmul(jobs)
        keep(res[n_main:])
        return res[0] if n_main == 1 else res[:n_main]

    proj = gather_behind(["w_attn_out", "w_ret_out", "w_o", "w_ffn_gate"],
                         lambda jobs: _mm(xb, wf["w_in"], "nn", BF16, "proj", jobs=jobs))
    aw = cfg.aw
    att = []
    for g, (_, dil) in enumerate(ATT_GROUPS):
        starts = [o + g * aw for o in (cfg.o_qa, cfg.o_ka, cfg.o_va)]
        if dil == 1:
            qkv, cols = (proj, proj, proj), tuple(s // aw for s in starts)
        else:
            qkv, cols = tuple(_deinterleave(proj[:, s:s + aw], dil) for s in starts), (0, 0, 0)
        o_g, lse_g = _attn_fwd(cfg, g, *qkv, cols)
        att.append((qkv, cols, _interleave(o_g, dil), _interleave(lse_g, dil)))
    o_att, lse = _attn_combine(cfg, [t[2] for t in att], [t[3] for t in att])
    y_att = _mm(o_att, wf["w_attn_out"], "nn", BF16, "y_att")
    rn, rstd_r, states = _ret_fwd(cfg, proj, cos, sin, lg)
    rg = _ret_gate(cfg, proj, rn, vec["ret_gn_g"])
    def mix_tiles(yr, ga, g2, ya):
        yr = yr.astype(BF16)
        return yr, _sigmoid(ga.astype(F32)) * ya.astype(F32) + _sigmoid(g2.astype(F32)) * yr.astype(F32)

    y_ret, mixed = _mm(rg, wf["w_ret_out"], "nn", None, "y_ret",
                       epi=_Epilogue([(proj, cfg.o_ga), (proj, cfg.o_g2), (y_att, 0)], [BF16, BF16], mix_tiles))
    mo = _mm(mixed, wf["w_o"], "nn", F32, "mixed_out")
    xhat1, h1, rstd1 = _ln1(cfg, x, mo, vec["ln1_g"], vec["ln1_b"])
    u = gather_behind(["w_ffn_up"], lambda jobs: _mm(h1, wf["w_ffn_gate"], "nt", BF16, "ffn_u", jobs=jobs))

    def act_tiles(t, u):
        t, u = t.astype(BF16), u.astype(F32)
        return t, u * _sigmoid(u) * t.astype(F32)

    t, a = gather_behind(["w_ffn_down"], lambda jobs: _mm(h1, wf["w_ffn_up"], "nt", None, "ffn_t", jobs=jobs,
                                                          epi=_Epilogue([(u, 0)], [BF16, BF16], act_tiles)), n_main=2)
    ffn = gather_behind(["w_ple_gate", "w_ple_up"], lambda jobs: _mm(a, wf["w_ffn_down"], "nn", F32, "ffn_down", jobs=jobs))
    hp = _mm(h1, wf["w_ple_gate"], "nn", BF16, "ple_gate")
    pu = _mm(pb, wf["w_ple_up"], "nn", BF16, "ple_up")
    loss, dz2, dz2b, d_pg, d_pu, g_ln2g, g_ln2b = _head(cfg, xhat1, ffn, hp, pu, tgt, vec["ln1_g"], vec["ln1_b"],
                                                        vec["ln2_g"], vec["ln2_b"])

    def act_bwd_tiles(da, u, t):
        da, u = da.astype(BF16).astype(F32), u.astype(F32)
        sg = _sigmoid(u)
        return da * t.astype(F32) * (sg * (1.0 + u * (1.0 - sg))), da * u * sg

    grad(a, dz2b, "w_ffn_down", "g_ffn_down")
    d_u, d_t = scatter_behind(lambda jobs: _mm(dz2b, wf["w_ffn_down"], "nt", None, "d_a", jobs=jobs,
                                               epi=_Epilogue([(u, 0), (t, 0)], [BF16, BF16], act_bwd_tiles)), n_main=2)
    grad(d_u, h1, "w_ffn_gate", "g_ffn_gate")
    dh = scatter_behind(lambda jobs: _mm(d_u, wf["w_ffn_gate"], "nn", F32, "dh_u", jobs=jobs))
    grad(d_t, h1, "w_ffn_up", "g_ffn_up")
    dh = scatter_behind(lambda jobs: _mm(d_t, wf["w_ffn_up"], "nn", F32, "dh_t", add=dh, jobs=jobs))
    dh = _mm(d_pg, wf["w_ple_gate"], "nt", F32, "dh_pg", add=dh)
    grad(h1, d_pg, "w_ple_gate", "g_ple_gate")
    grad(pb, d_pu, "w_ple_up", "g_ple_up")
    dz1, dz1b, g_ln1g, g_ln1b = _ln1_bwd(cfg, dh, dz2, xhat1, rstd1, vec["ln1_g"])
    def mix_bwd_tiles(dm, ga, g2, ya, yr):
        dm = dm.astype(BF16).astype(F32)
        sa, s2 = _sigmoid(ga.astype(F32)), _sigmoid(g2.astype(F32))
        return dm * sa, dm * s2, dm * ya.astype(F32) * sa * (1.0 - sa), dm * yr.astype(F32) * s2 * (1.0 - s2)

    d_ya, d_yr, d_ga, d_g2 = _mm(dz1b, wf["w_o"], "nt", None, "d_mixed",
                                 epi=_Epilogue([(proj, cfg.o_ga), (proj, cfg.o_g2), (y_att, 0), (y_ret, 0)], [BF16] * 4,
                                               mix_bwd_tiles))
    grad(mixed, dz1b, "w_o", "g_o")
    d_oatt = _mm(d_ya, wf["w_attn_out"], "nt", BF16, "d_oatt")
    d_rg = _mm(d_yr, wf["w_ret_out"], "nt", BF16, "d_rg")
    grad(o_att, d_ya, "w_attn_out", "g_attn_out")
    grad(rg, d_yr, "w_ret_out", "g_ret_out")
    d_gr, d_r, g_gn = _ret_gate_bwd(cfg, proj, rn, vec["ret_gn_g"], rstd_r, d_rg)
    d_qr, d_kr, d_vr = _ret_bwd(cfg, proj, d_r, states, cos, sin, lg)
    delta = _attn_delta(cfg, d_oatt, o_att)
    d_qkv = ([], [], [])
    for g, (_, dil) in enumerate(ATT_GROUPS):
        parts = _attn_bwd(cfg, g, *att[g][0], _deinterleave(d_oatt, dil), _deinterleave(lse, dil), _deinterleave(delta, dil),
                          att[g][1])
        for kept, t in zip(d_qkv, parts):
            kept.append(_interleave(t, dil))
    dproj = jnp.concatenate(d_qkv[0] + d_qkv[1] + d_qkv[2] + [d_qr, d_kr, d_vr, d_gr, d_ga, d_g2], axis=1)
    grad(xb, dproj, "w_in", "g_in", host=True)
    grad_x = scatter_behind(lambda jobs: _mm(dproj, wf["w_in"], "nt", F32, "grad_x", add=dz1, add_scale=cfg.alpha, jobs=jobs))
    vgrads = {"ret_gn_g": g_gn, "ln1_g": g_ln1g, "ln1_b": g_ln1b, "ln2_g": g_ln2g, "ln2_b": g_ln2b}
    if dist:
        grads = {n: _sum_share(pairs[n][0], pairs[n][1], got[n], _SHARD_AXIS[n], coords, "rs_" + n + "_share") for n in pairs}
    return loss, grad_x, grads, vgrads


def _step(cfg, x, p, tgt, w, m, v):
    mats = [n for n in _WEIGHTS if n in _SHARD_AXIS]
    ffn_pad = cfg.ffn_shard - cfg.ffn // 4

    def stored(n, t):
        return t[0].T if n in _TRANSPOSED else t[0]

    def given(n, t):
        return (t.T if n in _TRANSPOSED else t)[None]

    def shard_bf16(n):
        ws = stored(n, w[n]).astype(BF16)
        if n in _FFN_HIDDEN_ROWS and ffn_pad:
            ws = jnp.pad(ws, [(0, ffn_pad), (0, 0)])
        return ws

    vec = {n: w[n] for n in _VECTORS}
    coords = jnp.stack([lax.axis_index("x"), lax.axis_index("y"), lax.axis_index("c")]).astype(jnp.int32)
    loss, grad_x, red, vgrads = _local_grads(cfg, x[0], p[0, 0], tgt[0], {n: shard_bf16(n) for n in mats}, vec, coords)
    loss = lax.psum(loss[0, 0], ("x", "y", "c"))
    stacked = jnp.concatenate([vgrads[n] for n in _VECTORS] + [jnp.zeros((3, cfg.d), F32)], axis=0)
    vsum = _vec_all_reduce(stacked, "vec_all_reduce")
    for i, n in enumerate(_VECTORS):
        red[n] = vsum[i:i + 1]
    outs = [[], [], [], []]
    for n in _WEIGHTS:
        if n in _SHARD_AXIS:
            res = _adamw(stored(n, w[n]), red[n], stored(n, m[n]), stored(n, v[n]), "adamw_" + n)
            res = [given(n, t) for t in res]
        else:
            res = _adamw(w[n], red[n], m[n], v[n], "adamw_" + n)
        for kept, t in zip(outs, res):
            kept.append(t)
    return (loss, grad_x[None], *outs[0], *outs[1], *outs[2], *outs[3])


def kernel(x, p, w_in, w_attn_out, w_ret_out, ret_gn_g, w_o, ln1_g, ln1_b, w_ffn_gate, w_ffn_up, w_ffn_down, w_ple_gate, w_ple_up, ln2_g, ln2_b, loss_target, m_w_in, m_w_attn_out, m_w_ret_out, m_ret_gn_g, m_w_o, m_ln1_g, m_ln1_b, m_w_ffn_gate, m_w_ffn_up, m_w_ffn_down, m_w_ple_gate, m_w_ple_up, m_ln2_g, m_ln2_b, v_w_in, v_w_attn_out, v_w_ret_out, v_ret_gn_g, v_w_o, v_ln1_g, v_ln1_b, v_w_ffn_gate, v_w_ffn_up, v_w_ffn_down, v_w_ple_gate, v_w_ple_up, v_ln2_g, v_ln2_b):
    w = dict(zip(_WEIGHTS, (w_in, w_attn_out, w_ret_out, ret_gn_g, w_o, ln1_g, ln1_b, w_ffn_gate, w_ffn_up, w_ffn_down,
                            w_ple_gate, w_ple_up, ln2_g, ln2_b)))
    m = dict(zip(_WEIGHTS, (m_w_in, m_w_attn_out, m_w_ret_out, m_ret_gn_g, m_w_o, m_ln1_g, m_ln1_b, m_w_ffn_gate, m_w_ffn_up,
                            m_w_ffn_down, m_w_ple_gate, m_w_ple_up, m_ln2_g, m_ln2_b)))
    v = dict(zip(_WEIGHTS, (v_w_in, v_w_attn_out, v_w_ret_out, v_ret_gn_g, v_w_o, v_ln1_g, v_ln1_b, v_w_ffn_gate, v_w_ffn_up,
                            v_w_ffn_down, v_w_ple_gate, v_w_ple_up, v_ln2_g, v_ln2_b)))
    return _step(_FULL, x, p, loss_target, w, m, v)
```

```python
import functools
import math

import jax
import jax.numpy as jnp
import numpy as np
from jax import lax
from jax.experimental import pallas as pl
from jax.experimental.pallas import tpu as pltpu

F32 = jnp.float32
BF16 = jnp.bfloat16
MESH = pl.DeviceIdType.MESH
ANY = pl.BlockSpec(memory_space=pl.ANY)

ATT_BLOCK = 128
ATT_GROUPS = ((128, 1), (512, 4), (2048, 16))
LN_EPS = 1e-5
GN_EPS = 1e-6
NEG_INF = -1e30
ROPE_BASE = 10000.0
ADAM_LR, ADAM_B1, ADAM_B2, ADAM_EPS, ADAM_WD, ADAM_STEP = 0.001, 0.9, 0.999, 1e-08, 0.01, 10
VMEM_LIMIT = 56 * 1024 * 1024
STAGE_BYTES = 4 * 1024 * 1024
MM_VMEM_BYTES = 46 * 1024 * 1024 + 512 * 1024


class _Cfg:
    def __init__(self, seq, d, ple, ahd, ahg, rh, rqk, rv, ffn, cw):
        self.seq, self.d, self.ple, self.ahd, self.ahg = seq, d, ple, ahd, ahg
        self.rh, self.rqk, self.rv, self.ffn, self.cw = rh, rqk, rv, ffn, cw
        self.aw = ahg * ahd
        self.att_w = 3 * self.aw
        self.rqk_w = rh * rqk
        self.rv_w = rh * rv
        offs = np.cumsum([0] + [self.att_w] * 3 + [self.rqk_w] * 2 + [self.rv_w] * 2 + [d] * 2)
        (self.o_qa, self.o_ka, self.o_va, self.o_qr, self.o_kr, self.o_vr, self.o_gr, self.o_ga, self.o_g2,
         self.in_w) = [int(v) for v in offs]
        self.alpha = 2.0 ** 0.25
        self.ffn_shard = -(-(ffn // 4) // 128) * 128
        self.ffn_p = 4 * self.ffn_shard
        assert self.rv_w == d and d % cw == 0
        for o in (self.o_gr, self.o_ga, self.o_g2):
            assert o % cw == 0
        assert self.o_qr % rqk == 0 and self.o_kr % rqk == 0 and self.o_vr % rv == 0 and self.o_gr % rv == 0
        assert rqk // 2 % 128 == 0 and seq % (ATT_BLOCK * 16) == 0


_FULL = _Cfg(seq=4096, d=4096, ple=256, ahd=128, ahg=8, rh=8, rqk=256, rv=512, ffn=11008, cw=1024)


def _tile(n, target, q=128):
    t = min(n, target) // q * q
    while t >= q:
        if n % t == 0:
            return t
        t -= q
    return n


def _row_tile(n, row_bytes, target_bytes, q=8):
    best = None
    for t in range(q, n + 1, q):
        if n % t == 0 and (best is None or t * row_bytes <= target_bytes):
            best = t
            if t * row_bytes > target_bytes:
                break
    return n if best is None else best


def _mm_tiles(m, n, k, out_bytes_per_elem, tm=None):
    fixed_tm = tm is not None
    tm, tn = tm if fixed_tm else _tile(m, 1024), _tile(n, 1024)
    tk_most = k if k <= 4096 else _tile(k, 4096)
    tk = tk_most

    def need(tm_, tk_):
        acc = 4 if tk_ < k else 0
        return 4 * (tm_ + tn) * tk_ + tm_ * tn * (out_bytes_per_elem + acc + 4)

    while need(tm, tk) > MM_VMEM_BYTES:
        if tk > 2048:
            tk = _tile(k, tk - 128)
        elif not fixed_tm and tm > 512 and m % 512 == 0:
            tm, tk = 512, tk_most
        elif tk > 256:
            tk = _tile(k, tk - 128)
        else:
            break
    return tm, tn, tk


def _params(sem=None):
    return pltpu.CompilerParams(dimension_semantics=sem, vmem_limit_bytes=VMEM_LIMIT)


def _sigmoid(x):
    return 1.0 / (1.0 + jnp.exp(-x))


class _Job:
    MID_STEP = 0.6

    def __init__(self, inputs, out_shapes, n_sems, copies, relays=False):
        self.inputs, self.out_shapes, self.n_sems, self.copies = list(inputs), list(out_shapes), n_sems, copies
        self.relays = relays

    @staticmethod
    def all_copies(jobs, in_refs, out_refs, sem_refs):
        found, pi, po = {"start": [], "mid_recv": [], "mid_start": [], "end_send": [], "end_recv": []}, 0, 0
        for jb, sems in zip(jobs, sem_refs):
            part = jb.copies(in_refs[pi:pi + len(jb.inputs)], out_refs[po:po + len(jb.out_shapes)], sems)
            for key, cps in part.items():
                found[key] += cps
            pi, po = pi + len(jb.inputs), po + len(jb.out_shapes)
        return found

    @staticmethod
    def run(jobs, in_refs, out_refs, sem_refs, step, n_steps):
        if not jobs:
            return

        @pl.when(step == 0)
        def _():
            for cp in _Job.all_copies(jobs, in_refs, out_refs, sem_refs)["start"]:
                cp.start()

        if any(jb.relays for jb in jobs):
            @pl.when(step == int(_Job.MID_STEP * n_steps))
            def _():
                found = _Job.all_copies(jobs, in_refs, out_refs, sem_refs)
                for cp in found["mid_recv"]:
                    cp.wait_recv()
                for cp in found["mid_start"]:
                    cp.start()

    @staticmethod
    def finish(jobs, in_refs, out_refs, sem_refs):
        found = _Job.all_copies(jobs, in_refs, out_refs, sem_refs)
        for cp in found["end_send"]:
            cp.wait_send()
        for cp in found["end_recv"]:
            cp.wait_recv()


class _Epilogue:
    def __init__(self, extras, out_dtypes, fn):
        self.extras, self.out_dtypes, self.fn = list(extras), list(out_dtypes), fn


def _mm(a, b, mode, out_dtype, name, add=None, add_scale=1.0, jobs=(), epi=None):
    if mode == "nn":
        (m, k), (k2, n) = a.shape, b.shape
    elif mode == "nt":
        (m, k), (n, k2) = a.shape, b.shape
    else:
        (k, m), (k2, n) = a.shape, b.shape
    assert k == k2, (name, a.shape, b.shape)
    has_add = add is not None
    if epi is None:
        epi = _Epilogue([], [out_dtype], lambda r: (r,))
    tile_bytes = sum(2 * jnp.dtype(dt).itemsize for dt in epi.out_dtypes) + (8 if has_add else 0)
    tile_bytes += sum(2 * t.dtype.itemsize for t, _ in epi.extras)
    tm, tn, tk = _mm_tiles(m, math.gcd(n, *[off for _, off in epi.extras]), k, tile_bytes)
    nk = k // tk
    if mode == "tn":
        a_spec = pl.BlockSpec((tk, tm), lambda i, j, kk: (kk, i))
        dims = (((0,), (0,)), ((), ()))
    else:
        a_spec = pl.BlockSpec((tm, tk), lambda i, j, kk: (i, kk))
        dims = (((1,), (1,)), ((), ())) if mode == "nt" else (((1,), (0,)), ((), ()))
    if mode == "nt":
        b_spec = pl.BlockSpec((tn, tk), lambda i, j, kk: (j, kk))
    else:
        b_spec = pl.BlockSpec((tk, tn), lambda i, j, kk: (kk, j))
    o_spec = pl.BlockSpec((tm, tn), lambda i, j, kk: (i, j))
    ni, nj = m // tm, n // tn
    jobs = list(jobs)
    n_extra, n_out = len(epi.extras), len(epi.out_dtypes)
    n_main_in = (3 if has_add else 2) + n_extra
    n_job_in = sum(len(jb.inputs) for jb in jobs)
    n_job_out = sum(len(jb.out_shapes) for jb in jobs)
    n_acc = 1 if nk > 1 else 0

    def body(*refs):
        a_ref, b_ref = refs[0], refs[1]
        add_ref = refs[2] if has_add else None
        extra_refs = refs[n_main_in - n_extra:n_main_in]
        job_in = refs[n_main_in:n_main_in + n_job_in]
        o_refs = refs[n_main_in + n_job_in:n_main_in + n_job_in + n_out]
        job_out = refs[n_main_in + n_job_in + n_out:n_main_in + n_job_in + n_out + n_job_out]
        acc_ref = refs[n_main_in + n_job_in + n_out + n_job_out] if n_acc else None
        job_sems = refs[n_main_in + n_job_in + n_out + n_acc + n_job_out:]
        i, j, kk = pl.program_id(0), pl.program_id(1), pl.program_id(2)

        def finish(r):
            if has_add:
                r = r + add_scale * add_ref[...]
            for o_ref, tile in zip(o_refs, epi.fn(r, *[ref[...] for ref in extra_refs])):
                o_ref[...] = tile.astype(o_ref.dtype)

        step = (i * nj + j) * nk + kk
        _Job.run(jobs, job_in, job_out, job_sems, step, ni * nj * nk)
        part = lax.dot_general(a_ref[...], b_ref[...], dims, preferred_element_type=F32)
        if n_acc:
            @pl.when(kk == 0)
            def _():
                acc_ref[...] = part

            @pl.when(kk > 0)
            def _():
                acc_ref[...] += part

            @pl.when(kk == nk - 1)
            def _():
                finish(acc_ref[...])
        else:
            finish(part)

        if jobs:
            @pl.when(step == ni * nj * nk - 1)
            def _():
                _Job.finish(jobs, job_in, job_out, job_sems)

    job_inputs = [t for jb in jobs for t in jb.inputs]
    extra_specs = [pl.BlockSpec((tm, tn), functools.partial(lambda i, j, kk, shift: (i, j + shift), shift=off // tn))
                   for _, off in epi.extras]
    outs = pl.pallas_call(
        body, name=name, grid=(ni, nj, nk),
        in_specs=[a_spec, b_spec] + ([o_spec] if has_add else []) + extra_specs + [ANY] * n_job_in,
        out_specs=[o_spec] * n_out + [ANY] * n_job_out,
        out_shape=[jax.ShapeDtypeStruct((m, n), dt) for dt in epi.out_dtypes] + [s for jb in jobs for s in jb.out_shapes],
        scratch_shapes=[pltpu.VMEM((tm, tn), F32)] * n_acc + [pltpu.SemaphoreType.DMA((jb.n_sems,)) for jb in jobs],
        compiler_params=_params(("arbitrary",) * 3 if jobs else ("parallel", "parallel", "arbitrary")),
    )(*((a, b, add) if has_add else (a, b)), *[t for t, _ in epi.extras], *job_inputs)
    return outs if jobs or n_out > 1 else outs[0]


def _seg_blocks(cfg):
    return [cfg.seq // ATT_BLOCK // dil for _, dil in ATT_GROUPS]


def _col(tile, h):
    lane = lax.broadcasted_iota(jnp.int32, tile.shape, 1)
    return jnp.sum(jnp.where(lane == h, tile, 0.0), axis=1, keepdims=True)


def _set_col(tile, h, col):
    lane = lax.broadcasted_iota(jnp.int32, tile.shape, 1)
    return jnp.where(lane == h, col, tile)


_NT = (((1,), (1,)), ((), ()))
_TN = (((0,), (0,)), ((), ()))


def _dot(a, b):
    return jnp.dot(a, b, preferred_element_type=F32)


def _dot_nt(a, b):
    return lax.dot_general(a, b, _NT, preferred_element_type=F32)


def _dot_tn(a, b):
    return lax.dot_general(a, b, _TN, preferred_element_type=F32)


def _attn_fwd(cfg, g, q, k, v, cols=(0, 0, 0)):
    nb = cfg.seq // ATT_BLOCK
    seg_mask = _seg_blocks(cfg)[g] - 1
    scale = cfg.ahd ** -0.5
    blk = ATT_BLOCK

    def body(q_ref, kp_ref, kc_ref, vp_ref, vc_ref, o_ref, lse_ref):
        b = pl.program_id(0)
        has_prev = (b & seg_mask) != 0
        qi = lax.broadcasted_iota(jnp.int32, (blk, blk), 0)
        kj = lax.broadcasted_iota(jnp.int32, (blk, blk), 1)
        valid_c = kj <= qi
        valid_p = jnp.logical_and(kj >= qi, has_prev)
        lse = jnp.zeros((blk, cfg.ahg), F32)
        for h in range(cfg.ahg):
            hs = slice(h * cfg.ahd, (h + 1) * cfg.ahd)
            q = q_ref[:, hs]
            s_c = jnp.where(valid_c, _dot_nt(q, kc_ref[:, hs]) * scale, NEG_INF)
            s_p = jnp.where(valid_p, _dot_nt(q, kp_ref[:, hs]) * scale, NEG_INF)
            m = jnp.maximum(jnp.max(s_c, axis=1, keepdims=True), jnp.max(s_p, axis=1, keepdims=True))
            p_c, p_p = jnp.exp(s_c - m), jnp.exp(s_p - m)
            den = jnp.sum(p_c, axis=1, keepdims=True) + jnp.sum(p_p, axis=1, keepdims=True)
            o = _dot(p_c.astype(BF16), vc_ref[:, hs]) + _dot(p_p.astype(BF16), vp_ref[:, hs])
            o_ref[:, hs] = (o / den).astype(o_ref.dtype)
            lse = _set_col(lse, h, m + jnp.log(den))
        lse_ref[...] = lse

    def spec(col, shift):
        if shift:
            return pl.BlockSpec((blk, cfg.aw), lambda b: (jnp.maximum(b - 1, 0), col))
        return pl.BlockSpec((blk, cfg.aw), lambda b: (b, col))

    return pl.pallas_call(
        body, name="attn_fwd_%d" % g, grid=(nb,),
        in_specs=[spec(cols[0], 0), spec(cols[1], 1), spec(cols[1], 0), spec(cols[2], 1), spec(cols[2], 0)],
        out_specs=[spec(0, 0), pl.BlockSpec((blk, cfg.ahg), lambda b: (b, 0))],
        out_shape=[jax.ShapeDtypeStruct((cfg.seq, cfg.aw), BF16), jax.ShapeDtypeStruct((cfg.seq, cfg.ahg), F32)],
        compiler_params=_params(("parallel",)),
    )(q, k, k, v, v)


def _attn_combine(cfg, outs, lses):
    tm = 256

    def body(o0, o1, o2, l0, l1, l2, oa_ref, l_ref):
        l0, l1, l2 = l0[...], l1[...], l2[...]
        m = jnp.maximum(jnp.maximum(l0, l1), l2)
        big = m + jnp.log(jnp.exp(l0 - m) + jnp.exp(l1 - m) + jnp.exp(l2 - m))
        ws = [jnp.exp(l0 - big), jnp.exp(l1 - big), jnp.exp(l2 - big)]
        for h in range(cfg.ahg):
            hs = slice(h * cfg.ahd, (h + 1) * cfg.ahd)
            acc = _col(ws[0], h) * o0[:, hs].astype(F32)
            acc += _col(ws[1], h) * o1[:, hs].astype(F32)
            acc += _col(ws[2], h) * o2[:, hs].astype(F32)
            oa_ref[:, hs] = acc.astype(oa_ref.dtype)
        l_ref[...] = big

    wide = pl.BlockSpec((tm, cfg.aw), lambda i: (i, 0))
    narrow = pl.BlockSpec((tm, cfg.ahg), lambda i: (i, 0))
    return pl.pallas_call(
        body, name="attn_combine", grid=(cfg.seq // tm,), in_specs=[wide] * 3 + [narrow] * 3, out_specs=[wide, narrow],
        out_shape=[jax.ShapeDtypeStruct((cfg.seq, cfg.aw), BF16), jax.ShapeDtypeStruct((cfg.seq, cfg.ahg), F32)],
        compiler_params=_params(("parallel",)),
    )(*outs, *lses)


def _attn_delta(cfg, do, o):
    tm = 256

    def body(do_ref, o_ref, d_ref):
        out = jnp.zeros((tm, cfg.ahg), F32)
        for h in range(cfg.ahg):
            hs = slice(h * cfg.ahd, (h + 1) * cfg.ahd)
            prod = do_ref[:, hs].astype(F32) * o_ref[:, hs].astype(F32)
            out = _set_col(out, h, jnp.sum(prod, axis=1, keepdims=True))
        d_ref[...] = out

    row = pl.BlockSpec((tm, cfg.aw), lambda i: (i, 0))
    return pl.pallas_call(
        body, name="attn_delta", grid=(cfg.seq // tm,), in_specs=[row, row],
        out_specs=pl.BlockSpec((tm, cfg.ahg), lambda i: (i, 0)),
        out_shape=jax.ShapeDtypeStruct((cfg.seq, cfg.ahg), F32), compiler_params=_params(("parallel",)),
    )(do, o)


def _attn_bwd(cfg, g, q, k, v, do, l, d, cols=(0, 0, 0)):
    nb = cfg.seq // ATT_BLOCK
    seg_mask = _seg_blocks(cfg)[g] - 1
    scale = cfg.ahd ** -0.5
    blk = ATT_BLOCK

    def body(q_ref, qn_ref, do_ref, don_ref, l_ref, ln_ref, d_ref, dn_ref, kp_ref, kc_ref, vp_ref, vc_ref,
             dq_ref, dk_ref, dv_ref):
        b = pl.program_id(0)
        has_prev = (b & seg_mask) != 0
        has_next = jnp.logical_and(b + 1 < nb, ((b + 1) & seg_mask) != 0)
        qi = lax.broadcasted_iota(jnp.int32, (blk, blk), 0)
        kj = lax.broadcasted_iota(jnp.int32, (blk, blk), 1)
        valid_c = kj <= qi
        valid_p = jnp.logical_and(kj >= qi, has_prev)
        valid_n = jnp.logical_and(kj >= qi, has_next)
        lse, lse_n, dlt, dlt_n = l_ref[...], ln_ref[...], d_ref[...], dn_ref[...]
        for h in range(cfg.ahg):
            hs = slice(h * cfg.ahd, (h + 1) * cfg.ahd)
            q, qn, do, don = q_ref[:, hs], qn_ref[:, hs], do_ref[:, hs], don_ref[:, hs]
            kc, kp, vc, vp = kc_ref[:, hs], kp_ref[:, hs], vc_ref[:, hs], vp_ref[:, hs]
            lh, lnh, dh, dnh = _col(lse, h), _col(lse_n, h), _col(dlt, h), _col(dlt_n, h)
            p_c = jnp.where(valid_c, jnp.exp(_dot_nt(q, kc) * scale - lh), 0.0)
            p_p = jnp.where(valid_p, jnp.exp(_dot_nt(q, kp) * scale - lh), 0.0)
            ds_c = (p_c * (_dot_nt(do, vc) - dh)).astype(BF16)
            ds_p = (p_p * (_dot_nt(do, vp) - dh)).astype(BF16)
            dq_ref[:, hs] = ((_dot(ds_c, kc) + _dot(ds_p, kp)) * scale).astype(dq_ref.dtype)
            p_n = jnp.where(valid_n, jnp.exp(_dot_nt(qn, kc) * scale - lnh), 0.0)
            ds_n = (p_n * (_dot_nt(don, vc) - dnh)).astype(BF16)
            dk = _dot_tn(ds_c, q) + _dot_tn(ds_n, qn)
            dv = _dot_tn(p_c.astype(BF16), do) + _dot_tn(p_n.astype(BF16), don)
            dk_ref[:, hs] = (dk * scale).astype(dk_ref.dtype)
            dv_ref[:, hs] = dv.astype(dv_ref.dtype)

    def spec(width, shift, col=0):
        if shift == 0:
            return pl.BlockSpec((blk, width), lambda b: (b, col))
        if shift > 0:
            return pl.BlockSpec((blk, width), lambda b: (jnp.minimum(b + 1, nb - 1), col))
        return pl.BlockSpec((blk, width), lambda b: (jnp.maximum(b - 1, 0), col))

    w, hw = cfg.aw, cfg.ahg
    out = jax.ShapeDtypeStruct((cfg.seq, w), BF16)
    return pl.pallas_call(
        body, name="attn_bwd_%d" % g, grid=(nb,),
        in_specs=[spec(w, 0, cols[0]), spec(w, 1, cols[0]), spec(w, 0), spec(w, 1), spec(hw, 0), spec(hw, 1), spec(hw, 0),
                  spec(hw, 1), spec(w, -1, cols[1]), spec(w, 0, cols[1]), spec(w, -1, cols[2]), spec(w, 0, cols[2])],
        out_specs=[spec(w, 0)] * 3, out_shape=[out, out, out], compiler_params=_params(("parallel",)),
    )(q, q, do, do, l, l, d, d, k, k, v, v)


def _deinterleave(t, dil):
    s, w = t.shape
    return t if dil == 1 else t.reshape(s // dil, dil, w).transpose(1, 0, 2).reshape(s, w)


def _interleave(t, dil):
    s, w = t.shape
    return t if dil == 1 else t.reshape(dil, s // dil, w).transpose(1, 0, 2).reshape(s, w)


def _rope_tables(cfg):
    half = cfg.rqk // 2
    pos = jnp.arange(cfg.seq, dtype=F32)
    inv_freq = ROPE_BASE ** (-jnp.arange(half, dtype=F32) / half)
    ang = pos[:, None] * inv_freq[None, :]
    return jnp.cos(ang), jnp.sin(ang)


def _log_gamma(cfg):
    lg = jnp.log(1.0 - 2.0 ** (-5.0 - jnp.arange(cfg.rh, dtype=F32)))
    return jnp.broadcast_to(lg[:, None, None], (cfg.rh, 1, max(cfg.rqk, cfg.rv)))


def _rot(t, c, s, half):
    t1, t2 = t[:, :half], t[:, half:]
    return jnp.concatenate([t1 * c - t2 * s, t1 * s + t2 * c], axis=1)


def _unrot(d, c, s, half):
    d1, d2 = d[:, :half], d[:, half:]
    return jnp.concatenate([d1 * c + d2 * s, d2 * c - d1 * s], axis=1)


def _decays(lg_ref, cfg):
    blk = ATT_BLOCK
    lg_v, lg_k = lg_ref[:, :cfg.rv], lg_ref[:, :cfg.rqk]
    qi = lax.broadcasted_iota(jnp.int32, (blk, blk), 0)
    kj = lax.broadcasted_iota(jnp.int32, (blk, blk), 1)
    diff = (qi - kj).astype(F32)
    intra = jnp.where(diff >= 0, jnp.exp(jnp.maximum(diff, 0.0) * lg_ref[:, :blk]), 0.0)
    idx_v = lax.broadcasted_iota(jnp.int32, (blk, cfg.rv), 0).astype(F32)
    idx_k = lax.broadcasted_iota(jnp.int32, (blk, cfg.rqk), 0).astype(F32)
    cross = jnp.exp((idx_v + 1.0) * lg_v)
    state = jnp.exp((blk - 1.0 - idx_k) * lg_k)
    chunk = jnp.exp(float(blk) * lg_v)
    return intra, cross, state, chunk


def _ret_fwd(cfg, proj, cos, sin, lg, gn_g):
    blk, nc, half = ATT_BLOCK, cfg.seq // ATT_BLOCK, cfg.rqk // 2
    kscale = cfg.rqk ** -0.5

    def body(q_ref, k_ref, v_ref, cos_ref, sin_ref, lg_ref, g_ref, w_ref, rn_ref, rstd_ref, st_ref, rg_ref, state):
        n = pl.program_id(1)

        @pl.when(n == 0)
        def _():
            state[...] = jnp.zeros_like(state)

        intra, cross_d, state_d, chunk_d = _decays(lg_ref, cfg)
        c, s = cos_ref[...], sin_ref[...]
        qb = _rot(q_ref[...].astype(F32), c, s, half).astype(BF16)
        kf = _rot(k_ref[...].astype(F32), c, s, half) * kscale
        vb = v_ref[...]
        prev = state[...]
        st_ref[...] = prev.astype(BF16)
        att = _dot_nt(qb, kf.astype(BF16)) * intra
        out = _dot(att.astype(BF16), vb) + _dot(qb, prev.astype(BF16)) * cross_d
        state[...] = chunk_d * prev + _dot_tn((kf * state_d).astype(BF16), vb)
        mu = jnp.mean(out, axis=1, keepdims=True)
        cen = out - mu
        rstd = lax.rsqrt(jnp.mean(cen * cen, axis=1, keepdims=True) + GN_EPS)
        rn = (cen * rstd).astype(rn_ref.dtype)
        rn_ref[...] = rn
        rstd_ref[...] = rstd
        g = g_ref[...].astype(F32)
        rg_ref[...] = (g * _sigmoid(g) * (rn.astype(F32) * w_ref[...])).astype(rg_ref.dtype)

    oq, ok, ov, og = cfg.o_qr // cfg.rqk, cfg.o_kr // cfg.rqk, cfg.o_vr // cfg.rv, cfg.o_gr // cfg.rv
    tab = pl.BlockSpec((blk, half), lambda h, n: (n, 0))
    wide = pl.BlockSpec((blk, cfg.rv), lambda h, n: (n, h))
    return pl.pallas_call(
        body, name="ret_fwd", grid=(cfg.rh, nc),
        in_specs=[pl.BlockSpec((blk, cfg.rqk), lambda h, n: (n, oq + h)), pl.BlockSpec((blk, cfg.rqk), lambda h, n: (n, ok + h)),
                  pl.BlockSpec((blk, cfg.rv), lambda h, n: (n, ov + h)), tab, tab,
                  pl.BlockSpec((None, 1, lg.shape[2]), lambda h, n: (h, 0, 0)),
                  pl.BlockSpec((blk, cfg.rv), lambda h, n: (n, og + h)), pl.BlockSpec((1, cfg.rv), lambda h, n: (0, h))],
        out_specs=[wide, pl.BlockSpec((None, blk, 1), lambda h, n: (h, n, 0)),
                   pl.BlockSpec((None, None, cfg.rqk, cfg.rv), lambda h, n: (h, n, 0, 0)), wide],
        out_shape=[jax.ShapeDtypeStruct((cfg.seq, cfg.rv_w), BF16), jax.ShapeDtypeStruct((cfg.rh, cfg.seq, 1), F32),
                   jax.ShapeDtypeStruct((cfg.rh, nc, cfg.rqk, cfg.rv), BF16), jax.ShapeDtypeStruct((cfg.seq, cfg.rv_w), BF16)],
        scratch_shapes=[pltpu.VMEM((cfg.rqk, cfg.rv), F32)], compiler_params=_params(("parallel", "arbitrary")),
    )(proj, proj, proj, cos, sin, lg, proj, gn_g)


def _ret_bwd(cfg, proj, d_r, states, cos, sin, lg):
    blk, nc, half = ATT_BLOCK, cfg.seq // ATT_BLOCK, cfg.rqk // 2
    kscale = cfg.rqk ** -0.5

    def body(q_ref, k_ref, v_ref, do_ref, st_ref, cos_ref, sin_ref, lg_ref, dq_ref, dk_ref, dv_ref, dstate):
        n = pl.program_id(1)

        @pl.when(n == 0)
        def _():
            dstate[...] = jnp.zeros_like(dstate)

        intra, cross_d, state_d, chunk_d = _decays(lg_ref, cfg)
        c, s = cos_ref[...], sin_ref[...]
        qb = _rot(q_ref[...].astype(F32), c, s, half).astype(BF16)
        kf = _rot(k_ref[...].astype(F32), c, s, half) * kscale
        kb, ksb = kf.astype(BF16), (kf * state_d).astype(BF16)
        vb, prev = v_ref[...], st_ref[...]
        do = do_ref[...].astype(F32)
        dob, docb = do.astype(BF16), (do * cross_d).astype(BF16)
        dsb = dstate[...].astype(BF16)
        att = (_dot_nt(qb, kb) * intra).astype(BF16)
        datt = (_dot_nt(dob, vb) * intra).astype(BF16)
        d_q = _dot(datt, kb) + _dot_nt(docb, prev)
        d_k = _dot_tn(datt, qb) + _dot_nt(vb, dsb) * state_d
        d_v = _dot_tn(att, dob) + _dot(ksb, dsb)
        dstate[...] = chunk_d * dstate[...] + _dot_tn(qb, docb)
        dq_ref[...] = _unrot(d_q, c, s, half).astype(dq_ref.dtype)
        dk_ref[...] = _unrot(d_k * kscale, c, s, half).astype(dk_ref.dtype)
        dv_ref[...] = d_v.astype(dv_ref.dtype)

    oq, ok, ov = cfg.o_qr // cfg.rqk, cfg.o_kr // cfg.rqk, cfg.o_vr // cfg.rv
    last = nc - 1
    tab = pl.BlockSpec((blk, half), lambda h, n: (last - n, 0))
    qk_out = pl.BlockSpec((blk, cfg.rqk), lambda h, n: (last - n, h))
    v_out = pl.BlockSpec((blk, cfg.rv), lambda h, n: (last - n, h))
    return pl.pallas_call(
        body, name="ret_bwd", grid=(cfg.rh, nc),
        in_specs=[pl.BlockSpec((blk, cfg.rqk), lambda h, n: (last - n, oq + h)),
                  pl.BlockSpec((blk, cfg.rqk), lambda h, n: (last - n, ok + h)),
                  pl.BlockSpec((blk, cfg.rv), lambda h, n: (last - n, ov + h)), v_out,
                  pl.BlockSpec((None, None, cfg.rqk, cfg.rv), lambda h, n: (h, last - n, 0, 0)), tab, tab,
                  pl.BlockSpec((None, 1, lg.shape[2]), lambda h, n: (h, 0, 0))],
        out_specs=[qk_out, qk_out, v_out],
        out_shape=[jax.ShapeDtypeStruct((cfg.seq, cfg.rqk_w), BF16), jax.ShapeDtypeStruct((cfg.seq, cfg.rqk_w), BF16),
                   jax.ShapeDtypeStruct((cfg.seq, cfg.rv_w), BF16)],
        scratch_shapes=[pltpu.VMEM((cfg.rqk, cfg.rv), F32)], compiler_params=_params(("parallel", "arbitrary")),
    )(proj, proj, proj, d_r, states, cos, sin, lg)


def _ret_gate_bwd(cfg, proj, rn, gn_g, rstd, d_rg):
    tm, rv = 256, cfg.rv
    og = cfg.o_gr // rv

    def body(g_ref, rn_ref, w_ref, rstd_ref, drg_ref, dg_ref, dr_ref, gw_ref):
        i = pl.program_id(1)
        g, rn, w = g_ref[...].astype(F32), rn_ref[...].astype(F32), w_ref[...]
        drg = drg_ref[...].astype(F32)
        sg = _sigmoid(g)
        silu = g * sg
        dg_ref[...] = (drg * (rn * w) * (sg * (1.0 + g * (1.0 - sg)))).astype(dg_ref.dtype)
        drn = drg * silu * w
        part = jnp.sum(drg * silu * rn, axis=0, keepdims=True)

        @pl.when(i == 0)
        def _():
            gw_ref[...] = part

        @pl.when(i > 0)
        def _():
            gw_ref[...] += part

        m1 = jnp.mean(drn, axis=1, keepdims=True)
        m2 = jnp.mean(drn * rn, axis=1, keepdims=True)
        dr_ref[...] = (rstd_ref[...] * (drn - m1 - rn * m2)).astype(dr_ref.dtype)

    blk = pl.BlockSpec((tm, rv), lambda h, i: (i, h))
    vec = pl.BlockSpec((1, rv), lambda h, i: (0, h))
    out = jax.ShapeDtypeStruct((cfg.seq, cfg.rv_w), BF16)
    return pl.pallas_call(
        body, name="ret_gate_bwd", grid=(cfg.rh, cfg.seq // tm),
        in_specs=[pl.BlockSpec((tm, rv), lambda h, i: (i, og + h)), blk, vec, pl.BlockSpec((None, tm, 1), lambda h, i: (h, i, 0)), blk],
        out_specs=[blk, blk, vec], out_shape=[out, out, jax.ShapeDtypeStruct((1, cfg.rv_w), F32)],
        compiler_params=_params(("parallel", "arbitrary")),
    )(proj, rn, gn_g, rstd, d_rg)


def _ln1(cfg, x, mo, g, b):
    tm, d = 128, cfg.d

    def body(x_ref, mo_ref, g_ref, b_ref, xh_ref, h_ref, rstd_ref):
        z = cfg.alpha * x_ref[...] + mo_ref[...]
        cen = z - jnp.mean(z, axis=1, keepdims=True)
        rstd = lax.rsqrt(jnp.mean(cen * cen, axis=1, keepdims=True) + LN_EPS)
        xh = cen * rstd
        xh_ref[...] = xh
        h_ref[...] = (xh * g_ref[...] + b_ref[...]).astype(h_ref.dtype)
        rstd_ref[...] = rstd

    row = pl.BlockSpec((tm, d), lambda i: (i, 0))
    vec = pl.BlockSpec((1, d), lambda i: (0, 0))
    col = pl.BlockSpec((tm, 1), lambda i: (i, 0))
    return pl.pallas_call(
        body, name="ln1", grid=(cfg.seq // tm,), in_specs=[row, row, vec, vec], out_specs=[row, row, col],
        out_shape=[jax.ShapeDtypeStruct((cfg.seq, d), F32), jax.ShapeDtypeStruct((cfg.seq, d), BF16),
                   jax.ShapeDtypeStruct((cfg.seq, 1), F32)],
        compiler_params=_params(("parallel",)),
    )(x, mo, g, b)


def _head(cfg, xhat1, ffn, hp, pu, tgt, g1, b1, g2, b2):
    tm, d = 64, cfg.d

    def body(xh_ref, ffn_ref, hp_ref, pu_ref, tgt_ref, g1_ref, b1_ref, g2_ref, b2_ref,
             loss_ref, dzf_ref, dzb_ref, dpg_ref, dpu_ref, gg_ref, gb_ref):
        i = pl.program_id(0)
        h1 = xh_ref[...] * g1_ref[...] + b1_ref[...]
        sg, pu = _sigmoid(hp_ref[...].astype(F32)), pu_ref[...].astype(F32)
        z = cfg.alpha * h1 + ffn_ref[...] + sg * pu
        cen = z - jnp.mean(z, axis=1, keepdims=True)
        rstd = lax.rsqrt(jnp.mean(cen * cen, axis=1, keepdims=True) + LN_EPS)
        xh2 = cen * rstd
        err = xh2 * g2_ref[...] + b2_ref[...] - tgt_ref[...]
        dy = err * (1.0 / d)
        part_l = jnp.sum(jnp.sum(err * err, axis=1, keepdims=True), axis=0, keepdims=True) * (0.5 / d)
        part_g = jnp.sum(dy * xh2, axis=0, keepdims=True)
        part_b = jnp.sum(dy, axis=0, keepdims=True)

        @pl.when(i == 0)
        def _():
            loss_ref[...] = jnp.zeros_like(loss_ref)
            gg_ref[...] = jnp.zeros_like(gg_ref)
            gb_ref[...] = jnp.zeros_like(gb_ref)

        loss_ref[...] += jnp.broadcast_to(part_l, loss_ref.shape)
        gg_ref[...] += part_g
        gb_ref[...] += part_b
        dxh = dy * g2_ref[...]
        m1 = jnp.mean(dxh, axis=1, keepdims=True)
        m2 = jnp.mean(dxh * xh2, axis=1, keepdims=True)
        dz = rstd * (dxh - m1 - xh2 * m2)
        dzf_ref[...] = dz
        dzb_ref[...] = dz.astype(dzb_ref.dtype)
        dpg_ref[...] = (dz * pu * sg * (1.0 - sg)).astype(dpg_ref.dtype)
        dpu_ref[...] = (dz * sg).astype(dpu_ref.dtype)

    row = pl.BlockSpec((tm, d), lambda i: (i, 0))
    vec = pl.BlockSpec((1, d), lambda i: (0, 0))
    bf = jax.ShapeDtypeStruct((cfg.seq, d), BF16)
    vec_out = jax.ShapeDtypeStruct((1, d), F32)
    return pl.pallas_call(
        body, name="head", grid=(cfg.seq // tm,), in_specs=[row] * 5 + [vec] * 4,
        out_specs=[pl.BlockSpec((1, 128), lambda i: (0, 0)), row, row, row, row, vec, vec],
        out_shape=[jax.ShapeDtypeStruct((1, 128), F32), jax.ShapeDtypeStruct((cfg.seq, d), F32), bf, bf, bf, vec_out, vec_out],
        compiler_params=_params(("arbitrary",)),
    )(xhat1, ffn, hp, pu, tgt, g1, b1, g2, b2)


def _ln1_bwd(cfg, dh_mm, dz2, xhat1, rstd1, g1):
    tm, d = 128, cfg.d

    def body(dh_ref, dz_ref, xh_ref, rstd_ref, g_ref, dzf_ref, dzb_ref, gg_ref, gb_ref):
        i = pl.program_id(0)
        dh = cfg.alpha * dz_ref[...] + dh_ref[...]
        xh = xh_ref[...]

        @pl.when(i == 0)
        def _():
            gg_ref[...] = jnp.zeros_like(gg_ref)
            gb_ref[...] = jnp.zeros_like(gb_ref)

        gg_ref[...] += jnp.sum(dh * xh, axis=0, keepdims=True)
        gb_ref[...] += jnp.sum(dh, axis=0, keepdims=True)
        dxh = dh * g_ref[...]
        m1 = jnp.mean(dxh, axis=1, keepdims=True)
        m2 = jnp.mean(dxh * xh, axis=1, keepdims=True)
        dz = rstd_ref[...] * (dxh - m1 - xh * m2)
        dzf_ref[...] = dz
        dzb_ref[...] = dz.astype(dzb_ref.dtype)

    row = pl.BlockSpec((tm, d), lambda i: (i, 0))
    vec = pl.BlockSpec((1, d), lambda i: (0, 0))
    vec_out = jax.ShapeDtypeStruct((1, d), F32)
    return pl.pallas_call(
        body, name="ln1_bwd", grid=(cfg.seq // tm,),
        in_specs=[row, row, row, pl.BlockSpec((tm, 1), lambda i: (i, 0)), vec], out_specs=[row, row, vec, vec],
        out_shape=[jax.ShapeDtypeStruct((cfg.seq, d), F32), jax.ShapeDtypeStruct((cfg.seq, d), BF16), vec_out, vec_out],
        compiler_params=_params(("arbitrary",)),
    )(dh_mm, dz2, xhat1, rstd1, g1)


def _adamw(w, g, m, v, name):
    r, c = w.shape
    assert g.shape[0] >= r and g.shape[1] == c
    tr, tc = _row_tile(r, c * 4, 2 * 1024 * 1024), c
    bc1, bc2 = 1.0 - ADAM_B1 ** ADAM_STEP, 1.0 - ADAM_B2 ** ADAM_STEP

    def body(w_ref, g_ref, m_ref, v_ref, go_ref, d_ref, mo_ref, vo_ref):
        gg = g_ref[...]
        mn = ADAM_B1 * m_ref[...] + (1.0 - ADAM_B1) * gg
        vn = ADAM_B2 * v_ref[...] + (1.0 - ADAM_B2) * (gg * gg)
        go_ref[...] = gg
        d_ref[...] = -ADAM_LR * ((mn / bc1) / (jnp.sqrt(vn / bc2) + ADAM_EPS) + ADAM_WD * w_ref[...])
        mo_ref[...] = mn
        vo_ref[...] = vn

    blk = pl.BlockSpec((tr, tc), lambda i, j: (i, j))
    out = jax.ShapeDtypeStruct((r, c), F32)
    return pl.pallas_call(
        body, name=name, grid=(r // tr, c // tc), in_specs=[blk] * 4, out_specs=[blk] * 4, out_shape=[out] * 4,
        compiler_params=_params(("parallel", "parallel")),
    )(w, g, m, v)


def _place():
    x, y, c = lax.axis_index("x"), lax.axis_index("y"), lax.axis_index("c")
    others = [(1 - x, y), (x, 1 - y), (1 - x, 1 - y)]
    return x, y, c, others


def _all_gather(ws, ax, name, landed=None):
    r, cdim = ws.shape
    full_shape = (4 * r, cdim) if ax == 0 else (r, 4 * cdim)
    n_shard = r if ax == 0 else cdim
    n_half = r // 2
    tr = _row_tile(n_half, cdim * 2, STAGE_BYTES, 16)
    nq = n_half // tr
    slots = 3

    over_ici = landed is None

    def body(*refs):
        ws_ref = refs[0]
        (full_ref, buf, load_sems, write_sems, send_sems, pass_sems, recv_sems, sib_sem, relay_sems,
         diag_sems) = refs[1 if over_ici else 2:]
        x, y, c, others = _place()
        me = 2 * x + y
        sibling = (x, y, 1 - c)

        def region(chip, half, row0, rows):
            if ax == 0:
                return full_ref.at[pl.ds(chip * n_shard + half * n_half + row0, rows), :]
            return full_ref.at[pl.ds(half * n_half + row0, rows), pl.ds(chip * n_shard, n_shard)]

        pending = {}

        def free(slot):
            for cp, remote in pending.pop(slot, []):
                if remote:
                    cp.wait_send()
                else:
                    cp.wait()

        def load(src, slot):
            free(slot)
            cp = pltpu.make_async_copy(src, buf.at[slot], load_sems.at[slot])
            cp.start()
            cp.wait()

        step = 0
        for mine in (True, False):
            half = c if mine else 1 - c
            for q in range(nq):
                slot = step % slots
                load(ws_ref.at[pl.ds(half * n_half + q * tr, tr), :], slot)
                dst = region(me, half, q * tr, tr)
                cp = pltpu.make_async_copy(buf.at[slot], dst, write_sems.at[slot])
                cp.start()
                pending[slot] = [(cp, False)]
                if mine and over_ici:
                    for k, chip in enumerate(others[:2]):
                        cp = pltpu.make_async_remote_copy(
                            src_ref=buf.at[slot], dst_ref=dst, send_sem=send_sems.at[3 * slot + k], recv_sem=recv_sems.at[k],
                            device_id=(*chip, c), device_id_type=MESH)
                        cp.start()
                        pending[slot].append((cp, True))
                step += 1

        def landed_wait(part, sem):
            pltpu.make_async_remote_copy(src_ref=part, dst_ref=part, send_sem=send_sems.at[0], recv_sem=sem,
                                         device_id=(x, y, c), device_id_type=MESH).wait_recv()

        relays = []
        if over_ici:
            quarter = n_half // 2
            for k in (0, 1):
                landed_wait(region(2 * others[k][0] + others[k][1], c, 0, n_half), recv_sems.at[k])
            for k, (row0, to) in enumerate(((0, others[1]), (quarter, others[0]))):
                part = region(2 * others[k][0] + others[k][1], c, row0, quarter)
                cp = pltpu.make_async_remote_copy(src_ref=part, dst_ref=part, send_sem=relay_sems.at[k], recv_sem=diag_sems.at[k],
                                                  device_id=(*to, c), device_id_type=MESH)
                cp.start()
                relays.append(cp)
        for k, chip in enumerate(others):
            j = 2 * chip[0] + chip[1]
            if over_ici and k == 2:
                landed_wait(region(j, c, 0, quarter), diag_sems.at[0])
                landed_wait(region(j, c, quarter, quarter), diag_sems.at[1])
            for q in range(nq):
                slot = step % slots
                part = region(j, c, q * tr, tr)
                load(part, slot)
                cp = pltpu.make_async_remote_copy(src_ref=buf.at[slot], dst_ref=part, send_sem=pass_sems.at[slot],
                                                  recv_sem=sib_sem, device_id=sibling, device_id_type=MESH)
                cp.start()
                pending[slot] = [(cp, True)]
                step += 1
        for slot in list(pending):
            free(slot)
        for cp in relays:
            cp.wait_send()
        if ax == 0:
            three = full_ref.at[pl.ds(0, 3 * n_half), :]
        else:
            three = full_ref.at[pl.ds(0, n_half), pl.ds(0, 3 * n_shard)]
        pltpu.make_async_remote_copy(src_ref=three, dst_ref=three, send_sem=send_sems.at[0], recv_sem=sib_sem,
                                     device_id=(x, y, c), device_id_type=MESH).wait_recv()

    return pl.pallas_call(
        body, name=name, in_specs=[ANY] if over_ici else [ANY, ANY], out_specs=ANY,
        out_shape=jax.ShapeDtypeStruct(full_shape, ws.dtype), input_output_aliases={} if over_ici else {1: 0},
        scratch_shapes=[pltpu.VMEM((slots, tr, cdim), ws.dtype), pltpu.SemaphoreType.DMA((slots,)),
                        pltpu.SemaphoreType.DMA((slots,)), pltpu.SemaphoreType.DMA((3 * slots,)),
                        pltpu.SemaphoreType.DMA((slots,)), pltpu.SemaphoreType.DMA((3,)), pltpu.SemaphoreType.DMA,
                        pltpu.SemaphoreType.DMA((2,)), pltpu.SemaphoreType.DMA((2,))],
        compiler_params=pltpu.CompilerParams(vmem_limit_bytes=VMEM_LIMIT),
    )(*((ws,) if over_ici else (ws, landed)))


def _gather_job(ws, ax):
    r, cdim = ws.shape
    full_shape = (4 * r, cdim) if ax == 0 else (r, 4 * cdim)
    n_shard = r if ax == 0 else cdim
    n_half = r // 2
    quarter = n_half // 2

    def copies(ins, outs, sems):
        (ws_ref,), (full_ref,) = ins, outs
        x, y, c, others = _place()
        me = 2 * x + y

        def region(chip, row0, rows):
            if ax == 0:
                return full_ref.at[pl.ds(chip * n_shard + c * n_half + row0, rows), :]
            return full_ref.at[pl.ds(c * n_half + row0, rows), pl.ds(chip * n_shard, n_shard)]

        def remote(src, dst, send, recv, chip):
            return pltpu.make_async_remote_copy(src_ref=src, dst_ref=dst, send_sem=sems.at[send], recv_sem=sems.at[recv],
                                                device_id=(*chip, c), device_id_type=MESH)

        near = [2 * chip[0] + chip[1] for chip in others[:2]]
        far = 2 * others[2][0] + others[2][1]
        direct = [remote(ws_ref.at[pl.ds(c * n_half, n_half), :], region(me, 0, n_half), k, 2 + k, others[k]) for k in (0, 1)]
        landed = [remote(region(near[k], 0, n_half), region(near[k], 0, n_half), k, 2 + k, others[k]) for k in (0, 1)]
        relay = [remote(region(near[0], 0, quarter), region(near[0], 0, quarter), 4, 6, others[1]),
                 remote(region(near[1], quarter, quarter), region(near[1], quarter, quarter), 5, 7, others[0])]
        last = [remote(region(far, 0, quarter), region(far, 0, quarter), 4, 6, others[1]),
                remote(region(far, quarter, quarter), region(far, quarter, quarter), 5, 7, others[0])]
        return {"start": direct, "mid_recv": landed, "mid_start": relay, "end_send": direct + relay, "end_recv": last}

    return _Job([ws], [jax.ShapeDtypeStruct(full_shape, ws.dtype)], 8, copies, relays=True)


def _scatter_job(pair):
    _, n_half, ncol = pair.shape

    def copies(ins, outs, sems):
        (p_ref,), (got_ref,) = ins, outs
        x, y, c, others = _place()
        cps = [pltpu.make_async_remote_copy(
            src_ref=p_ref.at[k], dst_ref=got_ref.at[k], send_sem=sems.at[k], recv_sem=sems.at[3 + k],
            device_id=(*chip, c), device_id_type=MESH) for k, chip in enumerate(others)]
        return {"start": cps, "end_send": cps, "end_recv": cps}

    return _Job([pair], [jax.ShapeDtypeStruct((3, n_half, ncol), pair.dtype)], 6, copies)


def _shard_dims(shape, ax):
    rf, cf = shape
    n_shard = (rf if ax == 0 else cf) // 4
    return n_shard, (n_shard if ax == 0 else rf) // 2, (cf if ax == 0 else n_shard)


def _mm_tn_pair(a, b, ax, name, jobs=()):
    (k, m), (k2, n) = a.shape, b.shape
    assert k == k2
    n_shard, n_half, _ = _shard_dims((m, n), ax)
    tm = _tile(n_half, 1024)
    if tm < 512:
        tm = n_half
    tm, tn, tk = _mm_tiles(m, n, k, 4, tm=tm)
    ni, nj, nk = m // tm, n // tn, k // tk
    ntiles = ni * nj
    n_acc = 1 if nk > 1 else 0
    assert n_half % tm == 0 and ntiles >= 2
    jobs = list(jobs)
    n_job_in = sum(len(jb.inputs) for jb in jobs)
    n_job_out = sum(len(jb.out_shapes) for jb in jobs)

    def body(*refs):
        a_ref, b_ref = refs[0], refs[1]
        job_in = refs[2:2 + n_job_in]
        own_ref, recv_ref = refs[2 + n_job_in], refs[3 + n_job_in]
        job_out = refs[4 + n_job_in:4 + n_job_in + n_job_out]
        scratch = refs[4 + n_job_in + n_job_out:]
        acc_ref = scratch[0] if n_acc else None
        stage, local_sems, send_sems, recv_sem = scratch[n_acc:n_acc + 4]
        job_sems = scratch[n_acc + 4:]
        i, j, kk = pl.program_id(0), pl.program_id(1), pl.program_id(2)
        x, y, c, _ = _place()
        sibling = (x, y, 1 - c)
        t = i * nj + j

        _Job.run(jobs, job_in, job_out, job_sems, t * nk + kk, ntiles * nk)

        def is_mine(ii):
            row0 = ii * tm
            half = row0 // n_half if ax == 1 else (row0 % n_shard) // n_half
            return half == c

        def copies(tt):
            ii, jj, slot = tt // nj, tt % nj, tt % 2
            where = (pl.ds(ii * tm, tm), pl.ds(jj * tn, tn))
            local = pltpu.make_async_copy(stage.at[slot], own_ref.at[where], local_sems.at[slot])
            remote = pltpu.make_async_remote_copy(src_ref=stage.at[slot], dst_ref=recv_ref.at[where], send_sem=send_sems.at[slot],
                                                  recv_sem=recv_sem, device_id=sibling, device_id_type=MESH)
            return is_mine(ii), local, remote

        def wait_tile(tt):
            mine, local, remote = copies(tt)

            @pl.when(mine)
            def _():
                local.wait()

            @pl.when(jnp.logical_not(mine))
            def _():
                remote.wait_send()

        part = lax.dot_general(a_ref[...], b_ref[...], _TN, preferred_element_type=F32)
        if n_acc:
            @pl.when(kk == 0)
            def _():
                acc_ref[...] = part

            @pl.when(kk > 0)
            def _():
                acc_ref[...] += part

        @pl.when(kk == nk - 1)
        def _():
            @pl.when(t >= 2)
            def _():
                wait_tile(t - 2)

            stage[t % 2] = (acc_ref[...] if n_acc else part).astype(stage.dtype)
            mine, local, remote = copies(t)

            @pl.when(mine)
            def _():
                local.start()

            @pl.when(jnp.logical_not(mine))
            def _():
                remote.start()

            @pl.when(t == ntiles - 1)
            def _():
                wait_tile(t - 1)
                wait_tile(t)
                half_rows = recv_ref.at[pl.ds(0, m // 2), :]
                pltpu.make_async_remote_copy(src_ref=half_rows, dst_ref=half_rows, send_sem=send_sems.at[0], recv_sem=recv_sem,
                                             device_id=sibling, device_id_type=MESH).wait_recv()
                _Job.finish(jobs, job_in, job_out, job_sems)

    out = jax.ShapeDtypeStruct((m, n), BF16)
    return pl.pallas_call(
        body, name=name, grid=(ni, nj, nk),
        in_specs=[pl.BlockSpec((tk, tm), lambda i, j, kk: (kk, i)), pl.BlockSpec((tk, tn), lambda i, j, kk: (kk, j))]
        + [ANY] * n_job_in,
        out_specs=[ANY, ANY] + [ANY] * n_job_out, out_shape=[out, out] + [s for jb in jobs for s in jb.out_shapes],
        scratch_shapes=[pltpu.VMEM((tm, tn), F32)] * n_acc + [pltpu.VMEM((2, tm, tn), BF16), pltpu.SemaphoreType.DMA((2,)),
                                                               pltpu.SemaphoreType.DMA((2,)), pltpu.SemaphoreType.DMA]
        + [pltpu.SemaphoreType.DMA((jb.n_sems,)) for jb in jobs],
        compiler_params=_params(("arbitrary", "arbitrary", "arbitrary")),
    )(a, b, *[t for jb in jobs for t in jb.inputs])


def _other_chip(k, x, y):
    return jnp.where(k == 1, x, 1 - x), jnp.where(k == 0, y, 1 - y)


def _pairsum(own, recv, ax, coords, name):
    n_shard, n_half, ncol = _shard_dims(own.shape, ax)
    tr = _row_tile(n_half, ncol * 2, STAGE_BYTES // 2, 16)
    nr = n_half // tr

    def src_map(k):
        def index(r, co):
            jx, jy = _other_chip(k, co[0], co[1])
            j = 2 * jx + jy
            if ax == 0:
                return ((j * n_shard + co[2] * n_half) // tr + r, 0)
            return (co[2] * nr + r, j)
        return index

    def body(co_ref, o0, r0, o1, r1, o2, r2, out_ref):
        for k, (own_ref, recv_ref) in enumerate(((o0, r0), (o1, r1), (o2, r2))):
            out_ref[k] = (own_ref[...].astype(F32) + recv_ref[...].astype(F32)).astype(out_ref.dtype)

    blks = [pl.BlockSpec((tr, ncol), src_map(k)) for k in range(3) for _ in range(2)]
    return pl.pallas_call(
        body, name=name,
        grid_spec=pltpu.PrefetchScalarGridSpec(
            num_scalar_prefetch=1, grid=(nr,), in_specs=blks, out_specs=pl.BlockSpec((3, tr, ncol), lambda r, co: (0, r, 0))),
        out_shape=jax.ShapeDtypeStruct((3, n_half, ncol), BF16),
        compiler_params=_params(("parallel",)),
    )(coords, own, recv, own, recv, own, recv)


def _sum_share(own, recv, got, ax, coords, name):
    n_shard, n_half, ncol = _shard_dims(own.shape, ax)
    tr = _row_tile(n_half, ncol * 4, STAGE_BYTES, 16)
    nr = n_half // tr

    def src_map(r, co):
        me = 2 * co[0] + co[1]
        if ax == 0:
            return ((me * n_shard + co[2] * n_half) // tr + r, 0)
        return (co[2] * nr + r, me)

    def body(co_ref, own_ref, recv_ref, got_ref, out_ref, sbuf, local_sems, send_sems, recv_sem):
        r = pl.program_id(0)
        x, y, c, _ = _place()

        def copies(slot):
            rows = out_ref.at[pl.ds(c * n_half + r * tr, tr), :]
            local = pltpu.make_async_copy(sbuf.at[slot], rows, local_sems.at[slot])
            remote = pltpu.make_async_remote_copy(src_ref=sbuf.at[slot], dst_ref=rows, send_sem=send_sems.at[slot],
                                                  recv_sem=recv_sem, device_id=(x, y, 1 - c), device_id_type=MESH)
            return local, remote

        def wait_slot(slot):
            local, remote = copies(slot)
            local.wait()
            remote.wait_send()

        @pl.when(r >= 2)
        def _():
            wait_slot(r % 2)

        total = own_ref[...].astype(F32) + recv_ref[...].astype(F32)
        for kk in range(3):
            total = total + got_ref[kk].astype(F32)
        sbuf[r % 2] = total
        local, remote = copies(r % 2)
        local.start()
        remote.start()

        @pl.when(r == nr - 1)
        def _():
            wait_slot(r % 2)
            if nr >= 2:
                wait_slot(1 - r % 2)
            half = out_ref.at[pl.ds(0, n_half), :]
            pltpu.make_async_remote_copy(src_ref=half, dst_ref=half, send_sem=send_sems.at[0], recv_sem=recv_sem,
                                         device_id=(x, y, c), device_id_type=MESH).wait_recv()

    blk = pl.BlockSpec((tr, ncol), src_map)
    return pl.pallas_call(
        body, name=name,
        grid_spec=pltpu.PrefetchScalarGridSpec(
            num_scalar_prefetch=1, grid=(nr,), in_specs=[blk, blk, pl.BlockSpec((3, tr, ncol), lambda r, co: (0, r, 0))],
            out_specs=ANY,
            scratch_shapes=[pltpu.VMEM((2, tr, ncol), F32), pltpu.SemaphoreType.DMA((2,)), pltpu.SemaphoreType.DMA((2,)),
                            pltpu.SemaphoreType.DMA]),
        out_shape=jax.ShapeDtypeStruct((2 * n_half, ncol), F32),
        compiler_params=_params(("arbitrary",)),
    )(coords, own, recv, got)


def _vec_all_reduce(v, name):
    rows, d = v.shape

    def body(v_ref, out_ref, gath_ref, send_sems, recv_sems):
        x, y, c, _ = _place()
        me = 4 * x + 2 * y + c
        gath_ref[me] = v_ref[...]
        flips = [(fx, fy, fc) for fx in (0, 1) for fy in (0, 1) for fc in (0, 1)][1:]
        sends = []
        for k, (fx, fy, fc) in enumerate(flips):
            peer = (1 - x if fx else x, 1 - y if fy else y, 1 - c if fc else c)
            sends.append(pltpu.make_async_remote_copy(
                src_ref=v_ref, dst_ref=gath_ref.at[me], send_sem=send_sems.at[k], recv_sem=recv_sems.at[k],
                device_id=peer, device_id_type=MESH))
        for cp in sends:
            cp.start()
        for cp in sends:
            cp.wait()
        acc = gath_ref[0]
        for dev in range(1, 8):
            acc = acc + gath_ref[dev]
        out_ref[...] = acc

    vm = pl.BlockSpec(memory_space=pltpu.VMEM)
    return pl.pallas_call(
        body, name=name, in_specs=[vm], out_specs=vm, out_shape=jax.ShapeDtypeStruct((rows, d), F32),
        scratch_shapes=[pltpu.VMEM((8, rows, d), F32), pltpu.SemaphoreType.DMA((7,)), pltpu.SemaphoreType.DMA((7,))],
    )(v)


_TRANSPOSED = ("w_ffn_gate", "w_ffn_up")
_SHARD_AXIS = {"w_in": 1, "w_attn_out": 1, "w_ret_out": 0, "w_o": 0, "w_ffn_gate": 0, "w_ffn_up": 0, "w_ffn_down": 0,
               "w_ple_gate": 0, "w_ple_up": 1}
_FFN_HIDDEN_ROWS = ("w_ffn_gate", "w_ffn_up", "w_ffn_down")
_VECTORS = ("ret_gn_g", "ln1_g", "ln1_b", "ln2_g", "ln2_b")
_WEIGHTS = ("w_in", "w_attn_out", "w_ret_out", "ret_gn_g", "w_o", "ln1_g", "ln1_b", "w_ffn_gate", "w_ffn_up", "w_ffn_down",
            "w_ple_gate", "w_ple_up", "ln2_g", "ln2_b")


def _local_grads(cfg, x, p, tgt, w, vec, coords=None):
    dist = coords is not None
    xb, pb = x.astype(BF16), p.astype(BF16)
    cos, sin = _rope_tables(cfg)
    lg = _log_gamma(cfg)
    wf = {"w_in": _all_gather(w["w_in"], _SHARD_AXIS["w_in"], "ag_w_in")} if dist else dict(w)
    grads, pairs, got, waiting = {}, {}, {}, []

    def gather_behind(names, matmul, n_main=1):
        if not dist:
            return matmul(())
        res = matmul([_gather_job(w[n], _SHARD_AXIS[n]) for n in names])
        for n, full in zip(names, res[n_main:]):
            wf[n] = _all_gather(w[n], _SHARD_AXIS[n], "ag_" + n, landed=full)
        return res[0] if n_main == 1 else res[:n_main]

    def scatter_jobs():
        names = list(waiting)
        del waiting[:]
        return [_scatter_job(pairs[n][2]) for n in names], lambda landed: got.update(zip(names, landed))

    def grad(a, b, n, name, host=False):
        if not dist:
            grads[n] = _mm(a, b, "tn", BF16, name)
            return
        jobs, keep = scatter_jobs() if host else ((), None)
        own, recv, *landed = _mm_tn_pair(a, b, _SHARD_AXIS[n], name, jobs=jobs)
        if host:
            keep(landed)
        pairs[n] = (own, recv, _pairsum(own, recv, _SHARD_AXIS[n], coords, "rs_" + n + "_pairsum"))
        waiting.append(n)

    def scatter_behind(matmul, n_main=1):
        if not waiting:
            return matmul(())
        jobs, keep = scatter_jobs()
        res = matmul(jobs)
        keep(res[n_main:])
        return res[0] if n_main == 1 else res[:n_main]

    proj = gather_behind(["w_attn_out", "w_ret_out", "w_o", "w_ffn_gate"],
                         lambda jobs: _mm(xb, wf["w_in"], "nn", BF16, "proj", jobs=jobs))
    aw = cfg.aw
    att = []
    for g, (_, dil) in enumerate(ATT_GROUPS):
        starts = [o + g * aw for o in (cfg.o_qa, cfg.o_ka, cfg.o_va)]
        if dil == 1:
            qkv, cols = (proj, proj, proj), tuple(s // aw for s in starts)
        else:
            qkv, cols = tuple(_deinterleave(proj[:, s:s + aw], dil) for s in starts), (0, 0, 0)
        o_g, lse_g = _attn_fwd(cfg, g, *qkv, cols)
        att.append((qkv, cols, _interleave(o_g, dil), _interleave(lse_g, dil)))
    o_att, lse = _attn_combine(cfg, [t[2] for t in att], [t[3] for t in att])
    y_att = _mm(o_att, wf["w_attn_out"], "nn", BF16, "y_att")
    rn, rstd_r, states, rg = _ret_fwd(cfg, proj, cos, sin, lg, vec["ret_gn_g"])
    def mix_tiles(yr, ga, g2, ya):
        yr = yr.astype(BF16)
        return yr, _sigmoid(ga.astype(F32)) * ya.astype(F32) + _sigmoid(g2.astype(F32)) * yr.astype(F32)

    y_ret, mixed = _mm(rg, wf["w_ret_out"], "nn", None, "y_ret",
                       epi=_Epilogue([(proj, cfg.o_ga), (proj, cfg.o_g2), (y_att, 0)], [BF16, BF16], mix_tiles))
    mo = _mm(mixed, wf["w_o"], "nn", F32, "mixed_out")
    xhat1, h1, rstd1 = _ln1(cfg, x, mo, vec["ln1_g"], vec["ln1_b"])
    u = gather_behind(["w_ffn_up"], lambda jobs: _mm(h1, wf["w_ffn_gate"], "nt", BF16, "ffn_u", jobs=jobs))

    def act_tiles(t, u):
        t, u = t.astype(BF16), u.astype(F32)
        return t, u * _sigmoid(u) * t.astype(F32)

    t, a = gather_behind(["w_ffn_down"], lambda jobs: _mm(h1, wf["w_ffn_up"], "nt", None, "ffn_t", jobs=jobs,
                                                          epi=_Epilogue([(u, 0)], [BF16, BF16], act_tiles)), n_main=2)
    ffn = gather_behind(["w_ple_gate", "w_ple_up"], lambda jobs: _mm(a, wf["w_ffn_down"], "nn", F32, "ffn_down", jobs=jobs))
    hp = _mm(h1, wf["w_ple_gate"], "nn", BF16, "ple_gate")
    pu = _mm(pb, wf["w_ple_up"], "nn", BF16, "ple_up")
    loss, dz2, dz2b, d_pg, d_pu, g_ln2g, g_ln2b = _head(cfg, xhat1, ffn, hp, pu, tgt, vec["ln1_g"], vec["ln1_b"],
                                                        vec["ln2_g"], vec["ln2_b"])

    def act_bwd_tiles(da, u, t):
        da, u = da.astype(BF16).astype(F32), u.astype(F32)
        sg = _sigmoid(u)
        return da * t.astype(F32) * (sg * (1.0 + u * (1.0 - sg))), da * u * sg

    grad(a, dz2b, "w_ffn_down", "g_ffn_down")
    d_u, d_t = scatter_behind(lambda jobs: _mm(dz2b, wf["w_ffn_down"], "nt", None, "d_a", jobs=jobs,
                                               epi=_Epilogue([(u, 0), (t, 0)], [BF16, BF16], act_bwd_tiles)), n_main=2)
    grad(d_u, h1, "w_ffn_gate", "g_ffn_gate")
    dh = scatter_behind(lambda jobs: _mm(d_u, wf["w_ffn_gate"], "nn", F32, "dh_u", jobs=jobs))
    grad(d_t, h1, "w_ffn_up", "g_ffn_up")
    dh = scatter_behind(lambda jobs: _mm(d_t, wf["w_ffn_up"], "nn", F32, "dh_t", add=dh, jobs=jobs))
    dh = _mm(d_pg, wf["w_ple_gate"], "nt", F32, "dh_pg", add=dh)
    grad(h1, d_pg, "w_ple_gate", "g_ple_gate")
    grad(pb, d_pu, "w_ple_up", "g_ple_up")
    dz1, dz1b, g_ln1g, g_ln1b = _ln1_bwd(cfg, dh, dz2, xhat1, rstd1, vec["ln1_g"])
    def mix_bwd_tiles(dm, ga, g2, ya, yr):
        dm = dm.astype(BF16).astype(F32)
        sa, s2 = _sigmoid(ga.astype(F32)), _sigmoid(g2.astype(F32))
        return dm * sa, dm * s2, dm * ya.astype(F32) * sa * (1.0 - sa), dm * yr.astype(F32) * s2 * (1.0 - s2)

    d_ya, d_yr, d_ga, d_g2 = _mm(dz1b, wf["w_o"], "nt", None, "d_mixed",
                                 epi=_Epilogue([(proj, cfg.o_ga), (proj, cfg.o_g2), (y_att, 0), (y_ret, 0)], [BF16] * 4,
                                               mix_bwd_tiles))
    grad(mixed, dz1b, "w_o", "g_o")
    d_oatt = _mm(d_ya, wf["w_attn_out"], "nt", BF16, "d_oatt")
    d_rg = _mm(d_yr, wf["w_ret_out"], "nt", BF16, "d_rg")
    grad(o_att, d_ya, "w_attn_out", "g_attn_out")
    grad(rg, d_yr, "w_ret_out", "g_ret_out")
    d_gr, d_r, g_gn = _ret_gate_bwd(cfg, proj, rn, vec["ret_gn_g"], rstd_r, d_rg)
    d_qr, d_kr, d_vr = _ret_bwd(cfg, proj, d_r, states, cos, sin, lg)
    delta = _attn_delta(cfg, d_oatt, o_att)
    d_qkv = ([], [], [])
    for g, (_, dil) in enumerate(ATT_GROUPS):
        parts = _attn_bwd(cfg, g, *att[g][0], _deinterleave(d_oatt, dil), _deinterleave(lse, dil), _deinterleave(delta, dil),
                          att[g][1])
        for kept, t in zip(d_qkv, parts):
            kept.append(_interleave(t, dil))
    dproj = jnp.concatenate(d_qkv[0] + d_qkv[1] + d_qkv[2] + [d_qr, d_kr, d_vr, d_gr, d_ga, d_g2], axis=1)
    grad(xb, dproj, "w_in", "g_in", host=True)
    grad_x = scatter_behind(lambda jobs: _mm(dproj, wf["w_in"], "nt", F32, "grad_x", add=dz1, add_scale=cfg.alpha, jobs=jobs))
    vgrads = {"ret_gn_g": g_gn, "ln1_g": g_ln1g, "ln1_b": g_ln1b, "ln2_g": g_ln2g, "ln2_b": g_ln2b}
    if dist:
        grads = {n: _sum_share(pairs[n][0], pairs[n][1], got[n], _SHARD_AXIS[n], coords, "rs_" + n + "_share") for n in pairs}
    return loss, grad_x, grads, vgrads


def _step(cfg, x, p, tgt, w, m, v):
    mats = [n for n in _WEIGHTS if n in _SHARD_AXIS]
    ffn_pad = cfg.ffn_shard - cfg.ffn // 4

    def stored(n, t):
        return t[0].T if n in _TRANSPOSED else t[0]

    def given(n, t):
        return (t.T if n in _TRANSPOSED else t)[None]

    def shard_bf16(n):
        ws = stored(n, w[n]).astype(BF16)
        if n in _FFN_HIDDEN_ROWS and ffn_pad:
            ws = jnp.pad(ws, [(0, ffn_pad), (0, 0)])
        return ws

    vec = {n: w[n] for n in _VECTORS}
    coords = jnp.stack([lax.axis_index("x"), lax.axis_index("y"), lax.axis_index("c")]).astype(jnp.int32)
    loss, grad_x, red, vgrads = _local_grads(cfg, x[0], p[0, 0], tgt[0], {n: shard_bf16(n) for n in mats}, vec, coords)
    loss = lax.psum(loss[0, 0], ("x", "y", "c"))
    stacked = jnp.concatenate([vgrads[n] for n in _VECTORS] + [jnp.zeros((3, cfg.d), F32)], axis=0)
    vsum = _vec_all_reduce(stacked, "vec_all_reduce")
    for i, n in enumerate(_VECTORS):
        red[n] = vsum[i:i + 1]
    outs = [[], [], [], []]
    for n in _WEIGHTS:
        if n in _SHARD_AXIS:
            res = _adamw(stored(n, w[n]), red[n], stored(n, m[n]), stored(n, v[n]), "adamw_" + n)
            res = [given(n, t) for t in res]
        else:
            res = _adamw(w[n], red[n], m[n], v[n], "adamw_" + n)
        for kept, t in zip(outs, res):
            kept.append(t)
    return (loss, grad_x[None], *outs[0], *outs[1], *outs[2], *outs[3])


def kernel(x, p, w_in, w_attn_out, w_ret_out, ret_gn_g, w_o, ln1_g, ln1_b, w_ffn_gate, w_ffn_up, w_ffn_down, w_ple_gate, w_ple_up, ln2_g, ln2_b, loss_target, m_w_in, m_w_attn_out, m_w_ret_out, m_ret_gn_g, m_w_o, m_ln1_g, m_ln1_b, m_w_ffn_gate, m_w_ffn_up, m_w_ffn_down, m_w_ple_gate, m_w_ple_up, m_ln2_g, m_ln2_b, v_w_in, v_w_attn_out, v_w_ret_out, v_ret_gn_g, v_w_o, v_ln1_g, v_ln1_b, v_w_ffn_gate, v_w_ffn_up, v_w_ffn_down, v_w_ple_gate, v_w_ple_up, v_ln2_g, v_ln2_b):
    w = dict(zip(_WEIGHTS, (w_in, w_attn_out, w_ret_out, ret_gn_g, w_o, ln1_g, ln1_b, w_ffn_gate, w_ffn_up, w_ffn_down,
                            w_ple_gate, w_ple_up, ln2_g, ln2_b)))
    m = dict(zip(_WEIGHTS, (m_w_in, m_w_attn_out, m_w_ret_out, m_ret_gn_g, m_w_o, m_ln1_g, m_ln1_b, m_w_ffn_gate, m_w_ffn_up,
                            m_w_ffn_down, m_w_ple_gate, m_w_ple_up, m_ln2_g, m_ln2_b)))
    v = dict(zip(_WEIGHTS, (v_w_in, v_w_attn_out, v_w_ret_out, v_ret_gn_g, v_w_o, v_ln1_g, v_ln1_b, v_w_ffn_gate, v_w_ffn_up,
                            v_w_ffn_down, v_w_ple_gate, v_w_ple_up, v_ln2_g, v_ln2_b)))
    return _step(_FULL, x, p, loss_target, w, m, v)
```
